```python
import math
import jax, jax.numpy as jnp
from jax import lax
import numpy as np

D_MODEL = 2048
BATCH = 8
SEQ = 2048
DEPTH = 1
DEC_BATCH = 128
DEC_SEQ = 4
PAST_LEN = 2048
PAGE_SIZE = 128

M_HEADS = 4
M_DK = D_MODEL // 8
M_DV = D_MODEL // 8
M_WIDTH = M_HEADS * M_DV
CONV_W = 4
M_CHUNK = 64
A_HEADS = 8
A_HEAD_DIM = D_MODEL // 16
A_WIDTH = A_HEADS * A_HEAD_DIM
A_KV_HEADS = 2
IDX_HEADS = 8
IDX_DIM = 64
TOPK_MAX = 256
Q_BLOCK = 128
N_BUCKETS = 32
MAX_DISTANCE = 128
D_FF = 5632
N_MOD = 9
EPS = 1e-6
NEG = -1e30
PROJ_WIDTH = 4 * M_WIDTH + 2 * M_HEADS + A_WIDTH + 2 * A_KV_HEADS * A_HEAD_DIM + IDX_HEADS * IDX_DIM + IDX_DIM + IDX_HEADS

kernel_name = 'hymba_mlstm_dsa_macaron_step'


def proj_offsets():
    sizes = [M_WIDTH, M_WIDTH, M_WIDTH, M_WIDTH, M_HEADS, M_HEADS,
             A_WIDTH, A_KV_HEADS * A_HEAD_DIM, A_KV_HEADS * A_HEAD_DIM,
             IDX_HEADS * IDX_DIM, IDX_DIM, IDX_HEADS]
    return np.cumsum(sizes)[:-1].tolist()


def rms_norm(x, g):
    x32 = x.astype(jnp.float32)
    y = x32 * lax.rsqrt(jnp.mean(x32 * x32, axis=-1, keepdims=True) + EPS)
    return (y * g.astype(jnp.float32)).astype(x.dtype)


def swiglu(h, w_gate, w_up, w_down):
    return (jax.nn.silu(h @ w_gate) * (h @ w_up)) @ w_down


def t5_bucket(dist):
    max_exact = N_BUCKETS // 2
    d = jnp.maximum(dist, 0)
    ratio = jnp.log(jnp.maximum(d, 1).astype(jnp.float32) / max_exact) / math.log(MAX_DISTANCE / max_exact)
    large = max_exact + (ratio * (N_BUCKETS - max_exact)).astype(jnp.int32)
    large = jnp.minimum(large, N_BUCKETS - 1)
    return jnp.where(d < max_exact, d, large)


def causal_conv(x_new, buf, w):
    t = x_new.shape[1]
    xp = jnp.concatenate([buf.astype(x_new.dtype), x_new], axis=1)
    y = xp[:, 0:t] * w[0]
    for j in range(1, CONV_W):
        y = y + xp[:, j:j + t] * w[j]
    return y, xp[:, xp.shape[1] - (CONV_W - 1):]


def mlstm_inputs(q_raw, k_raw, v_raw, i_raw, f_raw, conv_buf, conv_w, gate_b):
    b, t, _ = q_raw.shape
    qk, new_buf = causal_conv(jnp.concatenate([q_raw, k_raw], axis=-1), conv_buf, conv_w)
    qk = jax.nn.silu(qk.astype(jnp.float32))

    def heads(a):
        return a.reshape(b, t, M_HEADS, -1).transpose(0, 2, 1, 3)

    q = heads(qk[..., :M_WIDTH])
    k = heads(qk[..., M_WIDTH:]) * (M_DK ** -0.5)
    v = heads(v_raw.astype(jnp.float32))
    gates = (jnp.concatenate([i_raw, f_raw], axis=-1) + gate_b).astype(jnp.float32).transpose(0, 2, 1)
    ig = gates[:, :M_HEADS]
    lf = jax.nn.log_sigmoid(gates[:, M_HEADS:])
    return q, k, v, ig, lf, new_buf


def mlstm_chunk(carry, inp):
    c0, n0, m0 = carry
    q, k, v, ig, lf = inp
    l = q.shape[2]
    bcum = jnp.cumsum(lf, axis=-1)
    causal = jnp.tril(jnp.ones((l, l), bool))
    d_log = jnp.where(causal, bcum[..., :, None] - bcum[..., None, :] + ig[..., None, :], -jnp.inf)
    s_log = bcum + m0[..., None]
    m = jnp.maximum(s_log, jnp.max(d_log, axis=-1))
    dw = jnp.exp(d_log - m[..., None])
    sw = jnp.exp(s_log - m)
    scores = jnp.einsum('bhld,bhsd->bhls', q, k) * dw
    num = jnp.einsum('bhls,bhse->bhle', scores, v) + sw[..., None] * jnp.einsum('bhld,bhde->bhle', q, c0)
    den = jnp.sum(scores, axis=-1) + sw * jnp.einsum('bhld,bhd->bhl', q, n0)
    h = num / jnp.maximum(jnp.abs(den), jnp.exp(-m))[..., None]
    m_new = m[..., -1]
    wl = jnp.exp(bcum[..., -1:] - bcum + ig - m_new[..., None])
    decay = jnp.exp(bcum[..., -1] + m0 - m_new)
    c_new = decay[..., None, None] * c0 + jnp.einsum('bhl,bhld,bhle->bhde', wl, k, v)
    n_new = decay[..., None] * n0 + jnp.einsum('bhl,bhld->bhd', wl, k)
    return (c_new, n_new, m_new), h


def mlstm_scan(state, q, k, v, ig, lf):
    b, hh, t, _ = q.shape
    l = M_CHUNK if t % M_CHUNK == 0 else t
    nc = t // l

    def chunks(a):
        return jnp.moveaxis(a.reshape(a.shape[:2] + (nc, l) + a.shape[3:]), 2, 0)

    state, h = lax.scan(mlstm_chunk, state, (chunks(q), chunks(k), chunks(v), chunks(ig), chunks(lf)))
    h = jnp.moveaxis(h, 0, 2).reshape(b, hh, t, -1)
    return state, h


def mlstm_out(h, o_raw, g):
    b, hh, t, dv = h.shape
    hn = h * lax.rsqrt(jnp.mean(h * h, axis=-1, keepdims=True) + EPS)
    hn = jnp.swapaxes(hn, 1, 2) * g.reshape(hh, dv).astype(jnp.float32)
    return (hn.reshape(b, t, hh * dv) * jax.nn.sigmoid(o_raw.astype(jnp.float32))).astype(o_raw.dtype)


def dsa_heads(q_raw, k_raw, v_raw, qi_raw, ki_raw, w_raw, qn_g, kn_g):
    b, t, _ = q_raw.shape
    q = rms_norm(q_raw.reshape(b, t, A_HEADS, A_HEAD_DIM), qn_g)
    k = rms_norm(k_raw.reshape(b, t, A_KV_HEADS, A_HEAD_DIM), kn_g)
    v = v_raw.reshape(b, t, A_KV_HEADS, A_HEAD_DIM)
    qi = qi_raw.reshape(b, t, IDX_HEADS, IDX_DIM)
    wi = w_raw * (IDX_HEADS ** -0.5)
    return q, k, v, qi, ki_raw, wi


def indexer_scores(qi, wi, ki, q_pos, k_pos):
    s = jnp.einsum('btjd,bsd->btjs', qi.astype(jnp.float32), ki.astype(jnp.float32)) * (IDX_DIM ** -0.5)
    score = jnp.einsum('btj,btjs->bts', wi.astype(jnp.float32), jax.nn.relu(s))
    return jnp.where(k_pos[None, None, :] <= q_pos[None, :, None], score, NEG)


def sparse_attend(q, kg, vg, sel_pos, q_pos, t5_bias):
    b, t, _, hd = q.shape
    g, r = A_KV_HEADS, A_HEADS // A_KV_HEADS
    n = sel_pos.shape[-1]
    qg = q.reshape(b, t, g, r, hd).astype(jnp.float32)
    logits = jnp.einsum('btgrd,btngd->btgrn', qg, kg.astype(jnp.float32)) * (hd ** -0.5)
    dist = q_pos[None, :, None] - sel_pos
    bias = t5_bias[t5_bucket(dist)].astype(jnp.float32)
    bias = bias.reshape(b, t, n, g, r).transpose(0, 1, 3, 4, 2)
    logits = jnp.where((dist >= 0)[:, :, None, None, :], logits + bias, NEG)
    p = jax.nn.softmax(logits, axis=-1)
    o = jnp.einsum('btgrn,btngd->btgrd', p, vg.astype(jnp.float32))
    return o.reshape(b, t, A_HEADS * hd).astype(q.dtype)


def dsa_prompt(q, k, v, qi, ki, wi, t5_bias):
    b, t = q.shape[:2]
    k_sel = min(TOPK_MAX, t // 4)
    nb = t // Q_BLOCK
    pos = jnp.arange(t, dtype=jnp.int32)

    def to_blocks(a):
        return jnp.swapaxes(a.reshape((a.shape[0], nb, Q_BLOCK) + a.shape[2:]), 0, 1)

    def gather_rows(rows, idx):
        return jax.vmap(lambda rb, ib: rb[ib])(rows, idx)

    def block(args):
        qb, qib, wb, pb = args
        score = indexer_scores(qib, wb, ki, pb, pos)
        _, idx = lax.top_k(score, k_sel)
        return sparse_attend(qb, gather_rows(k, idx), gather_rows(v, idx), idx, pb, t5_bias)

    out = lax.map(block, (to_blocks(q), to_blocks(qi), to_blocks(wi), pos.reshape(nb, Q_BLOCK)))
    return jnp.swapaxes(out, 0, 1).reshape(b, t, A_WIDTH)


def dsa_sample(q, k, v, qi, ki, wi, cache_k, cache_v, cache_idx_k, page_table, t5_bias):
    b, t = q.shape[:2]
    n_pages = page_table.shape[1]
    past = n_pages * PAGE_SIZE
    total = past + t
    k_sel = min(TOPK_MAX, total // 4)
    q_pos = past + jnp.arange(t, dtype=jnp.int32)
    k_pos = jnp.arange(total, dtype=jnp.int32)
    ki_past = cache_idx_k[page_table].reshape(b, past, IDX_DIM)
    ki_all = jnp.concatenate([ki_past.astype(ki.dtype), ki], axis=1)
    score = indexer_scores(qi, wi, ki_all, q_pos, k_pos)
    _, idx = lax.top_k(score, k_sel)
    in_past = idx < past
    pidx = jnp.minimum(idx, past - 1)
    phys = page_table[jnp.arange(b)[:, None, None], pidx // PAGE_SIZE] * PAGE_SIZE + pidx % PAGE_SIZE
    nidx = jnp.clip(idx - past, 0, t - 1)

    def gather(pool, rows):
        from_past = pool.reshape(-1, A_KV_HEADS, A_HEAD_DIM)[phys]
        from_new = jax.vmap(lambda rb, ib: rb[ib])(rows, nidx)
        return jnp.where(in_past[..., None, None], from_past.astype(rows.dtype), from_new)

    return sparse_attend(q, gather(cache_k, k), gather(cache_v, v), idx, q_pos, t5_bias)


def decoder_layer(x, c, rec_state, past, t5_bias, ffn1_norm_g, ffn1_w_gate, ffn1_w_up, ffn1_w_down,
                  mix_norm_g, w_in, mlstm_conv_w, mlstm_gate_b, mlstm_out_g, q_norm_g, k_norm_g, w_out,
                  ffn2_norm_g, ffn2_w_gate, ffn2_w_up, ffn2_w_down, w_ada, b_ada):
    mods = (c @ w_ada + b_ada)[:, None, :]
    sh1, sc1, g1, sh2, sc2, g2, sh3, sc3, g3 = jnp.split(mods, N_MOD, axis=-1)
    h = rms_norm(x, ffn1_norm_g) * (1 + sc1) + sh1
    x = x + 0.5 * g1 * swiglu(h, ffn1_w_gate, ffn1_w_up, ffn1_w_down)
    h = rms_norm(x, mix_norm_g) * (1 + sc2) + sh2
    (q_m, k_m, v_m, o_m, i_m, f_m, q_a, k_a, v_a, qi_a, ki_a, wi_a) = jnp.split(h @ w_in, proj_offsets(), axis=-1)
    c0, n0, m0, conv_buf = rec_state
    q, k, v, ig, lf, conv_new = mlstm_inputs(q_m, k_m, v_m, i_m, f_m, conv_buf, mlstm_conv_w, mlstm_gate_b)
    (c_new, n_new, m_new), hm = mlstm_scan((c0, n0, m0), q, k, v, ig, lf)
    mo = mlstm_out(hm, o_m, mlstm_out_g)
    qa, ka, va, qi, ki, wi = dsa_heads(q_a, k_a, v_a, qi_a, ki_a, wi_a, q_norm_g, k_norm_g)
    if past is None:
        ao = dsa_prompt(qa, ka, va, qi, ki, wi, t5_bias)
    else:
        ao = dsa_sample(qa, ka, va, qi, ki, wi, past[0], past[1], past[2], past[3], t5_bias)
    x = x + g2 * (jnp.concatenate([mo, ao], axis=-1) @ w_out)
    h = rms_norm(x, ffn2_norm_g) * (1 + sc3) + sh3
    x = x + 0.5 * g3 * swiglu(h, ffn2_w_gate, ffn2_w_up, ffn2_w_down)
    new_state = (ka, va, ki, c_new.astype(x.dtype), n_new.astype(x.dtype), m_new.astype(x.dtype), conv_new)
    return x, new_state


def setup_inputs(seed: int = 0) -> dict:
    key = jax.random.key(seed)
    ks = iter(jax.random.split(key, 40))

    def nrm(shape, scale=1.0):
        return scale * jax.random.normal(next(ks), shape, jnp.float32)

    def gain(shape):
        return 1.0 + nrm(shape, 0.05)

    d = D_MODEL
    n_pages = PAST_LEN // PAGE_SIZE
    n_used = DEC_BATCH * n_pages
    n_pool = n_used + max(1, n_used // 4)
    page_table = jax.random.permutation(next(ks), n_pool)[:n_used].reshape(DEC_BATCH, n_pages).astype(jnp.int32)
    return {
        'x_prompt': nrm((BATCH, SEQ, d)),
        'x_sample': nrm((DEC_BATCH, DEC_SEQ, d)),
        'c_prompt': nrm((BATCH, d)),
        'c_sample': nrm((DEC_BATCH, d)),
        'cache_k': nrm((DEPTH, n_pool, PAGE_SIZE, A_KV_HEADS, A_HEAD_DIM)),
        'cache_v': nrm((DEPTH, n_pool, PAGE_SIZE, A_KV_HEADS, A_HEAD_DIM)),
        'cache_idx_k': nrm((DEPTH, n_pool, PAGE_SIZE, IDX_DIM)),
        'page_table': page_table,
        'state_C': nrm((DEPTH, DEC_BATCH, M_HEADS, M_DK, M_DV), 0.1),
        'state_n': nrm((DEPTH, DEC_BATCH, M_HEADS, M_DK), 0.1),
        'state_m': nrm((DEPTH, DEC_BATCH, M_HEADS)),
        'state_conv': nrm((DEPTH, DEC_BATCH, CONV_W - 1, 2 * M_WIDTH)),
        'ffn1_norm_g': gain((DEPTH, d)),
        'ffn1_w_gate': nrm((DEPTH, d, D_FF), d ** -0.5),
        'ffn1_w_up': nrm((DEPTH, d, D_FF), d ** -0.5),
        'ffn1_w_down': nrm((DEPTH, D_FF, d), D_FF ** -0.5),
        'mix_norm_g': gain((DEPTH, d)),
        'w_in': nrm((DEPTH, d, PROJ_WIDTH), d ** -0.5),
        'mlstm_conv_w': nrm((DEPTH, CONV_W, 2 * M_WIDTH), CONV_W ** -0.5),
        'mlstm_gate_b': jnp.concatenate([nrm((DEPTH, M_HEADS), 0.1), 3.0 + nrm((DEPTH, M_HEADS), 0.5)], axis=-1),
        'mlstm_out_g': gain((DEPTH, M_WIDTH)),
        'q_norm_g': gain((DEPTH, A_HEAD_DIM)),
        'k_norm_g': gain((DEPTH, A_HEAD_DIM)),
        't5_bias': nrm((N_BUCKETS, A_HEADS), 0.5),
        'w_out': nrm((DEPTH, M_WIDTH + A_WIDTH, d), (M_WIDTH + A_WIDTH) ** -0.5),
        'ffn2_norm_g': gain((DEPTH, d)),
        'ffn2_w_gate': nrm((DEPTH, d, D_FF), d ** -0.5),
        'ffn2_w_up': nrm((DEPTH, d, D_FF), d ** -0.5),
        'ffn2_w_down': nrm((DEPTH, D_FF, d), D_FF ** -0.5),
        'w_ada': nrm((DEPTH, d, N_MOD * d), 0.5 * d ** -0.5),
        'b_ada': nrm((DEPTH, N_MOD * d), 0.02),
    }


def reference(x_prompt, x_sample, c_prompt, c_sample, cache_k, cache_v, cache_idx_k, page_table,
              state_C, state_n, state_m, state_conv, ffn1_norm_g, ffn1_w_gate, ffn1_w_up, ffn1_w_down,
              mix_norm_g, w_in, mlstm_conv_w, mlstm_gate_b, mlstm_out_g, q_norm_g, k_norm_g, t5_bias,
              w_out, ffn2_norm_g, ffn2_w_gate, ffn2_w_up, ffn2_w_down, w_ada, b_ada):
    xp, xs = x_prompt, x_sample
    bp = xp.shape[0]
    st_p, st_s = [], []
    for l in range(DEPTH):
        lw = (ffn1_norm_g[l], ffn1_w_gate[l], ffn1_w_up[l], ffn1_w_down[l], mix_norm_g[l], w_in[l],
              mlstm_conv_w[l], mlstm_gate_b[l], mlstm_out_g[l], q_norm_g[l], k_norm_g[l], w_out[l],
              ffn2_norm_g[l], ffn2_w_gate[l], ffn2_w_up[l], ffn2_w_down[l], w_ada[l], b_ada[l])
        init_p = (jnp.zeros((bp, M_HEADS, M_DK, M_DV), jnp.float32),
                  jnp.zeros((bp, M_HEADS, M_DK), jnp.float32),
                  jnp.zeros((bp, M_HEADS), jnp.float32),
                  jnp.zeros((bp, CONV_W - 1, 2 * M_WIDTH), xp.dtype))
        xp, sp = decoder_layer(xp, c_prompt, init_p, None, t5_bias, *lw)
        init_s = (state_C[l].astype(jnp.float32), state_n[l].astype(jnp.float32),
                  state_m[l].astype(jnp.float32), state_conv[l])
        past_s = (cache_k[l], cache_v[l], cache_idx_k[l], page_table)
        xs, ss = decoder_layer(xs, c_sample, init_s, past_s, t5_bias, *lw)
        st_p.append(sp)
        st_s.append(ss)
    return (xp, xs,
            jnp.stack([s[0] for s in st_p]), jnp.stack([s[1] for s in st_p]), jnp.stack([s[2] for s in st_p]),
            jnp.stack([s[3] for s in st_p]), jnp.stack([s[4] for s in st_p]), jnp.stack([s[5] for s in st_p]),
            jnp.stack([s[6] for s in st_p]),
            jnp.stack([s[0] for s in st_s]), jnp.stack([s[1] for s in st_s]), jnp.stack([s[2] for s in st_s]),
            jnp.stack([s[3] for s in st_s]), jnp.stack([s[4] for s in st_s]), jnp.stack([s[5] for s in st_s]),
            jnp.stack([s[6] for s in st_s]))
```

```python
import functools
import math

import numpy as np
import jax
import jax.numpy as jnp
from jax import lax
from jax.experimental import pallas as pl
from jax.experimental.pallas import tpu as pltpu

F32 = jnp.float32
BF16 = jnp.bfloat16

M_HEADS = 4
CONV_W = 4
M_CHUNK = 64
A_HEADS = 8
A_KV_HEADS = 2
IDX_HEADS = 8
IDX_DIM = 64
TOPK_MAX = 256
Q_BLOCK = 128
PAGE_SIZE = 128
N_BUCKETS = 32
MAX_DISTANCE = 128
N_MOD = 9
EPS = 1e-6
NEG = -1e30

LANE = 128
SUBLANE = 8
ROW_TILE = 512
VMEM_LIMIT = 56 * 1024 * 1024

TAIL_KI = 0
TAIL_WI = IDX_DIM
TAIL_IG = IDX_DIM + IDX_HEADS
TAIL_FG = TAIL_IG + M_HEADS

INT_MIN = -2 ** 31


def _params(sem, vmem=None):
    return pltpu.CompilerParams(dimension_semantics=sem, vmem_limit_bytes=vmem)


def _dot(a, b):
    return jnp.dot(a, b, preferred_element_type=F32)


def _dot_nt(a, b):
    return lax.dot_general(a, b, (((1,), (1,)), ((), ())), preferred_element_type=F32)


def _dot_tn(a, b):
    return lax.dot_general(a, b, (((0,), (0,)), ((), ())), preferred_element_type=F32)


def _rms(x, g):
    return x * lax.rsqrt(jnp.mean(x * x, axis=-1, keepdims=True) + EPS) * g


def _row_tile(n):
    return ROW_TILE if n % ROW_TILE == 0 else n


def _ada_kernel(c_ref, w_ref, b_ref, o_ref):
    o_ref[...] = _dot(c_ref[...].astype(BF16), w_ref[...].astype(BF16)) + b_ref[...]


def _ada(c_all, w_ada, b_ada):
    r, d = c_all.shape
    n = w_ada.shape[1]
    tn = 1024 if n % 1024 == 0 else n
    return pl.pallas_call(
        _ada_kernel,
        grid=(n // tn,),
        in_specs=[pl.BlockSpec((r, d), lambda j: (0, 0)),
                  pl.BlockSpec((d, tn), lambda j: (0, j)),
                  pl.BlockSpec((1, tn), lambda j: (0, j))],
        out_specs=pl.BlockSpec((r, tn), lambda j: (0, j)),
        out_shape=jax.ShapeDtypeStruct((r, n), F32),
        compiler_params=_params(("arbitrary",), VMEM_LIMIT),
        name="ada",
    )(c_all, w_ada, b_ada.reshape(1, n))


def _ffn_kernel(x_ref, sh_ref, sc_ref, gt_ref, g_ref, wg_ref, wu_ref, wd_ref, o_ref, h_ref, acc_ref):
    j = pl.program_id(1)

    @pl.when(j == 0)
    def _():
        h_ref[...] = (_rms(x_ref[...], g_ref[...]) * (1.0 + sc_ref[...]) + sh_ref[...]).astype(BF16)
        acc_ref[...] = jnp.zeros_like(acc_ref)

    h = h_ref[...]
    a = _dot(h, wg_ref[...])
    u = _dot(h, wu_ref[...])
    act = (a * jax.nn.sigmoid(a) * u).astype(BF16)
    acc_ref[...] += _dot(act, wd_ref[...])

    @pl.when(j == pl.num_programs(1) - 1)
    def _():
        o_ref[...] = x_ref[...] + 0.5 * gt_ref[...] * acc_ref[...]


def _mod_spec(mod, m, tm, d):
    if mod.ndim == 2:
        return pl.BlockSpec((tm, d), lambda i, j: (i, 0))
    tiles_per_seq = (m // mod.shape[0]) // tm
    return pl.BlockSpec((None, 1, d), lambda i, j: (i // tiles_per_seq, 0, 0))


def _ffn(x, sh, sc, gt, g, wg, wu, wd):
    m, d = x.shape
    f = wg.shape[1]
    tm = _row_tile(m)
    tf = 512 if f % 512 == 0 else f
    ms = _mod_spec(sh, m, tm, d)
    return pl.pallas_call(
        _ffn_kernel,
        grid=(m // tm, f // tf),
        in_specs=[pl.BlockSpec((tm, d), lambda i, j: (i, 0)), ms, ms, ms,
                  pl.BlockSpec((1, d), lambda i, j: (0, 0)),
                  pl.BlockSpec((d, tf), lambda i, j: (0, j)),
                  pl.BlockSpec((d, tf), lambda i, j: (0, j)),
                  pl.BlockSpec((tf, d), lambda i, j: (j, 0))],
        out_specs=pl.BlockSpec((tm, d), lambda i, j: (i, 0)),
        out_shape=jax.ShapeDtypeStruct((m, d), F32),
        scratch_shapes=[pltpu.VMEM((tm, d), BF16), pltpu.VMEM((tm, d), F32)],
        compiler_params=_params(("parallel", "arbitrary"), VMEM_LIMIT),
        name="ffn",
    )(x, sh, sc, gt, g.reshape(1, d), wg, wu, wd)


def _proj_kernel(x_ref, sh_ref, sc_ref, g_ref, w_ref, o_ref, h_ref):
    @pl.when(pl.program_id(1) == 0)
    def _():
        h_ref[...] = (_rms(x_ref[...], g_ref[...]) * (1.0 + sc_ref[...]) + sh_ref[...]).astype(BF16)

    o_ref[...] = _dot(h_ref[...], w_ref[...])


def _proj(x, sh, sc, g, w):
    m, d = x.shape
    n = w.shape[1]
    tm = _row_tile(m)
    tn = 7 * LANE
    assert n % tn == 0
    ms = _mod_spec(sh, m, tm, d)
    return pl.pallas_call(
        _proj_kernel,
        grid=(m // tm, n // tn),
        in_specs=[pl.BlockSpec((tm, d), lambda i, j: (i, 0)), ms, ms,
                  pl.BlockSpec((1, d), lambda i, j: (0, 0)),
                  pl.BlockSpec((d, tn), lambda i, j: (0, j))],
        out_specs=pl.BlockSpec((tm, tn), lambda i, j: (i, j)),
        out_shape=jax.ShapeDtypeStruct((m, n), F32),
        scratch_shapes=[pltpu.VMEM((tm, d), BF16)],
        compiler_params=_params(("parallel", "arbitrary"), VMEM_LIMIT),
        name="proj",
    )(x, sh, sc, g.reshape(1, d), w)


def _pack_w_in(w_in, mw, aw, kvw):
    d = w_in.shape[0]
    o = np.cumsum([0, mw, mw, mw, mw, M_HEADS, M_HEADS, aw, kvw, kvw, IDX_HEADS * IDX_DIM, IDX_DIM, IDX_HEADS])
    seg = lambda k: np.arange(o[k], o[k + 1])
    cols = np.concatenate([seg(0), seg(1), seg(2), seg(3), seg(6), seg(7), seg(8), seg(9),
                           seg(10), seg(11), seg(4), seg(5)])
    packed = jnp.take(w_in, jnp.asarray(cols, jnp.int32), axis=1).astype(BF16)
    pad = (-packed.shape[1]) % LANE
    return jnp.concatenate([packed, jnp.zeros((d, pad), BF16)], axis=1)


def _log_sigmoid(x):
    return jnp.minimum(x, 0.0) - jnp.log1p(jnp.exp(-jnp.abs(x)))


def _mlstm_kernel(qk_ref, v_ref, o_ref, tail_ref, prev_ref, cw_ref, gb_ref, og_ref, c0_ref, n0_ref, m0_ref,
                  mo_ref, cst_ref, nst_ref, mst_ref, xbuf_ref, *, L, valid, dk, dv):
    width = M_HEADS * dk

    @pl.when(pl.program_id(1) == 0)
    def _():
        xbuf_ref[0:SUBLANE, :] = prev_ref[...]
        cst_ref[...] = c0_ref[...]
        nst_ref[...] = n0_ref[...]
        mst_ref[...] = m0_ref[...]

    xbuf_ref[SUBLANE:SUBLANE + L, :] = qk_ref[...]
    base = SUBLANE - (CONV_W - 1)
    y = xbuf_ref[base:base + L, :] * cw_ref[0:1, :]
    for j in range(1, CONV_W):
        y = y + xbuf_ref[base + j:base + j + L, :] * cw_ref[j:j + 1, :]
    tail_rows = xbuf_ref[L:L + SUBLANE, :]
    xbuf_ref[0:SUBLANE, :] = tail_rows
    qk = y * jax.nn.sigmoid(y)

    gates = tail_ref[...] + gb_ref[...]
    row = lax.broadcasted_iota(jnp.int32, (L, LANE), 0)
    lf = _log_sigmoid(gates)
    ig = gates
    if valid < L:
        lf = jnp.where(row < valid, lf, 0.0)
        ig = jnp.where(row < valid, ig, -jnp.inf)
    bcum = lf
    s = 1
    while s < L:
        bcum = bcum + jnp.where(row >= s, pltpu.roll(bcum, s, axis=0), 0.0)
        s *= 2
    bmi_t = (pltpu.roll(ig, TAIL_FG - TAIL_IG, axis=1) - bcum).T

    ri = lax.broadcasted_iota(jnp.int32, (L, L), 0)
    ci = lax.broadcasted_iota(jnp.int32, (L, L), 1)
    causal = ri >= ci

    for h in range(M_HEADS):
        li, lfh = TAIL_IG + h, TAIL_FG + h
        a_col = bcum[:, lfh:lfh + 1]
        ig_col = ig[:, li:li + 1]
        b_row = bmi_t[lfh:lfh + 1, :]
        m0 = mst_ref[:, h:h + 1]
        d_log = jnp.where(causal, a_col + b_row, -jnp.inf)
        s_log = a_col + m0
        m_col = jnp.maximum(s_log, jnp.max(d_log, axis=-1, keepdims=True))
        dw = jnp.exp(d_log - m_col)
        sw = jnp.exp(s_log - m_col)

        q = qk[:, h * dk:(h + 1) * dk]
        k = qk[:, width + h * dk:width + (h + 1) * dk] * (dk ** -0.5)
        v = v_ref[:, h * dv:(h + 1) * dv]
        qb, kb = q.astype(BF16), k.astype(BF16)
        c_prev = cst_ref[h]
        n_prev = nst_ref[h:h + 1, :]
        scores = _dot_nt(qb, kb) * dw
        num = _dot(scores.astype(BF16), v.astype(BF16)) + sw * _dot(qb, c_prev.astype(BF16))
        den = jnp.sum(scores, axis=-1, keepdims=True) + sw * jnp.sum(q * n_prev, axis=-1, keepdims=True)
        hh = num / jnp.maximum(jnp.abs(den), jnp.exp(-m_col))

        m_new = m_col[L - 1:L, :]
        a_last = a_col[L - 1:L, :]
        wl = jnp.exp(a_last - a_col + ig_col - m_new)
        decay = jnp.exp(a_last + m0 - m_new)
        cst_ref[h] = decay * c_prev + _dot_tn(kb, (wl * v).astype(BF16))
        nst_ref[h:h + 1, :] = decay * n_prev + jnp.sum(wl * k, axis=0, keepdims=True)
        mst_ref[:, h:h + 1] = m_new

        hn = hh * lax.rsqrt(jnp.mean(hh * hh, axis=-1, keepdims=True) + EPS) * og_ref[:, h * dv:(h + 1) * dv]
        gate = jax.nn.sigmoid(o_ref[:, h * dv:(h + 1) * dv])
        mo_ref[:, h * dv:(h + 1) * dv] = (hn * gate).astype(mo_ref.dtype)


def _mlstm(p, prev8, conv_w, gate_row, out_g, c0, n0, m0, *, n_seq, L, valid, out_dtype):
    rows = p.shape[0]
    nc = rows // (n_seq * L)
    dk = c0.shape[2]
    dv = c0.shape[3]
    mw = M_HEADS * dv
    qkw = 2 * M_HEADS * dk
    assert qkw % mw == 0 and (2 * qkw) % mw == 0
    rmap = lambda b, c: b * nc + c
    kern = functools.partial(_mlstm_kernel, L=L, valid=valid, dk=dk, dv=dv)
    return pl.pallas_call(
        kern,
        grid=(n_seq, nc),
        in_specs=[pl.BlockSpec((L, qkw), lambda b, c: (rmap(b, c), 0)),
                  pl.BlockSpec((L, mw), lambda b, c: (rmap(b, c), qkw // mw)),
                  pl.BlockSpec((L, mw), lambda b, c: (rmap(b, c), qkw // mw + 1)),
                  pl.BlockSpec((L, LANE), lambda b, c: (rmap(b, c), p.shape[1] // LANE - 1)),
                  pl.BlockSpec((None, SUBLANE, qkw), lambda b, c: (b, 0, 0)),
                  pl.BlockSpec((CONV_W, qkw), lambda b, c: (0, 0)),
                  pl.BlockSpec((1, LANE), lambda b, c: (0, 0)),
                  pl.BlockSpec((1, mw), lambda b, c: (0, 0)),
                  pl.BlockSpec((None, M_HEADS, dk, dv), lambda b, c: (b, 0, 0, 0)),
                  pl.BlockSpec((None, M_HEADS, dk), lambda b, c: (b, 0, 0)),
                  pl.BlockSpec((None, 1, LANE), lambda b, c: (b, 0, 0))],
        out_specs=[pl.BlockSpec((L, mw), lambda b, c: (rmap(b, c), 0)),
                   pl.BlockSpec((None, M_HEADS, dk, dv), lambda b, c: (b, 0, 0, 0)),
                   pl.BlockSpec((None, M_HEADS, dk), lambda b, c: (b, 0, 0)),
                   pl.BlockSpec((None, 1, LANE), lambda b, c: (b, 0, 0))],
        out_shape=[jax.ShapeDtypeStruct((rows, mw), out_dtype),
                   jax.ShapeDtypeStruct(c0.shape, F32),
                   jax.ShapeDtypeStruct(n0.shape, F32),
                   jax.ShapeDtypeStruct(m0.shape, F32)],
        scratch_shapes=[pltpu.VMEM((SUBLANE + L, qkw), F32)],
        compiler_params=_params(("parallel", "arbitrary"), VMEM_LIMIT),
        name="mlstm",
    )(p, p, p, p, prev8, conv_w, gate_row, out_g.reshape(1, mw), c0, n0, m0)


def _bucket_np(dist):
    me = N_BUCKETS // 2
    d = np.maximum(dist, 0)
    ratio = np.log(np.maximum(d, 1).astype(np.float64) / me) / math.log(MAX_DISTANCE / me)
    large = np.minimum(me + (ratio * (N_BUCKETS - me)).astype(np.int64), N_BUCKETS - 1)
    return np.where(d < me, d, large).astype(np.int32)


def _bias_kernel(t5_ref, bkp_ref, bks_ref, far_ref, op_ref, os_ref):
    def lookup(bk, h):
        acc = jnp.zeros(bk.shape, F32)
        for b in range(N_BUCKETS):
            acc = jnp.where(bk == b, t5_ref[b, h], acc)
        return acc

    for h in range(A_HEADS):
        far = t5_ref[far_ref[0], h]
        for t in range(op_ref.shape[1]):
            op_ref[h, t] = lookup(bkp_ref[t], h) - far
        os_ref[h] = lookup(bks_ref[...], h)


def _bias_tables(t5_bias, t_prompt, past, t_new):
    a = np.arange(Q_BLOCK)
    rel = a[:, None] - a[None, :]
    bkp = np.stack([_bucket_np(rel), _bucket_np(rel + Q_BLOCK)])
    far = _bucket_np(np.arange(Q_BLOCK + 1, max(t_prompt, past + t_new) + Q_BLOCK))
    assert (far == far[0]).all(), "bias must be constant beyond one query block"
    nk = past + LANE
    tq = np.arange(SUBLANE)[:, None]
    bks = _bucket_np(past + tq - np.arange(nk)[None, :])
    return pl.pallas_call(
        _bias_kernel,
        in_specs=[pl.BlockSpec(memory_space=pltpu.SMEM),
                  pl.BlockSpec(memory_space=pltpu.VMEM),
                  pl.BlockSpec(memory_space=pltpu.VMEM),
                  pl.BlockSpec(memory_space=pltpu.SMEM)],
        out_specs=[pl.BlockSpec(memory_space=pltpu.VMEM), pl.BlockSpec(memory_space=pltpu.VMEM)],
        out_shape=[jax.ShapeDtypeStruct((A_HEADS, 2, Q_BLOCK, Q_BLOCK), F32),
                   jax.ShapeDtypeStruct((A_HEADS, SUBLANE, nk), F32)],
        name="t5_bias_tables",
    )(t5_bias, jnp.asarray(bkp), jnp.asarray(bks), jnp.asarray(far[:1]))


def _sort_key(score):
    bits = lax.bitcast_convert_type(score, jnp.int32)
    return jnp.where(bits < 0, bits ^ jnp.int32(0x7FFFFFFF), bits)


def _kth_key(count_ge, rows, k_sel):
    def body(it, ans_u):
        cand_u = ans_u | lax.shift_left(jnp.int32(1), jnp.int32(31) - it)
        cnt = count_ge(cand_u ^ jnp.int32(INT_MIN))
        return jnp.where(cnt >= k_sel, cand_u, ans_u)

    ans_u = lax.fori_loop(0, 32, body, jnp.zeros((rows, 1), jnp.int32))
    return ans_u ^ jnp.int32(INT_MIN)


def _strict_upper():
    r = lax.broadcasted_iota(jnp.int32, (LANE, LANE), 0)
    c = lax.broadcasted_iota(jnp.int32, (LANE, LANE), 1)
    return jnp.where(r < c, 1.0, 0.0).astype(BF16)


def _select_tiles(key_tiles, thr, k_sel):
    rows = thr.shape[0]
    c_gt = jnp.zeros((rows, 1), F32)
    for kt in key_tiles:
        c_gt = c_gt + jnp.sum(jnp.where(kt > thr, 1.0, 0.0), axis=-1, keepdims=True)
    need = k_sel - c_gt
    upper = _strict_upper()
    run = jnp.zeros((rows, 1), F32)
    out = []
    for kt in key_tiles:
        eq = jnp.where(kt == thr, 1.0, 0.0)
        before = _dot(eq.astype(BF16), upper) + run
        out.append(jnp.where(kt > thr, 1.0, jnp.where(before < need, eq, 0.0)))
        run = run + jnp.sum(eq, axis=-1, keepdims=True)
    return out


def _dsa_prompt_kernel(qa_ref, qi_ref, qtail_ref, k_ref, v_ref, ktail_ref, qg_ref, kg_ref, bias_ref,
                       ao_ref, kn_ref,
                       kb_ref, vb_ref, kib_ref, qn_ref, key_ref, sel_ref, m_ref, l_ref, acc_ref, *, T, k_sel, hd):
    i = pl.program_id(1)
    nt = T // LANE
    rep = A_HEADS // A_KV_HEADS
    scale = hd ** -0.5

    @pl.when(i == 0)
    def _():
        for g in range(A_KV_HEADS):
            kn = _rms(k_ref[:, g * hd:(g + 1) * hd], kg_ref[...])
            kn_ref[:, g * hd:(g + 1) * hd] = kn
            kb_ref[:, g * hd:(g + 1) * hd] = kn.astype(BF16)
        vb_ref[...] = v_ref[...].astype(BF16)
        kib_ref[...] = ktail_ref[:, TAIL_KI:TAIL_KI + IDX_DIM].astype(BF16)

    wi = qtail_ref[:, TAIL_WI:TAIL_WI + IDX_HEADS] * (IDX_HEADS ** -0.5) * (IDX_DIM ** -0.5)
    kib = kib_ref[...]
    score = jnp.zeros((Q_BLOCK, T), F32)
    for j in range(IDX_HEADS):
        s = _dot_nt(qi_ref[:, j * IDX_DIM:(j + 1) * IDX_DIM].astype(BF16), kib)
        score = score + wi[:, j:j + 1] * jnp.maximum(s, 0.0)
    t_glob = i * Q_BLOCK + lax.broadcasted_iota(jnp.int32, (Q_BLOCK, T), 0)
    col = lax.broadcasted_iota(jnp.int32, (Q_BLOCK, T), 1)
    score = jnp.where(col <= t_glob, score + 0.0, NEG)
    key_ref[...] = _sort_key(score)

    def count_ge(cand):
        return jnp.sum(jnp.where(key_ref[...] >= cand, 1.0, 0.0), axis=-1, keepdims=True)

    thr = _kth_key(count_ge, Q_BLOCK, float(k_sel))
    sel = _select_tiles([key_ref[:, t * LANE:(t + 1) * LANE] for t in range(nt)], thr, float(k_sel))
    for t in range(nt):
        sel_ref[t] = sel[t]

    for h in range(A_HEADS):
        qn_ref[h] = _rms(qa_ref[:, h * hd:(h + 1) * hd], qg_ref[...]).astype(BF16)
    m_ref[...] = jnp.full(m_ref.shape, NEG, F32)
    l_ref[...] = jnp.zeros(l_ref.shape, F32)
    acc_ref[...] = jnp.zeros(acc_ref.shape, F32)
    lr = lax.broadcasted_iota(jnp.int32, (Q_BLOCK, LANE), 0)
    lc = lax.broadcasted_iota(jnp.int32, (Q_BLOCK, LANE), 1)
    tril = jnp.where(lc <= lr, 1.0, 0.0)

    def tile(j, carry):
        start = pl.multiple_of(j * LANE, LANE)
        kt = kb_ref[pl.ds(start, LANE), :]
        vt = vb_ref[pl.ds(start, LANE), :]
        allowed = sel_ref[j] * jnp.maximum(tril, (j < i).astype(F32)) > 0.0
        on_diag = (j == i).astype(F32)
        below_diag = (j == i - 1).astype(F32)
        for h in range(A_HEADS):
            g = h // rep
            bias = on_diag * bias_ref[h, 0] + below_diag * bias_ref[h, 1]
            s = _dot_nt(qn_ref[h], kt[:, g * hd:(g + 1) * hd]) * scale + bias
            s = jnp.where(allowed, s, NEG)
            m_old = m_ref[h]
            m_new = jnp.maximum(m_old, jnp.max(s, axis=-1, keepdims=True))
            alpha = jnp.exp(m_old - m_new)
            p = jnp.where(allowed, jnp.exp(s - m_new), 0.0)
            l_ref[h] = alpha * l_ref[h] + jnp.sum(p, axis=-1, keepdims=True)
            acc_ref[h] = alpha * acc_ref[h] + _dot(p.astype(BF16), vt[:, g * hd:(g + 1) * hd])
            m_ref[h] = m_new
        return carry

    lax.fori_loop(0, i + 1, tile, 0)
    for h in range(A_HEADS):
        ao_ref[:, h * hd:(h + 1) * hd] = (acc_ref[h] / l_ref[h]).astype(ao_ref.dtype)


def _dsa_prompt(p, qn_g, kn_g, bias_p, *, n_seq, T, hd):
    rows = p.shape[0]
    nb = T // Q_BLOCK
    aw = A_HEADS * hd
    kvw = A_KV_HEADS * hd
    iw = IDX_HEADS * IDX_DIM
    k_sel = min(TOPK_MAX, T // 4)
    mcols = p.shape[1] - LANE - iw - 2 * kvw - aw
    assert mcols % aw == 0 and (mcols + aw) % kvw == 0 and (mcols + aw + 2 * kvw) % iw == 0
    qa_blk = mcols // aw
    k_blk = (mcols + aw) // kvw
    qi_blk = (mcols + aw + 2 * kvw) // iw
    tail_blk = p.shape[1] // LANE - 1
    kern = functools.partial(_dsa_prompt_kernel, T=T, k_sel=k_sel, hd=hd)
    return pl.pallas_call(
        kern,
        grid=(n_seq, nb),
        in_specs=[pl.BlockSpec((Q_BLOCK, aw), lambda b, i: (b * nb + i, qa_blk)),
                  pl.BlockSpec((Q_BLOCK, iw), lambda b, i: (b * nb + i, qi_blk)),
                  pl.BlockSpec((Q_BLOCK, LANE), lambda b, i: (b * nb + i, tail_blk)),
                  pl.BlockSpec((T, kvw), lambda b, i: (b, k_blk)),
                  pl.BlockSpec((T, kvw), lambda b, i: (b, k_blk + 1)),
                  pl.BlockSpec((T, LANE), lambda b, i: (b, tail_blk)),
                  pl.BlockSpec((1, hd), lambda b, i: (0, 0)),
                  pl.BlockSpec((1, hd), lambda b, i: (0, 0)),
                  pl.BlockSpec((A_HEADS, 2, Q_BLOCK, Q_BLOCK), lambda b, i: (0, 0, 0, 0))],
        out_specs=[pl.BlockSpec((Q_BLOCK, aw), lambda b, i: (b * nb + i, 0)),
                   pl.BlockSpec((T, kvw), lambda b, i: (b, 0))],
        out_shape=[jax.ShapeDtypeStruct((rows, aw), BF16),
                   jax.ShapeDtypeStruct((rows, kvw), F32)],
        scratch_shapes=[pltpu.VMEM((T, kvw), BF16), pltpu.VMEM((T, kvw), BF16), pltpu.VMEM((T, IDX_DIM), BF16),
                        pltpu.VMEM((A_HEADS, Q_BLOCK, hd), BF16),
                        pltpu.VMEM((Q_BLOCK, T), jnp.int32),
                        pltpu.VMEM((T // LANE, Q_BLOCK, LANE), F32),
                        pltpu.VMEM((A_HEADS, Q_BLOCK, 1), F32),
                        pltpu.VMEM((A_HEADS, Q_BLOCK, 1), F32),
                        pltpu.VMEM((A_HEADS, Q_BLOCK, hd), F32)],
        compiler_params=_params(("parallel", "arbitrary"), VMEM_LIMIT),
        name="dsa_prompt",
    )(p, p, p, p, p, p, qn_g.reshape(1, hd), kn_g.reshape(1, hd), bias_p)


def _dsa_sample_kernel(pt_ref, qa_ref, qi_ref, wi_ref, knew_ref, vnew_ref, kinew_ref, qg_ref, kg_ref, bias_ref,
                       *rest, n_pages, t_new, k_sel, hd):
    kp = rest[0:n_pages]
    vp = rest[n_pages:2 * n_pages]
    ip = rest[2 * n_pages:3 * n_pages]
    ao_ref, kn_ref = rest[3 * n_pages:]
    del pt_ref
    past = n_pages * PAGE_SIZE
    nk = past + LANE
    nt = n_pages + 1
    rep = A_HEADS // A_KV_HEADS
    rows_q = rep * SUBLANE
    scale = hd ** -0.5

    def pad_rows(x):
        return jnp.concatenate([x, jnp.zeros((LANE - x.shape[0], x.shape[1]), x.dtype)], axis=0)

    kn_new = jnp.concatenate([_rms(knew_ref[:, g * hd:(g + 1) * hd], kg_ref[...]) for g in range(A_KV_HEADS)], axis=1)
    kn_ref[...] = kn_new
    k_tiles = [kp[t][...].astype(BF16) for t in range(n_pages)] + [pad_rows(kn_new).astype(BF16)]
    v_tiles = [vp[t][...].astype(BF16) for t in range(n_pages)] + [pad_rows(vnew_ref[...]).astype(BF16)]
    i_tiles = [ip[t][...].astype(BF16) for t in range(n_pages)] + [pad_rows(kinew_ref[...]).astype(BF16)]

    qi = qi_ref[...].astype(BF16)
    wi = wi_ref[...] * (IDX_HEADS ** -0.5) * (IDX_DIM ** -0.5)
    tq = lax.broadcasted_iota(jnp.int32, (SUBLANE, LANE), 0)
    lane = lax.broadcasted_iota(jnp.int32, (SUBLANE, LANE), 1)
    key_tiles = []
    for t in range(nt):
        r = _dot_nt(qi, i_tiles[t])
        sc = jnp.zeros((SUBLANE, LANE), F32)
        for j in range(IDX_HEADS):
            sc = sc + wi[j * SUBLANE:(j + 1) * SUBLANE, :] * jnp.maximum(r[j * SUBLANE:(j + 1) * SUBLANE, :], 0.0)
        if t == n_pages:
            sc = jnp.where(lane <= tq, sc + 0.0, jnp.where(lane < t_new, NEG, -jnp.inf))
        else:
            sc = sc + 0.0
        key_tiles.append(_sort_key(sc))

    def count_ge(cand):
        cnt = jnp.zeros((SUBLANE, 1), F32)
        for kt in key_tiles:
            cnt = cnt + jnp.sum(jnp.where(kt >= cand, 1.0, 0.0), axis=-1, keepdims=True)
        return cnt

    thr = _kth_key(count_ge, SUBLANE, float(k_sel))
    sel = _select_tiles(key_tiles, thr, float(k_sel))
    sel[n_pages] = jnp.where(lane <= tq, sel[n_pages], 0.0)

    for g in range(A_KV_HEADS):
        qn = _rms(qa_ref[g], qg_ref[...]).astype(BF16)
        logits = []
        for t in range(nt):
            s = _dot_nt(qn, k_tiles[t][:, g * hd:(g + 1) * hd]) * scale + bias_ref[g, :, t * LANE:(t + 1) * LANE]
            allowed = jnp.concatenate([sel[t]] * rep, axis=0) > 0.0
            logits.append(jnp.where(allowed, s, NEG))
        m = logits[0].max(axis=-1, keepdims=True)
        for t in range(1, nt):
            m = jnp.maximum(m, logits[t].max(axis=-1, keepdims=True))
        l = jnp.zeros((rows_q, 1), F32)
        acc = jnp.zeros((rows_q, hd), F32)
        for t in range(nt):
            pr = jnp.exp(logits[t] - m)
            l = l + jnp.sum(pr, axis=-1, keepdims=True)
            acc = acc + _dot(pr.astype(BF16), v_tiles[t][:, g * hd:(g + 1) * hd])
        ao_ref[g] = acc / l


def _dsa_sample(qa, qi, wi, knew, vnew, kinew, qn_g, kn_g, bias_s, cache_k, cache_v, cache_i, page_table, *, t_new, hd):
    n_seq, n_pages = page_table.shape
    kvw = A_KV_HEADS * hd
    rep = A_HEADS // A_KV_HEADS
    rows_q = rep * SUBLANE
    nk = n_pages * PAGE_SIZE + LANE
    k_sel = min(TOPK_MAX, (n_pages * PAGE_SIZE + t_new) // 4)
    kern = functools.partial(_dsa_sample_kernel, n_pages=n_pages, t_new=t_new, k_sel=k_sel, hd=hd)

    def page_spec(width, t):
        return pl.BlockSpec((None, PAGE_SIZE, width), lambda b, pt: (pt[b * n_pages + t], 0, 0))

    in_specs = [pl.BlockSpec((None, A_KV_HEADS, rows_q, hd), lambda b, pt: (b, 0, 0, 0)),
                pl.BlockSpec((None, IDX_HEADS * SUBLANE, IDX_DIM), lambda b, pt: (b, 0, 0)),
                pl.BlockSpec((None, IDX_HEADS * SUBLANE, 1), lambda b, pt: (b, 0, 0)),
                pl.BlockSpec((None, SUBLANE, kvw), lambda b, pt: (b, 0, 0)),
                pl.BlockSpec((None, SUBLANE, kvw), lambda b, pt: (b, 0, 0)),
                pl.BlockSpec((None, SUBLANE, IDX_DIM), lambda b, pt: (b, 0, 0)),
                pl.BlockSpec((1, hd), lambda b, pt: (0, 0)),
                pl.BlockSpec((1, hd), lambda b, pt: (0, 0)),
                pl.BlockSpec((A_KV_HEADS, rows_q, nk), lambda b, pt: (0, 0, 0))]
    in_specs += [page_spec(kvw, t) for t in range(n_pages)]
    in_specs += [page_spec(kvw, t) for t in range(n_pages)]
    in_specs += [page_spec(IDX_DIM, t) for t in range(n_pages)]
    grid_spec = pltpu.PrefetchScalarGridSpec(
        num_scalar_prefetch=1,
        grid=(n_seq,),
        in_specs=in_specs,
        out_specs=[pl.BlockSpec((None, A_KV_HEADS, rows_q, hd), lambda b, pt: (b, 0, 0, 0)),
                   pl.BlockSpec((None, SUBLANE, kvw), lambda b, pt: (b, 0, 0))])
    return pl.pallas_call(
        kern,
        grid_spec=grid_spec,
        out_shape=[jax.ShapeDtypeStruct((n_seq, A_KV_HEADS, rows_q, hd), F32),
                   jax.ShapeDtypeStruct((n_seq, SUBLANE, kvw), F32)],
        compiler_params=_params(("arbitrary",), VMEM_LIMIT),
        name="dsa_sample",
    )(page_table.reshape(-1), qa, qi, wi, knew, vnew, kinew, qn_g.reshape(1, hd), kn_g.reshape(1, hd), bias_s,
      *([cache_k] * n_pages), *([cache_v] * n_pages), *([cache_i] * n_pages))


def _outproj_kernel(x_ref, mo_ref, ao_ref, gt_ref, wm_ref, wa_ref, o_ref):
    y = _dot(mo_ref[...].astype(BF16), wm_ref[...]) + _dot(ao_ref[...].astype(BF16), wa_ref[...])
    o_ref[...] = x_ref[...] + gt_ref[...] * y


def _outproj(x, mo, ao, gt, w_out):
    m, d = x.shape
    mw, aw = mo.shape[1], ao.shape[1]
    tm = _row_tile(m)
    ms = _mod_spec(gt, m, tm, d)
    row = lambda w: pl.BlockSpec((tm, w), lambda i, j: (i, 0))
    return pl.pallas_call(
        _outproj_kernel,
        grid=(m // tm, 1),
        in_specs=[row(d), row(mw), row(aw), ms,
                  pl.BlockSpec((mw, d), lambda i, j: (0, 0)),
                  pl.BlockSpec((aw, d), lambda i, j: (0, 0))],
        out_specs=row(d),
        out_shape=jax.ShapeDtypeStruct((m, d), F32),
        compiler_params=_params(("parallel", "arbitrary"), VMEM_LIMIT),
        name="outproj",
    )(x, mo, ao, gt, w_out[:mw], w_out[mw:])


def _pad_tokens(a, t_pad):
    pad = [(0, 0)] * a.ndim
    pad[1] = (0, t_pad - a.shape[1])
    return jnp.pad(a, pad)


def _layer(x3, mods, lw, bias_p, bias_s, rec_state, past):
    n_seq, T, d = x3.shape
    x = x3.reshape(n_seq * T, d)
    dk, dv = rec_state[0].shape[2], rec_state[0].shape[3]
    mw = M_HEADS * dv
    hd = lw["q_norm_g"].shape[0]
    aw, kvw = A_HEADS * hd, A_KV_HEADS * hd
    iw = IDX_HEADS * IDX_DIM
    per_seq = T % ROW_TILE == 0
    if per_seq:
        mod = lambda k: mods[:, k:k + 1, :]
    else:
        mod = lambda k: jnp.repeat(mods[:, k, :], T, axis=0)
    sh1, sc1, g1, sh2, sc2, g2, sh3, sc3, g3 = [mod(k) for k in range(N_MOD)]

    x = _ffn(x, sh1, sc1, g1, lw["ffn1_norm_g"], lw["ffn1_w_gate"], lw["ffn1_w_up"], lw["ffn1_w_down"])
    p = _proj(x, sh2, sc2, lw["mix_norm_g"], lw["w_in"])
    npk = p.shape[1]
    qkw = 2 * M_HEADS * dk
    c0, n0, m0, conv_buf = rec_state
    gate_row = jnp.zeros((1, LANE), F32).at[0, TAIL_IG:TAIL_IG + 2 * M_HEADS].set(lw["mlstm_gate_b"])
    m0_pad = jnp.zeros((n_seq, 1, LANE), F32).at[:, 0, :M_HEADS].set(m0)
    prev8 = jnp.zeros((n_seq, SUBLANE, qkw), F32).at[:, SUBLANE - (CONV_W - 1):, :].set(conv_buf)
    p3 = p.reshape(n_seq, T, npk)
    a_off = qkw + 2 * mw
    v_a = p3[:, :, a_off + aw + kvw:a_off + aw + 2 * kvw]
    ki = p3[:, :, npk - LANE + TAIL_KI:npk - LANE + TAIL_KI + IDX_DIM]
    conv_new = p3[:, T - (CONV_W - 1):, :qkw]

    if past is None:
        L = M_CHUNK if T % M_CHUNK == 0 else T
        mo, c_new, n_new, m_new = _mlstm(p, prev8, lw["mlstm_conv_w"], gate_row, lw["mlstm_out_g"], c0, n0, m0_pad,
                                         n_seq=n_seq, L=L, valid=L, out_dtype=BF16)
        ao, k_n = _dsa_prompt(p, lw["q_norm_g"], lw["k_norm_g"], bias_p, n_seq=n_seq, T=T, hd=hd)
        k_n = k_n.reshape(n_seq, T, A_KV_HEADS, hd)
    else:
        tp = SUBLANE
        assert T <= tp
        pp = _pad_tokens(p3, tp).reshape(n_seq * tp, npk)
        mo, c_new, n_new, m_new = _mlstm(pp, prev8, lw["mlstm_conv_w"], gate_row, lw["mlstm_out_g"], c0, n0, m0_pad,
                                         n_seq=n_seq, L=tp, valid=T, out_dtype=F32)
        mo = mo.reshape(n_seq, tp, mw)[:, :T].reshape(n_seq * T, mw)
        cache_k, cache_v, cache_i, page_table = past
        rep = A_HEADS // A_KV_HEADS
        qa = _pad_tokens(p3[:, :, a_off:a_off + aw], tp).reshape(n_seq, tp, A_KV_HEADS, rep, hd)
        qa = qa.transpose(0, 2, 3, 1, 4).reshape(n_seq, A_KV_HEADS, rep * tp, hd)
        qi = _pad_tokens(p3[:, :, a_off + aw + 2 * kvw:a_off + aw + 2 * kvw + iw], tp)
        qi = qi.reshape(n_seq, tp, IDX_HEADS, IDX_DIM).transpose(0, 2, 1, 3).reshape(n_seq, IDX_HEADS * tp, IDX_DIM)
        wi = _pad_tokens(p3[:, :, npk - LANE + TAIL_WI:npk - LANE + TAIL_WI + IDX_HEADS], tp)
        wi = wi.transpose(0, 2, 1).reshape(n_seq, IDX_HEADS * tp, 1)
        knew = _pad_tokens(p3[:, :, a_off + aw:a_off + aw + kvw], tp)
        n_pool = cache_k.shape[0]
        ao4, k_n8 = _dsa_sample(qa, qi, wi, knew, _pad_tokens(v_a, tp), _pad_tokens(ki, tp),
                                lw["q_norm_g"], lw["k_norm_g"], bias_s,
                                cache_k.reshape(n_pool, PAGE_SIZE, kvw), cache_v.reshape(n_pool, PAGE_SIZE, kvw),
                                cache_i, page_table, t_new=T, hd=hd)
        ao = ao4.reshape(n_seq, A_KV_HEADS, rep, tp, hd)[:, :, :, :T].transpose(0, 3, 1, 2, 4).reshape(n_seq * T, aw)
        k_n = k_n8[:, :T].reshape(n_seq, T, A_KV_HEADS, hd)

    x = _outproj(x, mo, ao, g2, lw["w_out"])
    x = _ffn(x, sh3, sc3, g3, lw["ffn2_norm_g"], lw["ffn2_w_gate"], lw["ffn2_w_up"], lw["ffn2_w_down"])
    state = (k_n, v_a.reshape(n_seq, T, A_KV_HEADS, hd), ki, c_new, n_new, m_new[:, 0, :M_HEADS], conv_new)
    return x.reshape(n_seq, T, d), state


def kernel(x_prompt, x_sample, c_prompt, c_sample, cache_k, cache_v, cache_idx_k, page_table, state_C, state_n,
           state_m, state_conv, ffn1_norm_g, ffn1_w_gate, ffn1_w_up, ffn1_w_down, mix_norm_g, w_in, mlstm_conv_w,
           mlstm_gate_b, mlstm_out_g, q_norm_g, k_norm_g, t5_bias, w_out, ffn2_norm_g, ffn2_w_gate, ffn2_w_up,
           ffn2_w_down, w_ada, b_ada):
    depth = w_in.shape[0]
    bp, tp_len, d = x_prompt.shape
    bs, ts_len, _ = x_sample.shape
    dk, dv = state_C.shape[3], state_C.shape[4]
    mw = M_HEADS * dv
    hd = q_norm_g.shape[1]
    past_len = page_table.shape[1] * PAGE_SIZE
    bias_p, bias_s = _bias_tables(t5_bias, tp_len, past_len, ts_len)
    rep = A_HEADS // A_KV_HEADS
    bias_s = bias_s.reshape(A_KV_HEADS, rep * SUBLANE, past_len + LANE)

    xp, xs = x_prompt, x_sample
    st_p, st_s = [], []
    for l in range(depth):
        lw = dict(ffn1_norm_g=ffn1_norm_g[l], ffn1_w_gate=ffn1_w_gate[l].astype(BF16),
                  ffn1_w_up=ffn1_w_up[l].astype(BF16), ffn1_w_down=ffn1_w_down[l].astype(BF16),
                  mix_norm_g=mix_norm_g[l], w_in=_pack_w_in(w_in[l], mw, A_HEADS * hd, A_KV_HEADS * hd),
                  mlstm_conv_w=mlstm_conv_w[l], mlstm_gate_b=mlstm_gate_b[l], mlstm_out_g=mlstm_out_g[l],
                  q_norm_g=q_norm_g[l], k_norm_g=k_norm_g[l], w_out=w_out[l].astype(BF16),
                  ffn2_norm_g=ffn2_norm_g[l], ffn2_w_gate=ffn2_w_gate[l].astype(BF16),
                  ffn2_w_up=ffn2_w_up[l].astype(BF16), ffn2_w_down=ffn2_w_down[l].astype(BF16))
        mods = _ada(jnp.concatenate([c_prompt, c_sample], axis=0), w_ada[l], b_ada[l])
        mods = mods.reshape(bp + bs, N_MOD, d)
        init_p = (jnp.zeros((bp, M_HEADS, dk, dv), F32), jnp.zeros((bp, M_HEADS, dk), F32),
                  jnp.zeros((bp, M_HEADS), F32), jnp.zeros((bp, CONV_W - 1, 2 * M_HEADS * dk), F32))
        xp, sp = _layer(xp, mods[:bp], lw, bias_p, bias_s, init_p, None)
        init_s = (state_C[l], state_n[l], state_m[l], state_conv[l])
        xs, ss = _layer(xs, mods[bp:], lw, bias_p, bias_s, init_s,
                        (cache_k[l], cache_v[l], cache_idx_k[l], page_table))
        st_p.append(sp)
        st_s.append(ss)
    stack = lambda sts, k: jnp.stack([s[k] for s in sts])
    return (xp, xs) + tuple(stack(st_p, k) for k in range(7)) + tuple(stack(st_s, k) for k in range(7))
```

```python
import functools
import math

import numpy as np
import jax
import jax.numpy as jnp
from jax import lax
from jax.experimental import pallas as pl
from jax.experimental.pallas import tpu as pltpu

F32 = jnp.float32
BF16 = jnp.bfloat16

M_HEADS = 4
CONV_W = 4
M_CHUNK = 64
A_HEADS = 8
A_KV_HEADS = 2
IDX_HEADS = 8
IDX_DIM = 64
TOPK_MAX = 256
Q_BLOCK = 128
PAGE_SIZE = 128
N_BUCKETS = 32
MAX_DISTANCE = 128
N_MOD = 9
EPS = 1e-6
NEG = -1e30

LANE = 128
SUBLANE = 8
ROW_TILE = 512
VMEM_LIMIT = 56 * 1024 * 1024

TAIL_KI = 0
TAIL_WI = IDX_DIM
TAIL_IG = IDX_DIM + IDX_HEADS
TAIL_FG = TAIL_IG + M_HEADS

INT_MIN = -2 ** 31


def _params(sem, vmem=None):
    return pltpu.CompilerParams(dimension_semantics=sem, vmem_limit_bytes=vmem)


def _dot(a, b):
    return jnp.dot(a, b, preferred_element_type=F32)


def _dot_nt(a, b):
    return lax.dot_general(a, b, (((1,), (1,)), ((), ())), preferred_element_type=F32)


def _dot_tn(a, b):
    return lax.dot_general(a, b, (((0,), (0,)), ((), ())), preferred_element_type=F32)


def _rms(x, g):
    return x * lax.rsqrt(jnp.mean(x * x, axis=-1, keepdims=True) + EPS) * g


def _row_tile(n):
    return ROW_TILE if n % ROW_TILE == 0 else n


def _ada_kernel(c_ref, w_ref, b_ref, o_ref):
    o_ref[...] = _dot(c_ref[...].astype(BF16), w_ref[...].astype(BF16)) + b_ref[...]


def _ada(c_all, w_ada, b_ada):
    r, d = c_all.shape
    n = w_ada.shape[1]
    tn = 1024 if n % 1024 == 0 else n
    return pl.pallas_call(
        _ada_kernel,
        grid=(n // tn,),
        in_specs=[pl.BlockSpec((r, d), lambda j: (0, 0)),
                  pl.BlockSpec((d, tn), lambda j: (0, j)),
                  pl.BlockSpec((1, tn), lambda j: (0, j))],
        out_specs=pl.BlockSpec((r, tn), lambda j: (0, j)),
        out_shape=jax.ShapeDtypeStruct((r, n), F32),
        compiler_params=_params(("arbitrary",), VMEM_LIMIT),
        name="ada",
    )(c_all, w_ada, b_ada.reshape(1, n))


def _ffn_kernel(x_ref, sh_ref, sc_ref, gt_ref, g_ref, wg_ref, wu_ref, wd_ref, o_ref, h_ref, acc_ref):
    j = pl.program_id(1)

    @pl.when(j == 0)
    def _():
        h_ref[...] = (_rms(x_ref[...], g_ref[...]) * (1.0 + sc_ref[...]) + sh_ref[...]).astype(BF16)
        acc_ref[...] = jnp.zeros_like(acc_ref)

    h = h_ref[...]
    a = _dot(h, wg_ref[...])
    u = _dot(h, wu_ref[...])
    act = (a * jax.nn.sigmoid(a) * u).astype(BF16)
    acc_ref[...] += _dot(act, wd_ref[...])

    @pl.when(j == pl.num_programs(1) - 1)
    def _():
        o_ref[...] = x_ref[...] + 0.5 * gt_ref[...] * acc_ref[...]


def _mod_spec(mod, m, tm, d):
    if mod.ndim == 2:
        return pl.BlockSpec((tm, d), lambda i, j: (i, 0))
    tiles_per_seq = (m // mod.shape[0]) // tm
    return pl.BlockSpec((None, 1, d), lambda i, j: (i // tiles_per_seq, 0, 0))


def _ffn(x, sh, sc, gt, g, wg, wu, wd):
    m, d = x.shape
    f = wg.shape[1]
    tm = _row_tile(m)
    tf = 512 if f % 512 == 0 else f
    ms = _mod_spec(sh, m, tm, d)
    return pl.pallas_call(
        _ffn_kernel,
        grid=(m // tm, f // tf),
        in_specs=[pl.BlockSpec((tm, d), lambda i, j: (i, 0)), ms, ms, ms,
                  pl.BlockSpec((1, d), lambda i, j: (0, 0)),
                  pl.BlockSpec((d, tf), lambda i, j: (0, j)),
                  pl.BlockSpec((d, tf), lambda i, j: (0, j)),
                  pl.BlockSpec((tf, d), lambda i, j: (j, 0))],
        out_specs=pl.BlockSpec((tm, d), lambda i, j: (i, 0)),
        out_shape=jax.ShapeDtypeStruct((m, d), F32),
        scratch_shapes=[pltpu.VMEM((tm, d), BF16), pltpu.VMEM((tm, d), F32)],
        compiler_params=_params(("parallel", "arbitrary"), VMEM_LIMIT),
        name="ffn",
    )(x, sh, sc, gt, g.reshape(1, d), wg, wu, wd)


def _proj_kernel(x_ref, sh_ref, sc_ref, g_ref, w_ref, o_ref, h_ref):
    @pl.when(pl.program_id(1) == 0)
    def _():
        h_ref[...] = (_rms(x_ref[...], g_ref[...]) * (1.0 + sc_ref[...]) + sh_ref[...]).astype(BF16)

    o_ref[...] = _dot(h_ref[...], w_ref[...])


def _proj(x, sh, sc, g, w):
    m, d = x.shape
    n = w.shape[1]
    tm = _row_tile(m)
    tn = 7 * LANE
    assert n % tn == 0
    ms = _mod_spec(sh, m, tm, d)
    return pl.pallas_call(
        _proj_kernel,
        grid=(m // tm, n // tn),
        in_specs=[pl.BlockSpec((tm, d), lambda i, j: (i, 0)), ms, ms,
                  pl.BlockSpec((1, d), lambda i, j: (0, 0)),
                  pl.BlockSpec((d, tn), lambda i, j: (0, j))],
        out_specs=pl.BlockSpec((tm, tn), lambda i, j: (i, j)),
        out_shape=jax.ShapeDtypeStruct((m, n), F32),
        scratch_shapes=[pltpu.VMEM((tm, d), BF16)],
        compiler_params=_params(("parallel", "arbitrary"), VMEM_LIMIT),
        name="proj",
    )(x, sh, sc, g.reshape(1, d), w)


def _pack_w_in(w_in, mw, aw, kvw):
    d = w_in.shape[0]
    o = np.cumsum([0, mw, mw, mw, mw, M_HEADS, M_HEADS, aw, kvw, kvw, IDX_HEADS * IDX_DIM, IDX_DIM, IDX_HEADS])
    seg = lambda k: np.arange(o[k], o[k + 1])
    cols = np.concatenate([seg(0), seg(1), seg(2), seg(3), seg(6), seg(7), seg(8), seg(9),
                           seg(10), seg(11), seg(4), seg(5)])
    packed = jnp.take(w_in, jnp.asarray(cols, jnp.int32), axis=1).astype(BF16)
    pad = (-packed.shape[1]) % LANE
    return jnp.concatenate([packed, jnp.zeros((d, pad), BF16)], axis=1)


def _log_sigmoid(x):
    return jnp.minimum(x, 0.0) - jnp.log1p(jnp.exp(-jnp.abs(x)))


def _mlstm_kernel(qk_ref, v_ref, o_ref, tail_ref, prev_ref, cw_ref, gb_ref, og_ref, c0_ref, n0_ref, m0_ref,
                  mo_ref, cst_ref, nst_ref, mst_ref, xbuf_ref, *, L, valid, dk, dv):
    width = M_HEADS * dk

    @pl.when(pl.program_id(1) == 0)
    def _():
        xbuf_ref[0:SUBLANE, :] = prev_ref[...]
        cst_ref[...] = c0_ref[...]
        nst_ref[...] = n0_ref[...]
        mst_ref[...] = m0_ref[...]

    xbuf_ref[SUBLANE:SUBLANE + L, :] = qk_ref[...]
    base = SUBLANE - (CONV_W - 1)
    y = xbuf_ref[base:base + L, :] * cw_ref[0:1, :]
    for j in range(1, CONV_W):
        y = y + xbuf_ref[base + j:base + j + L, :] * cw_ref[j:j + 1, :]
    tail_rows = xbuf_ref[L:L + SUBLANE, :]
    xbuf_ref[0:SUBLANE, :] = tail_rows
    qk = y * jax.nn.sigmoid(y)

    gates = tail_ref[...] + gb_ref[...]
    row = lax.broadcasted_iota(jnp.int32, (L, LANE), 0)
    lf = _log_sigmoid(gates)
    ig = gates
    if valid < L:
        lf = jnp.where(row < valid, lf, 0.0)
        ig = jnp.where(row < valid, ig, -jnp.inf)
    bcum = lf
    s = 1
    while s < L:
        bcum = bcum + jnp.where(row >= s, pltpu.roll(bcum, s, axis=0), 0.0)
        s *= 2
    bmi_t = (pltpu.roll(ig, TAIL_FG - TAIL_IG, axis=1) - bcum).T

    ri = lax.broadcasted_iota(jnp.int32, (L, L), 0)
    ci = lax.broadcasted_iota(jnp.int32, (L, L), 1)
    causal = ri >= ci

    for h in range(M_HEADS):
        li, lfh = TAIL_IG + h, TAIL_FG + h
        a_col = bcum[:, lfh:lfh + 1]
        ig_col = ig[:, li:li + 1]
        b_row = bmi_t[lfh:lfh + 1, :]
        m0 = mst_ref[:, h:h + 1]
        d_log = jnp.where(causal, a_col + b_row, -jnp.inf)
        s_log = a_col + m0
        m_col = jnp.maximum(s_log, jnp.max(d_log, axis=-1, keepdims=True))
        dw = jnp.exp(d_log - m_col)
        sw = jnp.exp(s_log - m_col)

        q = qk[:, h * dk:(h + 1) * dk]
        k = qk[:, width + h * dk:width + (h + 1) * dk] * (dk ** -0.5)
        v = v_ref[:, h * dv:(h + 1) * dv]
        qb, kb = q.astype(BF16), k.astype(BF16)
        c_prev = cst_ref[h]
        n_prev = nst_ref[h:h + 1, :]
        scores = _dot_nt(qb, kb) * dw
        num = _dot(scores.astype(BF16), v.astype(BF16)) + sw * _dot(qb, c_prev.astype(BF16))
        den = jnp.sum(scores, axis=-1, keepdims=True) + sw * jnp.sum(q * n_prev, axis=-1, keepdims=True)
        hh = num / jnp.maximum(jnp.abs(den), jnp.exp(-m_col))

        m_new = m_col[L - 1:L, :]
        a_last = a_col[L - 1:L, :]
        wl = jnp.exp(a_last - a_col + ig_col - m_new)
        decay = jnp.exp(a_last + m0 - m_new)
        cst_ref[h] = decay * c_prev + _dot_tn(kb, (wl * v).astype(BF16))
        nst_ref[h:h + 1, :] = decay * n_prev + jnp.sum(wl * k, axis=0, keepdims=True)
        mst_ref[:, h:h + 1] = m_new

        hn = hh * lax.rsqrt(jnp.mean(hh * hh, axis=-1, keepdims=True) + EPS) * og_ref[:, h * dv:(h + 1) * dv]
        gate = jax.nn.sigmoid(o_ref[:, h * dv:(h + 1) * dv])
        mo_ref[:, h * dv:(h + 1) * dv] = (hn * gate).astype(mo_ref.dtype)


def _mlstm(p, prev8, conv_w, gate_row, out_g, c0, n0, m0, *, n_seq, L, valid, out_dtype):
    rows = p.shape[0]
    nc = rows // (n_seq * L)
    dk = c0.shape[2]
    dv = c0.shape[3]
    mw = M_HEADS * dv
    qkw = 2 * M_HEADS * dk
    assert qkw % mw == 0 and (2 * qkw) % mw == 0
    rmap = lambda b, c: b * nc + c
    kern = functools.partial(_mlstm_kernel, L=L, valid=valid, dk=dk, dv=dv)
    return pl.pallas_call(
        kern,
        grid=(n_seq, nc),
        in_specs=[pl.BlockSpec((L, qkw), lambda b, c: (rmap(b, c), 0)),
                  pl.BlockSpec((L, mw), lambda b, c: (rmap(b, c), qkw // mw)),
                  pl.BlockSpec((L, mw), lambda b, c: (rmap(b, c), qkw // mw + 1)),
                  pl.BlockSpec((L, LANE), lambda b, c: (rmap(b, c), p.shape[1] // LANE - 1)),
                  pl.BlockSpec((None, SUBLANE, qkw), lambda b, c: (b, 0, 0)),
                  pl.BlockSpec((CONV_W, qkw), lambda b, c: (0, 0)),
                  pl.BlockSpec((1, LANE), lambda b, c: (0, 0)),
                  pl.BlockSpec((1, mw), lambda b, c: (0, 0)),
                  pl.BlockSpec((None, M_HEADS, dk, dv), lambda b, c: (b, 0, 0, 0)),
                  pl.BlockSpec((None, M_HEADS, dk), lambda b, c: (b, 0, 0)),
                  pl.BlockSpec((None, 1, LANE), lambda b, c: (b, 0, 0))],
        out_specs=[pl.BlockSpec((L, mw), lambda b, c: (rmap(b, c), 0)),
                   pl.BlockSpec((None, M_HEADS, dk, dv), lambda b, c: (b, 0, 0, 0)),
                   pl.BlockSpec((None, M_HEADS, dk), lambda b, c: (b, 0, 0)),
                   pl.BlockSpec((None, 1, LANE), lambda b, c: (b, 0, 0))],
        out_shape=[jax.ShapeDtypeStruct((rows, mw), out_dtype),
                   jax.ShapeDtypeStruct(c0.shape, F32),
                   jax.ShapeDtypeStruct(n0.shape, F32),
                   jax.ShapeDtypeStruct(m0.shape, F32)],
        scratch_shapes=[pltpu.VMEM((SUBLANE + L, qkw), F32)],
        compiler_params=_params(("parallel", "arbitrary"), VMEM_LIMIT),
        name="mlstm",
    )(p, p, p, p, prev8, conv_w, gate_row, out_g.reshape(1, mw), c0, n0, m0)


def _bucket_np(dist):
    me = N_BUCKETS // 2
    d = np.maximum(dist, 0)
    ratio = np.log(np.maximum(d, 1).astype(np.float64) / me) / math.log(MAX_DISTANCE / me)
    large = np.minimum(me + (ratio * (N_BUCKETS - me)).astype(np.int64), N_BUCKETS - 1)
    return np.where(d < me, d, large).astype(np.int32)


def _bias_kernel(t5_ref, bkp_ref, bks_ref, far_ref, op_ref, os_ref):
    def lookup(bk, h):
        acc = jnp.zeros(bk.shape, F32)
        for b in range(N_BUCKETS):
            acc = jnp.where(bk == b, t5_ref[b, h], acc)
        return acc

    for h in range(A_HEADS):
        op_ref[h] = lookup(bkp_ref[...], h) - t5_ref[far_ref[0], h]
        os_ref[h] = lookup(bks_ref[...], h)


def _bias_tables(t5_bias, t_prompt, past, t_new):
    r = np.arange(Q_BLOCK)[:, None]
    c = np.arange(2 * Q_BLOCK)[None, :]
    bkp = _bucket_np(Q_BLOCK + r - c)
    far = _bucket_np(np.arange(Q_BLOCK + 1, max(t_prompt, past + t_new) + Q_BLOCK))
    assert (far == far[0]).all(), "bias must be constant beyond one query block"
    nk = past + LANE
    tq = np.arange(SUBLANE)[:, None]
    bks = _bucket_np(past + tq - np.arange(nk)[None, :])
    return pl.pallas_call(
        _bias_kernel,
        in_specs=[pl.BlockSpec(memory_space=pltpu.SMEM),
                  pl.BlockSpec(memory_space=pltpu.VMEM),
                  pl.BlockSpec(memory_space=pltpu.VMEM),
                  pl.BlockSpec(memory_space=pltpu.SMEM)],
        out_specs=[pl.BlockSpec(memory_space=pltpu.VMEM), pl.BlockSpec(memory_space=pltpu.VMEM)],
        out_shape=[jax.ShapeDtypeStruct((A_HEADS, Q_BLOCK, 2 * Q_BLOCK), F32),
                   jax.ShapeDtypeStruct((A_HEADS, SUBLANE, nk), F32)],
        name="t5_bias_tables",
    )(t5_bias, jnp.asarray(bkp), jnp.asarray(bks), jnp.asarray(far[:1]))


def _sort_key(score):
    bits = lax.bitcast_convert_type(score, jnp.int32)
    return jnp.where(bits < 0, bits ^ jnp.int32(0x7FFFFFFF), bits)


def _kth_key(count_ge, rows, k_sel):
    def body(it, ans_u):
        cand_u = ans_u | lax.shift_left(jnp.int32(1), jnp.int32(31) - it)
        cnt = count_ge(cand_u ^ jnp.int32(INT_MIN))
        return jnp.where(cnt >= k_sel, cand_u, ans_u)

    ans_u = lax.fori_loop(0, 32, body, jnp.zeros((rows, 1), jnp.int32))
    return ans_u ^ jnp.int32(INT_MIN)


def _strict_upper():
    r = lax.broadcasted_iota(jnp.int32, (LANE, LANE), 0)
    c = lax.broadcasted_iota(jnp.int32, (LANE, LANE), 1)
    return jnp.where(r < c, 1.0, 0.0).astype(BF16)


def _select_tiles(key_tiles, thr, k_sel):
    rows = thr.shape[0]
    c_gt = jnp.zeros((rows, 1), F32)
    for kt in key_tiles:
        c_gt = c_gt + jnp.sum(jnp.where(kt > thr, 1.0, 0.0), axis=-1, keepdims=True)
    need = k_sel - c_gt
    upper = _strict_upper()
    run = jnp.zeros((rows, 1), F32)
    out = []
    for kt in key_tiles:
        eq = jnp.where(kt == thr, 1.0, 0.0)
        before = _dot(eq.astype(BF16), upper) + run
        out.append(jnp.where(kt > thr, 1.0, jnp.where(before < need, eq, 0.0)))
        run = run + jnp.sum(eq, axis=-1, keepdims=True)
    return out


FAR_STEP = 4


def _dsa_prompt_kernel(qa_ref, qi_ref, qtail_ref, k_ref, v_ref, ktail_ref, qg_ref, kg_ref, bias_ref,
                       ao_ref, kn_ref,
                       kb_ref, vb_ref, kib_ref, qn_ref, key_ref, negb_ref, *, T, k_sel, hd):
    i = pl.program_id(1)
    nb = T // Q_BLOCK
    rep = A_HEADS // A_KV_HEADS
    scale = hd ** -0.5
    near_w = 2 * LANE

    @pl.when(i == 0)
    def _():
        kb_ref[0:LANE, :] = jnp.zeros((LANE, kb_ref.shape[1]), BF16)
        vb_ref[0:LANE, :] = jnp.zeros((LANE, vb_ref.shape[1]), BF16)
        kib_ref[0:LANE, :] = jnp.zeros((LANE, IDX_DIM), BF16)
        for g in range(A_KV_HEADS):
            kn = _rms(k_ref[:, g * hd:(g + 1) * hd], kg_ref[...])
            kn_ref[:, g * hd:(g + 1) * hd] = kn
            kb_ref[LANE:, g * hd:(g + 1) * hd] = kn.astype(BF16)
        vb_ref[LANE:, :] = v_ref[...].astype(BF16)
        kib_ref[LANE:, :] = ktail_ref[:, TAIL_KI:TAIL_KI + IDX_DIM].astype(BF16)

    wi = qtail_ref[:, TAIL_WI:TAIL_WI + IDX_HEADS] * (IDX_HEADS ** -0.5) * (IDX_DIM ** -0.5)
    qi_b = [qi_ref[:, j * IDX_DIM:(j + 1) * IDX_DIM].astype(BF16) for j in range(IDX_HEADS)]
    for h in range(A_HEADS):
        qn_ref[h] = _rms(qa_ref[:, h * hd:(h + 1) * hd], qg_ref[...]).astype(BF16)

    near0 = pl.multiple_of(i * LANE, LANE)
    ki_near = kib_ref[pl.ds(near0, near_w), :]
    k_near = kb_ref[pl.ds(near0, near_w), :]
    v_near = vb_ref[pl.ds(near0, near_w), :]
    nr = lax.broadcasted_iota(jnp.int32, (Q_BLOCK, near_w), 0)
    nc = lax.broadcasted_iota(jnp.int32, (Q_BLOCK, near_w), 1)
    near_ok = jnp.where(nc <= nr + LANE, 1.0, 0.0) * jnp.maximum(jnp.where(nc >= LANE, 1.0, 0.0),
                                                                  (i >= 1).astype(F32))

    def scores(ki):
        sc = jnp.zeros((Q_BLOCK, ki.shape[0]), F32)
        for j in range(IDX_HEADS):
            sc = sc + wi[:, j:j + 1] * jnp.maximum(_dot_nt(qi_b[j], ki), 0.0)
        return sc

    def body(wf):
        fw = wf * LANE
        w_all = fw + near_w
        key_ref[:, fw:w_all] = _sort_key(jnp.where(near_ok > 0.0, scores(ki_near) + 0.0, NEG))
        if wf:
            far_ok = lax.broadcasted_iota(jnp.int32, (Q_BLOCK, fw), 1) < (i - 1) * LANE
            key_ref[:, 0:fw] = _sort_key(jnp.where(far_ok, scores(kib_ref[LANE:LANE + fw, :]) + 0.0, NEG))

        def count_ge(cand):
            return jnp.sum(jnp.where(key_ref[:, 0:w_all] >= cand, 1.0, 0.0), axis=-1, keepdims=True)

        thr = _kth_key(count_ge, Q_BLOCK, float(k_sel))
        sel = _select_tiles([key_ref[:, t * LANE:(t + 1) * LANE] for t in range(wf + 2)], thr, float(k_sel))
        for t in range(wf):
            ok = sel[t] * (t < i - 1).astype(F32)
            negb_ref[:, t * LANE:(t + 1) * LANE] = jnp.where(ok > 0.0, 0.0, NEG)
        for t in range(2):
            ok = sel[wf + t] * near_ok[:, t * LANE:(t + 1) * LANE]
            negb_ref[:, fw + t * LANE:fw + (t + 1) * LANE] = jnp.where(ok > 0.0, 0.0, NEG)

        for h in range(A_HEADS):
            cs = slice((h // rep) * hd, (h // rep + 1) * hd)
            qn = qn_ref[h]
            s_near = _dot_nt(qn, k_near[:, cs]) * scale + bias_ref[h] + negb_ref[:, fw:w_all]
            m = jnp.max(s_near, axis=-1, keepdims=True)
            if wf:
                s_far = _dot_nt(qn, kb_ref[LANE:LANE + fw, cs]) * scale + negb_ref[:, 0:fw]
                m = jnp.maximum(m, jnp.max(s_far, axis=-1, keepdims=True))
            p_near = jnp.exp(s_near - m)
            l = jnp.sum(p_near, axis=-1, keepdims=True)
            acc = _dot(p_near.astype(BF16), v_near[:, cs])
            if wf:
                p_far = jnp.exp(s_far - m)
                l = l + jnp.sum(p_far, axis=-1, keepdims=True)
                acc = acc + _dot(p_far.astype(BF16), vb_ref[LANE:LANE + fw, cs])
            ao_ref[:, h * hd:(h + 1) * hd] = (acc / l).astype(ao_ref.dtype)

    variants = sorted({min(FAR_STEP * -(-x // FAR_STEP), nb) for x in range(max(nb - 1, 1))})
    wf_needed = jnp.minimum((jnp.maximum(i - 1, 0) + FAR_STEP - 1) // FAR_STEP * FAR_STEP, nb)
    for wf in variants:
        pl.when(wf_needed == wf)(functools.partial(body, wf))


def _dsa_prompt(p, qn_g, kn_g, bias_p, *, n_seq, T, hd):
    rows = p.shape[0]
    nb = T // Q_BLOCK
    aw = A_HEADS * hd
    kvw = A_KV_HEADS * hd
    iw = IDX_HEADS * IDX_DIM
    k_sel = min(TOPK_MAX, T // 4)
    mcols = p.shape[1] - LANE - iw - 2 * kvw - aw
    assert mcols % aw == 0 and (mcols + aw) % kvw == 0 and (mcols + aw + 2 * kvw) % iw == 0
    qa_blk = mcols // aw
    k_blk = (mcols + aw) // kvw
    qi_blk = (mcols + aw + 2 * kvw) // iw
    tail_blk = p.shape[1] // LANE - 1
    kern = functools.partial(_dsa_prompt_kernel, T=T, k_sel=k_sel, hd=hd)
    return pl.pallas_call(
        kern,
        grid=(n_seq, nb),
        in_specs=[pl.BlockSpec((Q_BLOCK, aw), lambda b, i: (b * nb + i, qa_blk)),
                  pl.BlockSpec((Q_BLOCK, iw), lambda b, i: (b * nb + i, qi_blk)),
                  pl.BlockSpec((Q_BLOCK, LANE), lambda b, i: (b * nb + i, tail_blk)),
                  pl.BlockSpec((T, kvw), lambda b, i: (b, k_blk)),
                  pl.BlockSpec((T, kvw), lambda b, i: (b, k_blk + 1)),
                  pl.BlockSpec((T, LANE), lambda b, i: (b, tail_blk)),
                  pl.BlockSpec((1, hd), lambda b, i: (0, 0)),
                  pl.BlockSpec((1, hd), lambda b, i: (0, 0)),
                  pl.BlockSpec((A_HEADS, Q_BLOCK, 2 * Q_BLOCK), lambda b, i: (0, 0, 0))],
        out_specs=[pl.BlockSpec((Q_BLOCK, aw), lambda b, i: (b * nb + i, 0)),
                   pl.BlockSpec((T, kvw), lambda b, i: (b, 0))],
        out_shape=[jax.ShapeDtypeStruct((rows, aw), BF16),
                   jax.ShapeDtypeStruct((rows, kvw), F32)],
        scratch_shapes=[pltpu.VMEM((T + LANE, kvw), BF16), pltpu.VMEM((T + LANE, kvw), BF16),
                        pltpu.VMEM((T + LANE, IDX_DIM), BF16),
                        pltpu.VMEM((A_HEADS, Q_BLOCK, hd), BF16),
                        pltpu.VMEM((Q_BLOCK, T + 2 * LANE), jnp.int32),
                        pltpu.VMEM((Q_BLOCK, T + 2 * LANE), F32)],
        compiler_params=_params(("parallel", "arbitrary"), VMEM_LIMIT),
        name="dsa_prompt",
    )(p, p, p, p, p, p, qn_g.reshape(1, hd), kn_g.reshape(1, hd), bias_p)


def _dsa_sample_kernel(pt_ref, qa_ref, qi_ref, wi_ref, knew_ref, vnew_ref, kinew_ref, qg_ref, kg_ref, bias_ref,
                       *rest, n_pages, t_new, k_sel, hd):
    kp = rest[0:n_pages]
    vp = rest[n_pages:2 * n_pages]
    ip = rest[2 * n_pages:3 * n_pages]
    ao_ref, kn_ref = rest[3 * n_pages:]
    del pt_ref
    past = n_pages * PAGE_SIZE
    nk = past + LANE
    nt = n_pages + 1
    rep = A_HEADS // A_KV_HEADS
    rows_q = rep * SUBLANE
    scale = hd ** -0.5

    def pad_rows(x):
        return jnp.concatenate([x, jnp.zeros((LANE - x.shape[0], x.shape[1]), x.dtype)], axis=0)

    kn_new = jnp.concatenate([_rms(knew_ref[:, g * hd:(g + 1) * hd], kg_ref[...]) for g in range(A_KV_HEADS)], axis=1)
    kn_ref[...] = kn_new
    k_new_t = pad_rows(kn_new).astype(BF16)
    v_new_t = pad_rows(vnew_ref[...]).astype(BF16)
    i_tiles = [ip[t][...].astype(BF16) for t in range(n_pages)] + [pad_rows(kinew_ref[...]).astype(BF16)]

    qi = qi_ref[...].astype(BF16)
    wi = wi_ref[...] * (IDX_HEADS ** -0.5) * (IDX_DIM ** -0.5)
    tq = lax.broadcasted_iota(jnp.int32, (SUBLANE, LANE), 0)
    lane = lax.broadcasted_iota(jnp.int32, (SUBLANE, LANE), 1)
    key_tiles = []
    for t in range(nt):
        r = _dot_nt(qi, i_tiles[t])
        sc = jnp.zeros((SUBLANE, LANE), F32)
        for j in range(IDX_HEADS):
            sc = sc + wi[j * SUBLANE:(j + 1) * SUBLANE, :] * jnp.maximum(r[j * SUBLANE:(j + 1) * SUBLANE, :], 0.0)
        if t == n_pages:
            sc = jnp.where(lane <= tq, sc + 0.0, jnp.where(lane < t_new, NEG, -jnp.inf))
        else:
            sc = sc + 0.0
        key_tiles.append(_sort_key(sc))

    def count_ge(cand):
        cnt = jnp.zeros((SUBLANE, 1), F32)
        for kt in key_tiles:
            cnt = cnt + jnp.sum(jnp.where(kt >= cand, 1.0, 0.0), axis=-1, keepdims=True)
        return cnt

    thr = _kth_key(count_ge, SUBLANE, float(k_sel))
    sel = _select_tiles(key_tiles, thr, float(k_sel))
    sel[n_pages] = jnp.where(lane <= tq, sel[n_pages], 0.0)

    for g in range(A_KV_HEADS):
        qn = _rms(qa_ref[g], qg_ref[...]).astype(BF16)
        logits = []
        for t in range(nt):
            kt = kp[t][:, g, :].astype(BF16) if t < n_pages else k_new_t[:, g * hd:(g + 1) * hd]
            s = _dot_nt(qn, kt) * scale + bias_ref[g, :, t * LANE:(t + 1) * LANE]
            allowed = jnp.concatenate([sel[t]] * rep, axis=0) > 0.0
            logits.append(jnp.where(allowed, s, NEG))
        m = logits[0].max(axis=-1, keepdims=True)
        for t in range(1, nt):
            m = jnp.maximum(m, logits[t].max(axis=-1, keepdims=True))
        l = jnp.zeros((rows_q, 1), F32)
        acc = jnp.zeros((rows_q, hd), F32)
        for t in range(nt):
            pr = jnp.exp(logits[t] - m)
            l = l + jnp.sum(pr, axis=-1, keepdims=True)
            vt = vp[t][:, g, :].astype(BF16) if t < n_pages else v_new_t[:, g * hd:(g + 1) * hd]
            acc = acc + _dot(pr.astype(BF16), vt)
        ao_ref[g] = acc / l


def _dsa_sample(qa, qi, wi, knew, vnew, kinew, qn_g, kn_g, bias_s, cache_k, cache_v, cache_i, page_table, *, t_new, hd):
    n_seq, n_pages = page_table.shape
    kvw = A_KV_HEADS * hd
    rep = A_HEADS // A_KV_HEADS
    rows_q = rep * SUBLANE
    nk = n_pages * PAGE_SIZE + LANE
    k_sel = min(TOPK_MAX, (n_pages * PAGE_SIZE + t_new) // 4)
    kern = functools.partial(_dsa_sample_kernel, n_pages=n_pages, t_new=t_new, k_sel=k_sel, hd=hd)

    def page_spec(arr, t):
        zeros = (0,) * (arr.ndim - 1)
        return pl.BlockSpec((None,) + arr.shape[1:], lambda b, pt: (pt[b * n_pages + t],) + zeros)

    in_specs = [pl.BlockSpec((None, A_KV_HEADS, rows_q, hd), lambda b, pt: (b, 0, 0, 0)),
                pl.BlockSpec((None, IDX_HEADS * SUBLANE, IDX_DIM), lambda b, pt: (b, 0, 0)),
                pl.BlockSpec((None, IDX_HEADS * SUBLANE, 1), lambda b, pt: (b, 0, 0)),
                pl.BlockSpec((None, SUBLANE, kvw), lambda b, pt: (b, 0, 0)),
                pl.BlockSpec((None, SUBLANE, kvw), lambda b, pt: (b, 0, 0)),
                pl.BlockSpec((None, SUBLANE, IDX_DIM), lambda b, pt: (b, 0, 0)),
                pl.BlockSpec((1, hd), lambda b, pt: (0, 0)),
                pl.BlockSpec((1, hd), lambda b, pt: (0, 0)),
                pl.BlockSpec((A_KV_HEADS, rows_q, nk), lambda b, pt: (0, 0, 0))]
    in_specs += [page_spec(cache_k, t) for t in range(n_pages)]
    in_specs += [page_spec(cache_v, t) for t in range(n_pages)]
    in_specs += [page_spec(cache_i, t) for t in range(n_pages)]
    grid_spec = pltpu.PrefetchScalarGridSpec(
        num_scalar_prefetch=1,
        grid=(n_seq,),
        in_specs=in_specs,
        out_specs=[pl.BlockSpec((None, A_KV_HEADS, rows_q, hd), lambda b, pt: (b, 0, 0, 0)),
                   pl.BlockSpec((None, SUBLANE, kvw), lambda b, pt: (b, 0, 0))])
    return pl.pallas_call(
        kern,
        grid_spec=grid_spec,
        out_shape=[jax.ShapeDtypeStruct((n_seq, A_KV_HEADS, rows_q, hd), F32),
                   jax.ShapeDtypeStruct((n_seq, SUBLANE, kvw), F32)],
        compiler_params=_params(("arbitrary",), VMEM_LIMIT),
        name="dsa_sample",
    )(page_table.reshape(-1), qa, qi, wi, knew, vnew, kinew, qn_g.reshape(1, hd), kn_g.reshape(1, hd), bias_s,
      *([cache_k] * n_pages), *([cache_v] * n_pages), *([cache_i] * n_pages))


def _outproj_kernel(x_ref, mo_ref, ao_ref, gt_ref, wm_ref, wa_ref, o_ref):
    y = _dot(mo_ref[...].astype(BF16), wm_ref[...]) + _dot(ao_ref[...].astype(BF16), wa_ref[...])
    o_ref[...] = x_ref[...] + gt_ref[...] * y


def _outproj(x, mo, ao, gt, w_out):
    m, d = x.shape
    mw, aw = mo.shape[1], ao.shape[1]
    tm = _row_tile(m)
    ms = _mod_spec(gt, m, tm, d)
    row = lambda w: pl.BlockSpec((tm, w), lambda i, j: (i, 0))
    return pl.pallas_call(
        _outproj_kernel,
        grid=(m // tm, 1),
        in_specs=[row(d), row(mw), row(aw), ms,
                  pl.BlockSpec((mw, d), lambda i, j: (0, 0)),
                  pl.BlockSpec((aw, d), lambda i, j: (0, 0))],
        out_specs=row(d),
        out_shape=jax.ShapeDtypeStruct((m, d), F32),
        compiler_params=_params(("parallel", "arbitrary"), VMEM_LIMIT),
        name="outproj",
    )(x, mo, ao, gt, w_out[:mw], w_out[mw:])


def _pad_tokens(a, t_pad):
    pad = [(0, 0)] * a.ndim
    pad[1] = (0, t_pad - a.shape[1])
    return jnp.pad(a, pad)


def _layer(x3, mods, lw, bias_p, bias_s, rec_state, past):
    n_seq, T, d = x3.shape
    x = x3.reshape(n_seq * T, d)
    dk, dv = rec_state[0].shape[2], rec_state[0].shape[3]
    mw = M_HEADS * dv
    hd = lw["q_norm_g"].shape[0]
    aw, kvw = A_HEADS * hd, A_KV_HEADS * hd
    iw = IDX_HEADS * IDX_DIM
    per_seq = T % ROW_TILE == 0
    if per_seq:
        mod = lambda k: mods[:, k:k + 1, :]
    else:
        mod = lambda k: jnp.repeat(mods[:, k, :], T, axis=0)
    sh1, sc1, g1, sh2, sc2, g2, sh3, sc3, g3 = [mod(k) for k in range(N_MOD)]

    x = _ffn(x, sh1, sc1, g1, lw["ffn1_norm_g"], lw["ffn1_w_gate"], lw["ffn1_w_up"], lw["ffn1_w_down"])
    p = _proj(x, sh2, sc2, lw["mix_norm_g"], lw["w_in"])
    npk = p.shape[1]
    qkw = 2 * M_HEADS * dk
    c0, n0, m0, conv_buf = rec_state
    gate_row = jnp.zeros((1, LANE), F32).at[0, TAIL_IG:TAIL_IG + 2 * M_HEADS].set(lw["mlstm_gate_b"])
    m0_pad = jnp.zeros((n_seq, 1, LANE), F32).at[:, 0, :M_HEADS].set(m0)
    prev8 = jnp.zeros((n_seq, SUBLANE, qkw), F32).at[:, SUBLANE - (CONV_W - 1):, :].set(conv_buf)
    p3 = p.reshape(n_seq, T, npk)
    a_off = qkw + 2 * mw
    v_a = p3[:, :, a_off + aw + kvw:a_off + aw + 2 * kvw]
    ki = p3[:, :, npk - LANE + TAIL_KI:npk - LANE + TAIL_KI + IDX_DIM]
    conv_new = p3[:, T - (CONV_W - 1):, :qkw]

    if past is None:
        L = M_CHUNK if T % M_CHUNK == 0 else T
        mo, c_new, n_new, m_new = _mlstm(p, prev8, lw["mlstm_conv_w"], gate_row, lw["mlstm_out_g"], c0, n0, m0_pad,
                                         n_seq=n_seq, L=L, valid=L, out_dtype=BF16)
        ao, k_n = _dsa_prompt(p, lw["q_norm_g"], lw["k_norm_g"], bias_p, n_seq=n_seq, T=T, hd=hd)
        k_n = k_n.reshape(n_seq, T, A_KV_HEADS, hd)
    else:
        tp = SUBLANE
        assert T <= tp
        pp = _pad_tokens(p3, tp).reshape(n_seq * tp, npk)
        mo, c_new, n_new, m_new = _mlstm(pp, prev8, lw["mlstm_conv_w"], gate_row, lw["mlstm_out_g"], c0, n0, m0_pad,
                                         n_seq=n_seq, L=tp, valid=T, out_dtype=F32)
        mo = mo.reshape(n_seq, tp, mw)[:, :T].reshape(n_seq * T, mw)
        cache_k, cache_v, cache_i, page_table = past
        rep = A_HEADS // A_KV_HEADS
        qa = _pad_tokens(p3[:, :, a_off:a_off + aw], tp).reshape(n_seq, tp, A_KV_HEADS, rep, hd)
        qa = qa.transpose(0, 2, 3, 1, 4).reshape(n_seq, A_KV_HEADS, rep * tp, hd)
        qi = _pad_tokens(p3[:, :, a_off + aw + 2 * kvw:a_off + aw + 2 * kvw + iw], tp)
        qi = qi.reshape(n_seq, tp, IDX_HEADS, IDX_DIM).transpose(0, 2, 1, 3).reshape(n_seq, IDX_HEADS * tp, IDX_DIM)
        wi = _pad_tokens(p3[:, :, npk - LANE + TAIL_WI:npk - LANE + TAIL_WI + IDX_HEADS], tp)
        wi = wi.transpose(0, 2, 1).reshape(n_seq, IDX_HEADS * tp, 1)
        knew = _pad_tokens(p3[:, :, a_off + aw:a_off + aw + kvw], tp)
        ao4, k_n8 = _dsa_sample(qa, qi, wi, knew, _pad_tokens(v_a, tp), _pad_tokens(ki, tp),
                                lw["q_norm_g"], lw["k_norm_g"], bias_s,
                                cache_k, cache_v, cache_i, page_table, t_new=T, hd=hd)
        ao = ao4.reshape(n_seq, A_KV_HEADS, rep, tp, hd)[:, :, :, :T].transpose(0, 3, 1, 2, 4).reshape(n_seq * T, aw)
        k_n = k_n8[:, :T].reshape(n_seq, T, A_KV_HEADS, hd)

    x = _outproj(x, mo, ao, g2, lw["w_out"])
    x = _ffn(x, sh3, sc3, g3, lw["ffn2_norm_g"], lw["ffn2_w_gate"], lw["ffn2_w_up"], lw["ffn2_w_down"])
    state = (k_n, v_a.reshape(n_seq, T, A_KV_HEADS, hd), ki, c_new, n_new, m_new[:, 0, :M_HEADS], conv_new)
    return x.reshape(n_seq, T, d), state


def kernel(x_prompt, x_sample, c_prompt, c_sample, cache_k, cache_v, cache_idx_k, page_table, state_C, state_n,
           state_m, state_conv, ffn1_norm_g, ffn1_w_gate, ffn1_w_up, ffn1_w_down, mix_norm_g, w_in, mlstm_conv_w,
           mlstm_gate_b, mlstm_out_g, q_norm_g, k_norm_g, t5_bias, w_out, ffn2_norm_g, ffn2_w_gate, ffn2_w_up,
           ffn2_w_down, w_ada, b_ada):
    depth = w_in.shape[0]
    bp, tp_len, d = x_prompt.shape
    bs, ts_len, _ = x_sample.shape
    dk, dv = state_C.shape[3], state_C.shape[4]
    mw = M_HEADS * dv
    hd = q_norm_g.shape[1]
    past_len = page_table.shape[1] * PAGE_SIZE
    bias_p, bias_s = _bias_tables(t5_bias, tp_len, past_len, ts_len)
    rep = A_HEADS // A_KV_HEADS
    bias_s = bias_s.reshape(A_KV_HEADS, rep * SUBLANE, past_len + LANE)

    xp, xs = x_prompt, x_sample
    st_p, st_s = [], []
    for l in range(depth):
        lw = dict(ffn1_norm_g=ffn1_norm_g[l], ffn1_w_gate=ffn1_w_gate[l].astype(BF16),
                  ffn1_w_up=ffn1_w_up[l].astype(BF16), ffn1_w_down=ffn1_w_down[l].astype(BF16),
                  mix_norm_g=mix_norm_g[l], w_in=_pack_w_in(w_in[l], mw, A_HEADS * hd, A_KV_HEADS * hd),
                  mlstm_conv_w=mlstm_conv_w[l], mlstm_gate_b=mlstm_gate_b[l], mlstm_out_g=mlstm_out_g[l],
                  q_norm_g=q_norm_g[l], k_norm_g=k_norm_g[l], w_out=w_out[l].astype(BF16),
                  ffn2_norm_g=ffn2_norm_g[l], ffn2_w_gate=ffn2_w_gate[l].astype(BF16),
                  ffn2_w_up=ffn2_w_up[l].astype(BF16), ffn2_w_down=ffn2_w_down[l].astype(BF16))
        mods = _ada(jnp.concatenate([c_prompt, c_sample], axis=0), w_ada[l], b_ada[l])
        mods = mods.reshape(bp + bs, N_MOD, d)
        init_p = (jnp.zeros((bp, M_HEADS, dk, dv), F32), jnp.zeros((bp, M_HEADS, dk), F32),
                  jnp.zeros((bp, M_HEADS), F32), jnp.zeros((bp, CONV_W - 1, 2 * M_HEADS * dk), F32))
        xp, sp = _layer(xp, mods[:bp], lw, bias_p, bias_s, init_p, None)
        init_s = (state_C[l], state_n[l], state_m[l], state_conv[l])
        xs, ss = _layer(xs, mods[bp:], lw, bias_p, bias_s, init_s,
                        (cache_k[l], cache_v[l], cache_idx_k[l], page_table))
        st_p.append(sp)
        st_s.append(ss)
    stack = lambda sts, k: jnp.stack([s[k] for s in sts])
    return (xp, xs) + tuple(stack(st_p, k) for k in range(7)) + tuple(stack(st_s, k) for k in range(7))
```

```python
import functools
import math

import numpy as np
import jax
import jax.numpy as jnp
from jax import lax
from jax.experimental import pallas as pl
from jax.experimental.pallas import tpu as pltpu

F32 = jnp.float32
BF16 = jnp.bfloat16

M_HEADS = 4
CONV_W = 4
M_CHUNK = 64
A_HEADS = 8
A_KV_HEADS = 2
IDX_HEADS = 8
IDX_DIM = 64
TOPK_MAX = 256
Q_BLOCK = 128
PAGE_SIZE = 128
N_BUCKETS = 32
MAX_DISTANCE = 128
N_MOD = 9
EPS = 1e-6
NEG = -1e30

LANE = 128
SUBLANE = 8
ROW_TILE = 512
VMEM_LIMIT = 56 * 1024 * 1024

TAIL_KI = 0
TAIL_WI = IDX_DIM
TAIL_IG = IDX_DIM + IDX_HEADS
TAIL_FG = TAIL_IG + M_HEADS

INT_MIN = -2 ** 31


def _params(sem, vmem=None):
    return pltpu.CompilerParams(dimension_semantics=sem, vmem_limit_bytes=vmem)


def _dot(a, b):
    return jnp.dot(a, b, preferred_element_type=F32)


def _dot_nt(a, b):
    return lax.dot_general(a, b, (((1,), (1,)), ((), ())), preferred_element_type=F32)


def _dot_tn(a, b):
    return lax.dot_general(a, b, (((0,), (0,)), ((), ())), preferred_element_type=F32)


def _rms(x, g):
    return x * lax.rsqrt(jnp.mean(x * x, axis=-1, keepdims=True) + EPS) * g


def _row_tile(n):
    return ROW_TILE if n % ROW_TILE == 0 else n


REDUCE_CHAINS = 8


def _reduce_rows(x, op):
    n, c = x.shape
    groups = n // SUBLANE
    chains = REDUCE_CHAINS if groups % REDUCE_CHAINS == 0 else 1
    x3 = x.reshape(groups, SUBLANE, c)
    per = groups // chains
    parts = [op(x3[k * per:(k + 1) * per], axis=0) for k in range(chains)]
    while len(parts) > 1:
        parts = [op(jnp.stack(parts[k:k + 2]), axis=0) for k in range(0, len(parts), 2)]
    return op(parts[0], axis=0, keepdims=True)


def _ada_kernel(c_ref, w_ref, b_ref, o_ref):
    o_ref[...] = _dot(c_ref[...].astype(BF16), w_ref[...].astype(BF16)) + b_ref[...]


def _ada(c_all, w_ada, b_ada):
    r, d = c_all.shape
    n = w_ada.shape[1]
    tn = 1024 if n % 1024 == 0 else n
    return pl.pallas_call(
        _ada_kernel,
        grid=(n // tn,),
        in_specs=[pl.BlockSpec((r, d), lambda j: (0, 0)),
                  pl.BlockSpec((d, tn), lambda j: (0, j)),
                  pl.BlockSpec((1, tn), lambda j: (0, j))],
        out_specs=pl.BlockSpec((r, tn), lambda j: (0, j)),
        out_shape=jax.ShapeDtypeStruct((r, n), F32),
        compiler_params=_params(("arbitrary",), VMEM_LIMIT),
        name="ada",
    )(c_all, w_ada, b_ada.reshape(1, n))


def _ffn_kernel(x_ref, sh_ref, sc_ref, gt_ref, g_ref, wg_ref, wu_ref, wd_ref, o_ref, h_ref, acc_ref):
    j = pl.program_id(1)

    @pl.when(j == 0)
    def _():
        h_ref[...] = (_rms(x_ref[...], g_ref[...]) * (1.0 + sc_ref[...]) + sh_ref[...]).astype(BF16)
        acc_ref[...] = jnp.zeros_like(acc_ref)

    h = h_ref[...]
    a = _dot(h, wg_ref[...])
    u = _dot(h, wu_ref[...])
    act = (a * jax.nn.sigmoid(a) * u).astype(BF16)
    acc_ref[...] += _dot(act, wd_ref[...])

    @pl.when(j == pl.num_programs(1) - 1)
    def _():
        o_ref[...] = x_ref[...] + 0.5 * gt_ref[...] * acc_ref[...]


def _mod_spec(mod, m, tm, d):
    if mod.ndim == 2:
        return pl.BlockSpec((tm, d), lambda i, j: (i, 0))
    tiles_per_seq = (m // mod.shape[0]) // tm
    return pl.BlockSpec((None, 1, d), lambda i, j: (i // tiles_per_seq, 0, 0))


def _ffn(x, sh, sc, gt, g, wg, wu, wd):
    m, d = x.shape
    f = wg.shape[1]
    tm = _row_tile(m)
    tf = 512 if f % 512 == 0 else f
    ms = _mod_spec(sh, m, tm, d)
    return pl.pallas_call(
        _ffn_kernel,
        grid=(m // tm, f // tf),
        in_specs=[pl.BlockSpec((tm, d), lambda i, j: (i, 0)), ms, ms, ms,
                  pl.BlockSpec((1, d), lambda i, j: (0, 0)),
                  pl.BlockSpec((d, tf), lambda i, j: (0, j)),
                  pl.BlockSpec((d, tf), lambda i, j: (0, j)),
                  pl.BlockSpec((tf, d), lambda i, j: (j, 0))],
        out_specs=pl.BlockSpec((tm, d), lambda i, j: (i, 0)),
        out_shape=jax.ShapeDtypeStruct((m, d), F32),
        scratch_shapes=[pltpu.VMEM((tm, d), BF16), pltpu.VMEM((tm, d), F32)],
        compiler_params=_params(("parallel", "arbitrary"), VMEM_LIMIT),
        name="ffn",
    )(x, sh, sc, gt, g.reshape(1, d), wg, wu, wd)


def _proj_kernel(x_ref, sh_ref, sc_ref, g_ref, w_ref, o_ref, h_ref):
    @pl.when(pl.program_id(1) == 0)
    def _():
        h_ref[...] = (_rms(x_ref[...], g_ref[...]) * (1.0 + sc_ref[...]) + sh_ref[...]).astype(BF16)

    o_ref[...] = _dot(h_ref[...], w_ref[...])


def _proj(x, sh, sc, g, w):
    m, d = x.shape
    n = w.shape[1]
    tm = _row_tile(m)
    tn = 7 * LANE
    assert n % tn == 0
    ms = _mod_spec(sh, m, tm, d)
    return pl.pallas_call(
        _proj_kernel,
        grid=(m // tm, n // tn),
        in_specs=[pl.BlockSpec((tm, d), lambda i, j: (i, 0)), ms, ms,
                  pl.BlockSpec((1, d), lambda i, j: (0, 0)),
                  pl.BlockSpec((d, tn), lambda i, j: (0, j))],
        out_specs=pl.BlockSpec((tm, tn), lambda i, j: (i, j)),
        out_shape=jax.ShapeDtypeStruct((m, n), F32),
        scratch_shapes=[pltpu.VMEM((tm, d), BF16)],
        compiler_params=_params(("parallel", "arbitrary"), VMEM_LIMIT),
        name="proj",
    )(x, sh, sc, g.reshape(1, d), w)


def _pack_w_in(w_in, mw, aw, kvw):
    d = w_in.shape[0]
    o = np.cumsum([0, mw, mw, mw, mw, M_HEADS, M_HEADS, aw, kvw, kvw, IDX_HEADS * IDX_DIM, IDX_DIM, IDX_HEADS])
    seg = lambda k: np.arange(o[k], o[k + 1])
    cols = np.concatenate([seg(0), seg(1), seg(2), seg(3), seg(6), seg(7), seg(8), seg(9),
                           seg(10), seg(11), seg(4), seg(5)])
    packed = jnp.take(w_in, jnp.asarray(cols, jnp.int32), axis=1).astype(BF16)
    pad = (-packed.shape[1]) % LANE
    return jnp.concatenate([packed, jnp.zeros((d, pad), BF16)], axis=1)


def _log_sigmoid(x):
    return jnp.minimum(x, 0.0) - jnp.log1p(jnp.exp(-jnp.abs(x)))


def _mlstm_kernel(qk_ref, v_ref, o_ref, tail_ref, prev_ref, cw_ref, gb_ref, og_ref, c0_ref, n0_ref, m0_ref,
                  mo_ref, cst_ref, nst_ref, mst_ref, xbuf_ref, *, L, valid, dk, dv):
    width = M_HEADS * dk

    @pl.when(pl.program_id(1) == 0)
    def _():
        xbuf_ref[0:SUBLANE, :] = prev_ref[...]
        cst_ref[...] = c0_ref[...]
        nst_ref[...] = n0_ref[...]
        mst_ref[...] = m0_ref[...]

    xbuf_ref[SUBLANE:SUBLANE + L, :] = qk_ref[...]
    base = SUBLANE - (CONV_W - 1)
    y = xbuf_ref[base:base + L, :] * cw_ref[0:1, :]
    for j in range(1, CONV_W):
        y = y + xbuf_ref[base + j:base + j + L, :] * cw_ref[j:j + 1, :]
    tail_rows = xbuf_ref[L:L + SUBLANE, :]
    xbuf_ref[0:SUBLANE, :] = tail_rows
    qk = y * jax.nn.sigmoid(y)

    gates = tail_ref[...] + gb_ref[...]
    row = lax.broadcasted_iota(jnp.int32, (L, LANE), 0)
    lf = _log_sigmoid(gates)
    ig = gates
    if valid < L:
        lf = jnp.where(row < valid, lf, 0.0)
        ig = jnp.where(row < valid, ig, -jnp.inf)
    bcum = lf
    s = 1
    while s < L:
        bcum = bcum + jnp.where(row >= s, pltpu.roll(bcum, s, axis=0), 0.0)
        s *= 2
    bmi_t = (pltpu.roll(ig, TAIL_FG - TAIL_IG, axis=1) - bcum).T

    ri = lax.broadcasted_iota(jnp.int32, (L, L), 0)
    ci = lax.broadcasted_iota(jnp.int32, (L, L), 1)
    causal = ri >= ci

    for h in range(M_HEADS):
        li, lfh = TAIL_IG + h, TAIL_FG + h
        a_col = bcum[:, lfh:lfh + 1]
        ig_col = ig[:, li:li + 1]
        b_row = bmi_t[lfh:lfh + 1, :]
        m0 = mst_ref[:, h:h + 1]
        d_log = jnp.where(causal, a_col + b_row, -jnp.inf)
        s_log = a_col + m0
        m_col = jnp.maximum(s_log, jnp.max(d_log, axis=-1, keepdims=True))
        dw = jnp.exp(d_log - m_col)
        sw = jnp.exp(s_log - m_col)

        q = qk[:, h * dk:(h + 1) * dk]
        k = qk[:, width + h * dk:width + (h + 1) * dk] * (dk ** -0.5)
        v = v_ref[:, h * dv:(h + 1) * dv]
        qb, kb = q.astype(BF16), k.astype(BF16)
        c_prev = cst_ref[h]
        n_prev = nst_ref[h:h + 1, :]
        scores = _dot_nt(qb, kb) * dw
        num = _dot(scores.astype(BF16), v.astype(BF16)) + sw * _dot(qb, c_prev.astype(BF16))
        den = jnp.sum(scores, axis=-1, keepdims=True) + sw * jnp.sum(q * n_prev, axis=-1, keepdims=True)
        hh = num / jnp.maximum(jnp.abs(den), jnp.exp(-m_col))

        m_new = m_col[L - 1:L, :]
        a_last = a_col[L - 1:L, :]
        wl = jnp.exp(a_last - a_col + ig_col - m_new)
        decay = jnp.exp(a_last + m0 - m_new)
        cst_ref[h] = decay * c_prev + _dot_tn(kb, (wl * v).astype(BF16))
        nst_ref[h:h + 1, :] = decay * n_prev + jnp.sum(wl * k, axis=0, keepdims=True)
        mst_ref[:, h:h + 1] = m_new

        hn = hh * lax.rsqrt(jnp.mean(hh * hh, axis=-1, keepdims=True) + EPS) * og_ref[:, h * dv:(h + 1) * dv]
        gate = jax.nn.sigmoid(o_ref[:, h * dv:(h + 1) * dv])
        mo_ref[:, h * dv:(h + 1) * dv] = (hn * gate).astype(mo_ref.dtype)


def _mlstm(p, prev8, conv_w, gate_row, out_g, c0, n0, m0, *, n_seq, L, valid, out_dtype):
    rows = p.shape[0]
    nc = rows // (n_seq * L)
    dk = c0.shape[2]
    dv = c0.shape[3]
    mw = M_HEADS * dv
    qkw = 2 * M_HEADS * dk
    assert qkw % mw == 0 and (2 * qkw) % mw == 0
    rmap = lambda b, c: b * nc + c
    kern = functools.partial(_mlstm_kernel, L=L, valid=valid, dk=dk, dv=dv)
    return pl.pallas_call(
        kern,
        grid=(n_seq, nc),
        in_specs=[pl.BlockSpec((L, qkw), lambda b, c: (rmap(b, c), 0)),
                  pl.BlockSpec((L, mw), lambda b, c: (rmap(b, c), qkw // mw)),
                  pl.BlockSpec((L, mw), lambda b, c: (rmap(b, c), qkw // mw + 1)),
                  pl.BlockSpec((L, LANE), lambda b, c: (rmap(b, c), p.shape[1] // LANE - 1)),
                  pl.BlockSpec((None, SUBLANE, qkw), lambda b, c: (b, 0, 0)),
                  pl.BlockSpec((CONV_W, qkw), lambda b, c: (0, 0)),
                  pl.BlockSpec((1, LANE), lambda b, c: (0, 0)),
                  pl.BlockSpec((1, mw), lambda b, c: (0, 0)),
                  pl.BlockSpec((None, M_HEADS, dk, dv), lambda b, c: (b, 0, 0, 0)),
                  pl.BlockSpec((None, M_HEADS, dk), lambda b, c: (b, 0, 0)),
                  pl.BlockSpec((None, 1, LANE), lambda b, c: (b, 0, 0))],
        out_specs=[pl.BlockSpec((L, mw), lambda b, c: (rmap(b, c), 0)),
                   pl.BlockSpec((None, M_HEADS, dk, dv), lambda b, c: (b, 0, 0, 0)),
                   pl.BlockSpec((None, M_HEADS, dk), lambda b, c: (b, 0, 0)),
                   pl.BlockSpec((None, 1, LANE), lambda b, c: (b, 0, 0))],
        out_shape=[jax.ShapeDtypeStruct((rows, mw), out_dtype),
                   jax.ShapeDtypeStruct(c0.shape, F32),
                   jax.ShapeDtypeStruct(n0.shape, F32),
                   jax.ShapeDtypeStruct(m0.shape, F32)],
        scratch_shapes=[pltpu.VMEM((SUBLANE + L, qkw), F32)],
        compiler_params=_params(("parallel", "arbitrary"), VMEM_LIMIT),
        name="mlstm",
    )(p, p, p, p, prev8, conv_w, gate_row, out_g.reshape(1, mw), c0, n0, m0)


def _bucket_np(dist):
    me = N_BUCKETS // 2
    d = np.maximum(dist, 0)
    ratio = np.log(np.maximum(d, 1).astype(np.float64) / me) / math.log(MAX_DISTANCE / me)
    large = np.minimum(me + (ratio * (N_BUCKETS - me)).astype(np.int64), N_BUCKETS - 1)
    return np.where(d < me, d, large).astype(np.int32)


def _bias_kernel(t5_ref, bkp_ref, bks_ref, vs_ref, far_ref, op_ref, os_ref):
    rep = A_HEADS // A_KV_HEADS

    def lookup(bk, h):
        acc = jnp.zeros(bk.shape, F32)
        for b in range(N_BUCKETS):
            acc = jnp.where(bk == b, t5_ref[b, h], acc)
        return acc

    for h in range(A_HEADS):
        half = (h % 2) * Q_BLOCK
        op_ref[h // 2, :, half:half + Q_BLOCK] = lookup(bkp_ref[...], h) - t5_ref[far_ref[0], h]
        os_ref[h] = jnp.where(vs_ref[h // rep] > 0, lookup(bks_ref[...], h), NEG)


def _bias_tables(t5_bias, t_prompt, past, t_new):
    c = np.arange(2 * Q_BLOCK)[:, None]
    r = np.arange(Q_BLOCK)[None, :]
    bkp = _bucket_np(Q_BLOCK + r - c)
    far = _bucket_np(np.arange(Q_BLOCK + 1, max(t_prompt, past + t_new) + Q_BLOCK))
    assert (far == far[0]).all(), "bias must be constant beyond one query block"
    nk = past + LANE
    tq = np.arange(SUBLANE)[:, None]
    col = np.arange(2 * nk)[None, :]
    s_pos, s_grp = col // A_KV_HEADS, col % A_KV_HEADS
    bks = _bucket_np(past + tq - s_pos)
    vis = np.stack([(s_grp == g) & (s_pos - past <= tq) for g in range(A_KV_HEADS)]).astype(np.int32)
    return pl.pallas_call(
        _bias_kernel,
        in_specs=[pl.BlockSpec(memory_space=pltpu.SMEM),
                  pl.BlockSpec(memory_space=pltpu.VMEM),
                  pl.BlockSpec(memory_space=pltpu.VMEM),
                  pl.BlockSpec(memory_space=pltpu.VMEM),
                  pl.BlockSpec(memory_space=pltpu.SMEM)],
        out_specs=[pl.BlockSpec(memory_space=pltpu.VMEM), pl.BlockSpec(memory_space=pltpu.VMEM)],
        out_shape=[jax.ShapeDtypeStruct((A_HEADS // 2, 2 * Q_BLOCK, 2 * Q_BLOCK), F32),
                   jax.ShapeDtypeStruct((A_HEADS, SUBLANE, 2 * nk), F32)],
        name="t5_bias_tables",
    )(t5_bias, jnp.asarray(bkp), jnp.asarray(bks), jnp.asarray(vis), jnp.asarray(far[:1]))


def _sort_key(score):
    bits = lax.bitcast_convert_type(score, jnp.int32)
    return jnp.where(bits < 0, bits ^ jnp.int32(0x7FFFFFFF), bits)


def _kth_key(count_ge, shape, k_sel):
    def body(it, ans_u):
        cand_u = ans_u | lax.shift_left(jnp.int32(1), jnp.int32(31) - it)
        cnt = count_ge(cand_u ^ jnp.int32(INT_MIN))
        return jnp.where(cnt >= k_sel, cand_u, ans_u)

    ans_u = lax.fori_loop(0, 32, body, jnp.zeros(shape, jnp.int32))
    return ans_u ^ jnp.int32(INT_MIN)


def _triangle(below):
    r = lax.broadcasted_iota(jnp.int32, (LANE, LANE), 0)
    c = lax.broadcasted_iota(jnp.int32, (LANE, LANE), 1)
    return jnp.where((c < r) if below else (r < c), 1.0, 0.0).astype(BF16)


def _select_tiles(key_tiles, thr, k_sel, key_axis):
    count = lambda x: jnp.sum(x, axis=key_axis, keepdims=True)
    c_gt = jnp.zeros(thr.shape, F32)
    for kt in key_tiles:
        c_gt = c_gt + count(jnp.where(kt > thr, 1.0, 0.0))
    need = k_sel - c_gt
    tri = _triangle(below=(key_axis == 0))
    run = jnp.zeros(thr.shape, F32)
    out = []
    for kt in key_tiles:
        eq = jnp.where(kt == thr, 1.0, 0.0)
        eqb = eq.astype(BF16)
        before = (_dot(tri, eqb) if key_axis == 0 else _dot(eqb, tri)) + run
        out.append(jnp.where(kt > thr, 1.0, jnp.where(before < need, eq, 0.0)))
        run = run + count(eq)
    return out


FAR_STEP = 4


def _dsa_prompt_kernel(qa_ref, qi_ref, qtail_ref, k_ref, v_ref, ktail_ref, qg_ref, kg_ref, bias_ref,
                       ao_ref, kn_ref,
                       kb_ref, vt_ref, kib_ref, qn_ref, key_ref, negb_ref, *, T, k_sel, hd):
    i = pl.program_id(1)
    nb = T // Q_BLOCK
    rep = A_HEADS // A_KV_HEADS
    scale = hd ** -0.5
    near_w = 2 * LANE

    @pl.when(i == 0)
    def _():
        kb_ref[0:LANE, :] = jnp.zeros((LANE, kb_ref.shape[1]), BF16)
        vt_ref[:, 0:LANE] = jnp.zeros((vt_ref.shape[0], LANE), BF16)
        kib_ref[0:LANE, :] = jnp.zeros((LANE, IDX_DIM), BF16)
        for g in range(A_KV_HEADS):
            kn = _rms(k_ref[:, g * hd:(g + 1) * hd], kg_ref[...])
            kn_ref[:, g * hd:(g + 1) * hd] = kn
            kb_ref[LANE:, g * hd:(g + 1) * hd] = kn.astype(BF16)
        vt_ref[:, LANE:] = v_ref[...].T.astype(BF16)
        kib_ref[LANE:, :] = ktail_ref[:, TAIL_KI:TAIL_KI + IDX_DIM].astype(BF16)

    wi_t = qtail_ref[...].T[TAIL_WI:TAIL_WI + IDX_HEADS, :] * (IDX_HEADS ** -0.5) * (IDX_DIM ** -0.5)
    qi_pairs = [jnp.concatenate([qi_ref[:, (2 * a + hh) * IDX_DIM:(2 * a + hh + 1) * IDX_DIM] for hh in range(2)],
                                axis=0).astype(BF16) for a in range(IDX_HEADS // 2)]
    for h in range(A_HEADS):
        half = (h % 2) * Q_BLOCK
        qn_ref[h // 2, half:half + Q_BLOCK, :] = _rms(qa_ref[:, h * hd:(h + 1) * hd], qg_ref[...]).astype(BF16)

    near0 = pl.multiple_of(i * LANE, LANE)
    ki_near = kib_ref[pl.ds(near0, near_w), :]
    k_near = kb_ref[pl.ds(near0, near_w), :]
    vt_near = vt_ref[:, pl.ds(near0, near_w)]
    kc = lax.broadcasted_iota(jnp.int32, (near_w, Q_BLOCK), 0)
    qr = lax.broadcasted_iota(jnp.int32, (near_w, Q_BLOCK), 1)
    near_ok = jnp.where(kc <= qr + LANE, 1.0, 0.0) * jnp.maximum(jnp.where(kc >= LANE, 1.0, 0.0),
                                                                  (i >= 1).astype(F32))

    def scores(ki):
        sc = jnp.zeros((ki.shape[0], Q_BLOCK), F32)
        for a in range(IDX_HEADS // 2):
            s2 = _dot_nt(ki, qi_pairs[a])
            for hh in range(2):
                j = 2 * a + hh
                sc = sc + wi_t[j:j + 1, :] * jnp.maximum(s2[:, hh * Q_BLOCK:(hh + 1) * Q_BLOCK], 0.0)
        return sc

    def body(wf):
        fw = wf * LANE
        w_all = fw + near_w
        key_ref[fw:w_all, :] = _sort_key(jnp.where(near_ok > 0.0, scores(ki_near) + 0.0, NEG))
        if wf:
            far_ok = lax.broadcasted_iota(jnp.int32, (fw, Q_BLOCK), 0) < (i - 1) * LANE
            key_ref[0:fw, :] = _sort_key(jnp.where(far_ok, scores(kib_ref[LANE:LANE + fw, :]) + 0.0, NEG))

        def count_ge(cand):
            return _reduce_rows(jnp.where(key_ref[0:w_all, :] >= cand, 1.0, 0.0), jnp.sum)

        thr = _kth_key(count_ge, (1, Q_BLOCK), float(k_sel))
        sel = _select_tiles([key_ref[t * LANE:(t + 1) * LANE, :] for t in range(wf + 2)], thr, float(k_sel), 0)
        for t in range(wf):
            ok = sel[t] * (t < i - 1).astype(F32)
            negb_ref[t * LANE:(t + 1) * LANE, :] = jnp.where(ok > 0.0, 0.0, NEG)
        for t in range(2):
            ok = sel[wf + t] * near_ok[t * LANE:(t + 1) * LANE, :]
            negb_ref[fw + t * LANE:fw + (t + 1) * LANE, :] = jnp.where(ok > 0.0, 0.0, NEG)

        for a in range(A_HEADS // 2):
            g = (2 * a) // rep
            cs = slice(g * hd, (g + 1) * hd)
            qn2 = qn_ref[a]
            nb_near = negb_ref[fw:w_all, :]
            s_near = _dot_nt(k_near[:, cs], qn2) * scale + bias_ref[a] + jnp.concatenate([nb_near, nb_near], axis=1)
            m = _reduce_rows(s_near, jnp.max)
            if wf:
                nb_far = negb_ref[0:fw, :]
                s_far = _dot_nt(kb_ref[LANE:LANE + fw, cs], qn2) * scale + jnp.concatenate([nb_far, nb_far], axis=1)
                m = jnp.maximum(m, _reduce_rows(s_far, jnp.max))
            p_near = jnp.exp(s_near - m)
            l = _reduce_rows(p_near, jnp.sum)
            acc = _dot(vt_near[cs, :], p_near.astype(BF16))
            if wf:
                p_far = jnp.exp(s_far - m)
                l = l + _reduce_rows(p_far, jnp.sum)
                acc = acc + _dot(vt_ref[cs, LANE:LANE + fw], p_far.astype(BF16))
            out_t = acc / l
            for hh in range(2):
                h = 2 * a + hh
                ao_ref[:, h * hd:(h + 1) * hd] = out_t[:, hh * Q_BLOCK:(hh + 1) * Q_BLOCK].T.astype(ao_ref.dtype)

    variants = sorted({min(FAR_STEP * -(-x // FAR_STEP), nb) for x in range(max(nb - 1, 1))})
    wf_needed = jnp.minimum((jnp.maximum(i - 1, 0) + FAR_STEP - 1) // FAR_STEP * FAR_STEP, nb)
    for wf in variants:
        pl.when(wf_needed == wf)(functools.partial(body, wf))


def _dsa_prompt(p, qn_g, kn_g, bias_p, *, n_seq, T, hd):
    rows = p.shape[0]
    nb = T // Q_BLOCK
    aw = A_HEADS * hd
    kvw = A_KV_HEADS * hd
    iw = IDX_HEADS * IDX_DIM
    k_sel = min(TOPK_MAX, T // 4)
    mcols = p.shape[1] - LANE - iw - 2 * kvw - aw
    assert mcols % aw == 0 and (mcols + aw) % kvw == 0 and (mcols + aw + 2 * kvw) % iw == 0
    qa_blk = mcols // aw
    k_blk = (mcols + aw) // kvw
    qi_blk = (mcols + aw + 2 * kvw) // iw
    tail_blk = p.shape[1] // LANE - 1
    kern = functools.partial(_dsa_prompt_kernel, T=T, k_sel=k_sel, hd=hd)
    return pl.pallas_call(
        kern,
        grid=(n_seq, nb),
        in_specs=[pl.BlockSpec((Q_BLOCK, aw), lambda b, i: (b * nb + i, qa_blk)),
                  pl.BlockSpec((Q_BLOCK, iw), lambda b, i: (b * nb + i, qi_blk)),
                  pl.BlockSpec((Q_BLOCK, LANE), lambda b, i: (b * nb + i, tail_blk)),
                  pl.BlockSpec((T, kvw), lambda b, i: (b, k_blk)),
                  pl.BlockSpec((T, kvw), lambda b, i: (b, k_blk + 1)),
                  pl.BlockSpec((T, LANE), lambda b, i: (b, tail_blk)),
                  pl.BlockSpec((1, hd), lambda b, i: (0, 0)),
                  pl.BlockSpec((1, hd), lambda b, i: (0, 0)),
                  pl.BlockSpec((A_HEADS // 2, 2 * Q_BLOCK, 2 * Q_BLOCK), lambda b, i: (0, 0, 0))],
        out_specs=[pl.BlockSpec((Q_BLOCK, aw), lambda b, i: (b * nb + i, 0)),
                   pl.BlockSpec((T, kvw), lambda b, i: (b, 0))],
        out_shape=[jax.ShapeDtypeStruct((rows, aw), BF16),
                   jax.ShapeDtypeStruct((rows, kvw), F32)],
        scratch_shapes=[pltpu.VMEM((T + LANE, kvw), BF16), pltpu.VMEM((kvw, T + LANE), BF16),
                        pltpu.VMEM((T + LANE, IDX_DIM), BF16),
                        pltpu.VMEM((A_HEADS // 2, 2 * Q_BLOCK, hd), BF16),
                        pltpu.VMEM((T + 2 * LANE, Q_BLOCK), jnp.int32),
                        pltpu.VMEM((T + 2 * LANE, Q_BLOCK), F32)],
        compiler_params=_params(("parallel", "arbitrary"), VMEM_LIMIT),
        name="dsa_prompt",
    )(p, p, p, p, p, p, qn_g.reshape(1, hd), kn_g.reshape(1, hd), bias_p)


def _pad_rows(x, n):
    return jnp.concatenate([x, jnp.zeros((n - x.shape[0], x.shape[1]), x.dtype)], axis=0)


def _idx_sample_kernel(pt_ref, qi_ref, wi_ref, kinew_ref, *rest, n_pages, t_new):
    ip, sc_ref = rest[:n_pages], rest[n_pages]
    del pt_ref
    qi = qi_ref[...].astype(BF16)
    wi = wi_ref[...] * (IDX_HEADS ** -0.5) * (IDX_DIM ** -0.5)
    tq = lax.broadcasted_iota(jnp.int32, (SUBLANE, LANE), 0)
    lane = lax.broadcasted_iota(jnp.int32, (SUBLANE, LANE), 1)
    for t in range(n_pages + 1):
        kt = ip[t][...].astype(BF16) if t < n_pages else _pad_rows(kinew_ref[...], LANE).astype(BF16)
        r = _dot_nt(qi, kt)
        sc = jnp.zeros((SUBLANE, LANE), F32)
        for j in range(IDX_HEADS):
            sc = sc + wi[j * SUBLANE:(j + 1) * SUBLANE, :] * jnp.maximum(r[j * SUBLANE:(j + 1) * SUBLANE, :], 0.0)
        sc = sc + 0.0
        if t == n_pages:
            sc = jnp.where(lane <= tq, sc, jnp.where(lane < t_new, NEG, -jnp.inf))
        sc_ref[:, t * LANE:(t + 1) * LANE] = sc


def _select_kernel(sc_ref, sel_ref, key_ref, *, k_sel):
    rows, nk = sc_ref.shape
    key_ref[...] = _sort_key(sc_ref[...])

    def count_ge(cand):
        return jnp.sum(jnp.where(key_ref[...] >= cand, 1.0, 0.0), axis=1, keepdims=True)

    thr = _kth_key(count_ge, (rows, 1), float(k_sel))
    sel = _select_tiles([key_ref[:, t * LANE:(t + 1) * LANE] for t in range(nk // LANE)], thr, float(k_sel), 1)
    for t in range(nk // LANE):
        sel_ref[:, t * LANE:(t + 1) * LANE] = sel[t]


def _attn_sample_kernel(pt_ref, qa_ref, sel_ref, knew_ref, vnew_ref, qg_ref, kg_ref, bias_ref, *rest,
                        n_pages, hd):
    kp, vp = rest[:n_pages], rest[n_pages:2 * n_pages]
    ao_ref, kn_ref = rest[2 * n_pages:]
    del pt_ref
    nt = n_pages + 1
    pw = A_KV_HEADS * PAGE_SIZE
    scale = hd ** -0.5
    rows_q = qa_ref.shape[0]

    kn_new = _rms(knew_ref[...], kg_ref[...])
    kn_ref[...] = kn_new
    k_new = _pad_rows(kn_new, pw).astype(BF16)
    v_new = _pad_rows(vnew_ref[...], pw).astype(BF16)
    qn = _rms(qa_ref[...], qg_ref[...]).astype(BF16)

    tok = lax.broadcasted_iota(jnp.int32, (PAGE_SIZE, pw), 0)
    col = lax.broadcasted_iota(jnp.int32, (PAGE_SIZE, pw), 1)
    spread = jnp.where(col // A_KV_HEADS == tok, 1.0, 0.0).astype(BF16)
    sel_rows = jnp.concatenate([sel_ref[:, t * LANE:(t + 1) * LANE] for t in range(nt)], axis=0)
    negb = (_dot(sel_rows.astype(BF16), spread) - 1.0) * (-NEG)

    logits = []
    for t in range(nt):
        kt = kp[t][...].astype(BF16) if t < n_pages else k_new
        nb = jnp.concatenate([negb[t * SUBLANE:(t + 1) * SUBLANE, :]] * (rows_q // SUBLANE), axis=0)
        logits.append(_dot_nt(qn, kt) * scale + bias_ref[:, t * pw:(t + 1) * pw] + nb)
    m = logits[0].max(axis=-1, keepdims=True)
    for t in range(1, nt):
        m = jnp.maximum(m, logits[t].max(axis=-1, keepdims=True))
    l = jnp.zeros((rows_q, 1), F32)
    acc = jnp.zeros((rows_q, hd), F32)
    for t in range(nt):
        pr = jnp.exp(logits[t] - m)
        l = l + jnp.sum(pr, axis=-1, keepdims=True)
        acc = acc + _dot(pr.astype(BF16), vp[t][...].astype(BF16) if t < n_pages else v_new)
    ao_ref[...] = acc / l


def _dsa_sample(qa, qi, wi, knew, vnew, kinew, qn_g, kn_g, bias_s, cache_k, cache_v, cache_i, page_table, *, t_new, hd):
    n_seq, n_pages = page_table.shape
    nk = n_pages * PAGE_SIZE + LANE
    k_sel = min(TOPK_MAX, (n_pages * PAGE_SIZE + t_new) // 4)
    pt = page_table.reshape(-1)

    def page_spec(arr, t):
        return pl.BlockSpec((None,) + arr.shape[1:], lambda b, pt: (pt[b * n_pages + t], 0, 0))

    seq_spec = lambda arr: pl.BlockSpec((None,) + arr.shape[1:], lambda b, pt: (b, 0, 0))
    const_spec = lambda arr: pl.BlockSpec(arr.shape, lambda b, pt: (0, 0))

    scores = pl.pallas_call(
        functools.partial(_idx_sample_kernel, n_pages=n_pages, t_new=t_new),
        grid_spec=pltpu.PrefetchScalarGridSpec(
            num_scalar_prefetch=1, grid=(n_seq,),
            in_specs=[seq_spec(qi), seq_spec(wi), seq_spec(kinew)] + [page_spec(cache_i, t) for t in range(n_pages)],
            out_specs=pl.BlockSpec((SUBLANE, nk), lambda b, pt: (b, 0))),
        out_shape=jax.ShapeDtypeStruct((n_seq * SUBLANE, nk), F32),
        compiler_params=_params(("arbitrary",), VMEM_LIMIT),
        name="idx_sample",
    )(pt, qi, wi, kinew, *([cache_i] * n_pages))

    rows = scores.shape[0]
    tr = LANE if rows % LANE == 0 else rows
    sel = pl.pallas_call(
        functools.partial(_select_kernel, k_sel=k_sel),
        grid=(rows // tr,),
        in_specs=[pl.BlockSpec((tr, nk), lambda r: (r, 0))],
        out_specs=pl.BlockSpec((tr, nk), lambda r: (r, 0)),
        out_shape=jax.ShapeDtypeStruct((rows, nk), F32),
        scratch_shapes=[pltpu.VMEM((tr, nk), jnp.int32)],
        compiler_params=_params(("parallel",), VMEM_LIMIT),
        name="select_sample",
    )(scores)

    qg, kg = qn_g.reshape(1, hd), kn_g.reshape(1, hd)
    return pl.pallas_call(
        functools.partial(_attn_sample_kernel, n_pages=n_pages, hd=hd),
        grid_spec=pltpu.PrefetchScalarGridSpec(
            num_scalar_prefetch=1, grid=(n_seq,),
            in_specs=[seq_spec(qa), pl.BlockSpec((SUBLANE, nk), lambda b, pt: (b, 0)), seq_spec(knew), seq_spec(vnew),
                      const_spec(qg), const_spec(kg), const_spec(bias_s)]
                     + [page_spec(cache_k, t) for t in range(n_pages)]
                     + [page_spec(cache_v, t) for t in range(n_pages)],
            out_specs=[seq_spec(qa), seq_spec(knew)]),
        out_shape=[jax.ShapeDtypeStruct(qa.shape, F32), jax.ShapeDtypeStruct(knew.shape, F32)],
        compiler_params=_params(("arbitrary",), VMEM_LIMIT),
        name="attn_sample",
    )(pt, qa, sel, knew, vnew, qg, kg, bias_s, *([cache_k] * n_pages), *([cache_v] * n_pages))


def _outproj_kernel(x_ref, mo_ref, ao_ref, gt_ref, wm_ref, wa_ref, o_ref):
    y = _dot(mo_ref[...].astype(BF16), wm_ref[...]) + _dot(ao_ref[...].astype(BF16), wa_ref[...])
    o_ref[...] = x_ref[...] + gt_ref[...] * y


def _outproj(x, mo, ao, gt, w_out):
    m, d = x.shape
    mw, aw = mo.shape[1], ao.shape[1]
    tm = _row_tile(m)
    ms = _mod_spec(gt, m, tm, d)
    row = lambda w: pl.BlockSpec((tm, w), lambda i, j: (i, 0))
    return pl.pallas_call(
        _outproj_kernel,
        grid=(m // tm, 1),
        in_specs=[row(d), row(mw), row(aw), ms,
                  pl.BlockSpec((mw, d), lambda i, j: (0, 0)),
                  pl.BlockSpec((aw, d), lambda i, j: (0, 0))],
        out_specs=row(d),
        out_shape=jax.ShapeDtypeStruct((m, d), F32),
        compiler_params=_params(("parallel", "arbitrary"), VMEM_LIMIT),
        name="outproj",
    )(x, mo, ao, gt, w_out[:mw], w_out[mw:])


def _pad_tokens(a, t_pad):
    pad = [(0, 0)] * a.ndim
    pad[1] = (0, t_pad - a.shape[1])
    return jnp.pad(a, pad)


def _layer(x3, mods, lw, bias_p, bias_s, rec_state, past):
    n_seq, T, d = x3.shape
    x = x3.reshape(n_seq * T, d)
    dk, dv = rec_state[0].shape[2], rec_state[0].shape[3]
    mw = M_HEADS * dv
    hd = lw["q_norm_g"].shape[0]
    aw, kvw = A_HEADS * hd, A_KV_HEADS * hd
    iw = IDX_HEADS * IDX_DIM
    per_seq = T % ROW_TILE == 0
    if per_seq:
        mod = lambda k: mods[:, k:k + 1, :]
    else:
        mod = lambda k: jnp.repeat(mods[:, k, :], T, axis=0)
    sh1, sc1, g1, sh2, sc2, g2, sh3, sc3, g3 = [mod(k) for k in range(N_MOD)]

    x = _ffn(x, sh1, sc1, g1, lw["ffn1_norm_g"], lw["ffn1_w_gate"], lw["ffn1_w_up"], lw["ffn1_w_down"])
    p = _proj(x, sh2, sc2, lw["mix_norm_g"], lw["w_in"])
    npk = p.shape[1]
    qkw = 2 * M_HEADS * dk
    c0, n0, m0, conv_buf = rec_state
    gate_row = jnp.zeros((1, LANE), F32).at[0, TAIL_IG:TAIL_IG + 2 * M_HEADS].set(lw["mlstm_gate_b"])
    m0_pad = jnp.zeros((n_seq, 1, LANE), F32).at[:, 0, :M_HEADS].set(m0)
    prev8 = jnp.zeros((n_seq, SUBLANE, qkw), F32).at[:, SUBLANE - (CONV_W - 1):, :].set(conv_buf)
    p3 = p.reshape(n_seq, T, npk)
    a_off = qkw + 2 * mw
    v_a = p3[:, :, a_off + aw + kvw:a_off + aw + 2 * kvw]
    ki = p3[:, :, npk - LANE + TAIL_KI:npk - LANE + TAIL_KI + IDX_DIM]
    conv_new = p3[:, T - (CONV_W - 1):, :qkw]

    if past is None:
        L = M_CHUNK if T % M_CHUNK == 0 else T
        mo, c_new, n_new, m_new = _mlstm(p, prev8, lw["mlstm_conv_w"], gate_row, lw["mlstm_out_g"], c0, n0, m0_pad,
                                         n_seq=n_seq, L=L, valid=L, out_dtype=BF16)
        ao, k_n = _dsa_prompt(p, lw["q_norm_g"], lw["k_norm_g"], bias_p, n_seq=n_seq, T=T, hd=hd)
        k_n = k_n.reshape(n_seq, T, A_KV_HEADS, hd)
    else:
        tp = SUBLANE
        assert T <= tp
        pp = _pad_tokens(p3, tp).reshape(n_seq * tp, npk)
        mo, c_new, n_new, m_new = _mlstm(pp, prev8, lw["mlstm_conv_w"], gate_row, lw["mlstm_out_g"], c0, n0, m0_pad,
                                         n_seq=n_seq, L=tp, valid=T, out_dtype=F32)
        mo = mo.reshape(n_seq, tp, mw)[:, :T].reshape(n_seq * T, mw)
        cache_k, cache_v, cache_i, page_table = past
        rep = A_HEADS // A_KV_HEADS
        qa = _pad_tokens(p3[:, :, a_off:a_off + aw], tp).reshape(n_seq, tp, A_KV_HEADS, rep, hd)
        qa = qa.transpose(0, 2, 3, 1, 4).reshape(n_seq, A_HEADS * tp, hd)
        qi = _pad_tokens(p3[:, :, a_off + aw + 2 * kvw:a_off + aw + 2 * kvw + iw], tp)
        qi = qi.reshape(n_seq, tp, IDX_HEADS, IDX_DIM).transpose(0, 2, 1, 3).reshape(n_seq, IDX_HEADS * tp, IDX_DIM)
        wi = _pad_tokens(p3[:, :, npk - LANE + TAIL_WI:npk - LANE + TAIL_WI + IDX_HEADS], tp)
        wi = wi.transpose(0, 2, 1).reshape(n_seq, IDX_HEADS * tp, 1)
        knew = _pad_tokens(p3[:, :, a_off + aw:a_off + aw + kvw], tp).reshape(n_seq, tp * A_KV_HEADS, hd)
        vnew = _pad_tokens(v_a, tp).reshape(n_seq, tp * A_KV_HEADS, hd)
        n_pool = cache_k.shape[0]
        ao4, k_n8 = _dsa_sample(qa, qi, wi, knew, vnew, _pad_tokens(ki, tp), lw["q_norm_g"], lw["k_norm_g"], bias_s,
                                cache_k.reshape(n_pool, PAGE_SIZE * A_KV_HEADS, hd),
                                cache_v.reshape(n_pool, PAGE_SIZE * A_KV_HEADS, hd),
                                cache_i, page_table, t_new=T, hd=hd)
        k_n8 = k_n8.reshape(n_seq, tp, kvw)
        ao = ao4.reshape(n_seq, A_KV_HEADS, rep, tp, hd)[:, :, :, :T].transpose(0, 3, 1, 2, 4).reshape(n_seq * T, aw)
        k_n = k_n8[:, :T].reshape(n_seq, T, A_KV_HEADS, hd)

    x = _outproj(x, mo, ao, g2, lw["w_out"])
    x = _ffn(x, sh3, sc3, g3, lw["ffn2_norm_g"], lw["ffn2_w_gate"], lw["ffn2_w_up"], lw["ffn2_w_down"])
    state = (k_n, v_a.reshape(n_seq, T, A_KV_HEADS, hd), ki, c_new, n_new, m_new[:, 0, :M_HEADS], conv_new)
    return x.reshape(n_seq, T, d), state


def kernel(x_prompt, x_sample, c_prompt, c_sample, cache_k, cache_v, cache_idx_k, page_table, state_C, state_n,
           state_m, state_conv, ffn1_norm_g, ffn1_w_gate, ffn1_w_up, ffn1_w_down, mix_norm_g, w_in, mlstm_conv_w,
           mlstm_gate_b, mlstm_out_g, q_norm_g, k_norm_g, t5_bias, w_out, ffn2_norm_g, ffn2_w_gate, ffn2_w_up,
           ffn2_w_down, w_ada, b_ada):
    depth = w_in.shape[0]
    bp, tp_len, d = x_prompt.shape
    bs, ts_len, _ = x_sample.shape
    dk, dv = state_C.shape[3], state_C.shape[4]
    mw = M_HEADS * dv
    hd = q_norm_g.shape[1]
    past_len = page_table.shape[1] * PAGE_SIZE
    bias_p, bias_s = _bias_tables(t5_bias, tp_len, past_len, ts_len)
    rep = A_HEADS // A_KV_HEADS
    bias_s = bias_s.reshape(A_HEADS * SUBLANE, A_KV_HEADS * (past_len + LANE))

    xp, xs = x_prompt, x_sample
    st_p, st_s = [], []
    for l in range(depth):
        lw = dict(ffn1_norm_g=ffn1_norm_g[l], ffn1_w_gate=ffn1_w_gate[l].astype(BF16),
                  ffn1_w_up=ffn1_w_up[l].astype(BF16), ffn1_w_down=ffn1_w_down[l].astype(BF16),
                  mix_norm_g=mix_norm_g[l], w_in=_pack_w_in(w_in[l], mw, A_HEADS * hd, A_KV_HEADS * hd),
                  mlstm_conv_w=mlstm_conv_w[l], mlstm_gate_b=mlstm_gate_b[l], mlstm_out_g=mlstm_out_g[l],
                  q_norm_g=q_norm_g[l], k_norm_g=k_norm_g[l], w_out=w_out[l].astype(BF16),
                  ffn2_norm_g=ffn2_norm_g[l], ffn2_w_gate=ffn2_w_gate[l].astype(BF16),
                  ffn2_w_up=ffn2_w_up[l].astype(BF16), ffn2_w_down=ffn2_w_down[l].astype(BF16))
        mods = _ada(jnp.concatenate([c_prompt, c_sample], axis=0), w_ada[l], b_ada[l])
        mods = mods.reshape(bp + bs, N_MOD, d)
        init_p = (jnp.zeros((bp, M_HEADS, dk, dv), F32), jnp.zeros((bp, M_HEADS, dk), F32),
                  jnp.zeros((bp, M_HEADS), F32), jnp.zeros((bp, CONV_W - 1, 2 * M_HEADS * dk), F32))
        xp, sp = _layer(xp, mods[:bp], lw, bias_p, bias_s, init_p, None)
        init_s = (state_C[l], state_n[l], state_m[l], state_conv[l])
        xs, ss = _layer(xs, mods[bp:], lw, bias_p, bias_s, init_s,
                        (cache_k[l], cache_v[l], cache_idx_k[l], page_table))
        st_p.append(sp)
        st_s.append(ss)
    stack = lambda sts, k: jnp.stack([s[k] for s in sts])
    return (xp, xs) + tuple(stack(st_p, k) for k in range(7)) + tuple(stack(st_s, k) for k in range(7))
```

```python
import functools
import math

import numpy as np
import jax
import jax.numpy as jnp
from jax import lax
from jax.experimental import pallas as pl
from jax.experimental.pallas import tpu as pltpu

F32 = jnp.float32
BF16 = jnp.bfloat16

M_HEADS = 4
CONV_W = 4
M_CHUNK = 64
A_HEADS = 8
A_KV_HEADS = 2
IDX_HEADS = 8
IDX_DIM = 64
TOPK_MAX = 256
Q_BLOCK = 128
PAGE_SIZE = 128
N_BUCKETS = 32
MAX_DISTANCE = 128
N_MOD = 9
EPS = 1e-6
NEG = -1e30

LANE = 128
SUBLANE = 8
ROW_TILE = 512
VMEM_LIMIT = 56 * 1024 * 1024

TAIL_KI = 0
TAIL_WI = IDX_DIM
TAIL_IG = IDX_DIM + IDX_HEADS
TAIL_FG = TAIL_IG + M_HEADS

INT_MIN = -2 ** 31
LOG2E = math.log2(math.e)


def _params(sem, vmem=None):
    return pltpu.CompilerParams(dimension_semantics=sem, vmem_limit_bytes=vmem)


def _dot(a, b):
    return jnp.dot(a, b, preferred_element_type=F32)


def _dot_nt(a, b):
    return lax.dot_general(a, b, (((1,), (1,)), ((), ())), preferred_element_type=F32)


def _dot_tn(a, b):
    return lax.dot_general(a, b, (((0,), (0,)), ((), ())), preferred_element_type=F32)


def _rms(x, g):
    return x * lax.rsqrt(jnp.mean(x * x, axis=-1, keepdims=True) + EPS) * g


def _row_tile(n):
    return ROW_TILE if n % ROW_TILE == 0 else n


REDUCE_CHAINS = 8


def _reduce_rows(x, op):
    n, c = x.shape
    groups = n // SUBLANE
    chains = REDUCE_CHAINS if groups % REDUCE_CHAINS == 0 else 1
    x3 = x.reshape(groups, SUBLANE, c)
    per = groups // chains
    parts = [op(x3[k * per:(k + 1) * per], axis=0) for k in range(chains)]
    while len(parts) > 1:
        parts = [op(jnp.stack(parts[k:k + 2]), axis=0) for k in range(0, len(parts), 2)]
    return op(parts[0], axis=0, keepdims=True)


def _ada_kernel(c_ref, w_ref, b_ref, o_ref):
    o_ref[...] = _dot(c_ref[...].astype(BF16), w_ref[...].astype(BF16)) + b_ref[...]


def _ada(c_all, w_ada, b_ada):
    r, d = c_all.shape
    n = w_ada.shape[1]
    tn = 1024 if n % 1024 == 0 else n
    return pl.pallas_call(
        _ada_kernel,
        grid=(n // tn,),
        in_specs=[pl.BlockSpec((r, d), lambda j: (0, 0)),
                  pl.BlockSpec((d, tn), lambda j: (0, j)),
                  pl.BlockSpec((1, tn), lambda j: (0, j))],
        out_specs=pl.BlockSpec((r, tn), lambda j: (0, j)),
        out_shape=jax.ShapeDtypeStruct((r, n), F32),
        compiler_params=_params(("arbitrary",), VMEM_LIMIT),
        name="ada",
    )(c_all, w_ada, b_ada.reshape(1, n))


def _ffn_kernel(x_ref, sh_ref, sc_ref, gt_ref, g_ref, wg_ref, wu_ref, wd_ref, o_ref, *rest):
    *bf16_out, h_ref, acc_ref = rest
    j = pl.program_id(1)

    @pl.when(j == 0)
    def _():
        h_ref[...] = (_rms(x_ref[...], g_ref[...]) * (1.0 + sc_ref[...]) + sh_ref[...]).astype(BF16)
        acc_ref[...] = jnp.zeros_like(acc_ref)

    wg, wu, wd = wg_ref[...], wu_ref[...], wd_ref[...]
    if bf16_out:
        wg, wu, wd = wg.astype(BF16), wu.astype(BF16), wd.astype(BF16)
        for ref, w in zip(bf16_out, (wg, wu, wd)):
            ref[...] = w
    h = h_ref[...]
    a = _dot(h, wg)
    u = _dot(h, wu)
    act = (a * jax.nn.sigmoid(a) * u).astype(BF16)
    acc_ref[...] += _dot(act, wd)

    @pl.when(j == pl.num_programs(1) - 1)
    def _():
        o_ref[...] = x_ref[...] + 0.5 * gt_ref[...] * acc_ref[...]


def _mod_spec(mod, m, tm, d):
    if mod.ndim == 2:
        return pl.BlockSpec((tm, d), lambda i, j: (i, 0))
    tiles_per_seq = (m // mod.shape[0]) // tm
    return pl.BlockSpec((None, 1, d), lambda i, j: (i // tiles_per_seq, 0, 0))


def _ffn(x, sh, sc, gt, g, wg, wu, wd):
    m, d = x.shape
    f = wg.shape[1]
    tm = _row_tile(m)
    emit = wg.dtype != BF16
    assert not emit or m == tm
    tf = (256 if emit else 512) if f % 512 == 0 else f
    ms = _mod_spec(sh, m, tm, d)
    w_specs = [pl.BlockSpec((d, tf), lambda i, j: (0, j)),
               pl.BlockSpec((d, tf), lambda i, j: (0, j)),
               pl.BlockSpec((tf, d), lambda i, j: (j, 0))]
    out_specs = [pl.BlockSpec((tm, d), lambda i, j: (i, 0))]
    out_shape = [jax.ShapeDtypeStruct((m, d), F32)]
    if emit:
        out_specs += w_specs
        out_shape += [jax.ShapeDtypeStruct(w.shape, BF16) for w in (wg, wu, wd)]
    y, *wb = pl.pallas_call(
        _ffn_kernel,
        grid=(m // tm, f // tf),
        in_specs=[pl.BlockSpec((tm, d), lambda i, j: (i, 0)), ms, ms, ms,
                  pl.BlockSpec((1, d), lambda i, j: (0, 0))] + w_specs,
        out_specs=out_specs,
        out_shape=out_shape,
        scratch_shapes=[pltpu.VMEM((tm, d), BF16), pltpu.VMEM((tm, d), F32)],
        compiler_params=_params(("parallel", "arbitrary"), VMEM_LIMIT),
        name="ffn",
    )(x, sh, sc, gt, g.reshape(1, d), wg, wu, wd)
    return y, (tuple(wb) if emit else (wg, wu, wd))


def _proj_kernel(x_ref, sh_ref, sc_ref, g_ref, w_ref, o_ref, h_ref):
    @pl.when(pl.program_id(1) == 0)
    def _():
        h_ref[...] = (_rms(x_ref[...], g_ref[...]) * (1.0 + sc_ref[...]) + sh_ref[...]).astype(BF16)

    o_ref[...] = _dot(h_ref[...], w_ref[...])


def _proj(x, sh, sc, g, w):
    m, d = x.shape
    n = w.shape[1]
    tm = _row_tile(m)
    tn = 7 * LANE
    assert n % tn == 0
    ms = _mod_spec(sh, m, tm, d)
    return pl.pallas_call(
        _proj_kernel,
        grid=(m // tm, n // tn),
        in_specs=[pl.BlockSpec((tm, d), lambda i, j: (i, 0)), ms, ms,
                  pl.BlockSpec((1, d), lambda i, j: (0, 0)),
                  pl.BlockSpec((d, tn), lambda i, j: (0, j))],
        out_specs=pl.BlockSpec((tm, tn), lambda i, j: (i, j)),
        out_shape=jax.ShapeDtypeStruct((m, n), F32),
        scratch_shapes=[pltpu.VMEM((tm, d), BF16)],
        compiler_params=_params(("parallel", "arbitrary"), VMEM_LIMIT),
        name="proj",
    )(x, sh, sc, g.reshape(1, d), w)


def _pack_w_in(w_in, mw, aw, kvw):
    d = w_in.shape[0]
    o = np.cumsum([0, mw, mw, mw, mw, M_HEADS, M_HEADS, aw, kvw, kvw, IDX_HEADS * IDX_DIM, IDX_DIM, IDX_HEADS])
    seg = lambda k: w_in[:, o[k]:o[k + 1]].astype(BF16)
    order = (0, 1, 2, 3, 6, 7, 8, 9, 10, 11, 4, 5)
    pad = (-o[-1]) % LANE
    return jnp.concatenate([seg(k) for k in order] + [jnp.zeros((d, pad), BF16)], axis=1)


def _log_sigmoid(x):
    return jnp.minimum(x, 0.0) - jnp.log1p(jnp.exp(-jnp.abs(x)))


def _mlstm_kernel(qk_ref, v_ref, o_ref, tail_ref, prev_ref, cw_ref, gb_ref, og_ref, c0_ref, n0_ref, m0_ref,
                  mo_ref, cst_ref, nst_ref, mst_ref, xbuf_ref, *, L, valid, dk, dv):
    @pl.when(pl.program_id(1) == 0)
    def _():
        xbuf_ref[:, 0:SUBLANE, :] = prev_ref[...]
        cst_ref[...] = c0_ref[...]
        nst_ref[...] = n0_ref[...]
        mst_ref[...] = m0_ref[...]

    for sb in range(qk_ref.shape[0]):
        _mlstm_one(qk_ref.at[sb], v_ref.at[sb], o_ref.at[sb], tail_ref.at[sb], cw_ref, gb_ref, og_ref, mo_ref.at[sb],
                   cst_ref.at[sb], nst_ref.at[sb], mst_ref.at[sb], xbuf_ref.at[sb], L=L, valid=valid, dk=dk, dv=dv)


def _mlstm_one(qk_ref, v_ref, o_ref, tail_ref, cw_ref, gb_ref, og_ref, mo_ref, cst_ref, nst_ref, mst_ref, xbuf_ref,
               *, L, valid, dk, dv):
    width = M_HEADS * dk
    xbuf_ref[SUBLANE:SUBLANE + L, :] = qk_ref[...]
    base = SUBLANE - (CONV_W - 1)
    y = xbuf_ref[base:base + L, :] * cw_ref[0:1, :]
    for j in range(1, CONV_W):
        y = y + xbuf_ref[base + j:base + j + L, :] * cw_ref[j:j + 1, :]
    tail_rows = xbuf_ref[L:L + SUBLANE, :]
    xbuf_ref[0:SUBLANE, :] = tail_rows
    qk = y * jax.nn.sigmoid(y)

    gates = tail_ref[...] + gb_ref[...]
    row = lax.broadcasted_iota(jnp.int32, (L, LANE), 0)
    lf = _log_sigmoid(gates)
    ig = gates
    if valid < L:
        lf = jnp.where(row < valid, lf, 0.0)
        ig = jnp.where(row < valid, ig, -jnp.inf)
    bcum = lf
    s = 1
    while s < L:
        bcum = bcum + jnp.where(row >= s, pltpu.roll(bcum, s, axis=0), 0.0)
        s *= 2
    bmi_t = (pltpu.roll(ig, TAIL_FG - TAIL_IG, axis=1) - bcum).T

    ri = lax.broadcasted_iota(jnp.int32, (L, L), 0)
    ci = lax.broadcasted_iota(jnp.int32, (L, L), 1)
    causal = ri >= ci

    for h in range(M_HEADS):
        li, lfh = TAIL_IG + h, TAIL_FG + h
        a_col = bcum[:, lfh:lfh + 1]
        ig_col = ig[:, li:li + 1]
        b_row = bmi_t[lfh:lfh + 1, :]
        m0 = mst_ref[:, h:h + 1]
        d_log = jnp.where(causal, a_col + b_row, -jnp.inf)
        s_log = a_col + m0
        m_col = jnp.maximum(s_log, jnp.max(d_log, axis=-1, keepdims=True))
        dw = jnp.exp(d_log - m_col)
        sw = jnp.exp(s_log - m_col)

        q = qk[:, h * dk:(h + 1) * dk]
        k = qk[:, width + h * dk:width + (h + 1) * dk] * (dk ** -0.5)
        v = v_ref[:, h * dv:(h + 1) * dv]
        qb, kb = q.astype(BF16), k.astype(BF16)
        c_prev = cst_ref[h]
        n_prev = nst_ref[h:h + 1, :]
        scores = _dot_nt(qb, kb) * dw
        num = _dot(scores.astype(BF16), v.astype(BF16)) + sw * _dot(qb, c_prev.astype(BF16))
        den = jnp.sum(scores, axis=-1, keepdims=True) + sw * jnp.sum(q * n_prev, axis=-1, keepdims=True)
        hh = num / jnp.maximum(jnp.abs(den), jnp.exp(-m_col))

        m_new = m_col[L - 1:L, :]
        a_last = a_col[L - 1:L, :]
        wl = jnp.exp(a_last - a_col + ig_col - m_new)
        decay = jnp.exp(a_last + m0 - m_new)
        cst_ref[h] = decay * c_prev + _dot_tn(kb, (wl * v).astype(BF16))
        nst_ref[h:h + 1, :] = decay * n_prev + jnp.sum(wl * k, axis=0, keepdims=True)
        mst_ref[:, h:h + 1] = m_new

        hn = hh * lax.rsqrt(jnp.mean(hh * hh, axis=-1, keepdims=True) + EPS) * og_ref[:, h * dv:(h + 1) * dv]
        gate = jax.nn.sigmoid(o_ref[:, h * dv:(h + 1) * dv])
        mo_ref[:, h * dv:(h + 1) * dv] = (hn * gate).astype(mo_ref.dtype)


def _mlstm(p3, prev8, conv_w, gate_row, out_g, c0, n0, m0, *, L, valid, seqs_per_step, out_dtype):
    n_seq, t_all, npk = p3.shape
    nc = t_all // L
    sb = seqs_per_step
    assert n_seq % sb == 0
    dk = c0.shape[2]
    dv = c0.shape[3]
    mw = M_HEADS * dv
    qkw = 2 * M_HEADS * dk
    assert qkw % mw == 0 and (2 * qkw) % mw == 0
    kern = functools.partial(_mlstm_kernel, L=L, valid=valid, dk=dk, dv=dv)
    seq = lambda shape: pl.BlockSpec((sb,) + shape, lambda b, c: (b,) + (0,) * len(shape))
    return pl.pallas_call(
        kern,
        grid=(n_seq // sb, nc),
        in_specs=[pl.BlockSpec((sb, L, qkw), lambda b, c: (b, c, 0)),
                  pl.BlockSpec((sb, L, mw), lambda b, c: (b, c, qkw // mw)),
                  pl.BlockSpec((sb, L, mw), lambda b, c: (b, c, qkw // mw + 1)),
                  pl.BlockSpec((sb, L, LANE), lambda b, c: (b, c, npk // LANE - 1)),
                  seq((SUBLANE, qkw)),
                  pl.BlockSpec((CONV_W, qkw), lambda b, c: (0, 0)),
                  pl.BlockSpec((1, LANE), lambda b, c: (0, 0)),
                  pl.BlockSpec((1, mw), lambda b, c: (0, 0)),
                  seq((M_HEADS, dk, dv)), seq((M_HEADS, dk)), seq((1, LANE))],
        out_specs=[pl.BlockSpec((sb, L, mw), lambda b, c: (b, c, 0)),
                   seq((M_HEADS, dk, dv)), seq((M_HEADS, dk)), seq((1, LANE))],
        out_shape=[jax.ShapeDtypeStruct((n_seq, t_all, mw), out_dtype),
                   jax.ShapeDtypeStruct(c0.shape, F32),
                   jax.ShapeDtypeStruct(n0.shape, F32),
                   jax.ShapeDtypeStruct(m0.shape, F32)],
        scratch_shapes=[pltpu.VMEM((sb, SUBLANE + L, qkw), F32)],
        compiler_params=_params(("parallel", "arbitrary"), VMEM_LIMIT),
        name="mlstm",
    )(p3, p3, p3, p3, prev8, conv_w, gate_row, out_g.reshape(1, mw), c0, n0, m0)


def _bucket_np(dist):
    me = N_BUCKETS // 2
    d = np.maximum(dist, 0)
    ratio = np.log(np.maximum(d, 1).astype(np.float64) / me) / math.log(MAX_DISTANCE / me)
    large = np.minimum(me + (ratio * (N_BUCKETS - me)).astype(np.int64), N_BUCKETS - 1)
    return np.where(d < me, d, large).astype(np.int32)


def _bias_kernel(t5_ref, bkp_ref, bks_ref, vs_ref, far_ref, op_ref, os_ref):
    rep = A_HEADS // A_KV_HEADS

    def lookup(bk, h):
        acc = jnp.zeros(bk.shape, F32)
        for b in range(N_BUCKETS):
            acc = jnp.where(bk == b, t5_ref[b, h], acc)
        return acc

    for h in range(A_HEADS):
        half = (h % 2) * Q_BLOCK
        op_ref[h // 2, :, half:half + Q_BLOCK] = (lookup(bkp_ref[...], h) - t5_ref[far_ref[0], h]) * LOG2E
        os_ref[h] = jnp.where(vs_ref[h // rep] > 0, lookup(bks_ref[...], h), NEG)


def _bias_tables(t5_bias, t_prompt, past, t_new):
    c = np.arange(2 * Q_BLOCK)[:, None]
    r = np.arange(Q_BLOCK)[None, :]
    bkp = _bucket_np(Q_BLOCK + r - c)
    far = _bucket_np(np.arange(Q_BLOCK + 1, max(t_prompt, past + t_new) + Q_BLOCK))
    assert (far == far[0]).all(), "bias must be constant beyond one query block"
    nk = past + LANE
    tq = np.arange(SUBLANE)[:, None]
    col = np.arange(2 * nk)[None, :]
    s_pos, s_grp = col // A_KV_HEADS, col % A_KV_HEADS
    bks = _bucket_np(past + tq - s_pos)
    vis = np.stack([(s_grp == g) & (s_pos - past <= tq) for g in range(A_KV_HEADS)]).astype(np.int32)
    return pl.pallas_call(
        _bias_kernel,
        in_specs=[pl.BlockSpec(memory_space=pltpu.SMEM),
                  pl.BlockSpec(memory_space=pltpu.VMEM),
                  pl.BlockSpec(memory_space=pltpu.VMEM),
                  pl.BlockSpec(memory_space=pltpu.VMEM),
                  pl.BlockSpec(memory_space=pltpu.SMEM)],
        out_specs=[pl.BlockSpec(memory_space=pltpu.VMEM), pl.BlockSpec(memory_space=pltpu.VMEM)],
        out_shape=[jax.ShapeDtypeStruct((A_HEADS // 2, 2 * Q_BLOCK, 2 * Q_BLOCK), F32),
                   jax.ShapeDtypeStruct((A_HEADS, SUBLANE, 2 * nk), F32)],
        name="t5_bias_tables",
    )(t5_bias, jnp.asarray(bkp), jnp.asarray(bks), jnp.asarray(vis), jnp.asarray(far[:1]))


def _sort_key(score):
    bits = lax.bitcast_convert_type(score, jnp.int32)
    return jnp.where(bits < 0, bits ^ jnp.int32(0x7FFFFFFF), bits)


def _kth_key(count_ge, shape, k_sel):
    def body(it, ans_u):
        cand_u = ans_u | lax.shift_left(jnp.int32(1), jnp.int32(31) - it)
        cnt = count_ge(cand_u ^ jnp.int32(INT_MIN))
        return jnp.where(cnt >= k_sel, cand_u, ans_u)

    ans_u = lax.fori_loop(0, 32, body, jnp.zeros(shape, jnp.int32))
    return ans_u ^ jnp.int32(INT_MIN)


def _triangle(below):
    r = lax.broadcasted_iota(jnp.int32, (LANE, LANE), 0)
    c = lax.broadcasted_iota(jnp.int32, (LANE, LANE), 1)
    return jnp.where((c < r) if below else (r < c), 1.0, 0.0).astype(BF16)


def _select_tiles(key_tiles, thr, k_sel, key_axis):
    count = lambda x: jnp.sum(x, axis=key_axis, keepdims=True)
    c_gt = jnp.zeros(thr.shape, F32)
    for kt in key_tiles:
        c_gt = c_gt + count(jnp.where(kt > thr, 1.0, 0.0))
    need = k_sel - c_gt
    tri = _triangle(below=(key_axis == 0))
    run = jnp.zeros(thr.shape, F32)
    out = []
    for kt in key_tiles:
        eq = jnp.where(kt == thr, 1.0, 0.0)
        eqb = eq.astype(BF16)
        before = (_dot(tri, eqb) if key_axis == 0 else _dot(eqb, tri)) + run
        out.append(jnp.where(kt > thr, 1.0, jnp.where(before < need, eq, 0.0)))
        run = run + count(eq)
    return out


FAR_STEP = 4


def _dsa_prompt_kernel(qa_ref, qi_ref, qtail_ref, k_ref, v_ref, ktail_ref, qg_ref, kg_ref, bias_ref,
                       ao_ref, kn_ref,
                       kb_ref, vt_ref, kib_ref, qn_ref, key_ref, negb_ref, *, T, k_sel, hd):
    i = pl.program_id(1)
    nb = T // Q_BLOCK
    rep = A_HEADS // A_KV_HEADS
    scale = hd ** -0.5
    near_w = 2 * LANE

    @pl.when(i == 0)
    def _():
        kb_ref[0:LANE, :] = jnp.zeros((LANE, kb_ref.shape[1]), BF16)
        vt_ref[:, 0:LANE] = jnp.zeros((vt_ref.shape[0], LANE), BF16)
        kib_ref[0:LANE, :] = jnp.zeros((LANE, IDX_DIM), BF16)
        for g in range(A_KV_HEADS):
            kn = _rms(k_ref[:, g * hd:(g + 1) * hd], kg_ref[...])
            kn_ref[:, g * hd:(g + 1) * hd] = kn
            kb_ref[LANE:, g * hd:(g + 1) * hd] = kn.astype(BF16)
        vt_ref[:, LANE:] = v_ref[...].T.astype(BF16)
        kib_ref[LANE:, :] = ktail_ref[:, TAIL_KI:TAIL_KI + IDX_DIM].astype(BF16)

    wi_t = qtail_ref[...].T[TAIL_WI:TAIL_WI + IDX_HEADS, :] * (IDX_HEADS ** -0.5) * (IDX_DIM ** -0.5)
    qi_pairs = [jnp.concatenate([qi_ref[:, (2 * a + hh) * IDX_DIM:(2 * a + hh + 1) * IDX_DIM] for hh in range(2)],
                                axis=0).astype(BF16) for a in range(IDX_HEADS // 2)]
    for h in range(A_HEADS):
        half = (h % 2) * Q_BLOCK
        qn = _rms(qa_ref[:, h * hd:(h + 1) * hd], qg_ref[...]) * (scale * LOG2E)
        qn_ref[h // 2, half:half + Q_BLOCK, :] = qn.astype(BF16)

    near0 = pl.multiple_of(i * LANE, LANE)
    ki_near = kib_ref[pl.ds(near0, near_w), :]
    k_near = kb_ref[pl.ds(near0, near_w), :]
    vt_near = vt_ref[:, pl.ds(near0, near_w)]
    kc = lax.broadcasted_iota(jnp.int32, (near_w, Q_BLOCK), 0)
    qr = lax.broadcasted_iota(jnp.int32, (near_w, Q_BLOCK), 1)
    near_ok = jnp.where(kc <= qr + LANE, 1.0, 0.0) * jnp.maximum(jnp.where(kc >= LANE, 1.0, 0.0),
                                                                  (i >= 1).astype(F32))

    def scores(ki):
        sc = jnp.zeros((ki.shape[0], Q_BLOCK), F32)
        for a in range(IDX_HEADS // 2):
            s2 = _dot_nt(ki, qi_pairs[a])
            for hh in range(2):
                j = 2 * a + hh
                sc = sc + wi_t[j:j + 1, :] * jnp.maximum(s2[:, hh * Q_BLOCK:(hh + 1) * Q_BLOCK], 0.0)
        return sc

    def body(wf):
        fw = wf * LANE
        w_all = fw + near_w
        key_ref[fw:w_all, :] = _sort_key(jnp.where(near_ok > 0.0, scores(ki_near) + 0.0, NEG))
        if wf:
            far_ok = lax.broadcasted_iota(jnp.int32, (fw, Q_BLOCK), 0) < (i - 1) * LANE
            key_ref[0:fw, :] = _sort_key(jnp.where(far_ok, scores(kib_ref[LANE:LANE + fw, :]) + 0.0, NEG))

        def count_ge(cand):
            return _reduce_rows(jnp.where(key_ref[0:w_all, :] >= cand, 1.0, 0.0), jnp.sum)

        thr = _kth_key(count_ge, (1, Q_BLOCK), float(k_sel))
        sel = _select_tiles([key_ref[t * LANE:(t + 1) * LANE, :] for t in range(wf + 2)], thr, float(k_sel), 0)
        for t in range(wf):
            ok = sel[t] * (t < i - 1).astype(F32)
            negb_ref[t * LANE:(t + 1) * LANE, :] = jnp.where(ok > 0.0, 0.0, NEG)
        for t in range(2):
            ok = sel[wf + t] * near_ok[t * LANE:(t + 1) * LANE, :]
            negb_ref[fw + t * LANE:fw + (t + 1) * LANE, :] = jnp.where(ok > 0.0, 0.0, NEG)

        for a in range(A_HEADS // 2):
            g = (2 * a) // rep
            cs = slice(g * hd, (g + 1) * hd)
            qn2 = qn_ref[a]
            nb_near = negb_ref[fw:w_all, :]
            s_near = _dot_nt(k_near[:, cs], qn2) + bias_ref[a] + jnp.concatenate([nb_near, nb_near], axis=1)
            m = _reduce_rows(s_near, jnp.max)
            if wf:
                nb_far = negb_ref[0:fw, :]
                s_far = _dot_nt(kb_ref[LANE:LANE + fw, cs], qn2) + jnp.concatenate([nb_far, nb_far], axis=1)
                m = jnp.maximum(m, _reduce_rows(s_far, jnp.max))
            p_near = jnp.exp2(s_near - m)
            l = _reduce_rows(p_near, jnp.sum)
            acc = _dot(vt_near[cs, :], p_near.astype(BF16))
            if wf:
                p_far = jnp.exp2(s_far - m)
                l = l + _reduce_rows(p_far, jnp.sum)
                acc = acc + _dot(vt_ref[cs, LANE:LANE + fw], p_far.astype(BF16))
            out_t = acc / l
            for hh in range(2):
                h = 2 * a + hh
                ao_ref[:, h * hd:(h + 1) * hd] = out_t[:, hh * Q_BLOCK:(hh + 1) * Q_BLOCK].T.astype(ao_ref.dtype)

    variants = sorted({min(FAR_STEP * -(-x // FAR_STEP), nb) for x in range(max(nb - 1, 1))})
    wf_needed = jnp.minimum((jnp.maximum(i - 1, 0) + FAR_STEP - 1) // FAR_STEP * FAR_STEP, nb)
    for wf in variants:
        pl.when(wf_needed == wf)(functools.partial(body, wf))


def _dsa_prompt(p, qn_g, kn_g, bias_p, *, n_seq, T, hd):
    rows = p.shape[0]
    nb = T // Q_BLOCK
    aw = A_HEADS * hd
    kvw = A_KV_HEADS * hd
    iw = IDX_HEADS * IDX_DIM
    k_sel = min(TOPK_MAX, T // 4)
    mcols = p.shape[1] - LANE - iw - 2 * kvw - aw
    assert mcols % aw == 0 and (mcols + aw) % kvw == 0 and (mcols + aw + 2 * kvw) % iw == 0
    qa_blk = mcols // aw
    k_blk = (mcols + aw) // kvw
    qi_blk = (mcols + aw + 2 * kvw) // iw
    tail_blk = p.shape[1] // LANE - 1
    kern = functools.partial(_dsa_prompt_kernel, T=T, k_sel=k_sel, hd=hd)
    return pl.pallas_call(
        kern,
        grid=(n_seq, nb),
        in_specs=[pl.BlockSpec((Q_BLOCK, aw), lambda b, i: (b * nb + i, qa_blk)),
                  pl.BlockSpec((Q_BLOCK, iw), lambda b, i: (b * nb + i, qi_blk)),
                  pl.BlockSpec((Q_BLOCK, LANE), lambda b, i: (b * nb + i, tail_blk)),
                  pl.BlockSpec((T, kvw), lambda b, i: (b, k_blk)),
                  pl.BlockSpec((T, kvw), lambda b, i: (b, k_blk + 1)),
                  pl.BlockSpec((T, LANE), lambda b, i: (b, tail_blk)),
                  pl.BlockSpec((1, hd), lambda b, i: (0, 0)),
                  pl.BlockSpec((1, hd), lambda b, i: (0, 0)),
                  pl.BlockSpec((A_HEADS // 2, 2 * Q_BLOCK, 2 * Q_BLOCK), lambda b, i: (0, 0, 0))],
        out_specs=[pl.BlockSpec((Q_BLOCK, aw), lambda b, i: (b * nb + i, 0)),
                   pl.BlockSpec((T, kvw), lambda b, i: (b, 0))],
        out_shape=[jax.ShapeDtypeStruct((rows, aw), BF16),
                   jax.ShapeDtypeStruct((rows, kvw), F32)],
        scratch_shapes=[pltpu.VMEM((T + LANE, kvw), BF16), pltpu.VMEM((kvw, T + LANE), BF16),
                        pltpu.VMEM((T + LANE, IDX_DIM), BF16),
                        pltpu.VMEM((A_HEADS // 2, 2 * Q_BLOCK, hd), BF16),
                        pltpu.VMEM((T + 2 * LANE, Q_BLOCK), jnp.int32),
                        pltpu.VMEM((T + 2 * LANE, Q_BLOCK), F32)],
        compiler_params=_params(("parallel", "arbitrary"), VMEM_LIMIT),
        name="dsa_prompt",
    )(p, p, p, p, p, p, qn_g.reshape(1, hd), kn_g.reshape(1, hd), bias_p)


def _pad_rows(x, n):
    return jnp.concatenate([x, jnp.zeros((n - x.shape[0], x.shape[1]), x.dtype)], axis=0)


def _idx_sample_kernel(pt_ref, qi_ref, wi_ref, kinew_ref, *rest, n_pages, t_new):
    ip, sc_ref = rest[:n_pages], rest[n_pages]
    del pt_ref
    qi = qi_ref[...].astype(BF16)
    wi = wi_ref[...] * (IDX_HEADS ** -0.5) * (IDX_DIM ** -0.5)
    tq = lax.broadcasted_iota(jnp.int32, (SUBLANE, LANE), 0)
    lane = lax.broadcasted_iota(jnp.int32, (SUBLANE, LANE), 1)
    for t in range(n_pages + 1):
        kt = ip[t][...].astype(BF16) if t < n_pages else _pad_rows(kinew_ref[...], LANE).astype(BF16)
        r = _dot_nt(qi, kt)
        sc = jnp.zeros((SUBLANE, LANE), F32)
        for j in range(IDX_HEADS):
            sc = sc + wi[j * SUBLANE:(j + 1) * SUBLANE, :] * jnp.maximum(r[j * SUBLANE:(j + 1) * SUBLANE, :], 0.0)
        sc = sc + 0.0
        if t == n_pages:
            sc = jnp.where(lane <= tq, sc, jnp.where(lane < t_new, NEG, -jnp.inf))
        sc_ref[:, t * LANE:(t + 1) * LANE] = sc


def _select_kernel(sc_ref, sel_ref, key_ref, *, k_sel):
    rows, nk = sc_ref.shape
    key_ref[...] = _sort_key(sc_ref[...])

    def count_ge(cand):
        return jnp.sum(jnp.where(key_ref[...] >= cand, 1.0, 0.0), axis=1, keepdims=True)

    thr = _kth_key(count_ge, (rows, 1), float(k_sel))
    sel = _select_tiles([key_ref[:, t * LANE:(t + 1) * LANE] for t in range(nk // LANE)], thr, float(k_sel), 1)
    for t in range(nk // LANE):
        sel_ref[:, t * LANE:(t + 1) * LANE] = sel[t]


def _attn_sample_kernel(pt_ref, qa_ref, sel_ref, knew_ref, vnew_ref, qg_ref, kg_ref, bias_ref, *rest,
                        n_pages, hd):
    kp, vp = rest[:n_pages], rest[n_pages:2 * n_pages]
    ao_ref, kn_ref = rest[2 * n_pages:]
    del pt_ref
    nt = n_pages + 1
    pw = A_KV_HEADS * PAGE_SIZE
    scale = hd ** -0.5
    rows_q = qa_ref.shape[0]

    kn_new = _rms(knew_ref[...], kg_ref[...])
    kn_ref[...] = kn_new
    k_new = _pad_rows(kn_new, pw).astype(BF16)
    v_new = _pad_rows(vnew_ref[...], pw).astype(BF16)
    qn = _rms(qa_ref[...], qg_ref[...]).astype(BF16)

    tok = lax.broadcasted_iota(jnp.int32, (PAGE_SIZE, pw), 0)
    col = lax.broadcasted_iota(jnp.int32, (PAGE_SIZE, pw), 1)
    spread = jnp.where(col // A_KV_HEADS == tok, 1.0, 0.0).astype(BF16)
    sel_rows = jnp.concatenate([sel_ref[:, t * LANE:(t + 1) * LANE] for t in range(nt)], axis=0)
    negb = (_dot(sel_rows.astype(BF16), spread) - 1.0) * (-NEG)

    logits = []
    for t in range(nt):
        kt = kp[t][...].astype(BF16) if t < n_pages else k_new
        nb = jnp.concatenate([negb[t * SUBLANE:(t + 1) * SUBLANE, :]] * (rows_q // SUBLANE), axis=0)
        logits.append(_dot_nt(qn, kt) * scale + bias_ref[:, t * pw:(t + 1) * pw] + nb)
    m = logits[0].max(axis=-1, keepdims=True)
    for t in range(1, nt):
        m = jnp.maximum(m, logits[t].max(axis=-1, keepdims=True))
    l = jnp.zeros((rows_q, 1), F32)
    acc = jnp.zeros((rows_q, hd), F32)
    for t in range(nt):
        pr = jnp.exp(logits[t] - m)
        l = l + jnp.sum(pr, axis=-1, keepdims=True)
        acc = acc + _dot(pr.astype(BF16), vp[t][...].astype(BF16) if t < n_pages else v_new)
    ao_ref[...] = acc / l


def _dsa_sample(qa, qi, wi, knew, vnew, kinew, qn_g, kn_g, bias_s, cache_k, cache_v, cache_i, page_table, *, t_new, hd):
    n_seq, n_pages = page_table.shape
    nk = n_pages * PAGE_SIZE + LANE
    k_sel = min(TOPK_MAX, (n_pages * PAGE_SIZE + t_new) // 4)
    pt = page_table.reshape(-1)

    def page_spec(arr, t):
        return pl.BlockSpec((None,) + arr.shape[1:], lambda b, pt: (pt[b * n_pages + t], 0, 0))

    seq_spec = lambda arr: pl.BlockSpec((None,) + arr.shape[1:], lambda b, pt: (b, 0, 0))
    const_spec = lambda arr: pl.BlockSpec(arr.shape, lambda b, pt: (0, 0))

    scores = pl.pallas_call(
        functools.partial(_idx_sample_kernel, n_pages=n_pages, t_new=t_new),
        grid_spec=pltpu.PrefetchScalarGridSpec(
            num_scalar_prefetch=1, grid=(n_seq,),
            in_specs=[seq_spec(qi), seq_spec(wi), seq_spec(kinew)] + [page_spec(cache_i, t) for t in range(n_pages)],
            out_specs=pl.BlockSpec((SUBLANE, nk), lambda b, pt: (b, 0))),
        out_shape=jax.ShapeDtypeStruct((n_seq * SUBLANE, nk), F32),
        compiler_params=_params(("arbitrary",), VMEM_LIMIT),
        name="idx_sample",
    )(pt, qi, wi, kinew, *([cache_i] * n_pages))

    rows = scores.shape[0]
    tr = LANE if rows % LANE == 0 else rows
    sel = pl.pallas_call(
        functools.partial(_select_kernel, k_sel=k_sel),
        grid=(rows // tr,),
        in_specs=[pl.BlockSpec((tr, nk), lambda r: (r, 0))],
        out_specs=pl.BlockSpec((tr, nk), lambda r: (r, 0)),
        out_shape=jax.ShapeDtypeStruct((rows, nk), F32),
        scratch_shapes=[pltpu.VMEM((tr, nk), jnp.int32)],
        compiler_params=_params(("parallel",), VMEM_LIMIT),
        name="select_sample",
    )(scores)

    qg, kg = qn_g.reshape(1, hd), kn_g.reshape(1, hd)
    return pl.pallas_call(
        functools.partial(_attn_sample_kernel, n_pages=n_pages, hd=hd),
        grid_spec=pltpu.PrefetchScalarGridSpec(
            num_scalar_prefetch=1, grid=(n_seq,),
            in_specs=[seq_spec(qa), pl.BlockSpec((SUBLANE, nk), lambda b, pt: (b, 0)), seq_spec(knew), seq_spec(vnew),
                      const_spec(qg), const_spec(kg), const_spec(bias_s)]
                     + [page_spec(cache_k, t) for t in range(n_pages)]
                     + [page_spec(cache_v, t) for t in range(n_pages)],
            out_specs=[seq_spec(qa), seq_spec(knew)]),
        out_shape=[jax.ShapeDtypeStruct(qa.shape, F32), jax.ShapeDtypeStruct(knew.shape, F32)],
        compiler_params=_params(("arbitrary",), VMEM_LIMIT),
        name="attn_sample",
    )(pt, qa, sel, knew, vnew, qg, kg, bias_s, *([cache_k] * n_pages), *([cache_v] * n_pages))


def _outproj_kernel(x_ref, mo_ref, ao_ref, gt_ref, wm_ref, wa_ref, o_ref):
    y = _dot(mo_ref[...].astype(BF16), wm_ref[...]) + _dot(ao_ref[...].astype(BF16), wa_ref[...])
    o_ref[...] = x_ref[...] + gt_ref[...] * y


def _outproj(x, mo, ao, gt, w_out):
    m, d = x.shape
    mw, aw = mo.shape[1], ao.shape[1]
    tm = _row_tile(m)
    ms = _mod_spec(gt, m, tm, d)
    row = lambda w: pl.BlockSpec((tm, w), lambda i, j: (i, 0))
    return pl.pallas_call(
        _outproj_kernel,
        grid=(m // tm, 1),
        in_specs=[row(d), row(mw), row(aw), ms,
                  pl.BlockSpec((mw, d), lambda i, j: (0, 0)),
                  pl.BlockSpec((aw, d), lambda i, j: (0, 0))],
        out_specs=row(d),
        out_shape=jax.ShapeDtypeStruct((m, d), F32),
        compiler_params=_params(("parallel", "arbitrary"), VMEM_LIMIT),
        name="outproj",
    )(x, mo, ao, gt, w_out[:mw], w_out[mw:])


def _pad_tokens(a, t_pad):
    pad = [(0, 0)] * a.ndim
    pad[1] = (0, t_pad - a.shape[1])
    return jnp.pad(a, pad)


def _layer(x3, mods, lw, bias_p, bias_s, rec_state, past):
    n_seq, T, d = x3.shape
    x = x3.reshape(n_seq * T, d)
    dk, dv = rec_state[0].shape[2], rec_state[0].shape[3]
    mw = M_HEADS * dv
    hd = lw["q_norm_g"].shape[0]
    aw, kvw = A_HEADS * hd, A_KV_HEADS * hd
    iw = IDX_HEADS * IDX_DIM
    per_seq = T % ROW_TILE == 0
    if per_seq:
        mod = lambda k: mods[:, k:k + 1, :]
    else:
        mod = lambda k: jnp.repeat(mods[:, k, :], T, axis=0)
    sh1, sc1, g1, sh2, sc2, g2, sh3, sc3, g3 = [mod(k) for k in range(N_MOD)]

    x, ffn1_w = _ffn(x, sh1, sc1, g1, lw["ffn1_norm_g"], *lw["ffn1_w"])
    p = _proj(x, sh2, sc2, lw["mix_norm_g"], lw["w_in"])
    npk = p.shape[1]
    qkw = 2 * M_HEADS * dk
    c0, n0, m0, conv_buf = rec_state
    gate_row = jnp.zeros((1, LANE), F32).at[0, TAIL_IG:TAIL_IG + 2 * M_HEADS].set(lw["mlstm_gate_b"])
    m0_pad = jnp.zeros((n_seq, 1, LANE), F32).at[:, 0, :M_HEADS].set(m0)
    prev8 = jnp.zeros((n_seq, SUBLANE, qkw), F32).at[:, SUBLANE - (CONV_W - 1):, :].set(conv_buf)
    p3 = p.reshape(n_seq, T, npk)
    a_off = qkw + 2 * mw
    v_a = p3[:, :, a_off + aw + kvw:a_off + aw + 2 * kvw]
    ki = p3[:, :, npk - LANE + TAIL_KI:npk - LANE + TAIL_KI + IDX_DIM]
    conv_new = p3[:, T - (CONV_W - 1):, :qkw]

    if past is None:
        L = M_CHUNK if T % M_CHUNK == 0 else T
        mo, c_new, n_new, m_new = _mlstm(p3, prev8, lw["mlstm_conv_w"], gate_row, lw["mlstm_out_g"], c0, n0, m0_pad,
                                         L=L, valid=L, seqs_per_step=2 if n_seq % 2 == 0 else 1, out_dtype=BF16)
        mo = mo.reshape(n_seq * T, mw)
        ao, k_n = _dsa_prompt(p, lw["q_norm_g"], lw["k_norm_g"], bias_p, n_seq=n_seq, T=T, hd=hd)
        k_n = k_n.reshape(n_seq, T, A_KV_HEADS, hd)
    else:
        tp = SUBLANE
        assert T <= tp
        mo, c_new, n_new, m_new = _mlstm(_pad_tokens(p3, tp), prev8, lw["mlstm_conv_w"], gate_row, lw["mlstm_out_g"],
                                         c0, n0, m0_pad, L=tp, valid=T,
                                         seqs_per_step=SUBLANE if n_seq % SUBLANE == 0 else 1, out_dtype=F32)
        mo = mo[:, :T].reshape(n_seq * T, mw)
        cache_k, cache_v, cache_i, page_table = past
        rep = A_HEADS // A_KV_HEADS
        qa = _pad_tokens(p3[:, :, a_off:a_off + aw], tp).reshape(n_seq, tp, A_KV_HEADS, rep, hd)
        qa = qa.transpose(0, 2, 3, 1, 4).reshape(n_seq, A_HEADS * tp, hd)
        qi = _pad_tokens(p3[:, :, a_off + aw + 2 * kvw:a_off + aw + 2 * kvw + iw], tp)
        qi = qi.reshape(n_seq, tp, IDX_HEADS, IDX_DIM).transpose(0, 2, 1, 3).reshape(n_seq, IDX_HEADS * tp, IDX_DIM)
        wi = _pad_tokens(p3[:, :, npk - LANE + TAIL_WI:npk - LANE + TAIL_WI + IDX_HEADS], tp)
        wi = wi.transpose(0, 2, 1).reshape(n_seq, IDX_HEADS * tp, 1)
        knew = _pad_tokens(p3[:, :, a_off + aw:a_off + aw + kvw], tp).reshape(n_seq, tp * A_KV_HEADS, hd)
        vnew = _pad_tokens(v_a, tp).reshape(n_seq, tp * A_KV_HEADS, hd)
        n_pool = cache_k.shape[0]
        ao4, k_n8 = _dsa_sample(qa, qi, wi, knew, vnew, _pad_tokens(ki, tp), lw["q_norm_g"], lw["k_norm_g"], bias_s,
                                cache_k.reshape(n_pool, PAGE_SIZE * A_KV_HEADS, hd),
                                cache_v.reshape(n_pool, PAGE_SIZE * A_KV_HEADS, hd),
                                cache_i, page_table, t_new=T, hd=hd)
        k_n8 = k_n8.reshape(n_seq, tp, kvw)
        ao = ao4.reshape(n_seq, A_KV_HEADS, rep, tp, hd)[:, :, :, :T].transpose(0, 3, 1, 2, 4).reshape(n_seq * T, aw)
        k_n = k_n8[:, :T].reshape(n_seq, T, A_KV_HEADS, hd)

    x = _outproj(x, mo, ao, g2, lw["w_out"])
    x, ffn2_w = _ffn(x, sh3, sc3, g3, lw["ffn2_norm_g"], *lw["ffn2_w"])
    state = (k_n, v_a.reshape(n_seq, T, A_KV_HEADS, hd), ki, c_new, n_new, m_new[:, 0, :M_HEADS], conv_new)
    return x.reshape(n_seq, T, d), state, dict(lw, ffn1_w=ffn1_w, ffn2_w=ffn2_w)


def kernel(x_prompt, x_sample, c_prompt, c_sample, cache_k, cache_v, cache_idx_k, page_table, state_C, state_n,
           state_m, state_conv, ffn1_norm_g, ffn1_w_gate, ffn1_w_up, ffn1_w_down, mix_norm_g, w_in, mlstm_conv_w,
           mlstm_gate_b, mlstm_out_g, q_norm_g, k_norm_g, t5_bias, w_out, ffn2_norm_g, ffn2_w_gate, ffn2_w_up,
           ffn2_w_down, w_ada, b_ada):
    depth = w_in.shape[0]
    bp, tp_len, d = x_prompt.shape
    bs, ts_len, _ = x_sample.shape
    dk, dv = state_C.shape[3], state_C.shape[4]
    mw = M_HEADS * dv
    hd = q_norm_g.shape[1]
    past_len = page_table.shape[1] * PAGE_SIZE
    bias_p, bias_s = _bias_tables(t5_bias, tp_len, past_len, ts_len)
    rep = A_HEADS // A_KV_HEADS
    bias_s = bias_s.reshape(A_HEADS * SUBLANE, A_KV_HEADS * (past_len + LANE))

    xp, xs = x_prompt, x_sample
    st_p, st_s = [], []
    for l in range(depth):
        lw = dict(ffn1_norm_g=ffn1_norm_g[l], ffn1_w=(ffn1_w_gate[l], ffn1_w_up[l], ffn1_w_down[l]),
                  mix_norm_g=mix_norm_g[l], w_in=_pack_w_in(w_in[l], mw, A_HEADS * hd, A_KV_HEADS * hd),
                  mlstm_conv_w=mlstm_conv_w[l], mlstm_gate_b=mlstm_gate_b[l], mlstm_out_g=mlstm_out_g[l],
                  q_norm_g=q_norm_g[l], k_norm_g=k_norm_g[l], w_out=w_out[l].astype(BF16),
                  ffn2_norm_g=ffn2_norm_g[l], ffn2_w=(ffn2_w_gate[l], ffn2_w_up[l], ffn2_w_down[l]))
        mods = _ada(jnp.concatenate([c_prompt, c_sample], axis=0), w_ada[l], b_ada[l])
        mods = mods.reshape(bp + bs, N_MOD, d)
        init_s = (state_C[l], state_n[l], state_m[l], state_conv[l])
        xs, ss, lw = _layer(xs, mods[bp:], lw, bias_p, bias_s, init_s,
                            (cache_k[l], cache_v[l], cache_idx_k[l], page_table))
        init_p = (jnp.zeros((bp, M_HEADS, dk, dv), F32), jnp.zeros((bp, M_HEADS, dk), F32),
                  jnp.zeros((bp, M_HEADS), F32), jnp.zeros((bp, CONV_W - 1, 2 * M_HEADS * dk), F32))
        xp, sp, _ = _layer(xp, mods[:bp], lw, bias_p, bias_s, init_p, None)
        st_p.append(sp)
        st_s.append(ss)
    stack = lambda sts, k: sts[0][k][None] if depth == 1 else jnp.stack([s[k] for s in sts])
    return (xp, xs) + tuple(stack(st_p, k) for k in range(7)) + tuple(stack(st_s, k) for k in range(7))
```

```python
import functools
import math

import numpy as np
import jax
import jax.numpy as jnp
from jax import lax
from jax.experimental import pallas as pl
from jax.experimental.pallas import tpu as pltpu

F32 = jnp.float32
BF16 = jnp.bfloat16

M_HEADS = 4
CONV_W = 4
M_CHUNK = 64
A_HEADS = 8
A_KV_HEADS = 2
IDX_HEADS = 8
IDX_DIM = 64
TOPK_MAX = 256
Q_BLOCK = 128
PAGE_SIZE = 128
N_BUCKETS = 32
MAX_DISTANCE = 128
N_MOD = 9
EPS = 1e-6
NEG = -1e30

LANE = 128
SUBLANE = 8
ROW_TILE = 512
VMEM_LIMIT = 56 * 1024 * 1024

TAIL_KI = 0
TAIL_WI = IDX_DIM
TAIL_IG = IDX_DIM + IDX_HEADS
TAIL_FG = TAIL_IG + M_HEADS

INT_MIN = -2 ** 31
LOG2E = math.log2(math.e)


def _params(sem, vmem=None):
    return pltpu.CompilerParams(dimension_semantics=sem, vmem_limit_bytes=vmem)


def _dot(a, b):
    return jnp.dot(a, b, preferred_element_type=F32)


def _dot_nt(a, b):
    return lax.dot_general(a, b, (((1,), (1,)), ((), ())), preferred_element_type=F32)


def _dot_tn(a, b):
    return lax.dot_general(a, b, (((0,), (0,)), ((), ())), preferred_element_type=F32)


def _rms(x, g):
    return x * lax.rsqrt(jnp.mean(x * x, axis=-1, keepdims=True) + EPS) * g


def _row_tile(n):
    return ROW_TILE if n % ROW_TILE == 0 else n


REDUCE_CHAINS = 8


def _reduce_rows(x, op):
    n, c = x.shape
    groups = n // SUBLANE
    chains = REDUCE_CHAINS if groups % REDUCE_CHAINS == 0 else 1
    x3 = x.reshape(groups, SUBLANE, c)
    per = groups // chains
    parts = [op(x3[k * per:(k + 1) * per], axis=0) for k in range(chains)]
    while len(parts) > 1:
        parts = [op(jnp.stack(parts[k:k + 2]), axis=0) for k in range(0, len(parts), 2)]
    return op(parts[0], axis=0, keepdims=True)


def _ada_kernel(c_ref, w_ref, b_ref, o_ref):
    o_ref[...] = _dot(c_ref[...].astype(BF16), w_ref[...].astype(BF16)) + b_ref[...]


def _ada(c_all, w_ada, b_ada):
    r, d = c_all.shape
    n = w_ada.shape[1]
    tn = 1024 if n % 1024 == 0 else n
    return pl.pallas_call(
        _ada_kernel,
        grid=(n // tn,),
        in_specs=[pl.BlockSpec((r, d), lambda j: (0, 0)),
                  pl.BlockSpec((d, tn), lambda j: (0, j)),
                  pl.BlockSpec((1, tn), lambda j: (0, j))],
        out_specs=pl.BlockSpec((r, tn), lambda j: (0, j)),
        out_shape=jax.ShapeDtypeStruct((r, n), F32),
        compiler_params=_params(("arbitrary",), VMEM_LIMIT),
        name="ada",
    )(c_all, w_ada, b_ada.reshape(1, n))


def _ffn_kernel(x_ref, sh_ref, sc_ref, gt_ref, g_ref, wg_ref, wu_ref, wd_ref, o_ref, *rest):
    *bf16_out, h_ref, acc_ref = rest
    j = pl.program_id(1)

    @pl.when(j == 0)
    def _():
        h_ref[...] = (_rms(x_ref[...], g_ref[...]) * (1.0 + sc_ref[...]) + sh_ref[...]).astype(BF16)
        acc_ref[...] = jnp.zeros_like(acc_ref)

    wg, wu, wd = wg_ref[...], wu_ref[...], wd_ref[...]
    if bf16_out:
        wg, wu, wd = wg.astype(BF16), wu.astype(BF16), wd.astype(BF16)
        for ref, w in zip(bf16_out, (wg, wu, wd)):
            ref[...] = w
    h = h_ref[...]
    a = _dot(h, wg)
    u = _dot(h, wu)
    act = (a * jax.nn.sigmoid(a) * u).astype(BF16)
    acc_ref[...] += _dot(act, wd)

    @pl.when(j == pl.num_programs(1) - 1)
    def _():
        o_ref[...] = x_ref[...] + 0.5 * gt_ref[...] * acc_ref[...]


def _mod_spec(mod, m, tm, d):
    if mod.ndim == 2:
        return pl.BlockSpec((tm, d), lambda i, j: (i, 0))
    tiles_per_seq = (m // mod.shape[0]) // tm
    return pl.BlockSpec((None, 1, d), lambda i, j: (i // tiles_per_seq, 0, 0))


def _ffn(x, sh, sc, gt, g, wg, wu, wd):
    m, d = x.shape
    f = wg.shape[1]
    tm = _row_tile(m)
    emit = wg.dtype != BF16
    assert not emit or m == tm
    tf = (256 if emit else 512) if f % 512 == 0 else f
    ms = _mod_spec(sh, m, tm, d)
    w_specs = [pl.BlockSpec((d, tf), lambda i, j: (0, j)),
               pl.BlockSpec((d, tf), lambda i, j: (0, j)),
               pl.BlockSpec((tf, d), lambda i, j: (j, 0))]
    out_specs = [pl.BlockSpec((tm, d), lambda i, j: (i, 0))]
    out_shape = [jax.ShapeDtypeStruct((m, d), F32)]
    if emit:
        out_specs += w_specs
        out_shape += [jax.ShapeDtypeStruct(w.shape, BF16) for w in (wg, wu, wd)]
    y, *wb = pl.pallas_call(
        _ffn_kernel,
        grid=(m // tm, f // tf),
        in_specs=[pl.BlockSpec((tm, d), lambda i, j: (i, 0)), ms, ms, ms,
                  pl.BlockSpec((1, d), lambda i, j: (0, 0))] + w_specs,
        out_specs=out_specs,
        out_shape=out_shape,
        scratch_shapes=[pltpu.VMEM((tm, d), BF16), pltpu.VMEM((tm, d), F32)],
        compiler_params=_params(("parallel", "arbitrary"), VMEM_LIMIT),
        name="ffn",
    )(x, sh, sc, gt, g.reshape(1, d), wg, wu, wd)
    return y, (tuple(wb) if emit else (wg, wu, wd))


def _proj_kernel(x_ref, sh_ref, sc_ref, g_ref, w_ref, o_ref):
    h = (_rms(x_ref[...], g_ref[...]) * (1.0 + sc_ref[...]) + sh_ref[...]).astype(BF16)
    o_ref[...] = _dot(h, w_ref[...])


PROJ_ROW_TILE = 256


def _proj(x, sh, sc, g, w):
    m, d = x.shape
    n = w.shape[1]
    tm = PROJ_ROW_TILE if m % PROJ_ROW_TILE == 0 else m
    ms = _mod_spec(sh, m, tm, d)
    return pl.pallas_call(
        _proj_kernel,
        grid=(m // tm, 1),
        in_specs=[pl.BlockSpec((tm, d), lambda i, j: (i, 0)), ms, ms,
                  pl.BlockSpec((1, d), lambda i, j: (0, 0)),
                  pl.BlockSpec((d, n), lambda i, j: (0, 0), pipeline_mode=pl.Buffered(1))],
        out_specs=pl.BlockSpec((tm, n), lambda i, j: (i, 0)),
        out_shape=jax.ShapeDtypeStruct((m, n), F32),
        compiler_params=_params(("parallel", "arbitrary"), VMEM_LIMIT),
        name="proj",
    )(x, sh, sc, g.reshape(1, d), w)


def _pack_w_in(w_in, mw, aw, kvw):
    d = w_in.shape[0]
    o = np.cumsum([0, mw, mw, mw, mw, M_HEADS, M_HEADS, aw, kvw, kvw, IDX_HEADS * IDX_DIM, IDX_DIM, IDX_HEADS])
    seg = lambda k: w_in[:, o[k]:o[k + 1]].astype(BF16)
    order = (0, 1, 2, 3, 6, 7, 8, 9, 10, 11, 4, 5)
    pad = (-o[-1]) % LANE
    return jnp.concatenate([seg(k) for k in order] + [jnp.zeros((d, pad), BF16)], axis=1)


def _log_sigmoid(x):
    return jnp.minimum(x, 0.0) - jnp.log1p(jnp.exp(-jnp.abs(x)))


def _mlstm_kernel(qk_ref, v_ref, o_ref, tail_ref, prev_ref, cw_ref, gb_ref, og_ref, c0_ref, n0_ref, m0_ref,
                  mo_ref, cst_ref, nst_ref, mst_ref, xbuf_ref, *, L, valid, dk, dv):
    @pl.when(pl.program_id(1) == 0)
    def _():
        xbuf_ref[:, 0:SUBLANE, :] = prev_ref[...]
        cst_ref[...] = c0_ref[...]
        nst_ref[...] = n0_ref[...]
        mst_ref[...] = m0_ref[...]

    for sb in range(qk_ref.shape[0]):
        _mlstm_one(qk_ref.at[sb], v_ref.at[sb], o_ref.at[sb], tail_ref.at[sb], cw_ref, gb_ref, og_ref, mo_ref.at[sb],
                   cst_ref.at[sb], nst_ref.at[sb], mst_ref.at[sb], xbuf_ref.at[sb], L=L, valid=valid, dk=dk, dv=dv)


def _mlstm_one(qk_ref, v_ref, o_ref, tail_ref, cw_ref, gb_ref, og_ref, mo_ref, cst_ref, nst_ref, mst_ref, xbuf_ref,
               *, L, valid, dk, dv):
    width = M_HEADS * dk
    xbuf_ref[SUBLANE:SUBLANE + L, :] = qk_ref[...]
    base = SUBLANE - (CONV_W - 1)
    y = xbuf_ref[base:base + L, :] * cw_ref[0:1, :]
    for j in range(1, CONV_W):
        y = y + xbuf_ref[base + j:base + j + L, :] * cw_ref[j:j + 1, :]
    tail_rows = xbuf_ref[L:L + SUBLANE, :]
    xbuf_ref[0:SUBLANE, :] = tail_rows
    qk = y * jax.nn.sigmoid(y)

    gates = tail_ref[...] + gb_ref[...]
    row = lax.broadcasted_iota(jnp.int32, (L, LANE), 0)
    lf = _log_sigmoid(gates)
    ig = gates
    if valid < L:
        lf = jnp.where(row < valid, lf, 0.0)
        ig = jnp.where(row < valid, ig, -jnp.inf)
    bcum = lf
    s = 1
    while s < L:
        bcum = bcum + jnp.where(row >= s, pltpu.roll(bcum, s, axis=0), 0.0)
        s *= 2
    bmi_t = (pltpu.roll(ig, TAIL_FG - TAIL_IG, axis=1) - bcum).T

    ri = lax.broadcasted_iota(jnp.int32, (L, L), 0)
    ci = lax.broadcasted_iota(jnp.int32, (L, L), 1)
    causal = ri >= ci

    for h in range(M_HEADS):
        li, lfh = TAIL_IG + h, TAIL_FG + h
        a_col = bcum[:, lfh:lfh + 1]
        ig_col = ig[:, li:li + 1]
        b_row = bmi_t[lfh:lfh + 1, :]
        m0 = mst_ref[:, h:h + 1]
        d_log = jnp.where(causal, a_col + b_row, -jnp.inf)
        s_log = a_col + m0
        m_col = jnp.maximum(s_log, jnp.max(d_log, axis=-1, keepdims=True))
        dw = jnp.exp(d_log - m_col)
        sw = jnp.exp(s_log - m_col)

        q = qk[:, h * dk:(h + 1) * dk]
        k = qk[:, width + h * dk:width + (h + 1) * dk] * (dk ** -0.5)
        v = v_ref[:, h * dv:(h + 1) * dv]
        qb, kb = q.astype(BF16), k.astype(BF16)
        c_prev = cst_ref[h]
        n_prev = nst_ref[h:h + 1, :]
        scores = _dot_nt(qb, kb) * dw
        num = _dot(scores.astype(BF16), v.astype(BF16)) + sw * _dot(qb, c_prev.astype(BF16))
        den = jnp.sum(scores, axis=-1, keepdims=True) + sw * jnp.sum(q * n_prev, axis=-1, keepdims=True)
        hh = num / jnp.maximum(jnp.abs(den), jnp.exp(-m_col))

        m_new = m_col[L - 1:L, :]
        a_last = a_col[L - 1:L, :]
        wl = jnp.exp(a_last - a_col + ig_col - m_new)
        decay = jnp.exp(a_last + m0 - m_new)
        cst_ref[h] = decay * c_prev + _dot_tn(kb, (wl * v).astype(BF16))
        nst_ref[h:h + 1, :] = decay * n_prev + jnp.sum(wl * k, axis=0, keepdims=True)
        mst_ref[:, h:h + 1] = m_new

        hn = hh * lax.rsqrt(jnp.mean(hh * hh, axis=-1, keepdims=True) + EPS) * og_ref[:, h * dv:(h + 1) * dv]
        gate = jax.nn.sigmoid(o_ref[:, h * dv:(h + 1) * dv])
        mo_ref[:, h * dv:(h + 1) * dv] = (hn * gate).astype(mo_ref.dtype)


def _mlstm(p3, prev8, conv_w, gate_row, out_g, c0, n0, m0, *, L, valid, seqs_per_step, out_dtype):
    n_seq, t_all, npk = p3.shape
    nc = t_all // L
    sb = seqs_per_step
    assert n_seq % sb == 0
    dk = c0.shape[2]
    dv = c0.shape[3]
    mw = M_HEADS * dv
    qkw = 2 * M_HEADS * dk
    assert qkw % mw == 0 and (2 * qkw) % mw == 0
    kern = functools.partial(_mlstm_kernel, L=L, valid=valid, dk=dk, dv=dv)
    seq = lambda shape: pl.BlockSpec((sb,) + shape, lambda b, c: (b,) + (0,) * len(shape))
    return pl.pallas_call(
        kern,
        grid=(n_seq // sb, nc),
        in_specs=[pl.BlockSpec((sb, L, qkw), lambda b, c: (b, c, 0)),
                  pl.BlockSpec((sb, L, mw), lambda b, c: (b, c, qkw // mw)),
                  pl.BlockSpec((sb, L, mw), lambda b, c: (b, c, qkw // mw + 1)),
                  pl.BlockSpec((sb, L, LANE), lambda b, c: (b, c, npk // LANE - 1)),
                  seq((SUBLANE, qkw)),
                  pl.BlockSpec((CONV_W, qkw), lambda b, c: (0, 0)),
                  pl.BlockSpec((1, LANE), lambda b, c: (0, 0)),
                  pl.BlockSpec((1, mw), lambda b, c: (0, 0)),
                  seq((M_HEADS, dk, dv)), seq((M_HEADS, dk)), seq((1, LANE))],
        out_specs=[pl.BlockSpec((sb, L, mw), lambda b, c: (b, c, 0)),
                   seq((M_HEADS, dk, dv)), seq((M_HEADS, dk)), seq((1, LANE))],
        out_shape=[jax.ShapeDtypeStruct((n_seq, t_all, mw), out_dtype),
                   jax.ShapeDtypeStruct(c0.shape, F32),
                   jax.ShapeDtypeStruct(n0.shape, F32),
                   jax.ShapeDtypeStruct(m0.shape, F32)],
        scratch_shapes=[pltpu.VMEM((sb, SUBLANE + L, qkw), F32)],
        compiler_params=_params(("parallel", "arbitrary"), VMEM_LIMIT),
        name="mlstm",
    )(p3, p3, p3, p3, prev8, conv_w, gate_row, out_g.reshape(1, mw), c0, n0, m0)


def _bucket_np(dist):
    me = N_BUCKETS // 2
    d = np.maximum(dist, 0)
    ratio = np.log(np.maximum(d, 1).astype(np.float64) / me) / math.log(MAX_DISTANCE / me)
    large = np.minimum(me + (ratio * (N_BUCKETS - me)).astype(np.int64), N_BUCKETS - 1)
    return np.where(d < me, d, large).astype(np.int32)


def _bias_kernel(t5_ref, bkp_ref, bks_ref, vs_ref, far_ref, op_ref, os_ref):
    rep = A_HEADS // A_KV_HEADS

    def lookup(bk, h):
        acc = jnp.zeros(bk.shape, F32)
        for b in range(N_BUCKETS):
            acc = jnp.where(bk == b, t5_ref[b, h], acc)
        return acc

    for h in range(A_HEADS):
        half = (h % 2) * Q_BLOCK
        op_ref[h // 2, :, half:half + Q_BLOCK] = (lookup(bkp_ref[...], h) - t5_ref[far_ref[0], h]) * LOG2E
        os_ref[h] = jnp.where(vs_ref[h // rep] > 0, lookup(bks_ref[...], h), NEG)


def _bias_tables(t5_bias, t_prompt, past, t_new):
    c = np.arange(2 * Q_BLOCK)[:, None]
    r = np.arange(Q_BLOCK)[None, :]
    bkp = _bucket_np(Q_BLOCK + r - c)
    far = _bucket_np(np.arange(Q_BLOCK + 1, max(t_prompt, past + t_new) + Q_BLOCK))
    assert (far == far[0]).all(), "bias must be constant beyond one query block"
    nk = past + LANE
    tq = np.arange(SUBLANE)[:, None]
    col = np.arange(2 * nk)[None, :]
    s_pos, s_grp = col // A_KV_HEADS, col % A_KV_HEADS
    bks = _bucket_np(past + tq - s_pos)
    vis = np.stack([(s_grp == g) & (s_pos - past <= tq) for g in range(A_KV_HEADS)]).astype(np.int32)
    return pl.pallas_call(
        _bias_kernel,
        in_specs=[pl.BlockSpec(memory_space=pltpu.SMEM),
                  pl.BlockSpec(memory_space=pltpu.VMEM),
                  pl.BlockSpec(memory_space=pltpu.VMEM),
                  pl.BlockSpec(memory_space=pltpu.VMEM),
                  pl.BlockSpec(memory_space=pltpu.SMEM)],
        out_specs=[pl.BlockSpec(memory_space=pltpu.VMEM), pl.BlockSpec(memory_space=pltpu.VMEM)],
        out_shape=[jax.ShapeDtypeStruct((A_HEADS // 2, 2 * Q_BLOCK, 2 * Q_BLOCK), F32),
                   jax.ShapeDtypeStruct((A_HEADS, SUBLANE, 2 * nk), F32)],
        name="t5_bias_tables",
    )(t5_bias, jnp.asarray(bkp), jnp.asarray(bks), jnp.asarray(vis), jnp.asarray(far[:1]))


def _sort_key(score):
    bits = lax.bitcast_convert_type(score, jnp.int32)
    return jnp.where(bits < 0, bits ^ jnp.int32(0x7FFFFFFF), bits)


def _kth_key(count_ge, shape, k_sel):
    def body(it, ans_u):
        cand_u = ans_u | lax.shift_left(jnp.int32(1), jnp.int32(31) - it)
        cnt = count_ge(cand_u ^ jnp.int32(INT_MIN))
        return jnp.where(cnt >= k_sel, cand_u, ans_u)

    ans_u = lax.fori_loop(0, 32, body, jnp.zeros(shape, jnp.int32))
    return ans_u ^ jnp.int32(INT_MIN)


HALF = 16
HALF_BIAS = 1 << (HALF - 1)
PACKED_ROWS = 2 * SUBLANE


def _sum16(m):
    groups = m.shape[0] // PACKED_ROWS
    m3 = m.reshape(groups, PACKED_ROWS, m.shape[1])
    chains = REDUCE_CHAINS if groups % REDUCE_CHAINS == 0 else 1
    per = groups // chains
    parts = []
    for c in range(chains):
        acc = m3[c * per]
        for r in range(1, per):
            acc = acc + m3[c * per + r]
        parts.append(acc)
    while len(parts) > 1:
        parts = [parts[k] + parts[k + 1] for k in range(0, len(parts), 2)]
    return jnp.sum(parts[0].astype(jnp.int32), axis=0, keepdims=True)


def _count_ge16(ref, rows, cand):
    return _sum16(jnp.where(ref[0:rows, :] >= cand.astype(jnp.int16), jnp.int16(1), jnp.int16(0)))


def _bisect16(count_ge, k_sel, cols):
    def body(it, carry):
        ans_u, cnt_ans = carry
        cand_u = ans_u | lax.shift_left(jnp.int32(1), jnp.int32(HALF - 1) - it)
        cnt = count_ge(cand_u - HALF_BIAS)
        ok = cnt >= k_sel
        return jnp.where(ok, cand_u, ans_u), jnp.where(ok, cnt, cnt_ans)

    zero = jnp.zeros((1, cols), jnp.int32)
    ans_u, cnt_ans = lax.fori_loop(0, HALF, body, (zero, zero))
    return ans_u - HALF_BIAS, cnt_ans


def _kth_key_packed(key_ref, hi_ref, lo_ref, rows, k_sel):
    key = key_ref[0:rows, :]
    cols = key.shape[1]
    hi_ref[0:rows, :] = lax.shift_right_arithmetic(key, HALF).astype(jnp.int16)
    p_hi, n_ge = _bisect16(lambda c: _count_ge16(hi_ref, rows, c), k_sel, cols)
    lo = ((key & ((1 << HALF) - 1)) - HALF_BIAS).astype(jnp.int16)
    same = hi_ref[0:rows, :] == p_hi.astype(jnp.int16)
    lo_ref[0:rows, :] = jnp.where(same, lo, jnp.int16(-HALF_BIAS))
    n_above = n_ge - _sum16(jnp.where(same, jnp.int16(1), jnp.int16(0)))
    p_lo, _ = _bisect16(lambda c: n_above + _count_ge16(lo_ref, rows, c), k_sel, cols)
    return p_hi * (1 << HALF) + (p_lo + HALF_BIAS)


def _triangle(below):
    r = lax.broadcasted_iota(jnp.int32, (LANE, LANE), 0)
    c = lax.broadcasted_iota(jnp.int32, (LANE, LANE), 1)
    return jnp.where((c < r) if below else (r < c), 1.0, 0.0).astype(BF16)


def _select_tiles(key_tiles, thr, k_sel, key_axis):
    count = lambda x: jnp.sum(x, axis=key_axis, keepdims=True)
    c_gt = jnp.zeros(thr.shape, F32)
    for kt in key_tiles:
        c_gt = c_gt + count(jnp.where(kt > thr, 1.0, 0.0))
    need = k_sel - c_gt
    tri = _triangle(below=(key_axis == 0))
    run = jnp.zeros(thr.shape, F32)
    out = []
    for kt in key_tiles:
        eq = jnp.where(kt == thr, 1.0, 0.0)
        eqb = eq.astype(BF16)
        before = (_dot(tri, eqb) if key_axis == 0 else _dot(eqb, tri)) + run
        out.append(jnp.where(kt > thr, 1.0, jnp.where(before < need, eq, 0.0)))
        run = run + count(eq)
    return out


FAR_STEP = 2


def _dsa_prompt_kernel(qa_ref, qi_ref, qtail_ref, k_ref, v_ref, ktail_ref, qg_ref, kg_ref, bias_ref,
                       ao_ref, kn_ref,
                       kb_ref, vt_ref, kib_ref, qn_ref, key_ref, negb_ref, hi_ref, lo_ref, *, T, k_sel, hd):
    i = pl.program_id(1)
    nb = T // Q_BLOCK
    rep = A_HEADS // A_KV_HEADS
    scale = hd ** -0.5
    near_w = 2 * LANE

    @pl.when(i == 0)
    def _():
        kb_ref[0:LANE, :] = jnp.zeros((LANE, kb_ref.shape[1]), BF16)
        vt_ref[:, 0:LANE] = jnp.zeros((vt_ref.shape[0], LANE), BF16)
        kib_ref[0:LANE, :] = jnp.zeros((LANE, IDX_DIM), BF16)
        for g in range(A_KV_HEADS):
            kn = _rms(k_ref[:, g * hd:(g + 1) * hd], kg_ref[...])
            kn_ref[:, g * hd:(g + 1) * hd] = kn
            kb_ref[LANE:, g * hd:(g + 1) * hd] = kn.astype(BF16)
        vt_ref[:, LANE:] = v_ref[...].T.astype(BF16)
        kib_ref[LANE:, :] = ktail_ref[:, TAIL_KI:TAIL_KI + IDX_DIM].astype(BF16)

    wi_t = qtail_ref[...].T[TAIL_WI:TAIL_WI + IDX_HEADS, :] * (IDX_HEADS ** -0.5) * (IDX_DIM ** -0.5)
    qi_pairs = [jnp.concatenate([qi_ref[:, (2 * a + hh) * IDX_DIM:(2 * a + hh + 1) * IDX_DIM] for hh in range(2)],
                                axis=0).astype(BF16) for a in range(IDX_HEADS // 2)]
    for h in range(A_HEADS):
        half = (h % 2) * Q_BLOCK
        qn = _rms(qa_ref[:, h * hd:(h + 1) * hd], qg_ref[...]) * (scale * LOG2E)
        qn_ref[h // 2, half:half + Q_BLOCK, :] = qn.astype(BF16)

    near0 = pl.multiple_of(i * LANE, LANE)
    ki_near = kib_ref[pl.ds(near0, near_w), :]
    k_near = kb_ref[pl.ds(near0, near_w), :]
    vt_near = vt_ref[:, pl.ds(near0, near_w)]
    kc = lax.broadcasted_iota(jnp.int32, (near_w, Q_BLOCK), 0)
    qr = lax.broadcasted_iota(jnp.int32, (near_w, Q_BLOCK), 1)
    near_ok = jnp.where(kc <= qr + LANE, 1.0, 0.0) * jnp.maximum(jnp.where(kc >= LANE, 1.0, 0.0),
                                                                  (i >= 1).astype(F32))

    def scores(ki):
        sc = jnp.zeros((ki.shape[0], Q_BLOCK), F32)
        for a in range(IDX_HEADS // 2):
            s2 = _dot_nt(ki, qi_pairs[a])
            for hh in range(2):
                j = 2 * a + hh
                sc = sc + wi_t[j:j + 1, :] * jnp.maximum(s2[:, hh * Q_BLOCK:(hh + 1) * Q_BLOCK], 0.0)
        return sc

    def body(wf):
        fw = wf * LANE
        w_all = fw + near_w
        key_ref[fw:w_all, :] = _sort_key(jnp.where(near_ok > 0.0, scores(ki_near) + 0.0, NEG))
        if wf:
            far_ok = lax.broadcasted_iota(jnp.int32, (fw, Q_BLOCK), 0) < (i - 1) * LANE
            key_ref[0:fw, :] = _sort_key(jnp.where(far_ok, scores(kib_ref[LANE:LANE + fw, :]) + 0.0, NEG))

        thr = _kth_key_packed(key_ref, hi_ref, lo_ref, w_all, k_sel)
        sel = _select_tiles([key_ref[t * LANE:(t + 1) * LANE, :] for t in range(wf + 2)], thr, float(k_sel), 0)
        for t in range(wf):
            ok = sel[t] * (t < i - 1).astype(F32)
            negb_ref[t * LANE:(t + 1) * LANE, :] = jnp.where(ok > 0.0, 0.0, NEG)
        for t in range(2):
            ok = sel[wf + t] * near_ok[t * LANE:(t + 1) * LANE, :]
            negb_ref[fw + t * LANE:fw + (t + 1) * LANE, :] = jnp.where(ok > 0.0, 0.0, NEG)

        for a in range(A_HEADS // 2):
            g = (2 * a) // rep
            cs = slice(g * hd, (g + 1) * hd)
            qn2 = qn_ref[a]
            nb_near = negb_ref[fw:w_all, :]
            s_near = _dot_nt(k_near[:, cs], qn2) + bias_ref[a] + jnp.concatenate([nb_near, nb_near], axis=1)
            m = _reduce_rows(s_near, jnp.max)
            if wf:
                nb_far = negb_ref[0:fw, :]
                s_far = _dot_nt(kb_ref[LANE:LANE + fw, cs], qn2) + jnp.concatenate([nb_far, nb_far], axis=1)
                m = jnp.maximum(m, _reduce_rows(s_far, jnp.max))
            p_near = jnp.exp2(s_near - m)
            l = _reduce_rows(p_near, jnp.sum)
            acc = _dot(vt_near[cs, :], p_near.astype(BF16))
            if wf:
                p_far = jnp.exp2(s_far - m)
                l = l + _reduce_rows(p_far, jnp.sum)
                acc = acc + _dot(vt_ref[cs, LANE:LANE + fw], p_far.astype(BF16))
            out_t = acc / l
            for hh in range(2):
                h = 2 * a + hh
                ao_ref[:, h * hd:(h + 1) * hd] = out_t[:, hh * Q_BLOCK:(hh + 1) * Q_BLOCK].T.astype(ao_ref.dtype)

    variants = sorted({min(FAR_STEP * -(-x // FAR_STEP), nb) for x in range(max(nb - 1, 1))})
    wf_needed = jnp.minimum((jnp.maximum(i - 1, 0) + FAR_STEP - 1) // FAR_STEP * FAR_STEP, nb)
    for wf in variants:
        pl.when(wf_needed == wf)(functools.partial(body, wf))


def _dsa_prompt(p, qn_g, kn_g, bias_p, *, n_seq, T, hd):
    rows = p.shape[0]
    nb = T // Q_BLOCK
    aw = A_HEADS * hd
    kvw = A_KV_HEADS * hd
    iw = IDX_HEADS * IDX_DIM
    k_sel = min(TOPK_MAX, T // 4)
    mcols = p.shape[1] - LANE - iw - 2 * kvw - aw
    assert mcols % aw == 0 and (mcols + aw) % kvw == 0 and (mcols + aw + 2 * kvw) % iw == 0
    qa_blk = mcols // aw
    k_blk = (mcols + aw) // kvw
    qi_blk = (mcols + aw + 2 * kvw) // iw
    tail_blk = p.shape[1] // LANE - 1
    kern = functools.partial(_dsa_prompt_kernel, T=T, k_sel=k_sel, hd=hd)
    return pl.pallas_call(
        kern,
        grid=(n_seq, nb),
        in_specs=[pl.BlockSpec((Q_BLOCK, aw), lambda b, i: (b * nb + i, qa_blk)),
                  pl.BlockSpec((Q_BLOCK, iw), lambda b, i: (b * nb + i, qi_blk)),
                  pl.BlockSpec((Q_BLOCK, LANE), lambda b, i: (b * nb + i, tail_blk)),
                  pl.BlockSpec((T, kvw), lambda b, i: (b, k_blk)),
                  pl.BlockSpec((T, kvw), lambda b, i: (b, k_blk + 1)),
                  pl.BlockSpec((T, LANE), lambda b, i: (b, tail_blk)),
                  pl.BlockSpec((1, hd), lambda b, i: (0, 0)),
                  pl.BlockSpec((1, hd), lambda b, i: (0, 0)),
                  pl.BlockSpec((A_HEADS // 2, 2 * Q_BLOCK, 2 * Q_BLOCK), lambda b, i: (0, 0, 0))],
        out_specs=[pl.BlockSpec((Q_BLOCK, aw), lambda b, i: (b * nb + i, 0)),
                   pl.BlockSpec((T, kvw), lambda b, i: (b, 0))],
        out_shape=[jax.ShapeDtypeStruct((rows, aw), BF16),
                   jax.ShapeDtypeStruct((rows, kvw), F32)],
        scratch_shapes=[pltpu.VMEM((T + LANE, kvw), BF16), pltpu.VMEM((kvw, T + LANE), BF16),
                        pltpu.VMEM((T + LANE, IDX_DIM), BF16),
                        pltpu.VMEM((A_HEADS // 2, 2 * Q_BLOCK, hd), BF16),
                        pltpu.VMEM((T + 2 * LANE, Q_BLOCK), jnp.int32),
                        pltpu.VMEM((T + 2 * LANE, Q_BLOCK), F32),
                        pltpu.VMEM((T + 2 * LANE, Q_BLOCK), jnp.int16),
                        pltpu.VMEM((T + 2 * LANE, Q_BLOCK), jnp.int16)],
        compiler_params=_params(("parallel", "arbitrary"), VMEM_LIMIT),
        name="dsa_prompt",
    )(p, p, p, p, p, p, qn_g.reshape(1, hd), kn_g.reshape(1, hd), bias_p)


def _pad_rows(x, n):
    return jnp.concatenate([x, jnp.zeros((n - x.shape[0], x.shape[1]), x.dtype)], axis=0)


def _idx_sample_kernel(pt_ref, qi_ref, wi_ref, kinew_ref, *rest, n_pages, t_new):
    ip, sc_ref = rest[:n_pages], rest[n_pages]
    del pt_ref
    qi = qi_ref[...].astype(BF16)
    wi = wi_ref[...] * (IDX_HEADS ** -0.5) * (IDX_DIM ** -0.5)
    tq = lax.broadcasted_iota(jnp.int32, (SUBLANE, LANE), 0)
    lane = lax.broadcasted_iota(jnp.int32, (SUBLANE, LANE), 1)
    for t in range(n_pages + 1):
        r = _dot(qi, ip[t][...].astype(BF16)) if t < n_pages else _dot_nt(qi, _pad_rows(kinew_ref[...], LANE).astype(BF16))
        sc = jnp.zeros((SUBLANE, LANE), F32)
        for j in range(IDX_HEADS):
            sc = sc + wi[j * SUBLANE:(j + 1) * SUBLANE, :] * jnp.maximum(r[j * SUBLANE:(j + 1) * SUBLANE, :], 0.0)
        sc = sc + 0.0
        if t == n_pages:
            sc = jnp.where(lane <= tq, sc, jnp.where(lane < t_new, NEG, -jnp.inf))
        sc_ref[:, t * LANE:(t + 1) * LANE] = sc


def _select_kernel(sc_ref, sel_ref, key_ref, *, k_sel):
    rows, nk = sc_ref.shape
    key_ref[...] = _sort_key(sc_ref[...])

    def count_ge(cand):
        return jnp.sum(jnp.where(key_ref[...] >= cand, 1.0, 0.0), axis=1, keepdims=True)

    thr = _kth_key(count_ge, (rows, 1), float(k_sel))
    sel = _select_tiles([key_ref[:, t * LANE:(t + 1) * LANE] for t in range(nk // LANE)], thr, float(k_sel), 1)
    for t in range(nk // LANE):
        sel_ref[:, t * LANE:(t + 1) * LANE] = sel[t]


def _attn_sample_kernel(pt_ref, qa_ref, sel_ref, knew_ref, vnew_ref, qg_ref, kg_ref, bias_ref, *rest,
                        n_pages, hd):
    kp, vp = rest[:n_pages], rest[n_pages:2 * n_pages]
    ao_ref, kn_ref = rest[2 * n_pages:]
    del pt_ref
    nt = n_pages + 1
    pw = A_KV_HEADS * PAGE_SIZE
    scale = hd ** -0.5
    rows_q = qa_ref.shape[0]

    kn_new = _rms(knew_ref[...], kg_ref[...])
    kn_ref[...] = kn_new
    k_new = _pad_rows(kn_new, pw).astype(BF16)
    v_new = _pad_rows(vnew_ref[...], pw).astype(BF16)
    qn = _rms(qa_ref[...], qg_ref[...]).astype(BF16)

    tok = lax.broadcasted_iota(jnp.int32, (PAGE_SIZE, pw), 0)
    col = lax.broadcasted_iota(jnp.int32, (PAGE_SIZE, pw), 1)
    spread = jnp.where(col // A_KV_HEADS == tok, 1.0, 0.0).astype(BF16)
    sel_rows = jnp.concatenate([sel_ref[:, t * LANE:(t + 1) * LANE] for t in range(nt)], axis=0)
    negb = (_dot(sel_rows.astype(BF16), spread) - 1.0) * (-NEG)

    logits = []
    for t in range(nt):
        kt = kp[t][...].astype(BF16) if t < n_pages else k_new
        nb = jnp.concatenate([negb[t * SUBLANE:(t + 1) * SUBLANE, :]] * (rows_q // SUBLANE), axis=0)
        logits.append(_dot_nt(qn, kt) * scale + bias_ref[:, t * pw:(t + 1) * pw] + nb)
    m = logits[0].max(axis=-1, keepdims=True)
    for t in range(1, nt):
        m = jnp.maximum(m, logits[t].max(axis=-1, keepdims=True))
    l = jnp.zeros((rows_q, 1), F32)
    acc = jnp.zeros((rows_q, hd), F32)
    for t in range(nt):
        pr = jnp.exp(logits[t] - m)
        l = l + jnp.sum(pr, axis=-1, keepdims=True)
        acc = acc + _dot(pr.astype(BF16), vp[t][...].astype(BF16) if t < n_pages else v_new)
    ao_ref[...] = acc / l


def _dsa_sample(qa, qi, wi, knew, vnew, kinew, qn_g, kn_g, bias_s, cache_k, cache_v, cache_i, page_table, *, t_new, hd):
    n_seq, n_pages = page_table.shape
    nk = n_pages * PAGE_SIZE + LANE
    k_sel = min(TOPK_MAX, (n_pages * PAGE_SIZE + t_new) // 4)
    pt = page_table.reshape(-1)

    def page_spec(arr, t):
        return pl.BlockSpec((None,) + arr.shape[1:], lambda b, pt: (pt[b * n_pages + t], 0, 0))

    seq_spec = lambda arr: pl.BlockSpec((None,) + arr.shape[1:], lambda b, pt: (b, 0, 0))
    const_spec = lambda arr: pl.BlockSpec(arr.shape, lambda b, pt: (0, 0))

    scores = pl.pallas_call(
        functools.partial(_idx_sample_kernel, n_pages=n_pages, t_new=t_new),
        grid_spec=pltpu.PrefetchScalarGridSpec(
            num_scalar_prefetch=1, grid=(n_seq,),
            in_specs=[seq_spec(qi), seq_spec(wi), seq_spec(kinew)] + [page_spec(cache_i, t) for t in range(n_pages)],
            out_specs=pl.BlockSpec((SUBLANE, nk), lambda b, pt: (b, 0))),
        out_shape=jax.ShapeDtypeStruct((n_seq * SUBLANE, nk), F32),
        compiler_params=_params(("arbitrary",), VMEM_LIMIT),
        name="idx_sample",
    )(pt, qi, wi, kinew, *([cache_i] * n_pages))

    rows = scores.shape[0]
    tr = LANE if rows % LANE == 0 else rows
    sel = pl.pallas_call(
        functools.partial(_select_kernel, k_sel=k_sel),
        grid=(rows // tr,),
        in_specs=[pl.BlockSpec((tr, nk), lambda r: (r, 0))],
        out_specs=pl.BlockSpec((tr, nk), lambda r: (r, 0)),
        out_shape=jax.ShapeDtypeStruct((rows, nk), F32),
        scratch_shapes=[pltpu.VMEM((tr, nk), jnp.int32)],
        compiler_params=_params(("parallel",), VMEM_LIMIT),
        name="select_sample",
    )(scores)

    qg, kg = qn_g.reshape(1, hd), kn_g.reshape(1, hd)
    return pl.pallas_call(
        functools.partial(_attn_sample_kernel, n_pages=n_pages, hd=hd),
        grid_spec=pltpu.PrefetchScalarGridSpec(
            num_scalar_prefetch=1, grid=(n_seq,),
            in_specs=[seq_spec(qa), pl.BlockSpec((SUBLANE, nk), lambda b, pt: (b, 0)), seq_spec(knew), seq_spec(vnew),
                      const_spec(qg), const_spec(kg), const_spec(bias_s)]
                     + [page_spec(cache_k, t) for t in range(n_pages)]
                     + [page_spec(cache_v, t) for t in range(n_pages)],
            out_specs=[seq_spec(qa), seq_spec(knew)]),
        out_shape=[jax.ShapeDtypeStruct(qa.shape, F32), jax.ShapeDtypeStruct(knew.shape, F32)],
        compiler_params=_params(("arbitrary",), VMEM_LIMIT),
        name="attn_sample",
    )(pt, qa, sel, knew, vnew, qg, kg, bias_s, *([cache_k] * n_pages), *([cache_v] * n_pages))


def _outproj_kernel(x_ref, mo_ref, ao_ref, gt_ref, wm_ref, wa_ref, o_ref):
    y = _dot(mo_ref[...].astype(BF16), wm_ref[...]) + _dot(ao_ref[...].astype(BF16), wa_ref[...])
    o_ref[...] = x_ref[...] + gt_ref[...] * y


def _outproj(x, mo, ao, gt, w_out):
    m, d = x.shape
    mw, aw = mo.shape[1], ao.shape[1]
    tm = _row_tile(m)
    ms = _mod_spec(gt, m, tm, d)
    row = lambda w: pl.BlockSpec((tm, w), lambda i, j: (i, 0))
    return pl.pallas_call(
        _outproj_kernel,
        grid=(m // tm, 1),
        in_specs=[row(d), row(mw), row(aw), ms,
                  pl.BlockSpec((mw, d), lambda i, j: (0, 0)),
                  pl.BlockSpec((aw, d), lambda i, j: (0, 0))],
        out_specs=row(d),
        out_shape=jax.ShapeDtypeStruct((m, d), F32),
        compiler_params=_params(("parallel", "arbitrary"), VMEM_LIMIT),
        name="outproj",
    )(x, mo, ao, gt, w_out[:mw], w_out[mw:])


def _pad_tokens(a, t_pad):
    pad = [(0, 0)] * a.ndim
    pad[1] = (0, t_pad - a.shape[1])
    return jnp.pad(a, pad)


def _layer(x3, mods, lw, bias_p, bias_s, rec_state, past):
    n_seq, T, d = x3.shape
    x = x3.reshape(n_seq * T, d)
    dk, dv = rec_state[0].shape[2], rec_state[0].shape[3]
    mw = M_HEADS * dv
    hd = lw["q_norm_g"].shape[0]
    aw, kvw = A_HEADS * hd, A_KV_HEADS * hd
    iw = IDX_HEADS * IDX_DIM
    per_seq = T % ROW_TILE == 0
    if per_seq:
        mod = lambda k: mods[:, k:k + 1, :]
    else:
        mod = lambda k: jnp.repeat(mods[:, k, :], T, axis=0)
    sh1, sc1, g1, sh2, sc2, g2, sh3, sc3, g3 = [mod(k) for k in range(N_MOD)]

    x, ffn1_w = _ffn(x, sh1, sc1, g1, lw["ffn1_norm_g"], *lw["ffn1_w"])
    p = _proj(x, sh2, sc2, lw["mix_norm_g"], lw["w_in"])
    npk = p.shape[1]
    qkw = 2 * M_HEADS * dk
    c0, n0, m0, conv_buf = rec_state
    gate_row = jnp.zeros((1, LANE), F32).at[0, TAIL_IG:TAIL_IG + 2 * M_HEADS].set(lw["mlstm_gate_b"])
    m0_pad = jnp.zeros((n_seq, 1, LANE), F32).at[:, 0, :M_HEADS].set(m0)
    prev8 = jnp.zeros((n_seq, SUBLANE, qkw), F32).at[:, SUBLANE - (CONV_W - 1):, :].set(conv_buf)
    p3 = p.reshape(n_seq, T, npk)
    a_off = qkw + 2 * mw
    v_a = p3[:, :, a_off + aw + kvw:a_off + aw + 2 * kvw]
    ki = p3[:, :, npk - LANE + TAIL_KI:npk - LANE + TAIL_KI + IDX_DIM]
    conv_new = p3[:, T - (CONV_W - 1):, :qkw]

    if past is None:
        L = M_CHUNK if T % M_CHUNK == 0 else T
        mo, c_new, n_new, m_new = _mlstm(p3, prev8, lw["mlstm_conv_w"], gate_row, lw["mlstm_out_g"], c0, n0, m0_pad,
                                         L=L, valid=L, seqs_per_step=2 if n_seq % 2 == 0 else 1, out_dtype=BF16)
        mo = mo.reshape(n_seq * T, mw)
        ao, k_n = _dsa_prompt(p, lw["q_norm_g"], lw["k_norm_g"], bias_p, n_seq=n_seq, T=T, hd=hd)
        k_n = k_n.reshape(n_seq, T, A_KV_HEADS, hd)
    else:
        tp = SUBLANE
        assert T <= tp
        mo, c_new, n_new, m_new = _mlstm(_pad_tokens(p3, tp), prev8, lw["mlstm_conv_w"], gate_row, lw["mlstm_out_g"],
                                         c0, n0, m0_pad, L=tp, valid=T,
                                         seqs_per_step=SUBLANE if n_seq % SUBLANE == 0 else 1, out_dtype=F32)
        mo = mo[:, :T].reshape(n_seq * T, mw)
        cache_k, cache_v, cache_i, page_table = past
        rep = A_HEADS // A_KV_HEADS
        qa = _pad_tokens(p3[:, :, a_off:a_off + aw], tp).reshape(n_seq, tp, A_KV_HEADS, rep, hd)
        qa = qa.transpose(0, 2, 3, 1, 4).reshape(n_seq, A_HEADS * tp, hd)
        qi = _pad_tokens(p3[:, :, a_off + aw + 2 * kvw:a_off + aw + 2 * kvw + iw], tp)
        qi = qi.reshape(n_seq, tp, IDX_HEADS, IDX_DIM).transpose(0, 2, 1, 3).reshape(n_seq, IDX_HEADS * tp, IDX_DIM)
        wi = _pad_tokens(p3[:, :, npk - LANE + TAIL_WI:npk - LANE + TAIL_WI + IDX_HEADS], tp)
        wi = wi.transpose(0, 2, 1).reshape(n_seq, IDX_HEADS * tp, 1)
        knew = _pad_tokens(p3[:, :, a_off + aw:a_off + aw + kvw], tp).reshape(n_seq, tp * A_KV_HEADS, hd)
        vnew = _pad_tokens(v_a, tp).reshape(n_seq, tp * A_KV_HEADS, hd)
        n_pool = cache_k.shape[0]
        ao4, k_n8 = _dsa_sample(qa, qi, wi, knew, vnew, _pad_tokens(ki, tp), lw["q_norm_g"], lw["k_norm_g"], bias_s,
                                cache_k.reshape(n_pool, PAGE_SIZE * A_KV_HEADS, hd),
                                cache_v.reshape(n_pool, PAGE_SIZE * A_KV_HEADS, hd),
                                jnp.swapaxes(cache_i, 1, 2), page_table, t_new=T, hd=hd)
        k_n8 = k_n8.reshape(n_seq, tp, kvw)
        ao = ao4.reshape(n_seq, A_KV_HEADS, rep, tp, hd)[:, :, :, :T].transpose(0, 3, 1, 2, 4).reshape(n_seq * T, aw)
        k_n = k_n8[:, :T].reshape(n_seq, T, A_KV_HEADS, hd)

    x = _outproj(x, mo, ao, g2, lw["w_out"])
    x, ffn2_w = _ffn(x, sh3, sc3, g3, lw["ffn2_norm_g"], *lw["ffn2_w"])
    state = (k_n, v_a.reshape(n_seq, T, A_KV_HEADS, hd), ki, c_new, n_new, m_new[:, 0, :M_HEADS], conv_new)
    return x.reshape(n_seq, T, d), state, dict(lw, ffn1_w=ffn1_w, ffn2_w=ffn2_w)


def kernel(x_prompt, x_sample, c_prompt, c_sample, cache_k, cache_v, cache_idx_k, page_table, state_C, state_n,
           state_m, state_conv, ffn1_norm_g, ffn1_w_gate, ffn1_w_up, ffn1_w_down, mix_norm_g, w_in, mlstm_conv_w,
           mlstm_gate_b, mlstm_out_g, q_norm_g, k_norm_g, t5_bias, w_out, ffn2_norm_g, ffn2_w_gate, ffn2_w_up,
           ffn2_w_down, w_ada, b_ada):
    depth = w_in.shape[0]
    bp, tp_len, d = x_prompt.shape
    bs, ts_len, _ = x_sample.shape
    dk, dv = state_C.shape[3], state_C.shape[4]
    mw = M_HEADS * dv
    hd = q_norm_g.shape[1]
    past_len = page_table.shape[1] * PAGE_SIZE
    bias_p, bias_s = _bias_tables(t5_bias, tp_len, past_len, ts_len)
    rep = A_HEADS // A_KV_HEADS
    bias_s = bias_s.reshape(A_HEADS * SUBLANE, A_KV_HEADS * (past_len + LANE))

    xp, xs = x_prompt, x_sample
    st_p, st_s = [], []
    for l in range(depth):
        lw = dict(ffn1_norm_g=ffn1_norm_g[l], ffn1_w=(ffn1_w_gate[l], ffn1_w_up[l], ffn1_w_down[l]),
                  mix_norm_g=mix_norm_g[l], w_in=_pack_w_in(w_in[l], mw, A_HEADS * hd, A_KV_HEADS * hd),
                  mlstm_conv_w=mlstm_conv_w[l], mlstm_gate_b=mlstm_gate_b[l], mlstm_out_g=mlstm_out_g[l],
                  q_norm_g=q_norm_g[l], k_norm_g=k_norm_g[l], w_out=w_out[l].astype(BF16),
                  ffn2_norm_g=ffn2_norm_g[l], ffn2_w=(ffn2_w_gate[l], ffn2_w_up[l], ffn2_w_down[l]))
        mods = _ada(jnp.concatenate([c_prompt, c_sample], axis=0), w_ada[l], b_ada[l])
        mods = mods.reshape(bp + bs, N_MOD, d)
        init_s = (state_C[l], state_n[l], state_m[l], state_conv[l])
        xs, ss, lw = _layer(xs, mods[bp:], lw, bias_p, bias_s, init_s,
                            (cache_k[l], cache_v[l], cache_idx_k[l], page_table))
        init_p = (jnp.zeros((bp, M_HEADS, dk, dv), F32), jnp.zeros((bp, M_HEADS, dk), F32),
                  jnp.zeros((bp, M_HEADS), F32), jnp.zeros((bp, CONV_W - 1, 2 * M_HEADS * dk), F32))
        xp, sp, _ = _layer(xp, mods[:bp], lw, bias_p, bias_s, init_p, None)
        st_p.append(sp)
        st_s.append(ss)
    stack = lambda sts, k: sts[0][k][None] if depth == 1 else jnp.stack([s[k] for s in sts])
    return (xp, xs) + tuple(stack(st_p, k) for k in range(7)) + tuple(stack(st_s, k) for k in range(7))
```

```python
import functools
import math

import numpy as np
import jax
import jax.numpy as jnp
from jax import lax
from jax.experimental import pallas as pl
from jax.experimental.pallas import tpu as pltpu

F32 = jnp.float32
BF16 = jnp.bfloat16

M_HEADS = 4
CONV_W = 4
M_CHUNK = 64
A_HEADS = 8
A_KV_HEADS = 2
IDX_HEADS = 8
IDX_DIM = 64
TOPK_MAX = 256
Q_BLOCK = 128
PAGE_SIZE = 128
N_BUCKETS = 32
MAX_DISTANCE = 128
N_MOD = 9
EPS = 1e-6
NEG = -1e30

LANE = 128
SUBLANE = 8
ROW_TILE = 512
VMEM_LIMIT = 56 * 1024 * 1024

TAIL_KI = 0
TAIL_WI = IDX_DIM
TAIL_IG = IDX_DIM + IDX_HEADS
TAIL_FG = TAIL_IG + M_HEADS

INT_MIN = -2 ** 31
LOG2E = math.log2(math.e)


def _params(sem, vmem=None, flags=None):
    return pltpu.CompilerParams(dimension_semantics=sem, vmem_limit_bytes=vmem, flags=flags)


def _dot(a, b):
    return jnp.dot(a, b, preferred_element_type=F32)


def _dot_nt(a, b):
    return lax.dot_general(a, b, (((1,), (1,)), ((), ())), preferred_element_type=F32)


def _dot_tn(a, b):
    return lax.dot_general(a, b, (((0,), (0,)), ((), ())), preferred_element_type=F32)


def _rms(x, g):
    return x * lax.rsqrt(jnp.mean(x * x, axis=-1, keepdims=True) + EPS) * g


def _row_tile(n):
    return ROW_TILE if n % ROW_TILE == 0 else n


REDUCE_CHAINS = 8


def _reduce_rows(x, op):
    n, c = x.shape
    groups = n // SUBLANE
    chains = REDUCE_CHAINS if groups % REDUCE_CHAINS == 0 else 1
    x3 = x.reshape(groups, SUBLANE, c)
    per = groups // chains
    parts = [op(x3[k * per:(k + 1) * per], axis=0) for k in range(chains)]
    while len(parts) > 1:
        parts = [op(jnp.stack(parts[k:k + 2]), axis=0) for k in range(0, len(parts), 2)]
    return op(parts[0], axis=0, keepdims=True)


def _ada_kernel(c_ref, w_ref, b_ref, o_ref):
    o_ref[...] = _dot(c_ref[...].astype(BF16), w_ref[...].astype(BF16)) + b_ref[...]


def _ada(c_all, w_ada, b_ada):
    r, d = c_all.shape
    n = w_ada.shape[1]
    tn = 1024 if n % 1024 == 0 else n
    return pl.pallas_call(
        _ada_kernel,
        grid=(n // tn,),
        in_specs=[pl.BlockSpec((r, d), lambda j: (0, 0)),
                  pl.BlockSpec((d, tn), lambda j: (0, j)),
                  pl.BlockSpec((1, tn), lambda j: (0, j))],
        out_specs=pl.BlockSpec((r, tn), lambda j: (0, j)),
        out_shape=jax.ShapeDtypeStruct((r, n), F32),
        compiler_params=_params(("arbitrary",), VMEM_LIMIT),
        name="ada",
    )(c_all, w_ada, b_ada.reshape(1, n))


def _ffn_kernel(x_ref, sh_ref, sc_ref, gt_ref, g_ref, wg_ref, wu_ref, wd_ref, o_ref, *rest):
    *bf16_out, h_ref, acc_ref = rest
    j = pl.program_id(1)

    @pl.when(j == 0)
    def _():
        h_ref[...] = (_rms(x_ref[...], g_ref[...]) * (1.0 + sc_ref[...]) + sh_ref[...]).astype(BF16)
        acc_ref[...] = jnp.zeros_like(acc_ref)

    wg, wu, wd = wg_ref[...], wu_ref[...], wd_ref[...]
    if bf16_out:
        wg, wu, wd = wg.astype(BF16), wu.astype(BF16), wd.astype(BF16)
        for ref, w in zip(bf16_out, (wg, wu, wd)):
            ref[...] = w
    h = h_ref[...]
    a = _dot(h, wg)
    u = _dot(h, wu)
    act = (a * jax.nn.sigmoid(a) * u).astype(BF16)
    acc_ref[...] += _dot(act, wd)

    @pl.when(j == pl.num_programs(1) - 1)
    def _():
        o_ref[...] = x_ref[...] + 0.5 * gt_ref[...] * acc_ref[...]


def _mod_spec(mod, m, tm, d):
    if mod.ndim == 2:
        return pl.BlockSpec((tm, d), lambda i, j: (i, 0))
    tiles_per_seq = (m // mod.shape[0]) // tm
    return pl.BlockSpec((None, 1, d), lambda i, j: (i // tiles_per_seq, 0, 0))


def _ffn(x, sh, sc, gt, g, wg, wu, wd):
    m, d = x.shape
    f = wg.shape[1]
    tm = _row_tile(m)
    emit = wg.dtype != BF16
    assert not emit or m == tm
    tf = (256 if emit else 512) if f % 512 == 0 else f
    ms = _mod_spec(sh, m, tm, d)
    w_specs = [pl.BlockSpec((d, tf), lambda i, j: (0, j)),
               pl.BlockSpec((d, tf), lambda i, j: (0, j)),
               pl.BlockSpec((tf, d), lambda i, j: (j, 0))]
    out_specs = [pl.BlockSpec((tm, d), lambda i, j: (i, 0))]
    out_shape = [jax.ShapeDtypeStruct((m, d), F32)]
    if emit:
        out_specs += w_specs
        out_shape += [jax.ShapeDtypeStruct(w.shape, BF16) for w in (wg, wu, wd)]
    y, *wb = pl.pallas_call(
        _ffn_kernel,
        grid=(m // tm, f // tf),
        in_specs=[pl.BlockSpec((tm, d), lambda i, j: (i, 0)), ms, ms, ms,
                  pl.BlockSpec((1, d), lambda i, j: (0, 0))] + w_specs,
        out_specs=out_specs,
        out_shape=out_shape,
        scratch_shapes=[pltpu.VMEM((tm, d), BF16), pltpu.VMEM((tm, d), F32)],
        compiler_params=_params(("parallel", "arbitrary"), VMEM_LIMIT),
        name="ffn",
    )(x, sh, sc, gt, g.reshape(1, d), wg, wu, wd)
    return y, (tuple(wb) if emit else (wg, wu, wd))


def _proj_kernel(x_ref, sh_ref, sc_ref, g_ref, w_ref, *rest, q_cols):
    h = (_rms(x_ref[...], g_ref[...]) * (1.0 + sc_ref[...]) + sh_ref[...]).astype(BF16)
    p = _dot(h, w_ref[...])
    if not q_cols:
        rest[0][...] = p
        return
    qg_ref, o_ref, qn_ref, qib_ref = rest
    o_ref[...] = p
    a0, aw, i0, iw, hd = q_cols
    for hh in range(aw // hd):
        qn = _rms(p[:, a0 + hh * hd:a0 + (hh + 1) * hd], qg_ref[...]) * (hd ** -0.5 * LOG2E)
        qn_ref[:, hh * hd:(hh + 1) * hd] = qn.astype(BF16)
    qib_ref[...] = p[:, i0:i0 + iw].astype(BF16)


PROJ_ROW_TILE = 256


def _proj(x, sh, sc, g, w, qn_g=None, q_cols=None):
    m, d = x.shape
    n = w.shape[1]
    tm = PROJ_ROW_TILE if m % PROJ_ROW_TILE == 0 else m
    ms = _mod_spec(sh, m, tm, d)
    row = lambda width: pl.BlockSpec((tm, width), lambda i, j: (i, 0))
    in_specs = [row(d), ms, ms, pl.BlockSpec((1, d), lambda i, j: (0, 0)),
                pl.BlockSpec((d, n), lambda i, j: (0, 0), pipeline_mode=pl.Buffered(1))]
    args = [x, sh, sc, g.reshape(1, d), w]
    out_specs, out_shape = [row(n)], [jax.ShapeDtypeStruct((m, n), F32)]
    if q_cols:
        a0, aw, i0, iw, hd = q_cols
        in_specs.append(pl.BlockSpec((1, hd), lambda i, j: (0, 0)))
        args.append(qn_g.reshape(1, hd))
        out_specs += [row(aw), row(iw)]
        out_shape += [jax.ShapeDtypeStruct((m, aw), BF16), jax.ShapeDtypeStruct((m, iw), BF16)]
    out = pl.pallas_call(
        functools.partial(_proj_kernel, q_cols=q_cols),
        grid=(m // tm, 1),
        in_specs=in_specs,
        out_specs=out_specs,
        out_shape=out_shape,
        compiler_params=_params(("parallel", "arbitrary"), VMEM_LIMIT),
        name="proj",
    )(*args)
    return out if q_cols else (out[0], None, None)


def _pack_w_in(w_in, mw, aw, kvw):
    d = w_in.shape[0]
    o = np.cumsum([0, mw, mw, mw, mw, M_HEADS, M_HEADS, aw, kvw, kvw, IDX_HEADS * IDX_DIM, IDX_DIM, IDX_HEADS])
    seg = lambda k: w_in[:, o[k]:o[k + 1]]
    order = (0, 1, 2, 3, 6, 7, 8, 9, 10, 11, 4, 5)
    pad = (-o[-1]) % LANE
    return jnp.concatenate([seg(k) for k in order] + [jnp.zeros((d, pad), w_in.dtype)], axis=1).astype(BF16)


def _log_sigmoid(x):
    return jnp.minimum(x, 0.0) - jnp.log1p(jnp.exp(-jnp.abs(x)))


def _mlstm_kernel(qk_ref, v_ref, o_ref, tail_ref, prev_ref, cw_ref, gb_ref, og_ref, c0_ref, n0_ref, m0_ref,
                  mo_ref, cst_ref, nst_ref, mst_ref, xbuf_ref, *, L, valid, dk, dv):
    @pl.when(pl.program_id(1) == 0)
    def _():
        xbuf_ref[:, 0:SUBLANE, :] = prev_ref[...]
        cst_ref[...] = c0_ref[...]
        nst_ref[...] = n0_ref[...]
        mst_ref[...] = m0_ref[...]

    for sb in range(qk_ref.shape[0]):
        _mlstm_one(qk_ref.at[sb], v_ref.at[sb], o_ref.at[sb], tail_ref.at[sb], cw_ref, gb_ref, og_ref, mo_ref.at[sb],
                   cst_ref.at[sb], nst_ref.at[sb], mst_ref.at[sb], xbuf_ref.at[sb], L=L, valid=valid, dk=dk, dv=dv)


def _mlstm_one(qk_ref, v_ref, o_ref, tail_ref, cw_ref, gb_ref, og_ref, mo_ref, cst_ref, nst_ref, mst_ref, xbuf_ref,
               *, L, valid, dk, dv):
    width = M_HEADS * dk
    xbuf_ref[SUBLANE:SUBLANE + L, :] = qk_ref[...]
    base = SUBLANE - (CONV_W - 1)
    y = xbuf_ref[base:base + L, :] * cw_ref[0:1, :]
    for j in range(1, CONV_W):
        y = y + xbuf_ref[base + j:base + j + L, :] * cw_ref[j:j + 1, :]
    tail_rows = xbuf_ref[L:L + SUBLANE, :]
    xbuf_ref[0:SUBLANE, :] = tail_rows
    qk = y * jax.nn.sigmoid(y)

    gates = tail_ref[...] + gb_ref[...]
    row = lax.broadcasted_iota(jnp.int32, (L, LANE), 0)
    lf = _log_sigmoid(gates)
    ig = gates
    if valid < L:
        lf = jnp.where(row < valid, lf, 0.0)
        ig = jnp.where(row < valid, ig, -jnp.inf)
    bcum = lf
    s = 1
    while s < L:
        bcum = bcum + jnp.where(row >= s, pltpu.roll(bcum, s, axis=0), 0.0)
        s *= 2
    bmi_t = (pltpu.roll(ig, TAIL_FG - TAIL_IG, axis=1) - bcum).T

    ri = lax.broadcasted_iota(jnp.int32, (L, L), 0)
    ci = lax.broadcasted_iota(jnp.int32, (L, L), 1)
    causal = ri >= ci

    for h in range(M_HEADS):
        li, lfh = TAIL_IG + h, TAIL_FG + h
        a_col = bcum[:, lfh:lfh + 1]
        ig_col = ig[:, li:li + 1]
        b_row = bmi_t[lfh:lfh + 1, :]
        m0 = mst_ref[:, h:h + 1]
        d_log = jnp.where(causal, a_col + b_row, -jnp.inf)
        s_log = a_col + m0
        m_col = jnp.maximum(s_log, jnp.max(d_log, axis=-1, keepdims=True))
        dw = jnp.exp(d_log - m_col)
        sw = jnp.exp(s_log - m_col)

        q = qk[:, h * dk:(h + 1) * dk]
        k = qk[:, width + h * dk:width + (h + 1) * dk] * (dk ** -0.5)
        v = v_ref[:, h * dv:(h + 1) * dv]
        qb, kb = q.astype(BF16), k.astype(BF16)
        c_prev = cst_ref[h]
        n_prev = nst_ref[h:h + 1, :]
        scores = _dot_nt(qb, kb) * dw
        num = _dot(scores.astype(BF16), v.astype(BF16)) + sw * _dot(qb, c_prev.astype(BF16))
        den = jnp.sum(scores, axis=-1, keepdims=True) + sw * jnp.sum(q * n_prev, axis=-1, keepdims=True)
        hh = num / jnp.maximum(jnp.abs(den), jnp.exp(-m_col))

        m_new = m_col[L - 1:L, :]
        a_last = a_col[L - 1:L, :]
        wl = jnp.exp(a_last - a_col + ig_col - m_new)
        decay = jnp.exp(a_last + m0 - m_new)
        cst_ref[h] = decay * c_prev + _dot_tn(kb, (wl * v).astype(BF16))
        nst_ref[h:h + 1, :] = decay * n_prev + jnp.sum(wl * k, axis=0, keepdims=True)
        mst_ref[:, h:h + 1] = m_new

        hn = hh * lax.rsqrt(jnp.mean(hh * hh, axis=-1, keepdims=True) + EPS) * og_ref[:, h * dv:(h + 1) * dv]
        gate = jax.nn.sigmoid(o_ref[:, h * dv:(h + 1) * dv])
        mo_ref[:, h * dv:(h + 1) * dv] = (hn * gate).astype(mo_ref.dtype)


def _mlstm(p3, prev8, conv_w, gate_row, out_g, c0, n0, m0, *, L, valid, seqs_per_step, out_dtype):
    n_seq, t_all, npk = p3.shape
    nc = t_all // L
    sb = seqs_per_step
    assert n_seq % sb == 0
    dk = c0.shape[2]
    dv = c0.shape[3]
    mw = M_HEADS * dv
    qkw = 2 * M_HEADS * dk
    assert qkw % mw == 0 and (2 * qkw) % mw == 0
    kern = functools.partial(_mlstm_kernel, L=L, valid=valid, dk=dk, dv=dv)
    seq = lambda shape: pl.BlockSpec((sb,) + shape, lambda b, c: (b,) + (0,) * len(shape))
    return pl.pallas_call(
        kern,
        grid=(n_seq // sb, nc),
        in_specs=[pl.BlockSpec((sb, L, qkw), lambda b, c: (b, c, 0)),
                  pl.BlockSpec((sb, L, mw), lambda b, c: (b, c, qkw // mw)),
                  pl.BlockSpec((sb, L, mw), lambda b, c: (b, c, qkw // mw + 1)),
                  pl.BlockSpec((sb, L, LANE), lambda b, c: (b, c, npk // LANE - 1)),
                  seq((SUBLANE, qkw)),
                  pl.BlockSpec((CONV_W, qkw), lambda b, c: (0, 0)),
                  pl.BlockSpec((1, LANE), lambda b, c: (0, 0)),
                  pl.BlockSpec((1, mw), lambda b, c: (0, 0)),
                  seq((M_HEADS, dk, dv)), seq((M_HEADS, dk)), seq((1, LANE))],
        out_specs=[pl.BlockSpec((sb, L, mw), lambda b, c: (b, c, 0)),
                   seq((M_HEADS, dk, dv)), seq((M_HEADS, dk)), seq((1, LANE))],
        out_shape=[jax.ShapeDtypeStruct((n_seq, t_all, mw), out_dtype),
                   jax.ShapeDtypeStruct(c0.shape, F32),
                   jax.ShapeDtypeStruct(n0.shape, F32),
                   jax.ShapeDtypeStruct(m0.shape, F32)],
        scratch_shapes=[pltpu.VMEM((sb, SUBLANE + L, qkw), F32)],
        compiler_params=_params(("parallel", "arbitrary"), VMEM_LIMIT),
        name="mlstm",
    )(p3, p3, p3, p3, prev8, conv_w, gate_row, out_g.reshape(1, mw), c0, n0, m0)


def _bucket_np(dist):
    me = N_BUCKETS // 2
    d = np.maximum(dist, 0)
    ratio = np.log(np.maximum(d, 1).astype(np.float64) / me) / math.log(MAX_DISTANCE / me)
    large = np.minimum(me + (ratio * (N_BUCKETS - me)).astype(np.int64), N_BUCKETS - 1)
    return np.where(d < me, d, large).astype(np.int32)


def _bias_kernel(t5_ref, bkp_ref, bks_ref, vs_ref, far_ref, op_ref, os_ref):
    rep = A_HEADS // A_KV_HEADS

    def lookup(bk, h):
        acc = jnp.zeros(bk.shape, F32)
        for b in range(N_BUCKETS):
            acc = jnp.where(bk == b, t5_ref[b, h], acc)
        return acc

    for h in range(A_HEADS):
        half = (h % 2) * Q_BLOCK
        op_ref[h // 2, :, half:half + Q_BLOCK] = (lookup(bkp_ref[...], h) - t5_ref[far_ref[0], h]) * LOG2E
        os_ref[h] = jnp.where(vs_ref[h // rep] > 0, lookup(bks_ref[...], h), NEG)


def _bias_tables(t5_bias, t_prompt, past, t_new):
    c = np.arange(2 * Q_BLOCK)[:, None]
    r = np.arange(Q_BLOCK)[None, :]
    bkp = _bucket_np(Q_BLOCK + r - c)
    far = _bucket_np(np.arange(Q_BLOCK + 1, max(t_prompt, past + t_new) + Q_BLOCK))
    assert (far == far[0]).all(), "bias must be constant beyond one query block"
    nk = past + LANE
    tq = np.arange(SUBLANE)[:, None]
    col = np.arange(2 * nk)[None, :]
    s_pos, s_grp = col // A_KV_HEADS, col % A_KV_HEADS
    bks = _bucket_np(past + tq - s_pos)
    vis = np.stack([(s_grp == g) & (s_pos - past <= tq) for g in range(A_KV_HEADS)]).astype(np.int32)
    return pl.pallas_call(
        _bias_kernel,
        in_specs=[pl.BlockSpec(memory_space=pltpu.SMEM),
                  pl.BlockSpec(memory_space=pltpu.VMEM),
                  pl.BlockSpec(memory_space=pltpu.VMEM),
                  pl.BlockSpec(memory_space=pltpu.VMEM),
                  pl.BlockSpec(memory_space=pltpu.SMEM)],
        out_specs=[pl.BlockSpec(memory_space=pltpu.VMEM), pl.BlockSpec(memory_space=pltpu.VMEM)],
        out_shape=[jax.ShapeDtypeStruct((A_HEADS // 2, 2 * Q_BLOCK, 2 * Q_BLOCK), F32),
                   jax.ShapeDtypeStruct((A_HEADS, SUBLANE, 2 * nk), F32)],
        name="t5_bias_tables",
    )(t5_bias, jnp.asarray(bkp), jnp.asarray(bks), jnp.asarray(vis), jnp.asarray(far[:1]))


def _sort_key(score):
    bits = lax.bitcast_convert_type(score, jnp.int32)
    return jnp.where(bits < 0, bits ^ jnp.int32(0x7FFFFFFF), bits)


def _kth_key(count_ge, shape, k_sel):
    def body(it, ans_u):
        cand_u = ans_u | lax.shift_left(jnp.int32(1), jnp.int32(31) - it)
        cnt = count_ge(cand_u ^ jnp.int32(INT_MIN))
        return jnp.where(cnt >= k_sel, cand_u, ans_u)

    ans_u = lax.fori_loop(0, 32, body, jnp.zeros(shape, jnp.int32))
    return ans_u ^ jnp.int32(INT_MIN)


def _triangle(below):
    r = lax.broadcasted_iota(jnp.int32, (LANE, LANE), 0)
    c = lax.broadcasted_iota(jnp.int32, (LANE, LANE), 1)
    return jnp.where((c < r) if below else (r < c), 1.0, 0.0).astype(BF16)


def _select_tiles(key_tiles, thr, k_sel, key_axis):
    count = lambda x: jnp.sum(x, axis=key_axis, keepdims=True)
    c_gt = jnp.zeros(thr.shape, F32)
    for kt in key_tiles:
        c_gt = c_gt + count(jnp.where(kt > thr, 1.0, 0.0))
    need = k_sel - c_gt
    tri = _triangle(below=(key_axis == 0))
    run = jnp.zeros(thr.shape, F32)
    out = []
    for kt in key_tiles:
        eq = jnp.where(kt == thr, 1.0, 0.0)
        eqb = eq.astype(BF16)
        before = (_dot(tri, eqb) if key_axis == 0 else _dot(eqb, tri)) + run
        out.append(jnp.where(kt > thr, 1.0, jnp.where(before < need, eq, 0.0)))
        run = run + count(eq)
    return out


FAR_STEP = 2


def _dsa_prompt_kernel(qn_ref, qi_ref, qtail_ref, k_ref, v_ref, ktail_ref, kg_ref, bias_ref,
                       ao_ref, kn_ref,
                       kb_ref, vt_ref, kib_ref, key_ref, negb_ref, *, T, k_sel, hd):
    i = pl.program_id(1)
    nb = T // Q_BLOCK
    rep = A_HEADS // A_KV_HEADS
    near_w = 2 * LANE

    @pl.when(i == 0)
    def _():
        kb_ref[0:LANE, :] = jnp.zeros((LANE, kb_ref.shape[1]), BF16)
        vt_ref[:, 0:LANE] = jnp.zeros((vt_ref.shape[0], LANE), BF16)
        kib_ref[0:LANE, :] = jnp.zeros((LANE, IDX_DIM), BF16)
        for g in range(A_KV_HEADS):
            kn = _rms(k_ref[:, g * hd:(g + 1) * hd], kg_ref[...])
            kn_ref[:, g * hd:(g + 1) * hd] = kn
            kb_ref[LANE:, g * hd:(g + 1) * hd] = kn.astype(BF16)
        vt_ref[:, LANE:] = v_ref[...].T.astype(BF16)
        kib_ref[LANE:, :] = ktail_ref[:, TAIL_KI:TAIL_KI + IDX_DIM].astype(BF16)

    wi_t = qtail_ref[...].T[TAIL_WI:TAIL_WI + IDX_HEADS, :] * (IDX_HEADS ** -0.5) * (IDX_DIM ** -0.5)
    qi_pairs = [jnp.concatenate([qi_ref[:, (2 * a + hh) * IDX_DIM:(2 * a + hh + 1) * IDX_DIM] for hh in range(2)],
                                axis=0) for a in range(IDX_HEADS // 2)]
    qn_pairs = [jnp.concatenate([qn_ref[:, (2 * a + hh) * hd:(2 * a + hh + 1) * hd] for hh in range(2)], axis=0)
                for a in range(A_HEADS // 2)]

    near0 = pl.multiple_of(i * LANE, LANE)
    ki_near = kib_ref[pl.ds(near0, near_w), :]
    k_near = kb_ref[pl.ds(near0, near_w), :]
    vt_near = vt_ref[:, pl.ds(near0, near_w)]
    kc = lax.broadcasted_iota(jnp.int32, (near_w, Q_BLOCK), 0)
    qr = lax.broadcasted_iota(jnp.int32, (near_w, Q_BLOCK), 1)
    near_ok = jnp.where(kc <= qr + LANE, 1.0, 0.0) * jnp.maximum(jnp.where(kc >= LANE, 1.0, 0.0),
                                                                  (i >= 1).astype(F32))

    def scores(ki):
        sc = jnp.zeros((ki.shape[0], Q_BLOCK), F32)
        for a in range(IDX_HEADS // 2):
            s2 = _dot_nt(ki, qi_pairs[a])
            for hh in range(2):
                j = 2 * a + hh
                sc = sc + wi_t[j:j + 1, :] * jnp.maximum(s2[:, hh * Q_BLOCK:(hh + 1) * Q_BLOCK], 0.0)
        return sc

    def body(wf):
        fw = wf * LANE
        w_all = fw + near_w
        key_ref[fw:w_all, :] = _sort_key(jnp.where(near_ok > 0.0, scores(ki_near) + 0.0, NEG))
        if wf:
            far_ok = lax.broadcasted_iota(jnp.int32, (fw, Q_BLOCK), 0) < (i - 1) * LANE
            key_ref[0:fw, :] = _sort_key(jnp.where(far_ok, scores(kib_ref[LANE:LANE + fw, :]) + 0.0, NEG))

        def count_ge(cand):
            return _reduce_rows(jnp.where(key_ref[0:w_all, :] >= cand, 1.0, 0.0), jnp.sum)

        thr = _kth_key(count_ge, (1, Q_BLOCK), float(k_sel))
        sel = _select_tiles([key_ref[t * LANE:(t + 1) * LANE, :] for t in range(wf + 2)], thr, float(k_sel), 0)
        for t in range(wf):
            ok = sel[t] * (t < i - 1).astype(F32)
            negb_ref[t * LANE:(t + 1) * LANE, :] = jnp.where(ok > 0.0, 0.0, NEG)
        for t in range(2):
            ok = sel[wf + t] * near_ok[t * LANE:(t + 1) * LANE, :]
            negb_ref[fw + t * LANE:fw + (t + 1) * LANE, :] = jnp.where(ok > 0.0, 0.0, NEG)

        for a in range(A_HEADS // 2):
            g = (2 * a) // rep
            cs = slice(g * hd, (g + 1) * hd)
            qn2 = qn_pairs[a]
            nb_near = negb_ref[fw:w_all, :]
            s_near = _dot_nt(k_near[:, cs], qn2) + bias_ref[a] + jnp.concatenate([nb_near, nb_near], axis=1)
            m = _reduce_rows(s_near, jnp.max)
            if wf:
                nb_far = negb_ref[0:fw, :]
                s_far = _dot_nt(kb_ref[LANE:LANE + fw, cs], qn2) + jnp.concatenate([nb_far, nb_far], axis=1)
                m = jnp.maximum(m, _reduce_rows(s_far, jnp.max))
            p_near = jnp.exp2(s_near - m)
            l = _reduce_rows(p_near, jnp.sum)
            acc = _dot(vt_near[cs, :], p_near.astype(BF16))
            if wf:
                p_far = jnp.exp2(s_far - m)
                l = l + _reduce_rows(p_far, jnp.sum)
                acc = acc + _dot(vt_ref[cs, LANE:LANE + fw], p_far.astype(BF16))
            out_t = acc / l
            for hh in range(2):
                h = 2 * a + hh
                ao_ref[:, h * hd:(h + 1) * hd] = out_t[:, hh * Q_BLOCK:(hh + 1) * Q_BLOCK].T.astype(ao_ref.dtype)

    variants = sorted({min(FAR_STEP * -(-x // FAR_STEP), nb) for x in range(max(nb - 1, 1))})
    wf_needed = jnp.minimum((jnp.maximum(i - 1, 0) + FAR_STEP - 1) // FAR_STEP * FAR_STEP, nb)
    for wf in variants:
        pl.when(wf_needed == wf)(functools.partial(body, wf))


def _dsa_prompt(p, qn, qib, kn_g, bias_p, *, n_seq, T, hd):
    rows = p.shape[0]
    nb = T // Q_BLOCK
    aw = A_HEADS * hd
    kvw = A_KV_HEADS * hd
    iw = IDX_HEADS * IDX_DIM
    k_sel = min(TOPK_MAX, T // 4)
    kcol = p.shape[1] - LANE - iw - 2 * kvw
    assert kcol % kvw == 0
    k_blk = kcol // kvw
    tail_blk = p.shape[1] // LANE - 1
    kern = functools.partial(_dsa_prompt_kernel, T=T, k_sel=k_sel, hd=hd)
    return pl.pallas_call(
        kern,
        grid=(n_seq, nb),
        in_specs=[pl.BlockSpec((Q_BLOCK, aw), lambda b, i: (b * nb + i, 0)),
                  pl.BlockSpec((Q_BLOCK, iw), lambda b, i: (b * nb + i, 0)),
                  pl.BlockSpec((Q_BLOCK, LANE), lambda b, i: (b * nb + i, tail_blk)),
                  pl.BlockSpec((T, kvw), lambda b, i: (b, k_blk)),
                  pl.BlockSpec((T, kvw), lambda b, i: (b, k_blk + 1)),
                  pl.BlockSpec((T, LANE), lambda b, i: (b, tail_blk)),
                  pl.BlockSpec((1, hd), lambda b, i: (0, 0)),
                  pl.BlockSpec((A_HEADS // 2, 2 * Q_BLOCK, 2 * Q_BLOCK), lambda b, i: (0, 0, 0))],
        out_specs=[pl.BlockSpec((Q_BLOCK, aw), lambda b, i: (b * nb + i, 0)),
                   pl.BlockSpec((T, kvw), lambda b, i: (b, 0))],
        out_shape=[jax.ShapeDtypeStruct((rows, aw), BF16),
                   jax.ShapeDtypeStruct((rows, kvw), F32)],
        scratch_shapes=[pltpu.VMEM((T + LANE, kvw), BF16), pltpu.VMEM((kvw, T + LANE), BF16),
                        pltpu.VMEM((T + LANE, IDX_DIM), BF16),
                        pltpu.VMEM((T + 2 * LANE, Q_BLOCK), jnp.int32),
                        pltpu.VMEM((T + 2 * LANE, Q_BLOCK), F32)],
        compiler_params=_params(("parallel", "arbitrary"), VMEM_LIMIT),
        name="dsa_prompt",
    )(qn, qib, p, p, p, p, kn_g.reshape(1, hd), bias_p)


def _pad_rows(x, n):
    return jnp.concatenate([x, jnp.zeros((n - x.shape[0], x.shape[1]), x.dtype)], axis=0)


def _idx_sample_kernel(pt_ref, qi_ref, wi_ref, kinew_ref, *rest, n_pages, t_new):
    ip, sc_ref = rest[:n_pages], rest[n_pages]
    del pt_ref
    qi = qi_ref[...].astype(BF16)
    wi = wi_ref[...] * (IDX_HEADS ** -0.5) * (IDX_DIM ** -0.5)
    tq = lax.broadcasted_iota(jnp.int32, (SUBLANE, LANE), 0)
    lane = lax.broadcasted_iota(jnp.int32, (SUBLANE, LANE), 1)
    for t in range(n_pages + 1):
        r = _dot(qi, ip[t][...].astype(BF16)) if t < n_pages else _dot_nt(qi, _pad_rows(kinew_ref[...], LANE).astype(BF16))
        sc = jnp.zeros((SUBLANE, LANE), F32)
        for j in range(IDX_HEADS):
            sc = sc + wi[j * SUBLANE:(j + 1) * SUBLANE, :] * jnp.maximum(r[j * SUBLANE:(j + 1) * SUBLANE, :], 0.0)
        sc = sc + 0.0
        if t == n_pages:
            sc = jnp.where(lane <= tq, sc, jnp.where(lane < t_new, NEG, -jnp.inf))
        sc_ref[:, t * LANE:(t + 1) * LANE] = sc


def _select_kernel(sc_ref, sel_ref, key_ref, *, k_sel):
    rows, nk = sc_ref.shape
    key_ref[...] = _sort_key(sc_ref[...])

    def count_ge(cand):
        return jnp.sum(jnp.where(key_ref[...] >= cand, 1.0, 0.0), axis=1, keepdims=True)

    thr = _kth_key(count_ge, (rows, 1), float(k_sel))
    sel = _select_tiles([key_ref[:, t * LANE:(t + 1) * LANE] for t in range(nk // LANE)], thr, float(k_sel), 1)
    for t in range(nk // LANE):
        sel_ref[:, t * LANE:(t + 1) * LANE] = sel[t]


def _attn_sample_kernel(pt_ref, qa_ref, sel_ref, knew_ref, vnew_ref, qg_ref, kg_ref, bias_ref, *rest,
                        n_pages, hd):
    kp, vp = rest[:n_pages], rest[n_pages:2 * n_pages]
    ao_ref, kn_ref = rest[2 * n_pages:]
    del pt_ref
    nt = n_pages + 1
    pw = A_KV_HEADS * PAGE_SIZE
    scale = hd ** -0.5
    rows_q = qa_ref.shape[0]

    kn_new = _rms(knew_ref[...], kg_ref[...])
    kn_ref[...] = kn_new
    k_new = _pad_rows(kn_new, pw).astype(BF16)
    v_new = _pad_rows(vnew_ref[...], pw).astype(BF16)
    qn = _rms(qa_ref[...], qg_ref[...]).astype(BF16)

    tok = lax.broadcasted_iota(jnp.int32, (PAGE_SIZE, pw), 0)
    col = lax.broadcasted_iota(jnp.int32, (PAGE_SIZE, pw), 1)
    spread = jnp.where(col // A_KV_HEADS == tok, 1.0, 0.0).astype(BF16)
    sel_rows = jnp.concatenate([sel_ref[:, t * LANE:(t + 1) * LANE] for t in range(nt)], axis=0)
    negb = (_dot(sel_rows.astype(BF16), spread) - 1.0) * (-NEG)

    logits = []
    for t in range(nt):
        kt = kp[t][...].astype(BF16) if t < n_pages else k_new
        nb = jnp.concatenate([negb[t * SUBLANE:(t + 1) * SUBLANE, :]] * (rows_q // SUBLANE), axis=0)
        logits.append(_dot_nt(qn, kt) * scale + bias_ref[:, t * pw:(t + 1) * pw] + nb)
    m = logits[0].max(axis=-1, keepdims=True)
    for t in range(1, nt):
        m = jnp.maximum(m, logits[t].max(axis=-1, keepdims=True))
    l = jnp.zeros((rows_q, 1), F32)
    acc = jnp.zeros((rows_q, hd), F32)
    for t in range(nt):
        pr = jnp.exp(logits[t] - m)
        l = l + jnp.sum(pr, axis=-1, keepdims=True)
        acc = acc + _dot(pr.astype(BF16), vp[t][...].astype(BF16) if t < n_pages else v_new)
    ao_ref[...] = acc / l


def _dsa_sample(qa, qi, wi, knew, vnew, kinew, qn_g, kn_g, bias_s, cache_k, cache_v, cache_i, page_table, *, t_new, hd):
    n_seq, n_pages = page_table.shape
    nk = n_pages * PAGE_SIZE + LANE
    k_sel = min(TOPK_MAX, (n_pages * PAGE_SIZE + t_new) // 4)
    pt = page_table.reshape(-1)

    def page_spec(arr, t):
        return pl.BlockSpec((None,) + arr.shape[1:], lambda b, pt: (pt[b * n_pages + t], 0, 0))

    seq_spec = lambda arr: pl.BlockSpec((None,) + arr.shape[1:], lambda b, pt: (b, 0, 0))
    const_spec = lambda arr: pl.BlockSpec(arr.shape, lambda b, pt: (0, 0))

    scores = pl.pallas_call(
        functools.partial(_idx_sample_kernel, n_pages=n_pages, t_new=t_new),
        grid_spec=pltpu.PrefetchScalarGridSpec(
            num_scalar_prefetch=1, grid=(n_seq,),
            in_specs=[seq_spec(qi), seq_spec(wi), seq_spec(kinew)] + [page_spec(cache_i, t) for t in range(n_pages)],
            out_specs=pl.BlockSpec((SUBLANE, nk), lambda b, pt: (b, 0))),
        out_shape=jax.ShapeDtypeStruct((n_seq * SUBLANE, nk), F32),
        compiler_params=_params(("arbitrary",), VMEM_LIMIT),
        name="idx_sample",
    )(pt, qi, wi, kinew, *([cache_i] * n_pages))

    rows = scores.shape[0]
    tr = LANE if rows % LANE == 0 else rows
    sel = pl.pallas_call(
        functools.partial(_select_kernel, k_sel=k_sel),
        grid=(rows // tr,),
        in_specs=[pl.BlockSpec((tr, nk), lambda r: (r, 0))],
        out_specs=pl.BlockSpec((tr, nk), lambda r: (r, 0)),
        out_shape=jax.ShapeDtypeStruct((rows, nk), F32),
        scratch_shapes=[pltpu.VMEM((tr, nk), jnp.int32)],
        compiler_params=_params(("parallel",), VMEM_LIMIT),
        name="select_sample",
    )(scores)

    qg, kg = qn_g.reshape(1, hd), kn_g.reshape(1, hd)
    return pl.pallas_call(
        functools.partial(_attn_sample_kernel, n_pages=n_pages, hd=hd),
        grid_spec=pltpu.PrefetchScalarGridSpec(
            num_scalar_prefetch=1, grid=(n_seq,),
            in_specs=[seq_spec(qa), pl.BlockSpec((SUBLANE, nk), lambda b, pt: (b, 0)), seq_spec(knew), seq_spec(vnew),
                      const_spec(qg), const_spec(kg), const_spec(bias_s)]
                     + [page_spec(cache_k, t) for t in range(n_pages)]
                     + [page_spec(cache_v, t) for t in range(n_pages)],
            out_specs=[seq_spec(qa), seq_spec(knew)]),
        out_shape=[jax.ShapeDtypeStruct(qa.shape, F32), jax.ShapeDtypeStruct(knew.shape, F32)],
        compiler_params=_params(("arbitrary",), VMEM_LIMIT),
        name="attn_sample",
    )(pt, qa, sel, knew, vnew, qg, kg, bias_s, *([cache_k] * n_pages), *([cache_v] * n_pages))


def _outproj_kernel(x_ref, mo_ref, ao_ref, gt_ref, wm_ref, wa_ref, o_ref):
    y = _dot(mo_ref[...].astype(BF16), wm_ref[...]) + _dot(ao_ref[...].astype(BF16), wa_ref[...])
    o_ref[...] = x_ref[...] + gt_ref[...] * y


def _outproj(x, mo, ao, gt, w_out):
    m, d = x.shape
    mw, aw = mo.shape[1], ao.shape[1]
    tm = _row_tile(m)
    ms = _mod_spec(gt, m, tm, d)
    row = lambda w: pl.BlockSpec((tm, w), lambda i, j: (i, 0))
    return pl.pallas_call(
        _outproj_kernel,
        grid=(m // tm, 1),
        in_specs=[row(d), row(mw), row(aw), ms,
                  pl.BlockSpec((mw, d), lambda i, j: (0, 0)),
                  pl.BlockSpec((aw, d), lambda i, j: (0, 0))],
        out_specs=row(d),
        out_shape=jax.ShapeDtypeStruct((m, d), F32),
        compiler_params=_params(("parallel", "arbitrary"), VMEM_LIMIT),
        name="outproj",
    )(x, mo, ao, gt, w_out[:mw], w_out[mw:])


def _pad_tokens(a, t_pad):
    pad = [(0, 0)] * a.ndim
    pad[1] = (0, t_pad - a.shape[1])
    return jnp.pad(a, pad)


def _layer(x3, mods, lw, bias_p, bias_s, rec_state, past):
    n_seq, T, d = x3.shape
    x = x3.reshape(n_seq * T, d)
    dk, dv = rec_state[0].shape[2], rec_state[0].shape[3]
    mw = M_HEADS * dv
    hd = lw["q_norm_g"].shape[0]
    aw, kvw = A_HEADS * hd, A_KV_HEADS * hd
    iw = IDX_HEADS * IDX_DIM
    per_seq = T % ROW_TILE == 0
    if per_seq:
        mod = lambda k: mods[:, k:k + 1, :]
    else:
        mod = lambda k: jnp.repeat(mods[:, k, :], T, axis=0)
    sh1, sc1, g1, sh2, sc2, g2, sh3, sc3, g3 = [mod(k) for k in range(N_MOD)]

    x, ffn1_w = _ffn(x, sh1, sc1, g1, lw["ffn1_norm_g"], *lw["ffn1_w"])
    qkw = 2 * M_HEADS * dk
    a_off = qkw + 2 * mw
    q_cols = (a_off, aw, a_off + aw + 2 * kvw, iw, hd) if past is None else None
    p, qn, qib = _proj(x, sh2, sc2, lw["mix_norm_g"], lw["w_in"], lw["q_norm_g"], q_cols)
    npk = p.shape[1]
    c0, n0, m0, conv_buf = rec_state
    gate_row = jnp.zeros((1, LANE), F32).at[0, TAIL_IG:TAIL_IG + 2 * M_HEADS].set(lw["mlstm_gate_b"])
    m0_pad = jnp.zeros((n_seq, 1, LANE), F32).at[:, 0, :M_HEADS].set(m0)
    prev8 = jnp.zeros((n_seq, SUBLANE, qkw), F32).at[:, SUBLANE - (CONV_W - 1):, :].set(conv_buf)
    p3 = p.reshape(n_seq, T, npk)
    v_a = p3[:, :, a_off + aw + kvw:a_off + aw + 2 * kvw]
    ki = p3[:, :, npk - LANE + TAIL_KI:npk - LANE + TAIL_KI + IDX_DIM]
    conv_new = p3[:, T - (CONV_W - 1):, :qkw]

    if past is None:
        L = M_CHUNK if T % M_CHUNK == 0 else T
        mo, c_new, n_new, m_new = _mlstm(p3, prev8, lw["mlstm_conv_w"], gate_row, lw["mlstm_out_g"], c0, n0, m0_pad,
                                         L=L, valid=L, seqs_per_step=2 if n_seq % 2 == 0 else 1, out_dtype=BF16)
        mo = mo.reshape(n_seq * T, mw)
        ao, k_n = _dsa_prompt(p, qn, qib, lw["k_norm_g"], bias_p, n_seq=n_seq, T=T, hd=hd)
        k_n = k_n.reshape(n_seq, T, A_KV_HEADS, hd)
    else:
        tp = SUBLANE
        assert T <= tp
        mo, c_new, n_new, m_new = _mlstm(_pad_tokens(p3, tp), prev8, lw["mlstm_conv_w"], gate_row, lw["mlstm_out_g"],
                                         c0, n0, m0_pad, L=tp, valid=T,
                                         seqs_per_step=SUBLANE if n_seq % SUBLANE == 0 else 1, out_dtype=F32)
        mo = mo[:, :T].reshape(n_seq * T, mw)
        cache_k, cache_v, cache_i, page_table = past
        rep = A_HEADS // A_KV_HEADS
        qa = _pad_tokens(p3[:, :, a_off:a_off + aw], tp).reshape(n_seq, tp, A_KV_HEADS, rep, hd)
        qa = qa.transpose(0, 2, 3, 1, 4).reshape(n_seq, A_HEADS * tp, hd)
        qi = _pad_tokens(p3[:, :, a_off + aw + 2 * kvw:a_off + aw + 2 * kvw + iw], tp)
        qi = qi.reshape(n_seq, tp, IDX_HEADS, IDX_DIM).transpose(0, 2, 1, 3).reshape(n_seq, IDX_HEADS * tp, IDX_DIM)
        wi = _pad_tokens(p3[:, :, npk - LANE + TAIL_WI:npk - LANE + TAIL_WI + IDX_HEADS], tp)
        wi = wi.transpose(0, 2, 1).reshape(n_seq, IDX_HEADS * tp, 1)
        knew = _pad_tokens(p3[:, :, a_off + aw:a_off + aw + kvw], tp).reshape(n_seq, tp * A_KV_HEADS, hd)
        vnew = _pad_tokens(v_a, tp).reshape(n_seq, tp * A_KV_HEADS, hd)
        n_pool = cache_k.shape[0]
        ao4, k_n8 = _dsa_sample(qa, qi, wi, knew, vnew, _pad_tokens(ki, tp), lw["q_norm_g"], lw["k_norm_g"], bias_s,
                                cache_k.reshape(n_pool, PAGE_SIZE * A_KV_HEADS, hd),
                                cache_v.reshape(n_pool, PAGE_SIZE * A_KV_HEADS, hd),
                                jnp.swapaxes(cache_i, 1, 2), page_table, t_new=T, hd=hd)
        k_n8 = k_n8.reshape(n_seq, tp, kvw)
        ao = ao4.reshape(n_seq, A_KV_HEADS, rep, tp, hd)[:, :, :, :T].transpose(0, 3, 1, 2, 4).reshape(n_seq * T, aw)
        k_n = k_n8[:, :T].reshape(n_seq, T, A_KV_HEADS, hd)

    x = _outproj(x, mo, ao, g2, lw["w_out"])
    x, ffn2_w = _ffn(x, sh3, sc3, g3, lw["ffn2_norm_g"], *lw["ffn2_w"])
    state = (k_n, v_a.reshape(n_seq, T, A_KV_HEADS, hd), ki, c_new, n_new, m_new[:, 0, :M_HEADS], conv_new)
    return x.reshape(n_seq, T, d), state, dict(lw, ffn1_w=ffn1_w, ffn2_w=ffn2_w)


def kernel(x_prompt, x_sample, c_prompt, c_sample, cache_k, cache_v, cache_idx_k, page_table, state_C, state_n,
           state_m, state_conv, ffn1_norm_g, ffn1_w_gate, ffn1_w_up, ffn1_w_down, mix_norm_g, w_in, mlstm_conv_w,
           mlstm_gate_b, mlstm_out_g, q_norm_g, k_norm_g, t5_bias, w_out, ffn2_norm_g, ffn2_w_gate, ffn2_w_up,
           ffn2_w_down, w_ada, b_ada):
    depth = w_in.shape[0]
    bp, tp_len, d = x_prompt.shape
    bs, ts_len, _ = x_sample.shape
    dk, dv = state_C.shape[3], state_C.shape[4]
    mw = M_HEADS * dv
    hd = q_norm_g.shape[1]
    past_len = page_table.shape[1] * PAGE_SIZE
    bias_p, bias_s = _bias_tables(t5_bias, tp_len, past_len, ts_len)
    rep = A_HEADS // A_KV_HEADS
    bias_s = bias_s.reshape(A_HEADS * SUBLANE, A_KV_HEADS * (past_len + LANE))

    xp, xs = x_prompt, x_sample
    st_p, st_s = [], []
    for l in range(depth):
        lw = dict(ffn1_norm_g=ffn1_norm_g[l], ffn1_w=(ffn1_w_gate[l], ffn1_w_up[l], ffn1_w_down[l]),
                  mix_norm_g=mix_norm_g[l], w_in=_pack_w_in(w_in[l], mw, A_HEADS * hd, A_KV_HEADS * hd),
                  mlstm_conv_w=mlstm_conv_w[l], mlstm_gate_b=mlstm_gate_b[l], mlstm_out_g=mlstm_out_g[l],
                  q_norm_g=q_norm_g[l], k_norm_g=k_norm_g[l], w_out=w_out[l].astype(BF16),
                  ffn2_norm_g=ffn2_norm_g[l], ffn2_w=(ffn2_w_gate[l], ffn2_w_up[l], ffn2_w_down[l]))
        mods = _ada(jnp.concatenate([c_prompt, c_sample], axis=0), w_ada[l], b_ada[l])
        mods = mods.reshape(bp + bs, N_MOD, d)
        init_s = (state_C[l], state_n[l], state_m[l], state_conv[l])
        xs, ss, lw = _layer(xs, mods[bp:], lw, bias_p, bias_s, init_s,
                            (cache_k[l], cache_v[l], cache_idx_k[l], page_table))
        init_p = (jnp.zeros((bp, M_HEADS, dk, dv), F32), jnp.zeros((bp, M_HEADS, dk), F32),
                  jnp.zeros((bp, M_HEADS), F32), jnp.zeros((bp, CONV_W - 1, 2 * M_HEADS * dk), F32))
        xp, sp, _ = _layer(xp, mods[:bp], lw, bias_p, bias_s, init_p, None)
        st_p.append(sp)
        st_s.append(ss)
    stack = lambda sts, k: sts[0][k][None] if depth == 1 else jnp.stack([s[k] for s in sts])
    return (xp, xs) + tuple(stack(st_p, k) for k in range(7)) + tuple(stack(st_s, k) for k in range(7))
```

```python
import functools
import math

import numpy as np
import jax
import jax.numpy as jnp
from jax import lax
from jax.experimental import pallas as pl
from jax.experimental.pallas import tpu as pltpu

F32 = jnp.float32
BF16 = jnp.bfloat16

M_HEADS = 4
CONV_W = 4
M_CHUNK = 64
A_HEADS = 8
A_KV_HEADS = 2
IDX_HEADS = 8
IDX_DIM = 64
TOPK_MAX = 256
Q_BLOCK = 128
PAGE_SIZE = 128
N_BUCKETS = 32
MAX_DISTANCE = 128
N_MOD = 9
EPS = 1e-6
NEG = -1e30

LANE = 128
SUBLANE = 8
ROW_TILE = 512
VMEM_LIMIT = 56 * 1024 * 1024

TAIL_KI = 0
TAIL_WI = IDX_DIM
TAIL_IG = IDX_DIM + IDX_HEADS
TAIL_FG = TAIL_IG + M_HEADS

INT_MIN = -2 ** 31
LOG2E = math.log2(math.e)


def _params(sem, vmem=None, flags=None):
    return pltpu.CompilerParams(dimension_semantics=sem, vmem_limit_bytes=vmem, flags=flags)


def _dot(a, b):
    return jnp.dot(a, b, preferred_element_type=F32)


def _dot_nt(a, b):
    return lax.dot_general(a, b, (((1,), (1,)), ((), ())), preferred_element_type=F32)


def _dot_tn(a, b):
    return lax.dot_general(a, b, (((0,), (0,)), ((), ())), preferred_element_type=F32)


def _rms(x, g):
    return x * lax.rsqrt(jnp.mean(x * x, axis=-1, keepdims=True) + EPS) * g


def _row_tile(n):
    return ROW_TILE if n % ROW_TILE == 0 else n


REDUCE_CHAINS = 4


def _reduce_rows(x, op):
    n, c = x.shape
    groups = n // SUBLANE
    chains = REDUCE_CHAINS if groups % REDUCE_CHAINS == 0 else 1
    x3 = x.reshape(groups, SUBLANE, c)
    per = groups // chains
    parts = [op(x3[k * per:(k + 1) * per], axis=0) for k in range(chains)]
    while len(parts) > 1:
        parts = [op(jnp.stack(parts[k:k + 2]), axis=0) for k in range(0, len(parts), 2)]
    return op(parts[0], axis=0, keepdims=True)


def _ada_kernel(c_ref, w_ref, b_ref, o_ref):
    o_ref[...] = _dot(c_ref[...].astype(BF16), w_ref[...].astype(BF16)) + b_ref[...]


def _ada(c_all, w_ada, b_ada):
    r, d = c_all.shape
    n = w_ada.shape[1]
    tn = 1024 if n % 1024 == 0 else n
    return pl.pallas_call(
        _ada_kernel,
        grid=(n // tn,),
        in_specs=[pl.BlockSpec((r, d), lambda j: (0, 0)),
                  pl.BlockSpec((d, tn), lambda j: (0, j)),
                  pl.BlockSpec((1, tn), lambda j: (0, j))],
        out_specs=pl.BlockSpec((r, tn), lambda j: (0, j)),
        out_shape=jax.ShapeDtypeStruct((r, n), F32),
        compiler_params=_params(("arbitrary",), VMEM_LIMIT),
        name="ada",
    )(c_all, w_ada, b_ada.reshape(1, n))


def _adaln(x, g, sc, sh):
    return (_rms(x, g) * (1.0 + sc) + sh).astype(BF16)


def _ffn_kernel(x_ref, gt_ref, *rest, prenormed, emit):
    if prenormed:
        hin_ref, wg_ref, wu_ref, wd_ref, o_ref, *rest = rest
    else:
        sh_ref, sc_ref, g_ref, wg_ref, wu_ref, wd_ref, o_ref, *rest = rest
    bf16_out, rest = (rest[:3], rest[3:]) if emit else ((), rest)
    acc_ref = rest[-1]
    j = pl.program_id(1)

    @pl.when(j == 0)
    def _():
        if not prenormed:
            rest[0][...] = _adaln(x_ref[...], g_ref[...], sc_ref[...], sh_ref[...])
        acc_ref[...] = jnp.zeros_like(acc_ref)

    wg, wu, wd = wg_ref[...], wu_ref[...], wd_ref[...]
    if emit:
        wg, wu, wd = wg.astype(BF16), wu.astype(BF16), wd.astype(BF16)
        for ref, w in zip(bf16_out, (wg, wu, wd)):
            ref[...] = w
    h = hin_ref[...] if prenormed else rest[0][...]
    a = _dot(h, wg)
    u = _dot(h, wu)
    act = (a * jax.nn.sigmoid(a) * u).astype(BF16)
    acc_ref[...] += _dot(act, wd)

    @pl.when(j == pl.num_programs(1) - 1)
    def _():
        o_ref[...] = x_ref[...] + 0.5 * gt_ref[...] * acc_ref[...]


def _mod_spec(mod, m, tm, d):
    if mod.ndim == 2:
        return pl.BlockSpec((tm, d), lambda i, j: (i, 0))
    tiles_per_seq = (m // mod.shape[0]) // tm
    return pl.BlockSpec((None, 1, d), lambda i, j: (i // tiles_per_seq, 0, 0))


def _ffn(x, gt, wg, wu, wd, *, norm=None, h=None):
    m, d = x.shape
    f = wg.shape[1]
    tm = _row_tile(m)
    emit = wg.dtype != BF16
    assert not emit or m == tm
    tf = (256 if emit else 512) if f % 512 == 0 else f
    row = pl.BlockSpec((tm, d), lambda i, j: (i, 0))
    w_specs = [pl.BlockSpec((d, tf), lambda i, j: (0, j)),
               pl.BlockSpec((d, tf), lambda i, j: (0, j)),
               pl.BlockSpec((tf, d), lambda i, j: (j, 0))]
    if h is None:
        sh, sc, g = norm
        ms = _mod_spec(sh, m, tm, d)
        pre_specs, pre_args = [ms, ms, pl.BlockSpec((1, d), lambda i, j: (0, 0))], [sh, sc, g.reshape(1, d)]
        scratch = [pltpu.VMEM((tm, d), BF16)]
    else:
        pre_specs, pre_args, scratch = [row], [h], []
    out_specs = [row]
    out_shape = [jax.ShapeDtypeStruct((m, d), F32)]
    if emit:
        out_specs += w_specs
        out_shape += [jax.ShapeDtypeStruct(w.shape, BF16) for w in (wg, wu, wd)]
    y, *wb = pl.pallas_call(
        functools.partial(_ffn_kernel, prenormed=h is not None, emit=emit),
        grid=(m // tm, f // tf),
        in_specs=[row, _mod_spec(gt, m, tm, d)] + pre_specs + w_specs,
        out_specs=out_specs,
        out_shape=out_shape,
        scratch_shapes=scratch + [pltpu.VMEM((tm, d), F32)],
        compiler_params=_params(("parallel", "arbitrary"), VMEM_LIMIT),
        name="ffn",
    )(x, gt, *pre_args, wg, wu, wd)
    return y, (tuple(wb) if emit else (wg, wu, wd))


def _conv_silu(xbuf_ref, x_new, cw_ref):
    rows = x_new.shape[0]
    xbuf_ref[SUBLANE:SUBLANE + rows, :] = x_new
    base = SUBLANE - (CONV_W - 1)
    y = xbuf_ref[base:base + rows, :] * cw_ref[0:1, :]
    for j in range(1, CONV_W):
        y = y + xbuf_ref[base + j:base + j + rows, :] * cw_ref[j:j + 1, :]
    tail_rows = xbuf_ref[rows:rows + SUBLANE, :]
    xbuf_ref[0:SUBLANE, :] = tail_rows
    return y * jax.nn.sigmoid(y)


def _proj_kernel(x_ref, sh_ref, sc_ref, g_ref, w_ref, *rest, q_cols):
    if q_cols:
        qg_ref, o_ref, qn_ref, qib_ref = rest
    p = _dot(_adaln(x_ref[...], g_ref[...], sc_ref[...], sh_ref[...]), w_ref[...])
    if not q_cols:
        rest[0][...] = p
        return
    o_ref[...] = p
    a0, aw, i0, iw, hd = q_cols
    for hh in range(aw // hd):
        qn = _rms(p[:, a0 + hh * hd:a0 + (hh + 1) * hd], qg_ref[...]) * (hd ** -0.5 * LOG2E)
        qn_ref[:, hh * hd:(hh + 1) * hd] = qn.astype(BF16)
    qib_ref[...] = p[:, i0:i0 + iw].astype(BF16)


PROJ_ROW_TILE = 256


def _proj(x, sh, sc, g, w, prompt=None):
    m, d = x.shape
    n = w.shape[1]
    tm = PROJ_ROW_TILE if m % PROJ_ROW_TILE == 0 else m
    ms = _mod_spec(sh, m, tm, d)
    row = lambda width: pl.BlockSpec((tm, width), lambda i, j: (i, 0))
    in_specs = [row(d), ms, ms, pl.BlockSpec((1, d), lambda i, j: (0, 0)),
                pl.BlockSpec((d, n), lambda i, j: (0, 0), pipeline_mode=pl.Buffered(1))]
    args = [x, sh, sc, g.reshape(1, d), w]
    out_specs, out_shape = [row(n)], [jax.ShapeDtypeStruct((m, n), F32)]
    q_cols = None
    if prompt:
        qn_g, q_cols = prompt
        a0, aw, i0, iw, hd = q_cols
        in_specs.append(pl.BlockSpec((1, hd), lambda i, j: (0, 0)))
        args.append(qn_g.reshape(1, hd))
        out_specs += [row(aw), row(iw)]
        out_shape += [jax.ShapeDtypeStruct((m, aw), BF16), jax.ShapeDtypeStruct((m, iw), BF16)]
    out = pl.pallas_call(
        functools.partial(_proj_kernel, q_cols=q_cols),
        grid=(m // tm, 1),
        in_specs=in_specs,
        out_specs=out_specs,
        out_shape=out_shape,
        compiler_params=_params(("parallel", "arbitrary"), VMEM_LIMIT),
        name="proj",
    )(*args)
    return out if prompt else (out[0], None, None)


def _pack_w_in(w_in, mw, aw, kvw):
    d = w_in.shape[0]
    o = np.cumsum([0, mw, mw, mw, mw, M_HEADS, M_HEADS, aw, kvw, kvw, IDX_HEADS * IDX_DIM, IDX_DIM, IDX_HEADS])
    seg = lambda k: w_in[:, o[k]:o[k + 1]]
    order = (0, 1, 2, 3, 6, 7, 8, 9, 10, 11, 4, 5)
    pad = (-o[-1]) % LANE
    return jnp.concatenate([seg(k) for k in order] + [jnp.zeros((d, pad), w_in.dtype)], axis=1).astype(BF16)


def _log_sigmoid(x):
    return jnp.minimum(x, 0.0) - jnp.log1p(jnp.exp(-jnp.abs(x)))


def _mlstm_kernel(qk_ref, v_ref, o_ref, tail_ref, prev_ref, cw_ref, gb_ref, og_ref, c0_ref, n0_ref, m0_ref,
                  mo_ref, cst_ref, nst_ref, mst_ref, xbuf_ref, *, L, valid, dk, dv):
    @pl.when(pl.program_id(1) == 0)
    def _():
        xbuf_ref[:, 0:SUBLANE, :] = prev_ref[...]
        cst_ref[...] = c0_ref[...]
        nst_ref[...] = n0_ref[...]
        mst_ref[...] = m0_ref[...]

    for sb in range(qk_ref.shape[0]):
        _mlstm_one(qk_ref.at[sb], v_ref.at[sb], o_ref.at[sb], tail_ref.at[sb], cw_ref, gb_ref, og_ref, mo_ref.at[sb],
                   cst_ref.at[sb], nst_ref.at[sb], mst_ref.at[sb], xbuf_ref.at[sb], L=L, valid=valid, dk=dk, dv=dv)


def _mlstm_one(qk_ref, v_ref, o_ref, tail_ref, cw_ref, gb_ref, og_ref, mo_ref, cst_ref, nst_ref, mst_ref, xbuf_ref,
               *, L, valid, dk, dv):
    width = M_HEADS * dk
    qk = _conv_silu(xbuf_ref, qk_ref[...], cw_ref)

    gates = tail_ref[...] + gb_ref[...]
    row = lax.broadcasted_iota(jnp.int32, (L, LANE), 0)
    lf = _log_sigmoid(gates)
    ig = gates
    if valid < L:
        lf = jnp.where(row < valid, lf, 0.0)
        ig = jnp.where(row < valid, ig, -jnp.inf)
    bcum = lf
    s = 1
    while s < L:
        bcum = bcum + jnp.where(row >= s, pltpu.roll(bcum, s, axis=0), 0.0)
        s *= 2
    bmi_t = (pltpu.roll(ig, TAIL_FG - TAIL_IG, axis=1) - bcum).T

    ri = lax.broadcasted_iota(jnp.int32, (L, L), 0)
    ci = lax.broadcasted_iota(jnp.int32, (L, L), 1)
    causal = ri >= ci

    for h in range(M_HEADS):
        li, lfh = TAIL_IG + h, TAIL_FG + h
        a_col = bcum[:, lfh:lfh + 1]
        ig_col = ig[:, li:li + 1]
        b_row = bmi_t[lfh:lfh + 1, :]
        m0 = mst_ref[:, h:h + 1]
        d_log = jnp.where(causal, a_col + b_row, -jnp.inf)
        s_log = a_col + m0
        m_col = jnp.maximum(s_log, jnp.max(d_log, axis=-1, keepdims=True))
        dw = jnp.exp(d_log - m_col)
        sw = jnp.exp(s_log - m_col)

        q = qk[:, h * dk:(h + 1) * dk]
        k = qk[:, width + h * dk:width + (h + 1) * dk] * (dk ** -0.5)
        v = v_ref[:, h * dv:(h + 1) * dv]
        qb, kb = q.astype(BF16), k.astype(BF16)
        c_prev = cst_ref[h]
        n_prev = nst_ref[h:h + 1, :]
        scores = _dot_nt(qb, kb) * dw
        num = _dot(scores.astype(BF16), v.astype(BF16)) + sw * _dot(qb, c_prev.astype(BF16))
        den = jnp.sum(scores, axis=-1, keepdims=True) + sw * jnp.sum(q * n_prev, axis=-1, keepdims=True)
        hh = num / jnp.maximum(jnp.abs(den), jnp.exp(-m_col))

        m_new = m_col[L - 1:L, :]
        a_last = a_col[L - 1:L, :]
        wl = jnp.exp(a_last - a_col + ig_col - m_new)
        decay = jnp.exp(a_last + m0 - m_new)
        cst_ref[h] = decay * c_prev + _dot_tn(kb, (wl * v).astype(BF16))
        nst_ref[h:h + 1, :] = decay * n_prev + jnp.sum(wl * k, axis=0, keepdims=True)
        mst_ref[:, h:h + 1] = m_new

        hn = hh * lax.rsqrt(jnp.mean(hh * hh, axis=-1, keepdims=True) + EPS) * og_ref[:, h * dv:(h + 1) * dv]
        gate = jax.nn.sigmoid(o_ref[:, h * dv:(h + 1) * dv])
        mo_ref[:, h * dv:(h + 1) * dv] = (hn * gate).astype(mo_ref.dtype)


def _mlstm(p3, prev8, conv_w, gate_row, out_g, c0, n0, m0, *, L, valid, seqs_per_step, out_dtype):
    n_seq, t_all, npk = p3.shape
    nc = t_all // L
    sb = seqs_per_step
    assert n_seq % sb == 0
    dk = c0.shape[2]
    dv = c0.shape[3]
    mw = M_HEADS * dv
    qkw = 2 * M_HEADS * dk
    assert qkw % mw == 0 and (2 * qkw) % mw == 0
    kern = functools.partial(_mlstm_kernel, L=L, valid=valid, dk=dk, dv=dv)
    seq = lambda shape: pl.BlockSpec((sb,) + shape, lambda b, c: (b,) + (0,) * len(shape))
    return pl.pallas_call(
        kern,
        grid=(n_seq // sb, nc),
        in_specs=[pl.BlockSpec((sb, L, qkw), lambda b, c: (b, c, 0)),
                  pl.BlockSpec((sb, L, mw), lambda b, c: (b, c, qkw // mw)),
                  pl.BlockSpec((sb, L, mw), lambda b, c: (b, c, qkw // mw + 1)),
                  pl.BlockSpec((sb, L, LANE), lambda b, c: (b, c, npk // LANE - 1)),
                  seq((SUBLANE, qkw)),
                  pl.BlockSpec((CONV_W, qkw), lambda b, c: (0, 0)),
                  pl.BlockSpec((1, LANE), lambda b, c: (0, 0)),
                  pl.BlockSpec((1, mw), lambda b, c: (0, 0)),
                  seq((M_HEADS, dk, dv)), seq((M_HEADS, dk)), seq((1, LANE))],
        out_specs=[pl.BlockSpec((sb, L, mw), lambda b, c: (b, c, 0)),
                   seq((M_HEADS, dk, dv)), seq((M_HEADS, dk)), seq((1, LANE))],
        out_shape=[jax.ShapeDtypeStruct((n_seq, t_all, mw), out_dtype),
                   jax.ShapeDtypeStruct(c0.shape, F32),
                   jax.ShapeDtypeStruct(n0.shape, F32),
                   jax.ShapeDtypeStruct(m0.shape, F32)],
        scratch_shapes=[pltpu.VMEM((sb, SUBLANE + L, qkw), F32)],
        compiler_params=_params(("parallel", "arbitrary"), VMEM_LIMIT),
        name="mlstm",
    )(p3, p3, p3, p3, prev8, conv_w, gate_row, out_g.reshape(1, mw), c0, n0, m0)


def _bucket_np(dist):
    me = N_BUCKETS // 2
    d = np.maximum(dist, 0)
    ratio = np.log(np.maximum(d, 1).astype(np.float64) / me) / math.log(MAX_DISTANCE / me)
    large = np.minimum(me + (ratio * (N_BUCKETS - me)).astype(np.int64), N_BUCKETS - 1)
    return np.where(d < me, d, large).astype(np.int32)


def _bias_kernel(t5_ref, bkp_ref, bks_ref, vs_ref, far_ref, op_ref, os_ref):
    rep = A_HEADS // A_KV_HEADS

    def lookup(bk, h):
        acc = jnp.zeros(bk.shape, F32)
        for b in range(N_BUCKETS):
            acc = jnp.where(bk == b, t5_ref[b, h], acc)
        return acc

    for h in range(A_HEADS):
        half = (h % 2) * Q_BLOCK
        op_ref[h // 2, :, half:half + Q_BLOCK] = (lookup(bkp_ref[...], h) - t5_ref[far_ref[0], h]) * LOG2E
        os_ref[h] = jnp.where(vs_ref[h // rep] > 0, lookup(bks_ref[...], h), NEG)


def _bias_tables(t5_bias, t_prompt, past, t_new):
    c = np.arange(2 * Q_BLOCK)[:, None]
    r = np.arange(Q_BLOCK)[None, :]
    bkp = _bucket_np(Q_BLOCK + r - c)
    far = _bucket_np(np.arange(Q_BLOCK + 1, max(t_prompt, past + t_new) + Q_BLOCK))
    assert (far == far[0]).all(), "bias must be constant beyond one query block"
    nk = past + LANE
    tq = np.arange(SUBLANE)[:, None]
    col = np.arange(2 * nk)[None, :]
    s_pos, s_grp = col // A_KV_HEADS, col % A_KV_HEADS
    bks = _bucket_np(past + tq - s_pos)
    vis = np.stack([(s_grp == g) & (s_pos - past <= tq) for g in range(A_KV_HEADS)]).astype(np.int32)
    return pl.pallas_call(
        _bias_kernel,
        in_specs=[pl.BlockSpec(memory_space=pltpu.SMEM),
                  pl.BlockSpec(memory_space=pltpu.VMEM),
                  pl.BlockSpec(memory_space=pltpu.VMEM),
                  pl.BlockSpec(memory_space=pltpu.VMEM),
                  pl.BlockSpec(memory_space=pltpu.SMEM)],
        out_specs=[pl.BlockSpec(memory_space=pltpu.VMEM), pl.BlockSpec(memory_space=pltpu.VMEM)],
        out_shape=[jax.ShapeDtypeStruct((A_HEADS // 2, 2 * Q_BLOCK, 2 * Q_BLOCK), F32),
                   jax.ShapeDtypeStruct((A_HEADS, SUBLANE, 2 * nk), F32)],
        name="t5_bias_tables",
    )(t5_bias, jnp.asarray(bkp), jnp.asarray(bks), jnp.asarray(vis), jnp.asarray(far[:1]))


def _sort_key(score):
    bits = lax.bitcast_convert_type(score, jnp.int32)
    return jnp.where(bits < 0, bits ^ jnp.int32(0x7FFFFFFF), bits)


def _kth_key(count_ge, shape, k_sel):
    def body(it, ans_u):
        cand_u = ans_u | lax.shift_left(jnp.int32(1), jnp.int32(31) - it)
        cnt = count_ge(cand_u ^ jnp.int32(INT_MIN))
        return jnp.where(cnt >= k_sel, cand_u, ans_u)

    ans_u = lax.fori_loop(0, 32, body, jnp.zeros(shape, jnp.int32))
    return ans_u ^ jnp.int32(INT_MIN)


def _triangle(below):
    r = lax.broadcasted_iota(jnp.int32, (LANE, LANE), 0)
    c = lax.broadcasted_iota(jnp.int32, (LANE, LANE), 1)
    return jnp.where((c < r) if below else (r < c), 1.0, 0.0).astype(BF16)


def _select_tiles(key_tiles, thr, k_sel, key_axis):
    count = lambda x: jnp.sum(x, axis=key_axis, keepdims=True)
    c_gt = jnp.zeros(thr.shape, F32)
    for kt in key_tiles:
        c_gt = c_gt + count(jnp.where(kt > thr, 1.0, 0.0))
    need = k_sel - c_gt
    tri = _triangle(below=(key_axis == 0))
    run = jnp.zeros(thr.shape, F32)
    out = []
    for kt in key_tiles:
        eq = jnp.where(kt == thr, 1.0, 0.0)
        eqb = eq.astype(BF16)
        before = (_dot(tri, eqb) if key_axis == 0 else _dot(eqb, tri)) + run
        out.append(jnp.where(kt > thr, 1.0, jnp.where(before < need, eq, 0.0)))
        run = run + count(eq)
    return out


FAR_STEP = 2


def _dsa_prompt_kernel(qn_ref, qi_ref, qtail_ref, k_ref, v_ref, ktail_ref, kg_ref, bias_ref,
                       ao_ref, kn_ref, vo_ref,
                       kb_ref, vt_ref, kib_ref, key_ref, negb_ref, *, T, k_sel, hd):
    i = pl.program_id(1)
    nb = T // Q_BLOCK
    rep = A_HEADS // A_KV_HEADS
    near_w = 2 * LANE

    @pl.when(i == 0)
    def _():
        kb_ref[0:LANE, :] = jnp.zeros((LANE, kb_ref.shape[1]), BF16)
        vt_ref[:, 0:LANE] = jnp.zeros((vt_ref.shape[0], LANE), BF16)
        kib_ref[0:LANE, :] = jnp.zeros((LANE, IDX_DIM), BF16)
        for g in range(A_KV_HEADS):
            kn = _rms(k_ref[:, g * hd:(g + 1) * hd], kg_ref[...])
            kn_ref[pl.ds(g, T, stride=A_KV_HEADS), :] = kn
            vo_ref[pl.ds(g, T, stride=A_KV_HEADS), :] = v_ref[:, g * hd:(g + 1) * hd]
            kb_ref[LANE:, g * hd:(g + 1) * hd] = kn.astype(BF16)
        vt_ref[:, LANE:] = v_ref[...].T.astype(BF16)
        kib_ref[LANE:, :] = ktail_ref[:, TAIL_KI:TAIL_KI + IDX_DIM].astype(BF16)

    wi_t = qtail_ref[...].T[TAIL_WI:TAIL_WI + IDX_HEADS, :] * (IDX_HEADS ** -0.5) * (IDX_DIM ** -0.5)
    qi_pairs = [jnp.concatenate([qi_ref[:, (2 * a + hh) * IDX_DIM:(2 * a + hh + 1) * IDX_DIM] for hh in range(2)],
                                axis=0) for a in range(IDX_HEADS // 2)]
    qn_pairs = [jnp.concatenate([qn_ref[:, (2 * a + hh) * hd:(2 * a + hh + 1) * hd] for hh in range(2)], axis=0)
                for a in range(A_HEADS // 2)]

    near0 = pl.multiple_of(i * LANE, LANE)
    ki_near = kib_ref[pl.ds(near0, near_w), :]
    k_near = kb_ref[pl.ds(near0, near_w), :]
    vt_near = vt_ref[:, pl.ds(near0, near_w)]
    kc = lax.broadcasted_iota(jnp.int32, (near_w, Q_BLOCK), 0)
    qr = lax.broadcasted_iota(jnp.int32, (near_w, Q_BLOCK), 1)
    near_ok = jnp.where(kc <= qr + LANE, 1.0, 0.0) * jnp.maximum(jnp.where(kc >= LANE, 1.0, 0.0),
                                                                  (i >= 1).astype(F32))

    def scores(ki):
        sc = jnp.zeros((ki.shape[0], Q_BLOCK), F32)
        for a in range(IDX_HEADS // 2):
            s2 = _dot_nt(ki, qi_pairs[a])
            for hh in range(2):
                j = 2 * a + hh
                sc = sc + wi_t[j:j + 1, :] * jnp.maximum(s2[:, hh * Q_BLOCK:(hh + 1) * Q_BLOCK], 0.0)
        return sc

    def body(wf):
        fw = wf * LANE
        w_all = fw + near_w
        key_ref[fw:w_all, :] = _sort_key(jnp.where(near_ok > 0.0, scores(ki_near) + 0.0, NEG))
        if wf:
            far_ok = lax.broadcasted_iota(jnp.int32, (fw, Q_BLOCK), 0) < (i - 1) * LANE
            key_ref[0:fw, :] = _sort_key(jnp.where(far_ok, scores(kib_ref[LANE:LANE + fw, :]) + 0.0, NEG))

        def count_ge(cand):
            return _reduce_rows(jnp.where(key_ref[0:w_all, :] >= cand, 1.0, 0.0), jnp.sum)

        thr = _kth_key(count_ge, (1, Q_BLOCK), float(k_sel))
        sel = _select_tiles([key_ref[t * LANE:(t + 1) * LANE, :] for t in range(wf + 2)], thr, float(k_sel), 0)
        for t in range(wf):
            ok = sel[t] * (t < i - 1).astype(F32)
            negb_ref[t * LANE:(t + 1) * LANE, :] = jnp.where(ok > 0.0, 0.0, NEG)
        for t in range(2):
            ok = sel[wf + t] * near_ok[t * LANE:(t + 1) * LANE, :]
            negb_ref[fw + t * LANE:fw + (t + 1) * LANE, :] = jnp.where(ok > 0.0, 0.0, NEG)

        for a in range(A_HEADS // 2):
            g = (2 * a) // rep
            cs = slice(g * hd, (g + 1) * hd)
            qn2 = qn_pairs[a]
            nb_near = negb_ref[fw:w_all, :]
            s_near = _dot_nt(k_near[:, cs], qn2) + bias_ref[a] + jnp.concatenate([nb_near, nb_near], axis=1)
            m = _reduce_rows(s_near, jnp.max)
            if wf:
                nb_far = negb_ref[0:fw, :]
                s_far = _dot_nt(kb_ref[LANE:LANE + fw, cs], qn2) + jnp.concatenate([nb_far, nb_far], axis=1)
                m = jnp.maximum(m, _reduce_rows(s_far, jnp.max))
            p_near = jnp.exp2(s_near - m)
            l = _reduce_rows(p_near, jnp.sum)
            acc = _dot(vt_near[cs, :], p_near.astype(BF16))
            if wf:
                p_far = jnp.exp2(s_far - m)
                l = l + _reduce_rows(p_far, jnp.sum)
                acc = acc + _dot(vt_ref[cs, LANE:LANE + fw], p_far.astype(BF16))
            out_t = acc / l
            for hh in range(2):
                h = 2 * a + hh
                ao_ref[:, h * hd:(h + 1) * hd] = out_t[:, hh * Q_BLOCK:(hh + 1) * Q_BLOCK].T.astype(ao_ref.dtype)

    variants = sorted({min(FAR_STEP * -(-x // FAR_STEP), nb) for x in range(max(nb - 1, 1))})
    wf_needed = jnp.minimum((jnp.maximum(i - 1, 0) + FAR_STEP - 1) // FAR_STEP * FAR_STEP, nb)
    for wf in variants:
        pl.when(wf_needed == wf)(functools.partial(body, wf))


def _dsa_prompt(p, qn, qib, kn_g, bias_p, *, n_seq, T, hd):
    rows = p.shape[0]
    nb = T // Q_BLOCK
    aw = A_HEADS * hd
    kvw = A_KV_HEADS * hd
    iw = IDX_HEADS * IDX_DIM
    k_sel = min(TOPK_MAX, T // 4)
    kcol = p.shape[1] - LANE - iw - 2 * kvw
    assert kcol % kvw == 0
    k_blk = kcol // kvw
    tail_blk = p.shape[1] // LANE - 1
    kern = functools.partial(_dsa_prompt_kernel, T=T, k_sel=k_sel, hd=hd)
    return pl.pallas_call(
        kern,
        grid=(n_seq, nb),
        in_specs=[pl.BlockSpec((Q_BLOCK, aw), lambda b, i: (b * nb + i, 0)),
                  pl.BlockSpec((Q_BLOCK, iw), lambda b, i: (b * nb + i, 0)),
                  pl.BlockSpec((Q_BLOCK, LANE), lambda b, i: (b * nb + i, tail_blk)),
                  pl.BlockSpec((T, kvw), lambda b, i: (b, k_blk)),
                  pl.BlockSpec((T, kvw), lambda b, i: (b, k_blk + 1)),
                  pl.BlockSpec((T, LANE), lambda b, i: (b, tail_blk)),
                  pl.BlockSpec((1, hd), lambda b, i: (0, 0)),
                  pl.BlockSpec((A_HEADS // 2, 2 * Q_BLOCK, 2 * Q_BLOCK), lambda b, i: (0, 0, 0))],
        out_specs=[pl.BlockSpec((Q_BLOCK, aw), lambda b, i: (b * nb + i, 0)),
                   pl.BlockSpec((A_KV_HEADS * T, hd), lambda b, i: (b, 0)),
                   pl.BlockSpec((A_KV_HEADS * T, hd), lambda b, i: (b, 0))],
        out_shape=[jax.ShapeDtypeStruct((rows, aw), BF16),
                   jax.ShapeDtypeStruct((A_KV_HEADS * rows, hd), F32),
                   jax.ShapeDtypeStruct((A_KV_HEADS * rows, hd), F32)],
        scratch_shapes=[pltpu.VMEM((T + LANE, kvw), BF16), pltpu.VMEM((kvw, T + LANE), BF16),
                        pltpu.VMEM((T + LANE, IDX_DIM), BF16),
                        pltpu.VMEM((T + 2 * LANE, Q_BLOCK), jnp.int32),
                        pltpu.VMEM((T + 2 * LANE, Q_BLOCK), F32)],
        compiler_params=_params(("parallel", "arbitrary"), VMEM_LIMIT),
        name="dsa_prompt",
    )(qn, qib, p, p, p, p, kn_g.reshape(1, hd), bias_p)


def _pad_rows(x, n):
    return jnp.concatenate([x, jnp.zeros((n - x.shape[0], x.shape[1]), x.dtype)], axis=0)


def _idx_sample_kernel(pt_ref, qi_ref, wi_ref, kinew_ref, *rest, n_pages, t_new):
    ip, sc_ref = rest[:n_pages], rest[n_pages]
    del pt_ref
    qi = qi_ref[...].astype(BF16)
    wi = wi_ref[...] * (IDX_HEADS ** -0.5) * (IDX_DIM ** -0.5)
    tq = lax.broadcasted_iota(jnp.int32, (SUBLANE, LANE), 0)
    lane = lax.broadcasted_iota(jnp.int32, (SUBLANE, LANE), 1)
    for t in range(n_pages + 1):
        r = _dot(qi, ip[t][...].astype(BF16)) if t < n_pages else _dot_nt(qi, _pad_rows(kinew_ref[...], LANE).astype(BF16))
        sc = jnp.zeros((SUBLANE, LANE), F32)
        for j in range(IDX_HEADS):
            sc = sc + wi[j * SUBLANE:(j + 1) * SUBLANE, :] * jnp.maximum(r[j * SUBLANE:(j + 1) * SUBLANE, :], 0.0)
        sc = sc + 0.0
        if t == n_pages:
            sc = jnp.where(lane <= tq, sc, jnp.where(lane < t_new, NEG, -jnp.inf))
        sc_ref[:, t * LANE:(t + 1) * LANE] = sc


def _select_kernel(sc_ref, sel_ref, key_ref, *, k_sel):
    rows, nk = sc_ref.shape
    key_ref[...] = _sort_key(sc_ref[...])

    def count_ge(cand):
        return jnp.sum(jnp.where(key_ref[...] >= cand, 1.0, 0.0), axis=1, keepdims=True)

    thr = _kth_key(count_ge, (rows, 1), float(k_sel))
    sel = _select_tiles([key_ref[:, t * LANE:(t + 1) * LANE] for t in range(nk // LANE)], thr, float(k_sel), 1)
    for t in range(nk // LANE):
        sel_ref[:, t * LANE:(t + 1) * LANE] = sel[t]


def _attn_sample_kernel(pt_ref, qa_ref, sel_ref, knew_ref, vnew_ref, qg_ref, kg_ref, bias_ref, *rest,
                        n_pages, hd):
    kp, vp = rest[:n_pages], rest[n_pages:2 * n_pages]
    ao_ref, kn_ref = rest[2 * n_pages:]
    del pt_ref
    nt = n_pages + 1
    pw = A_KV_HEADS * PAGE_SIZE
    scale = hd ** -0.5
    rows_q = qa_ref.shape[0]

    kn_new = _rms(knew_ref[...], kg_ref[...])
    kn_ref[...] = kn_new
    k_new = _pad_rows(kn_new, pw).astype(BF16)
    v_new = _pad_rows(vnew_ref[...], pw).astype(BF16)
    qn = _rms(qa_ref[...], qg_ref[...]).astype(BF16)

    tok = lax.broadcasted_iota(jnp.int32, (PAGE_SIZE, pw), 0)
    col = lax.broadcasted_iota(jnp.int32, (PAGE_SIZE, pw), 1)
    spread = jnp.where(col // A_KV_HEADS == tok, 1.0, 0.0).astype(BF16)
    sel_rows = jnp.concatenate([sel_ref[:, t * LANE:(t + 1) * LANE] for t in range(nt)], axis=0)
    negb = (_dot(sel_rows.astype(BF16), spread) - 1.0) * (-NEG)

    logits = []
    for t in range(nt):
        kt = kp[t][...].astype(BF16) if t < n_pages else k_new
        nb = jnp.concatenate([negb[t * SUBLANE:(t + 1) * SUBLANE, :]] * (rows_q // SUBLANE), axis=0)
        logits.append(_dot_nt(qn, kt) * scale + bias_ref[:, t * pw:(t + 1) * pw] + nb)
    m = logits[0].max(axis=-1, keepdims=True)
    for t in range(1, nt):
        m = jnp.maximum(m, logits[t].max(axis=-1, keepdims=True))
    l = jnp.zeros((rows_q, 1), F32)
    acc = jnp.zeros((rows_q, hd), F32)
    for t in range(nt):
        pr = jnp.exp(logits[t] - m)
        l = l + jnp.sum(pr, axis=-1, keepdims=True)
        acc = acc + _dot(pr.astype(BF16), vp[t][...].astype(BF16) if t < n_pages else v_new)
    ao_ref[...] = acc / l


def _dsa_sample(qa, qi, wi, knew, vnew, kinew, qn_g, kn_g, bias_s, cache_k, cache_v, cache_i, page_table, *, t_new, hd):
    n_seq, n_pages = page_table.shape
    nk = n_pages * PAGE_SIZE + LANE
    k_sel = min(TOPK_MAX, (n_pages * PAGE_SIZE + t_new) // 4)
    pt = page_table.reshape(-1)

    def page_spec(arr, t):
        return pl.BlockSpec((None,) + arr.shape[1:], lambda b, pt: (pt[b * n_pages + t], 0, 0))

    seq_spec = lambda arr: pl.BlockSpec((None,) + arr.shape[1:], lambda b, pt: (b, 0, 0))
    const_spec = lambda arr: pl.BlockSpec(arr.shape, lambda b, pt: (0, 0))

    scores = pl.pallas_call(
        functools.partial(_idx_sample_kernel, n_pages=n_pages, t_new=t_new),
        grid_spec=pltpu.PrefetchScalarGridSpec(
            num_scalar_prefetch=1, grid=(n_seq,),
            in_specs=[seq_spec(qi), seq_spec(wi), seq_spec(kinew)] + [page_spec(cache_i, t) for t in range(n_pages)],
            out_specs=pl.BlockSpec((SUBLANE, nk), lambda b, pt: (b, 0))),
        out_shape=jax.ShapeDtypeStruct((n_seq * SUBLANE, nk), F32),
        compiler_params=_params(("arbitrary",), VMEM_LIMIT),
        name="idx_sample",
    )(pt, qi, wi, kinew, *([cache_i] * n_pages))

    rows = scores.shape[0]
    tr = LANE if rows % LANE == 0 else rows
    sel = pl.pallas_call(
        functools.partial(_select_kernel, k_sel=k_sel),
        grid=(rows // tr,),
        in_specs=[pl.BlockSpec((tr, nk), lambda r: (r, 0))],
        out_specs=pl.BlockSpec((tr, nk), lambda r: (r, 0)),
        out_shape=jax.ShapeDtypeStruct((rows, nk), F32),
        scratch_shapes=[pltpu.VMEM((tr, nk), jnp.int32)],
        compiler_params=_params(("parallel",), VMEM_LIMIT),
        name="select_sample",
    )(scores)

    qg, kg = qn_g.reshape(1, hd), kn_g.reshape(1, hd)
    return pl.pallas_call(
        functools.partial(_attn_sample_kernel, n_pages=n_pages, hd=hd),
        grid_spec=pltpu.PrefetchScalarGridSpec(
            num_scalar_prefetch=1, grid=(n_seq,),
            in_specs=[seq_spec(qa), pl.BlockSpec((SUBLANE, nk), lambda b, pt: (b, 0)), seq_spec(knew), seq_spec(vnew),
                      const_spec(qg), const_spec(kg), const_spec(bias_s)]
                     + [page_spec(cache_k, t) for t in range(n_pages)]
                     + [page_spec(cache_v, t) for t in range(n_pages)],
            out_specs=[seq_spec(qa), seq_spec(knew)]),
        out_shape=[jax.ShapeDtypeStruct(qa.shape, F32), jax.ShapeDtypeStruct(knew.shape, F32)],
        compiler_params=_params(("arbitrary",), VMEM_LIMIT),
        name="attn_sample",
    )(pt, qa, sel, knew, vnew, qg, kg, bias_s, *([cache_k] * n_pages), *([cache_v] * n_pages))


def _outproj_kernel(x_ref, mo_ref, ao_ref, gt_ref, sh_ref, sc_ref, g_ref, wm_ref, wa_ref, o_ref, h_ref):
    y = _dot(mo_ref[...].astype(BF16), wm_ref[...]) + _dot(ao_ref[...].astype(BF16), wa_ref[...])
    x = x_ref[...] + gt_ref[...] * y
    o_ref[...] = x
    h_ref[...] = _adaln(x, g_ref[...], sc_ref[...], sh_ref[...])


def _outproj(x, mo, ao, gt, w_out, sh, sc, g):
    m, d = x.shape
    mw, aw = mo.shape[1], ao.shape[1]
    tm = _row_tile(m)
    ms = _mod_spec(gt, m, tm, d)
    row = lambda w: pl.BlockSpec((tm, w), lambda i, j: (i, 0))
    return pl.pallas_call(
        _outproj_kernel,
        grid=(m // tm, 1),
        in_specs=[row(d), row(mw), row(aw), ms, ms, ms, pl.BlockSpec((1, d), lambda i, j: (0, 0)),
                  pl.BlockSpec((mw, d), lambda i, j: (0, 0)),
                  pl.BlockSpec((aw, d), lambda i, j: (0, 0))],
        out_specs=[row(d), row(d)],
        out_shape=[jax.ShapeDtypeStruct((m, d), F32), jax.ShapeDtypeStruct((m, d), BF16)],
        compiler_params=_params(("parallel", "arbitrary"), VMEM_LIMIT),
        name="outproj",
    )(x, mo, ao, gt, sh, sc, g.reshape(1, d), w_out[:mw], w_out[mw:])


def _pad_tokens(a, t_pad):
    pad = [(0, 0)] * a.ndim
    pad[1] = (0, t_pad - a.shape[1])
    return jnp.pad(a, pad)


def _layer(x3, mods, lw, bias_p, bias_s, rec_state, past):
    n_seq, T, d = x3.shape
    x = x3.reshape(n_seq * T, d)
    dk, dv = rec_state[0].shape[2], rec_state[0].shape[3]
    mw = M_HEADS * dv
    hd = lw["q_norm_g"].shape[0]
    aw, kvw = A_HEADS * hd, A_KV_HEADS * hd
    iw = IDX_HEADS * IDX_DIM
    per_seq = T % ROW_TILE == 0
    if per_seq:
        mod = lambda k: mods[:, k:k + 1, :]
    else:
        mod = lambda k: jnp.repeat(mods[:, k, :], T, axis=0)
    sh1, sc1, g1, sh2, sc2, g2, sh3, sc3, g3 = [mod(k) for k in range(N_MOD)]

    x, ffn1_w = _ffn(x, g1, *lw["ffn1_w"], norm=(sh1, sc1, lw["ffn1_norm_g"]))
    qkw = 2 * M_HEADS * dk
    a_off = qkw + 2 * mw
    c0, n0, m0, conv_buf = rec_state
    prev8 = jnp.zeros((n_seq, SUBLANE, qkw), F32).at[:, SUBLANE - (CONV_W - 1):, :].set(conv_buf)
    prompt = None
    if past is None:
        prompt = (lw["q_norm_g"], (a_off, aw, a_off + aw + 2 * kvw, iw, hd))
    p, qn, qib = _proj(x, sh2, sc2, lw["mix_norm_g"], lw["w_in"], prompt)
    npk = p.shape[1]
    gate_row = jnp.zeros((1, LANE), F32).at[0, TAIL_IG:TAIL_IG + 2 * M_HEADS].set(lw["mlstm_gate_b"])
    m0_pad = jnp.zeros((n_seq, 1, LANE), F32).at[:, 0, :M_HEADS].set(m0)
    p3 = p.reshape(n_seq, T, npk)
    v_a = p3[:, :, a_off + aw + kvw:a_off + aw + 2 * kvw]
    ki = p3[:, :, npk - LANE + TAIL_KI:npk - LANE + TAIL_KI + IDX_DIM]
    conv_new = p3[:, T - (CONV_W - 1):, :qkw]

    if past is None:
        L = M_CHUNK if T % M_CHUNK == 0 else T
        mo, c_new, n_new, m_new = _mlstm(p3, prev8, lw["mlstm_conv_w"], gate_row, lw["mlstm_out_g"], c0, n0, m0_pad,
                                         L=L, valid=L, seqs_per_step=2 if n_seq % 2 == 0 else 1, out_dtype=BF16)
        mo = mo.reshape(n_seq * T, mw)
        ao, k_n, v_n = _dsa_prompt(p, qn, qib, lw["k_norm_g"], bias_p, n_seq=n_seq, T=T, hd=hd)
        k_n = k_n.reshape(n_seq, T, A_KV_HEADS, hd)
        v_n = v_n.reshape(n_seq, T, A_KV_HEADS, hd)
    else:
        tp = SUBLANE
        assert T <= tp
        mo, c_new, n_new, m_new = _mlstm(_pad_tokens(p3, tp), prev8, lw["mlstm_conv_w"], gate_row, lw["mlstm_out_g"],
                                         c0, n0, m0_pad, L=tp, valid=T,
                                         seqs_per_step=SUBLANE if n_seq % SUBLANE == 0 else 1, out_dtype=F32)
        mo = mo[:, :T].reshape(n_seq * T, mw)
        cache_k, cache_v, cache_i, page_table = past
        rep = A_HEADS // A_KV_HEADS
        qa = _pad_tokens(p3[:, :, a_off:a_off + aw], tp).reshape(n_seq, tp, A_KV_HEADS, rep, hd)
        qa = qa.transpose(0, 2, 3, 1, 4).reshape(n_seq, A_HEADS * tp, hd)
        qi = _pad_tokens(p3[:, :, a_off + aw + 2 * kvw:a_off + aw + 2 * kvw + iw], tp)
        qi = qi.reshape(n_seq, tp, IDX_HEADS, IDX_DIM).transpose(0, 2, 1, 3).reshape(n_seq, IDX_HEADS * tp, IDX_DIM)
        wi = _pad_tokens(p3[:, :, npk - LANE + TAIL_WI:npk - LANE + TAIL_WI + IDX_HEADS], tp)
        wi = wi.transpose(0, 2, 1).reshape(n_seq, IDX_HEADS * tp, 1)
        knew = _pad_tokens(p3[:, :, a_off + aw:a_off + aw + kvw], tp).reshape(n_seq, tp * A_KV_HEADS, hd)
        vnew = _pad_tokens(v_a, tp).reshape(n_seq, tp * A_KV_HEADS, hd)
        n_pool = cache_k.shape[0]
        ao4, k_n8 = _dsa_sample(qa, qi, wi, knew, vnew, _pad_tokens(ki, tp), lw["q_norm_g"], lw["k_norm_g"], bias_s,
                                cache_k.reshape(n_pool, PAGE_SIZE * A_KV_HEADS, hd),
                                cache_v.reshape(n_pool, PAGE_SIZE * A_KV_HEADS, hd),
                                jnp.swapaxes(cache_i, 1, 2), page_table, t_new=T, hd=hd)
        k_n8 = k_n8.reshape(n_seq, tp, kvw)
        ao = ao4.reshape(n_seq, A_KV_HEADS, rep, tp, hd)[:, :, :, :T].transpose(0, 3, 1, 2, 4).reshape(n_seq * T, aw)
        k_n = k_n8[:, :T].reshape(n_seq, T, A_KV_HEADS, hd)
        v_n = v_a.reshape(n_seq, T, A_KV_HEADS, hd)

    x, h3 = _outproj(x, mo, ao, g2, lw["w_out"], sh3, sc3, lw["ffn2_norm_g"])
    x, ffn2_w = _ffn(x, g3, *lw["ffn2_w"], h=h3)
    state = (k_n, v_n, ki, c_new, n_new, m_new[:, 0, :M_HEADS], conv_new)
    return x.reshape(n_seq, T, d), state, dict(lw, ffn1_w=ffn1_w, ffn2_w=ffn2_w)


def kernel(x_prompt, x_sample, c_prompt, c_sample, cache_k, cache_v, cache_idx_k, page_table, state_C, state_n,
           state_m, state_conv, ffn1_norm_g, ffn1_w_gate, ffn1_w_up, ffn1_w_down, mix_norm_g, w_in, mlstm_conv_w,
           mlstm_gate_b, mlstm_out_g, q_norm_g, k_norm_g, t5_bias, w_out, ffn2_norm_g, ffn2_w_gate, ffn2_w_up,
           ffn2_w_down, w_ada, b_ada):
    depth = w_in.shape[0]
    bp, tp_len, d = x_prompt.shape
    bs, ts_len, _ = x_sample.shape
    dk, dv = state_C.shape[3], state_C.shape[4]
    mw = M_HEADS * dv
    hd = q_norm_g.shape[1]
    past_len = page_table.shape[1] * PAGE_SIZE
    bias_p, bias_s = _bias_tables(t5_bias, tp_len, past_len, ts_len)
    rep = A_HEADS // A_KV_HEADS
    bias_s = bias_s.reshape(A_HEADS * SUBLANE, A_KV_HEADS * (past_len + LANE))

    xp, xs = x_prompt, x_sample
    st_p, st_s = [], []
    for l in range(depth):
        lw = dict(ffn1_norm_g=ffn1_norm_g[l], ffn1_w=(ffn1_w_gate[l], ffn1_w_up[l], ffn1_w_down[l]),
                  mix_norm_g=mix_norm_g[l], w_in=_pack_w_in(w_in[l], mw, A_HEADS * hd, A_KV_HEADS * hd),
                  mlstm_conv_w=mlstm_conv_w[l], mlstm_gate_b=mlstm_gate_b[l], mlstm_out_g=mlstm_out_g[l],
                  q_norm_g=q_norm_g[l], k_norm_g=k_norm_g[l], w_out=w_out[l].astype(BF16),
                  ffn2_norm_g=ffn2_norm_g[l], ffn2_w=(ffn2_w_gate[l], ffn2_w_up[l], ffn2_w_down[l]))
        mods = _ada(jnp.concatenate([c_prompt, c_sample], axis=0), w_ada[l], b_ada[l])
        mods = mods.reshape(bp + bs, N_MOD, d)
        init_s = (state_C[l], state_n[l], state_m[l], state_conv[l])
        xs, ss, lw = _layer(xs, mods[bp:], lw, bias_p, bias_s, init_s,
                            (cache_k[l], cache_v[l], cache_idx_k[l], page_table))
        init_p = (jnp.zeros((bp, M_HEADS, dk, dv), F32), jnp.zeros((bp, M_HEADS, dk), F32),
                  jnp.zeros((bp, M_HEADS), F32), jnp.zeros((bp, CONV_W - 1, 2 * M_HEADS * dk), F32))
        xp, sp, _ = _layer(xp, mods[:bp], lw, bias_p, bias_s, init_p, None)
        st_p.append(sp)
        st_s.append(ss)
    stack = lambda sts, k: sts[0][k][None] if depth == 1 else jnp.stack([s[k] for s in sts])
    return (xp, xs) + tuple(stack(st_p, k) for k in range(7)) + tuple(stack(st_s, k) for k in range(7))
```

```python
import functools
import math

import numpy as np
import jax
import jax.numpy as jnp
from jax import lax
from jax.experimental import pallas as pl
from jax.experimental.pallas import tpu as pltpu

F32 = jnp.float32
BF16 = jnp.bfloat16

M_HEADS = 4
CONV_W = 4
M_CHUNK = 64
A_HEADS = 8
A_KV_HEADS = 2
IDX_HEADS = 8
IDX_DIM = 64
TOPK_MAX = 256
Q_BLOCK = 128
PAGE_SIZE = 128
N_BUCKETS = 32
MAX_DISTANCE = 128
N_MOD = 9
EPS = 1e-6
NEG = -1e30

LANE = 128
SUBLANE = 8
ROW_TILE = 512
VMEM_LIMIT = 56 * 1024 * 1024

TAIL_KI = 0
TAIL_WI = IDX_DIM
TAIL_IG = IDX_DIM + IDX_HEADS
TAIL_FG = TAIL_IG + M_HEADS

INT_MIN = -2 ** 31
LOG2E = math.log2(math.e)


def _params(sem, vmem=None, flags=None):
    return pltpu.CompilerParams(dimension_semantics=sem, vmem_limit_bytes=vmem, flags=flags)


def _dot(a, b):
    return jnp.dot(a, b, preferred_element_type=F32)


def _dot_nt(a, b):
    return lax.dot_general(a, b, (((1,), (1,)), ((), ())), preferred_element_type=F32)


def _dot_tn(a, b):
    return lax.dot_general(a, b, (((0,), (0,)), ((), ())), preferred_element_type=F32)


def _rms(x, g):
    return x * lax.rsqrt(jnp.mean(x * x, axis=-1, keepdims=True) + EPS) * g


def _row_tile(n):
    return ROW_TILE if n % ROW_TILE == 0 else n


REDUCE_CHAINS = 4


def _reduce_rows(x, op):
    n, c = x.shape
    groups = n // SUBLANE
    chains = REDUCE_CHAINS if groups % REDUCE_CHAINS == 0 else 1
    x3 = x.reshape(groups, SUBLANE, c)
    per = groups // chains
    parts = [op(x3[k * per:(k + 1) * per], axis=0) for k in range(chains)]
    while len(parts) > 1:
        parts = [op(jnp.stack(parts[k:k + 2]), axis=0) for k in range(0, len(parts), 2)]
    return op(parts[0], axis=0, keepdims=True)


def _ada_kernel(c_ref, w_ref, b_ref, o_ref):
    o_ref[...] = _dot(c_ref[...].astype(BF16), w_ref[...].astype(BF16)) + b_ref[...]


def _ada(c_all, w_ada, b_ada):
    r, d = c_all.shape
    n = w_ada.shape[1]
    tn = 1024 if n % 1024 == 0 else n
    return pl.pallas_call(
        _ada_kernel,
        grid=(n // tn,),
        in_specs=[pl.BlockSpec((r, d), lambda j: (0, 0)),
                  pl.BlockSpec((d, tn), lambda j: (0, j)),
                  pl.BlockSpec((1, tn), lambda j: (0, j))],
        out_specs=pl.BlockSpec((r, tn), lambda j: (0, j)),
        out_shape=jax.ShapeDtypeStruct((r, n), F32),
        compiler_params=_params(("arbitrary",), VMEM_LIMIT),
        name="ada",
    )(c_all, w_ada, b_ada.reshape(1, n))


def _adaln(x, g, sc, sh):
    return (_rms(x, g) * (1.0 + sc) + sh).astype(BF16)


def _ffn_kernel(x_ref, gt_ref, *rest, prenormed, emit):
    if prenormed:
        hin_ref, wg_ref, wu_ref, wd_ref, o_ref, *rest = rest
    else:
        sh_ref, sc_ref, g_ref, wg_ref, wu_ref, wd_ref, o_ref, *rest = rest
    bf16_out, rest = (rest[:3], rest[3:]) if emit else ((), rest)
    acc_ref = rest[-1]
    j = pl.program_id(1)

    @pl.when(j == 0)
    def _():
        if not prenormed:
            rest[0][...] = _adaln(x_ref[...], g_ref[...], sc_ref[...], sh_ref[...])
        acc_ref[...] = jnp.zeros_like(acc_ref)

    wg, wu, wd = wg_ref[...], wu_ref[...], wd_ref[...]
    if emit:
        wg, wu, wd = wg.astype(BF16), wu.astype(BF16), wd.astype(BF16)
        for ref, w in zip(bf16_out, (wg, wu, wd)):
            ref[...] = w
    h = hin_ref[...] if prenormed else rest[0][...]
    a = _dot(h, wg)
    u = _dot(h, wu)
    act = (a * jax.nn.sigmoid(a) * u).astype(BF16)
    acc_ref[...] += _dot(act, wd)

    @pl.when(j == pl.num_programs(1) - 1)
    def _():
        o_ref[...] = x_ref[...] + 0.5 * gt_ref[...] * acc_ref[...]


def _mod_spec(mod, m, tm, d):
    arr, k = mod
    if arr.ndim == 2:
        return pl.BlockSpec((tm, d), lambda i, j: (i, k))
    tiles_per_seq = (m // arr.shape[0]) // tm
    return pl.BlockSpec((None, 1, d), lambda i, j: (i // tiles_per_seq, 0, k))


def _ffn(x, gt, wg, wu, wd, *, norm=None, h=None):
    m, d = x.shape
    f = wg.shape[1]
    tm = _row_tile(m)
    emit = wg.dtype != BF16
    assert not emit or m == tm
    tf = (256 if emit else 512) if f % 512 == 0 else f
    row = pl.BlockSpec((tm, d), lambda i, j: (i, 0))
    w_specs = [pl.BlockSpec((d, tf), lambda i, j: (0, j)),
               pl.BlockSpec((d, tf), lambda i, j: (0, j)),
               pl.BlockSpec((tf, d), lambda i, j: (j, 0))]
    if h is None:
        sh, sc, g = norm
        pre_specs = [_mod_spec(sh, m, tm, d), _mod_spec(sc, m, tm, d), pl.BlockSpec((1, d), lambda i, j: (0, 0))]
        pre_args = [sh[0], sc[0], g.reshape(1, d)]
        scratch = [pltpu.VMEM((tm, d), BF16)]
    else:
        pre_specs, pre_args, scratch = [row], [h], []
    out_specs = [row]
    out_shape = [jax.ShapeDtypeStruct((m, d), F32)]
    if emit:
        out_specs += w_specs
        out_shape += [jax.ShapeDtypeStruct(w.shape, BF16) for w in (wg, wu, wd)]
    y, *wb = pl.pallas_call(
        functools.partial(_ffn_kernel, prenormed=h is not None, emit=emit),
        grid=(m // tm, f // tf),
        in_specs=[row, _mod_spec(gt, m, tm, d)] + pre_specs + w_specs,
        out_specs=out_specs,
        out_shape=out_shape,
        scratch_shapes=scratch + [pltpu.VMEM((tm, d), F32)],
        compiler_params=_params(("parallel", "arbitrary"), VMEM_LIMIT),
        name="ffn",
    )(x, gt[0], *pre_args, wg, wu, wd)
    return y, (tuple(wb) if emit else (wg, wu, wd))


def _conv_silu(xbuf_ref, x_new, cw_ref):
    rows = x_new.shape[0]
    xbuf_ref[SUBLANE:SUBLANE + rows, :] = x_new
    base = SUBLANE - (CONV_W - 1)
    y = xbuf_ref[base:base + rows, :] * cw_ref[0:1, :]
    for j in range(1, CONV_W):
        y = y + xbuf_ref[base + j:base + j + rows, :] * cw_ref[j:j + 1, :]
    tail_rows = xbuf_ref[rows:rows + SUBLANE, :]
    xbuf_ref[0:SUBLANE, :] = tail_rows
    return y * jax.nn.sigmoid(y)


def _proj_kernel(x_ref, sh_ref, sc_ref, g_ref, wm_ref, wt_ref, *rest, q_cols):
    h = _adaln(x_ref[...], g_ref[...], sc_ref[...], sh_ref[...])
    nm = wm_ref.shape[1]
    o_ref = rest[1] if q_cols else rest[0]
    o_ref[:, 0:nm] = _dot(h, wm_ref[...])
    p = _dot(h, wt_ref[...])
    o_ref[:, nm:] = p
    if not q_cols:
        return
    qg_ref, _, qn_ref, qib_ref = rest
    a0, aw, i0, iw, hd = q_cols
    for hh in range(aw // hd):
        qn = _rms(p[:, a0 - nm + hh * hd:a0 - nm + (hh + 1) * hd], qg_ref[...]) * (hd ** -0.5 * LOG2E)
        qn_ref[:, hh * hd:(hh + 1) * hd] = qn.astype(BF16)
    qib_ref[...] = p[:, i0 - nm:i0 - nm + iw].astype(BF16)


PROJ_ROW_TILE = 256


def _proj(x, sh, sc, g, w, prompt=None):
    m, d = x.shape
    wm, wt = w
    n = wm.shape[1] + wt.shape[1]
    tm = PROJ_ROW_TILE if m % PROJ_ROW_TILE == 0 else m
    row = lambda width: pl.BlockSpec((tm, width), lambda i, j: (i, 0))
    resident = lambda arr: pl.BlockSpec(arr.shape, lambda i, j: (0, 0), pipeline_mode=pl.Buffered(1))
    in_specs = [row(d), _mod_spec(sh, m, tm, d), _mod_spec(sc, m, tm, d), pl.BlockSpec((1, d), lambda i, j: (0, 0)),
                resident(wm), resident(wt)]
    args = [x, sh[0], sc[0], g.reshape(1, d), wm, wt]
    out_specs, out_shape = [row(n)], [jax.ShapeDtypeStruct((m, n), F32)]
    q_cols = None
    if prompt:
        qn_g, q_cols = prompt
        a0, aw, i0, iw, hd = q_cols
        assert a0 >= wm.shape[1]
        in_specs.append(pl.BlockSpec((1, hd), lambda i, j: (0, 0)))
        args.append(qn_g.reshape(1, hd))
        out_specs += [row(aw), row(iw)]
        out_shape += [jax.ShapeDtypeStruct((m, aw), BF16), jax.ShapeDtypeStruct((m, iw), BF16)]
    out = pl.pallas_call(
        functools.partial(_proj_kernel, q_cols=q_cols),
        grid=(m // tm, 1),
        in_specs=in_specs,
        out_specs=out_specs,
        out_shape=out_shape,
        compiler_params=_params(("parallel", "arbitrary"), VMEM_LIMIT),
        name="proj",
    )(*args)
    return out if prompt else (out[0], None, None)


def _pack_w_in(w_in, mw, aw, kvw):
    d = w_in.shape[0]
    o = np.cumsum([0, mw, mw, mw, mw, M_HEADS, M_HEADS, aw, kvw, kvw, IDX_HEADS * IDX_DIM, IDX_DIM, IDX_HEADS])
    seg = lambda k: w_in[:, o[k]:o[k + 1]]
    pad = (-o[-1]) % LANE
    tail = jnp.concatenate([seg(k) for k in (6, 7, 8, 9, 10, 11, 4, 5)] + [jnp.zeros((d, pad), w_in.dtype)], axis=1)
    return w_in[:, :o[4]].astype(BF16), tail.astype(BF16)


def _log_sigmoid(x):
    return jnp.minimum(x, 0.0) - jnp.log1p(jnp.exp(-jnp.abs(x)))


def _mlstm_kernel(qk_ref, v_ref, o_ref, tail_ref, prev_ref, cw_ref, gb_ref, og_ref, c0_ref, n0_ref, m0_ref,
                  mo_ref, cst_ref, nst_ref, mst_ref, xbuf_ref, *, L, valid, dk, dv):
    @pl.when(pl.program_id(1) == 0)
    def _():
        xbuf_ref[:, 0:SUBLANE, :] = prev_ref[...]
        cst_ref[...] = c0_ref[...]
        nst_ref[...] = n0_ref[...]
        mst_ref[...] = m0_ref[...]

    for sb in range(qk_ref.shape[0]):
        _mlstm_one(qk_ref.at[sb], v_ref.at[sb], o_ref.at[sb], tail_ref.at[sb], cw_ref, gb_ref, og_ref, mo_ref.at[sb],
                   cst_ref.at[sb], nst_ref.at[sb], mst_ref.at[sb], xbuf_ref.at[sb], L=L, valid=valid, dk=dk, dv=dv)


def _mlstm_one(qk_ref, v_ref, o_ref, tail_ref, cw_ref, gb_ref, og_ref, mo_ref, cst_ref, nst_ref, mst_ref, xbuf_ref,
               *, L, valid, dk, dv):
    width = M_HEADS * dk
    qk = _conv_silu(xbuf_ref, qk_ref[...], cw_ref)

    gates = tail_ref[...] + gb_ref[...]
    row = lax.broadcasted_iota(jnp.int32, (L, LANE), 0)
    lf = _log_sigmoid(gates)
    ig = gates
    if valid < L:
        lf = jnp.where(row < valid, lf, 0.0)
        ig = jnp.where(row < valid, ig, -jnp.inf)
    bcum = lf
    s = 1
    while s < L:
        bcum = bcum + jnp.where(row >= s, pltpu.roll(bcum, s, axis=0), 0.0)
        s *= 2
    bmi_t = (pltpu.roll(ig, TAIL_FG - TAIL_IG, axis=1) - bcum).T

    ri = lax.broadcasted_iota(jnp.int32, (L, L), 0)
    ci = lax.broadcasted_iota(jnp.int32, (L, L), 1)
    causal = ri >= ci

    for h in range(M_HEADS):
        li, lfh = TAIL_IG + h, TAIL_FG + h
        a_col = bcum[:, lfh:lfh + 1]
        ig_col = ig[:, li:li + 1]
        b_row = bmi_t[lfh:lfh + 1, :]
        m0 = mst_ref[:, h:h + 1]
        d_log = jnp.where(causal, a_col + b_row, -jnp.inf)
        s_log = a_col + m0
        m_col = jnp.maximum(s_log, jnp.max(d_log, axis=-1, keepdims=True))
        dw = jnp.exp(d_log - m_col)
        sw = jnp.exp(s_log - m_col)

        q = qk[:, h * dk:(h + 1) * dk]
        k = qk[:, width + h * dk:width + (h + 1) * dk] * (dk ** -0.5)
        v = v_ref[:, h * dv:(h + 1) * dv]
        qb, kb = q.astype(BF16), k.astype(BF16)
        c_prev = cst_ref[h]
        n_prev = nst_ref[h:h + 1, :]
        scores = _dot_nt(qb, kb) * dw
        num = _dot(scores.astype(BF16), v.astype(BF16)) + sw * _dot(qb, c_prev.astype(BF16))
        den = jnp.sum(scores, axis=-1, keepdims=True) + sw * jnp.sum(q * n_prev, axis=-1, keepdims=True)
        hh = num / jnp.maximum(jnp.abs(den), jnp.exp(-m_col))

        m_new = m_col[L - 1:L, :]
        a_last = a_col[L - 1:L, :]
        wl = jnp.exp(a_last - a_col + ig_col - m_new)
        decay = jnp.exp(a_last + m0 - m_new)
        cst_ref[h] = decay * c_prev + _dot_tn(kb, (wl * v).astype(BF16))
        nst_ref[h:h + 1, :] = decay * n_prev + jnp.sum(wl * k, axis=0, keepdims=True)
        mst_ref[:, h:h + 1] = m_new

        hn = hh * lax.rsqrt(jnp.mean(hh * hh, axis=-1, keepdims=True) + EPS) * og_ref[:, h * dv:(h + 1) * dv]
        gate = jax.nn.sigmoid(o_ref[:, h * dv:(h + 1) * dv])
        mo_ref[:, h * dv:(h + 1) * dv] = (hn * gate).astype(mo_ref.dtype)


def _mlstm(p3, prev8, conv_w, gate_row, out_g, c0, n0, m0, *, L, valid, seqs_per_step, out_dtype):
    n_seq, t_all, npk = p3.shape
    nc = t_all // L
    sb = seqs_per_step
    assert n_seq % sb == 0
    dk = c0.shape[2]
    dv = c0.shape[3]
    mw = M_HEADS * dv
    qkw = 2 * M_HEADS * dk
    assert qkw % mw == 0 and (2 * qkw) % mw == 0
    kern = functools.partial(_mlstm_kernel, L=L, valid=valid, dk=dk, dv=dv)
    seq = lambda shape: pl.BlockSpec((sb,) + shape, lambda b, c: (b,) + (0,) * len(shape))
    return pl.pallas_call(
        kern,
        grid=(n_seq // sb, nc),
        in_specs=[pl.BlockSpec((sb, L, qkw), lambda b, c: (b, c, 0)),
                  pl.BlockSpec((sb, L, mw), lambda b, c: (b, c, qkw // mw)),
                  pl.BlockSpec((sb, L, mw), lambda b, c: (b, c, qkw // mw + 1)),
                  pl.BlockSpec((sb, L, LANE), lambda b, c: (b, c, npk // LANE - 1)),
                  seq((SUBLANE, qkw)),
                  pl.BlockSpec((CONV_W, qkw), lambda b, c: (0, 0)),
                  pl.BlockSpec((1, LANE), lambda b, c: (0, 0)),
                  pl.BlockSpec((1, mw), lambda b, c: (0, 0)),
                  seq((M_HEADS, dk, dv)), seq((M_HEADS, dk)), seq((1, LANE))],
        out_specs=[pl.BlockSpec((sb, L, mw), lambda b, c: (b, c, 0)),
                   seq((M_HEADS, dk, dv)), seq((M_HEADS, dk)), seq((1, LANE))],
        out_shape=[jax.ShapeDtypeStruct((n_seq, t_all, mw), out_dtype),
                   jax.ShapeDtypeStruct(c0.shape, F32),
                   jax.ShapeDtypeStruct(n0.shape, F32),
                   jax.ShapeDtypeStruct(m0.shape, F32)],
        scratch_shapes=[pltpu.VMEM((sb, SUBLANE + L, qkw), F32)],
        compiler_params=_params(("parallel", "arbitrary"), VMEM_LIMIT),
        name="mlstm",
    )(p3, p3, p3, p3, prev8, conv_w, gate_row, out_g.reshape(1, mw), c0, n0, m0)


def _bucket_np(dist):
    me = N_BUCKETS // 2
    d = np.maximum(dist, 0)
    ratio = np.log(np.maximum(d, 1).astype(np.float64) / me) / math.log(MAX_DISTANCE / me)
    large = np.minimum(me + (ratio * (N_BUCKETS - me)).astype(np.int64), N_BUCKETS - 1)
    return np.where(d < me, d, large).astype(np.int32)


def _bias_kernel(t5_ref, bkp_ref, bks_ref, vs_ref, far_ref, op_ref, os_ref):
    rep = A_HEADS // A_KV_HEADS

    def lookup(bk, h):
        acc = jnp.zeros(bk.shape, F32)
        for b in range(N_BUCKETS):
            acc = jnp.where(bk == b, t5_ref[b, h], acc)
        return acc

    for h in range(A_HEADS):
        half = (h % 2) * Q_BLOCK
        op_ref[h // 2, :, half:half + Q_BLOCK] = (lookup(bkp_ref[...], h) - t5_ref[far_ref[0], h]) * LOG2E
        os_ref[h] = jnp.where(vs_ref[h // rep] > 0, lookup(bks_ref[...], h), NEG)


def _bias_tables(t5_bias, t_prompt, past, t_new):
    c = np.arange(2 * Q_BLOCK)[:, None]
    r = np.arange(Q_BLOCK)[None, :]
    bkp = _bucket_np(Q_BLOCK + r - c)
    far = _bucket_np(np.arange(Q_BLOCK + 1, max(t_prompt, past + t_new) + Q_BLOCK))
    assert (far == far[0]).all(), "bias must be constant beyond one query block"
    nk = past + LANE
    tq = np.arange(SUBLANE)[:, None]
    col = np.arange(2 * nk)[None, :]
    s_pos, s_grp = col // A_KV_HEADS, col % A_KV_HEADS
    bks = _bucket_np(past + tq - s_pos)
    vis = np.stack([(s_grp == g) & (s_pos - past <= tq) for g in range(A_KV_HEADS)]).astype(np.int32)
    return pl.pallas_call(
        _bias_kernel,
        in_specs=[pl.BlockSpec(memory_space=pltpu.SMEM),
                  pl.BlockSpec(memory_space=pltpu.VMEM),
                  pl.BlockSpec(memory_space=pltpu.VMEM),
                  pl.BlockSpec(memory_space=pltpu.VMEM),
                  pl.BlockSpec(memory_space=pltpu.SMEM)],
        out_specs=[pl.BlockSpec(memory_space=pltpu.VMEM), pl.BlockSpec(memory_space=pltpu.VMEM)],
        out_shape=[jax.ShapeDtypeStruct((A_HEADS // 2, 2 * Q_BLOCK, 2 * Q_BLOCK), F32),
                   jax.ShapeDtypeStruct((A_HEADS, SUBLANE, 2 * nk), F32)],
        name="t5_bias_tables",
    )(t5_bias, jnp.asarray(bkp), jnp.asarray(bks), jnp.asarray(vis), jnp.asarray(far[:1]))


def _sort_key(score):
    bits = lax.bitcast_convert_type(score, jnp.int32)
    return jnp.where(bits < 0, bits ^ jnp.int32(0x7FFFFFFF), bits)


def _kth_key(count_ge, shape, k_sel):
    def body(it, ans_u):
        cand_u = ans_u | lax.shift_left(jnp.int32(1), jnp.int32(31) - it)
        cnt = count_ge(cand_u ^ jnp.int32(INT_MIN))
        return jnp.where(cnt >= k_sel, cand_u, ans_u)

    ans_u = lax.fori_loop(0, 32, body, jnp.zeros(shape, jnp.int32))
    return ans_u ^ jnp.int32(INT_MIN)


def _triangle(below):
    r = lax.broadcasted_iota(jnp.int32, (LANE, LANE), 0)
    c = lax.broadcasted_iota(jnp.int32, (LANE, LANE), 1)
    return jnp.where((c < r) if below else (r < c), 1.0, 0.0).astype(BF16)


def _select_tiles(key_tiles, thr, k_sel, key_axis):
    count = lambda x: jnp.sum(x, axis=key_axis, keepdims=True)
    c_gt = jnp.zeros(thr.shape, F32)
    for kt in key_tiles:
        c_gt = c_gt + count(jnp.where(kt > thr, 1.0, 0.0))
    need = k_sel - c_gt
    tri = _triangle(below=(key_axis == 0))
    run = jnp.zeros(thr.shape, F32)
    out = []
    for kt in key_tiles:
        eq = jnp.where(kt == thr, 1.0, 0.0)
        eqb = eq.astype(BF16)
        before = (_dot(tri, eqb) if key_axis == 0 else _dot(eqb, tri)) + run
        out.append(jnp.where(kt > thr, 1.0, jnp.where(before < need, eq, 0.0)))
        run = run + count(eq)
    return out


FAR_STEP = 2


def _dsa_prompt_kernel(qn_ref, qi_ref, qtail_ref, k_ref, v_ref, ktail_ref, kg_ref, bias_ref,
                       ao_ref, kn_ref, vo_ref,
                       kb_ref, vt_ref, kib_ref, key_ref, negb_ref, *, T, k_sel, hd):
    i = pl.program_id(1)
    nb = T // Q_BLOCK
    rep = A_HEADS // A_KV_HEADS
    near_w = 2 * LANE

    @pl.when(i == 0)
    def _():
        kb_ref[0:LANE, :] = jnp.zeros((LANE, kb_ref.shape[1]), BF16)
        vt_ref[:, 0:LANE] = jnp.zeros((vt_ref.shape[0], LANE), BF16)
        kib_ref[0:LANE, :] = jnp.zeros((LANE, IDX_DIM), BF16)
        for g in range(A_KV_HEADS):
            kn = _rms(k_ref[:, g * hd:(g + 1) * hd], kg_ref[...])
            kn_ref[pl.ds(g, T, stride=A_KV_HEADS), :] = kn
            vo_ref[pl.ds(g, T, stride=A_KV_HEADS), :] = v_ref[:, g * hd:(g + 1) * hd]
            kb_ref[LANE:, g * hd:(g + 1) * hd] = kn.astype(BF16)
        vt_ref[:, LANE:] = v_ref[...].T.astype(BF16)
        kib_ref[LANE:, :] = ktail_ref[:, TAIL_KI:TAIL_KI + IDX_DIM].astype(BF16)

    wi_t = qtail_ref[...].T[TAIL_WI:TAIL_WI + IDX_HEADS, :] * (IDX_HEADS ** -0.5) * (IDX_DIM ** -0.5)
    qi_pairs = [jnp.concatenate([qi_ref[:, (2 * a + hh) * IDX_DIM:(2 * a + hh + 1) * IDX_DIM] for hh in range(2)],
                                axis=0) for a in range(IDX_HEADS // 2)]
    qn_pairs = [jnp.concatenate([qn_ref[:, (2 * a + hh) * hd:(2 * a + hh + 1) * hd] for hh in range(2)], axis=0)
                for a in range(A_HEADS // 2)]

    near0 = pl.multiple_of(i * LANE, LANE)
    ki_near = kib_ref[pl.ds(near0, near_w), :]
    k_near = kb_ref[pl.ds(near0, near_w), :]
    vt_near = vt_ref[:, pl.ds(near0, near_w)]
    kc = lax.broadcasted_iota(jnp.int32, (near_w, Q_BLOCK), 0)
    qr = lax.broadcasted_iota(jnp.int32, (near_w, Q_BLOCK), 1)
    near_ok = jnp.where(kc <= qr + LANE, 1.0, 0.0) * jnp.maximum(jnp.where(kc >= LANE, 1.0, 0.0),
                                                                  (i >= 1).astype(F32))

    def scores(ki):
        sc = jnp.zeros((ki.shape[0], Q_BLOCK), F32)
        for a in range(IDX_HEADS // 2):
            s2 = _dot_nt(ki, qi_pairs[a])
            for hh in range(2):
                j = 2 * a + hh
                sc = sc + wi_t[j:j + 1, :] * jnp.maximum(s2[:, hh * Q_BLOCK:(hh + 1) * Q_BLOCK], 0.0)
        return sc

    def body(wf):
        fw = wf * LANE
        w_all = fw + near_w
        key_ref[fw:w_all, :] = _sort_key(jnp.where(near_ok > 0.0, scores(ki_near) + 0.0, NEG))
        if wf:
            far_ok = lax.broadcasted_iota(jnp.int32, (fw, Q_BLOCK), 0) < (i - 1) * LANE
            key_ref[0:fw, :] = _sort_key(jnp.where(far_ok, scores(kib_ref[LANE:LANE + fw, :]) + 0.0, NEG))

        def count_ge(cand):
            return _reduce_rows(jnp.where(key_ref[0:w_all, :] >= cand, 1.0, 0.0), jnp.sum)

        thr = _kth_key(count_ge, (1, Q_BLOCK), float(k_sel))
        sel = _select_tiles([key_ref[t * LANE:(t + 1) * LANE, :] for t in range(wf + 2)], thr, float(k_sel), 0)
        for t in range(wf):
            ok = sel[t] * (t < i - 1).astype(F32)
            negb_ref[t * LANE:(t + 1) * LANE, :] = jnp.where(ok > 0.0, 0.0, NEG)
        for t in range(2):
            ok = sel[wf + t] * near_ok[t * LANE:(t + 1) * LANE, :]
            negb_ref[fw + t * LANE:fw + (t + 1) * LANE, :] = jnp.where(ok > 0.0, 0.0, NEG)

        for a in range(A_HEADS // 2):
            g = (2 * a) // rep
            cs = slice(g * hd, (g + 1) * hd)
            qn2 = qn_pairs[a]
            nb_near = negb_ref[fw:w_all, :]
            s_near = _dot_nt(k_near[:, cs], qn2) + bias_ref[a] + jnp.concatenate([nb_near, nb_near], axis=1)
            m = _reduce_rows(s_near, jnp.max)
            if wf:
                nb_far = negb_ref[0:fw, :]
                s_far = _dot_nt(kb_ref[LANE:LANE + fw, cs], qn2) + jnp.concatenate([nb_far, nb_far], axis=1)
                m = jnp.maximum(m, _reduce_rows(s_far, jnp.max))
            p_near = jnp.exp2(s_near - m)
            l = _reduce_rows(p_near, jnp.sum)
            acc = _dot(vt_near[cs, :], p_near.astype(BF16))
            if wf:
                p_far = jnp.exp2(s_far - m)
                l = l + _reduce_rows(p_far, jnp.sum)
                acc = acc + _dot(vt_ref[cs, LANE:LANE + fw], p_far.astype(BF16))
            out_t = acc / l
            for hh in range(2):
                h = 2 * a + hh
                ao_ref[:, h * hd:(h + 1) * hd] = out_t[:, hh * Q_BLOCK:(hh + 1) * Q_BLOCK].T.astype(ao_ref.dtype)

    variants = sorted({min(FAR_STEP * -(-x // FAR_STEP), nb) for x in range(max(nb - 1, 1))})
    wf_needed = jnp.minimum((jnp.maximum(i - 1, 0) + FAR_STEP - 1) // FAR_STEP * FAR_STEP, nb)
    for wf in variants:
        pl.when(wf_needed == wf)(functools.partial(body, wf))


def _dsa_prompt(p, qn, qib, kn_g, bias_p, *, n_seq, T, hd):
    rows = p.shape[0]
    nb = T // Q_BLOCK
    aw = A_HEADS * hd
    kvw = A_KV_HEADS * hd
    iw = IDX_HEADS * IDX_DIM
    k_sel = min(TOPK_MAX, T // 4)
    kcol = p.shape[1] - LANE - iw - 2 * kvw
    assert kcol % kvw == 0
    k_blk = kcol // kvw
    tail_blk = p.shape[1] // LANE - 1
    kern = functools.partial(_dsa_prompt_kernel, T=T, k_sel=k_sel, hd=hd)
    return pl.pallas_call(
        kern,
        grid=(n_seq, nb),
        in_specs=[pl.BlockSpec((Q_BLOCK, aw), lambda b, i: (b * nb + i, 0)),
                  pl.BlockSpec((Q_BLOCK, iw), lambda b, i: (b * nb + i, 0)),
                  pl.BlockSpec((Q_BLOCK, LANE), lambda b, i: (b * nb + i, tail_blk)),
                  pl.BlockSpec((T, kvw), lambda b, i: (b, k_blk)),
                  pl.BlockSpec((T, kvw), lambda b, i: (b, k_blk + 1)),
                  pl.BlockSpec((T, LANE), lambda b, i: (b, tail_blk)),
                  pl.BlockSpec((1, hd), lambda b, i: (0, 0)),
                  pl.BlockSpec((A_HEADS // 2, 2 * Q_BLOCK, 2 * Q_BLOCK), lambda b, i: (0, 0, 0))],
        out_specs=[pl.BlockSpec((Q_BLOCK, aw), lambda b, i: (b * nb + i, 0)),
                   pl.BlockSpec((A_KV_HEADS * T, hd), lambda b, i: (b, 0)),
                   pl.BlockSpec((A_KV_HEADS * T, hd), lambda b, i: (b, 0))],
        out_shape=[jax.ShapeDtypeStruct((rows, aw), BF16),
                   jax.ShapeDtypeStruct((A_KV_HEADS * rows, hd), F32),
                   jax.ShapeDtypeStruct((A_KV_HEADS * rows, hd), F32)],
        scratch_shapes=[pltpu.VMEM((T + LANE, kvw), BF16), pltpu.VMEM((kvw, T + LANE), BF16),
                        pltpu.VMEM((T + LANE, IDX_DIM), BF16),
                        pltpu.VMEM((T + 2 * LANE, Q_BLOCK), jnp.int32),
                        pltpu.VMEM((T + 2 * LANE, Q_BLOCK), F32)],
        compiler_params=_params(("parallel", "arbitrary"), VMEM_LIMIT),
        name="dsa_prompt",
    )(qn, qib, p, p, p, p, kn_g.reshape(1, hd), bias_p)


def _pad_rows(x, n):
    return jnp.concatenate([x, jnp.zeros((n - x.shape[0], x.shape[1]), x.dtype)], axis=0)


def _idx_sample_kernel(pt_ref, qi_ref, wi_ref, kinew_ref, *rest, n_pages, t_new):
    ip, sc_ref = rest[:n_pages], rest[n_pages]
    del pt_ref
    qi = qi_ref[...].astype(BF16)
    wi = wi_ref[...] * (IDX_HEADS ** -0.5) * (IDX_DIM ** -0.5)
    tq = lax.broadcasted_iota(jnp.int32, (SUBLANE, LANE), 0)
    lane = lax.broadcasted_iota(jnp.int32, (SUBLANE, LANE), 1)
    for t in range(n_pages + 1):
        r = _dot(qi, ip[t][...].astype(BF16)) if t < n_pages else _dot_nt(qi, _pad_rows(kinew_ref[...], LANE).astype(BF16))
        sc = jnp.zeros((SUBLANE, LANE), F32)
        for j in range(IDX_HEADS):
            sc = sc + wi[j * SUBLANE:(j + 1) * SUBLANE, :] * jnp.maximum(r[j * SUBLANE:(j + 1) * SUBLANE, :], 0.0)
        sc = sc + 0.0
        if t == n_pages:
            sc = jnp.where(lane <= tq, sc, jnp.where(lane < t_new, NEG, -jnp.inf))
        sc_ref[:, t * LANE:(t + 1) * LANE] = sc


def _select_kernel(sc_ref, sel_ref, key_ref, *, k_sel):
    rows, nk = sc_ref.shape
    key_ref[...] = _sort_key(sc_ref[...])

    def count_ge(cand):
        return jnp.sum(jnp.where(key_ref[...] >= cand, 1.0, 0.0), axis=1, keepdims=True)

    thr = _kth_key(count_ge, (rows, 1), float(k_sel))
    sel = _select_tiles([key_ref[:, t * LANE:(t + 1) * LANE] for t in range(nk // LANE)], thr, float(k_sel), 1)
    for t in range(nk // LANE):
        sel_ref[:, t * LANE:(t + 1) * LANE] = sel[t]


def _attn_sample_kernel(pt_ref, qa_ref, sel_ref, knew_ref, vnew_ref, qg_ref, kg_ref, bias_ref, *rest,
                        n_pages, hd):
    kp, vp = rest[:n_pages], rest[n_pages:2 * n_pages]
    ao_ref, kn_ref = rest[2 * n_pages:]
    del pt_ref
    nt = n_pages + 1
    pw = A_KV_HEADS * PAGE_SIZE
    scale = hd ** -0.5
    rows_q = qa_ref.shape[0]

    kn_new = _rms(knew_ref[...], kg_ref[...])
    kn_ref[...] = kn_new
    k_new = _pad_rows(kn_new, pw).astype(BF16)
    v_new = _pad_rows(vnew_ref[...], pw).astype(BF16)
    qn = _rms(qa_ref[...], qg_ref[...]).astype(BF16)

    tok = lax.broadcasted_iota(jnp.int32, (PAGE_SIZE, pw), 0)
    col = lax.broadcasted_iota(jnp.int32, (PAGE_SIZE, pw), 1)
    spread = jnp.where(col // A_KV_HEADS == tok, 1.0, 0.0).astype(BF16)
    sel_rows = jnp.concatenate([sel_ref[:, t * LANE:(t + 1) * LANE] for t in range(nt)], axis=0)
    negb = (_dot(sel_rows.astype(BF16), spread) - 1.0) * (-NEG)

    logits = []
    for t in range(nt):
        kt = kp[t][...].astype(BF16) if t < n_pages else k_new
        nb = jnp.concatenate([negb[t * SUBLANE:(t + 1) * SUBLANE, :]] * (rows_q // SUBLANE), axis=0)
        logits.append(_dot_nt(qn, kt) * scale + bias_ref[:, t * pw:(t + 1) * pw] + nb)
    m = logits[0].max(axis=-1, keepdims=True)
    for t in range(1, nt):
        m = jnp.maximum(m, logits[t].max(axis=-1, keepdims=True))
    l = jnp.zeros((rows_q, 1), F32)
    acc = jnp.zeros((rows_q, hd), F32)
    for t in range(nt):
        pr = jnp.exp(logits[t] - m)
        l = l + jnp.sum(pr, axis=-1, keepdims=True)
        acc = acc + _dot(pr.astype(BF16), vp[t][...].astype(BF16) if t < n_pages else v_new)
    ao_ref[...] = acc / l


def _dsa_sample(qa, qi, wi, knew, vnew, kinew, qn_g, kn_g, bias_s, cache_k, cache_v, cache_i, page_table, *, t_new, hd):
    n_seq, n_pages = page_table.shape
    nk = n_pages * PAGE_SIZE + LANE
    k_sel = min(TOPK_MAX, (n_pages * PAGE_SIZE + t_new) // 4)
    pt = page_table.reshape(-1)

    def page_spec(arr, t):
        return pl.BlockSpec((None,) + arr.shape[1:], lambda b, pt: (pt[b * n_pages + t], 0, 0))

    seq_spec = lambda arr: pl.BlockSpec((None,) + arr.shape[1:], lambda b, pt: (b, 0, 0))
    const_spec = lambda arr: pl.BlockSpec(arr.shape, lambda b, pt: (0, 0))

    scores = pl.pallas_call(
        functools.partial(_idx_sample_kernel, n_pages=n_pages, t_new=t_new),
        grid_spec=pltpu.PrefetchScalarGridSpec(
            num_scalar_prefetch=1, grid=(n_seq,),
            in_specs=[seq_spec(qi), seq_spec(wi), seq_spec(kinew)] + [page_spec(cache_i, t) for t in range(n_pages)],
            out_specs=pl.BlockSpec((SUBLANE, nk), lambda b, pt: (b, 0))),
        out_shape=jax.ShapeDtypeStruct((n_seq * SUBLANE, nk), F32),
        compiler_params=_params(("arbitrary",), VMEM_LIMIT),
        name="idx_sample",
    )(pt, qi, wi, kinew, *([cache_i] * n_pages))

    rows = scores.shape[0]
    tr = LANE if rows % LANE == 0 else rows
    sel = pl.pallas_call(
        functools.partial(_select_kernel, k_sel=k_sel),
        grid=(rows // tr,),
        in_specs=[pl.BlockSpec((tr, nk), lambda r: (r, 0))],
        out_specs=pl.BlockSpec((tr, nk), lambda r: (r, 0)),
        out_shape=jax.ShapeDtypeStruct((rows, nk), F32),
        scratch_shapes=[pltpu.VMEM((tr, nk), jnp.int32)],
        compiler_params=_params(("parallel",), VMEM_LIMIT),
        name="select_sample",
    )(scores)

    qg, kg = qn_g.reshape(1, hd), kn_g.reshape(1, hd)
    return pl.pallas_call(
        functools.partial(_attn_sample_kernel, n_pages=n_pages, hd=hd),
        grid_spec=pltpu.PrefetchScalarGridSpec(
            num_scalar_prefetch=1, grid=(n_seq,),
            in_specs=[seq_spec(qa), pl.BlockSpec((SUBLANE, nk), lambda b, pt: (b, 0)), seq_spec(knew), seq_spec(vnew),
                      const_spec(qg), const_spec(kg), const_spec(bias_s)]
                     + [page_spec(cache_k, t) for t in range(n_pages)]
                     + [page_spec(cache_v, t) for t in range(n_pages)],
            out_specs=[seq_spec(qa), seq_spec(knew)]),
        out_shape=[jax.ShapeDtypeStruct(qa.shape, F32), jax.ShapeDtypeStruct(knew.shape, F32)],
        compiler_params=_params(("arbitrary",), VMEM_LIMIT),
        name="attn_sample",
    )(pt, qa, sel, knew, vnew, qg, kg, bias_s, *([cache_k] * n_pages), *([cache_v] * n_pages))


def _outproj_kernel(x_ref, mo_ref, ao_ref, gt_ref, sh_ref, sc_ref, g_ref, wm_ref, wa_ref, o_ref, h_ref):
    y = _dot(mo_ref[...].astype(BF16), wm_ref[...]) + _dot(ao_ref[...].astype(BF16), wa_ref[...])
    x = x_ref[...] + gt_ref[...] * y
    o_ref[...] = x
    h_ref[...] = _adaln(x, g_ref[...], sc_ref[...], sh_ref[...])


def _outproj(x, mo, ao, gt, w_out, sh, sc, g):
    m, d = x.shape
    mw, aw = mo.shape[1], ao.shape[1]
    tm = _row_tile(m)
    row = lambda w: pl.BlockSpec((tm, w), lambda i, j: (i, 0))
    return pl.pallas_call(
        _outproj_kernel,
        grid=(m // tm, 1),
        in_specs=[row(d), row(mw), row(aw), _mod_spec(gt, m, tm, d), _mod_spec(sh, m, tm, d), _mod_spec(sc, m, tm, d),
                  pl.BlockSpec((1, d), lambda i, j: (0, 0)),
                  pl.BlockSpec((mw, d), lambda i, j: (0, 0)),
                  pl.BlockSpec((aw, d), lambda i, j: (0, 0))],
        out_specs=[row(d), row(d)],
        out_shape=[jax.ShapeDtypeStruct((m, d), F32), jax.ShapeDtypeStruct((m, d), BF16)],
        compiler_params=_params(("parallel", "arbitrary"), VMEM_LIMIT),
        name="outproj",
    )(x, mo, ao, gt[0], sh[0], sc[0], g.reshape(1, d), w_out[:mw], w_out[mw:])


def _pad_tokens(a, t_pad):
    pad = [(0, 0)] * a.ndim
    pad[1] = (0, t_pad - a.shape[1])
    return jnp.pad(a, pad)


def _layer(x3, mods, lw, bias_p, bias_s, rec_state, past):
    n_seq, T, d = x3.shape
    x = x3.reshape(n_seq * T, d)
    dk, dv = rec_state[0].shape[2], rec_state[0].shape[3]
    mw = M_HEADS * dv
    hd = lw["q_norm_g"].shape[0]
    aw, kvw = A_HEADS * hd, A_KV_HEADS * hd
    iw = IDX_HEADS * IDX_DIM
    sh1, sc1, g1, sh2, sc2, g2, sh3, sc3, g3 = [(mods, k) for k in range(N_MOD)]

    x, ffn1_w = _ffn(x, g1, *lw["ffn1_w"], norm=(sh1, sc1, lw["ffn1_norm_g"]))
    qkw = 2 * M_HEADS * dk
    a_off = qkw + 2 * mw
    c0, n0, m0, conv_buf = rec_state
    prev8 = jnp.zeros((n_seq, SUBLANE, qkw), F32).at[:, SUBLANE - (CONV_W - 1):, :].set(conv_buf)
    prompt = None
    if past is None:
        prompt = (lw["q_norm_g"], (a_off, aw, a_off + aw + 2 * kvw, iw, hd))
    p, qn, qib = _proj(x, sh2, sc2, lw["mix_norm_g"], lw["w_in"], prompt)
    npk = p.shape[1]
    gate_row = jnp.zeros((1, LANE), F32).at[0, TAIL_IG:TAIL_IG + 2 * M_HEADS].set(lw["mlstm_gate_b"])
    m0_pad = jnp.zeros((n_seq, 1, LANE), F32).at[:, 0, :M_HEADS].set(m0)
    p3 = p.reshape(n_seq, T, npk)
    v_a = p3[:, :, a_off + aw + kvw:a_off + aw + 2 * kvw]
    ki = p3[:, :, npk - LANE + TAIL_KI:npk - LANE + TAIL_KI + IDX_DIM]
    conv_new = p3[:, T - (CONV_W - 1):, :qkw]

    if past is None:
        L = M_CHUNK if T % M_CHUNK == 0 else T
        mo, c_new, n_new, m_new = _mlstm(p3, prev8, lw["mlstm_conv_w"], gate_row, lw["mlstm_out_g"], c0, n0, m0_pad,
                                         L=L, valid=L, seqs_per_step=2 if n_seq % 2 == 0 else 1, out_dtype=BF16)
        mo = mo.reshape(n_seq * T, mw)
        ao, k_n, v_n = _dsa_prompt(p, qn, qib, lw["k_norm_g"], bias_p, n_seq=n_seq, T=T, hd=hd)
        k_n = k_n.reshape(n_seq, T, A_KV_HEADS, hd)
        v_n = v_n.reshape(n_seq, T, A_KV_HEADS, hd)
    else:
        tp = SUBLANE
        assert T <= tp
        mo, c_new, n_new, m_new = _mlstm(_pad_tokens(p3, tp), prev8, lw["mlstm_conv_w"], gate_row, lw["mlstm_out_g"],
                                         c0, n0, m0_pad, L=tp, valid=T,
                                         seqs_per_step=SUBLANE if n_seq % SUBLANE == 0 else 1, out_dtype=F32)
        mo = mo[:, :T].reshape(n_seq * T, mw)
        cache_k, cache_v, cache_i, page_table = past
        rep = A_HEADS // A_KV_HEADS
        qa = _pad_tokens(p3[:, :, a_off:a_off + aw], tp).reshape(n_seq, tp, A_KV_HEADS, rep, hd)
        qa = qa.transpose(0, 2, 3, 1, 4).reshape(n_seq, A_HEADS * tp, hd)
        qi = _pad_tokens(p3[:, :, a_off + aw + 2 * kvw:a_off + aw + 2 * kvw + iw], tp)
        qi = qi.reshape(n_seq, tp, IDX_HEADS, IDX_DIM).transpose(0, 2, 1, 3).reshape(n_seq, IDX_HEADS * tp, IDX_DIM)
        wi = _pad_tokens(p3[:, :, npk - LANE + TAIL_WI:npk - LANE + TAIL_WI + IDX_HEADS], tp)
        wi = wi.transpose(0, 2, 1).reshape(n_seq, IDX_HEADS * tp, 1)
        knew = _pad_tokens(p3[:, :, a_off + aw:a_off + aw + kvw], tp).reshape(n_seq, tp * A_KV_HEADS, hd)
        vnew = _pad_tokens(v_a, tp).reshape(n_seq, tp * A_KV_HEADS, hd)
        n_pool = cache_k.shape[0]
        ao4, k_n8 = _dsa_sample(qa, qi, wi, knew, vnew, _pad_tokens(ki, tp), lw["q_norm_g"], lw["k_norm_g"], bias_s,
                                cache_k.reshape(n_pool, PAGE_SIZE * A_KV_HEADS, hd),
                                cache_v.reshape(n_pool, PAGE_SIZE * A_KV_HEADS, hd),
                                jnp.swapaxes(cache_i, 1, 2), page_table, t_new=T, hd=hd)
        k_n8 = k_n8.reshape(n_seq, tp, kvw)
        ao = ao4.reshape(n_seq, A_KV_HEADS, rep, tp, hd)[:, :, :, :T].transpose(0, 3, 1, 2, 4).reshape(n_seq * T, aw)
        k_n = k_n8[:, :T].reshape(n_seq, T, A_KV_HEADS, hd)
        v_n = v_a.reshape(n_seq, T, A_KV_HEADS, hd)

    x, h3 = _outproj(x, mo, ao, g2, lw["w_out"], sh3, sc3, lw["ffn2_norm_g"])
    x, ffn2_w = _ffn(x, g3, *lw["ffn2_w"], h=h3)
    state = (k_n, v_n, ki, c_new, n_new, m_new[:, 0, :M_HEADS], conv_new)
    return x.reshape(n_seq, T, d), state, dict(lw, ffn1_w=ffn1_w, ffn2_w=ffn2_w)


def kernel(x_prompt, x_sample, c_prompt, c_sample, cache_k, cache_v, cache_idx_k, page_table, state_C, state_n,
           state_m, state_conv, ffn1_norm_g, ffn1_w_gate, ffn1_w_up, ffn1_w_down, mix_norm_g, w_in, mlstm_conv_w,
           mlstm_gate_b, mlstm_out_g, q_norm_g, k_norm_g, t5_bias, w_out, ffn2_norm_g, ffn2_w_gate, ffn2_w_up,
           ffn2_w_down, w_ada, b_ada):
    depth = w_in.shape[0]
    bp, tp_len, d = x_prompt.shape
    bs, ts_len, _ = x_sample.shape
    dk, dv = state_C.shape[3], state_C.shape[4]
    mw = M_HEADS * dv
    hd = q_norm_g.shape[1]
    past_len = page_table.shape[1] * PAGE_SIZE
    bias_p, bias_s = _bias_tables(t5_bias, tp_len, past_len, ts_len)
    rep = A_HEADS // A_KV_HEADS
    bias_s = bias_s.reshape(A_HEADS * SUBLANE, A_KV_HEADS * (past_len + LANE))

    xp, xs = x_prompt, x_sample
    st_p, st_s = [], []
    for l in range(depth):
        lw = dict(ffn1_norm_g=ffn1_norm_g[l], ffn1_w=(ffn1_w_gate[l], ffn1_w_up[l], ffn1_w_down[l]),
                  mix_norm_g=mix_norm_g[l], w_in=_pack_w_in(w_in[l], mw, A_HEADS * hd, A_KV_HEADS * hd),
                  mlstm_conv_w=mlstm_conv_w[l], mlstm_gate_b=mlstm_gate_b[l], mlstm_out_g=mlstm_out_g[l],
                  q_norm_g=q_norm_g[l], k_norm_g=k_norm_g[l], w_out=w_out[l].astype(BF16),
                  ffn2_norm_g=ffn2_norm_g[l], ffn2_w=(ffn2_w_gate[l], ffn2_w_up[l], ffn2_w_down[l]))
        n_s = bs * ts_len
        mods = _ada(jnp.concatenate([jnp.repeat(c_sample, ts_len, axis=0), c_prompt], axis=0), w_ada[l], b_ada[l])
        per_seq = tp_len % ROW_TILE == 0
        mods_p = mods[n_s:].reshape(bp, 1, N_MOD * d) if per_seq else jnp.repeat(mods[n_s:], tp_len, axis=0)
        init_s = (state_C[l], state_n[l], state_m[l], state_conv[l])
        xs, ss, lw = _layer(xs, mods, lw, bias_p, bias_s, init_s,
                            (cache_k[l], cache_v[l], cache_idx_k[l], page_table))
        init_p = (jnp.zeros((bp, M_HEADS, dk, dv), F32), jnp.zeros((bp, M_HEADS, dk), F32),
                  jnp.zeros((bp, M_HEADS), F32), jnp.zeros((bp, CONV_W - 1, 2 * M_HEADS * dk), F32))
        xp, sp, _ = _layer(xp, mods_p, lw, bias_p, bias_s, init_p, None)
        st_p.append(sp)
        st_s.append(ss)
    stack = lambda sts, k: sts[0][k][None] if depth == 1 else jnp.stack([s[k] for s in sts])
    return (xp, xs) + tuple(stack(st_p, k) for k in range(7)) + tuple(stack(st_s, k) for k in range(7))
```

```python
import functools
import math

import numpy as np
import jax
import jax.numpy as jnp
from jax import lax
from jax.experimental import pallas as pl
from jax.experimental.pallas import tpu as pltpu

F32 = jnp.float32
BF16 = jnp.bfloat16

M_HEADS = 4
CONV_W = 4
M_CHUNK = 64
A_HEADS = 8
A_KV_HEADS = 2
IDX_HEADS = 8
IDX_DIM = 64
TOPK_MAX = 256
Q_BLOCK = 128
PAGE_SIZE = 128
N_BUCKETS = 32
MAX_DISTANCE = 128
N_MOD = 9
EPS = 1e-6
NEG = -1e30

LANE = 128
SUBLANE = 8
ROW_TILE = 512
VMEM_LIMIT = 56 * 1024 * 1024

TAIL_KI = 0
TAIL_WI = IDX_DIM
TAIL_IG = IDX_DIM + IDX_HEADS
TAIL_FG = TAIL_IG + M_HEADS

INT_MIN = -2 ** 31
LOG2E = math.log2(math.e)


def _params(sem, vmem=None, flags=None):
    return pltpu.CompilerParams(dimension_semantics=sem, vmem_limit_bytes=vmem, flags=flags)


def _dot(a, b):
    return jnp.dot(a, b, preferred_element_type=F32)


def _dot_nt(a, b):
    return lax.dot_general(a, b, (((1,), (1,)), ((), ())), preferred_element_type=F32)


def _dot_tn(a, b):
    return lax.dot_general(a, b, (((0,), (0,)), ((), ())), preferred_element_type=F32)


def _rms(x, g):
    return x * lax.rsqrt(jnp.mean(x * x, axis=-1, keepdims=True) + EPS) * g


def _row_tile(n):
    return ROW_TILE if n % ROW_TILE == 0 else n


REDUCE_CHAINS = 4


def _reduce_rows(x, op):
    n, c = x.shape
    groups = n // SUBLANE
    chains = REDUCE_CHAINS if groups % REDUCE_CHAINS == 0 else 1
    x3 = x.reshape(groups, SUBLANE, c)
    per = groups // chains
    parts = [op(x3[k * per:(k + 1) * per], axis=0) for k in range(chains)]
    while len(parts) > 1:
        parts = [op(jnp.stack(parts[k:k + 2]), axis=0) for k in range(0, len(parts), 2)]
    return op(parts[0], axis=0, keepdims=True)


def _ada_kernel(c_ref, w_ref, b_ref, o_ref):
    o_ref[...] = _dot(c_ref[...].astype(BF16), w_ref[...].astype(BF16)) + b_ref[...]


def _ada(c_all, w_ada, b_ada):
    r, d = c_all.shape
    n = w_ada.shape[1]
    tn = 1024 if n % 1024 == 0 else n
    return pl.pallas_call(
        _ada_kernel,
        grid=(n // tn,),
        in_specs=[pl.BlockSpec((r, d), lambda j: (0, 0)),
                  pl.BlockSpec((d, tn), lambda j: (0, j)),
                  pl.BlockSpec((1, tn), lambda j: (0, j))],
        out_specs=pl.BlockSpec((r, tn), lambda j: (0, j)),
        out_shape=jax.ShapeDtypeStruct((r, n), F32),
        compiler_params=_params(("arbitrary",), VMEM_LIMIT),
        name="ada",
    )(c_all, w_ada, b_ada.reshape(1, n))


def _adaln(x, g, sc, sh):
    return (_rms(x, g) * (1.0 + sc) + sh).astype(BF16)


def _ffn_kernel(x_ref, gt_ref, *rest, prenormed, emit):
    if prenormed:
        hin_ref, wg_ref, wu_ref, wd_ref, o_ref, *rest = rest
    else:
        sh_ref, sc_ref, g_ref, wg_ref, wu_ref, wd_ref, o_ref, *rest = rest
    bf16_out, rest = (rest[:3], rest[3:]) if emit else ((), rest)
    acc_ref = rest[-1]
    j = pl.program_id(1)

    @pl.when(j == 0)
    def _():
        if not prenormed:
            rest[0][...] = _adaln(x_ref[...], g_ref[...], sc_ref[...], sh_ref[...])
        acc_ref[...] = jnp.zeros_like(acc_ref)

    wg, wu, wd = wg_ref[...], wu_ref[...], wd_ref[...]
    if emit:
        wg, wu, wd = wg.astype(BF16), wu.astype(BF16), wd.astype(BF16)
        for ref, w in zip(bf16_out, (wg, wu, wd)):
            ref[...] = w
    h = hin_ref[...] if prenormed else rest[0][...]
    a = _dot(h, wg)
    u = _dot(h, wu)
    act = (a * jax.nn.sigmoid(a) * u).astype(BF16)
    acc_ref[...] += _dot(act, wd)

    @pl.when(j == pl.num_programs(1) - 1)
    def _():
        o_ref[...] = x_ref[...] + 0.5 * gt_ref[...] * acc_ref[...]


def _mod_spec(mod, m, tm, d):
    arr, k = mod
    if arr.ndim == 2:
        return pl.BlockSpec((tm, d), lambda i, j: (i, k))
    tiles_per_seq = (m // arr.shape[0]) // tm
    return pl.BlockSpec((None, 1, d), lambda i, j: (i // tiles_per_seq, 0, k))


def _ffn(x, gt, wg, wu, wd, *, norm=None, h=None):
    m, d = x.shape
    f = wg.shape[1]
    tm = _row_tile(m)
    emit = wg.dtype != BF16
    assert not emit or m == tm
    tf = (256 if emit else 512) if f % 512 == 0 else f
    row = pl.BlockSpec((tm, d), lambda i, j: (i, 0))
    w_specs = [pl.BlockSpec((d, tf), lambda i, j: (0, j)),
               pl.BlockSpec((d, tf), lambda i, j: (0, j)),
               pl.BlockSpec((tf, d), lambda i, j: (j, 0))]
    if h is None:
        sh, sc, g = norm
        pre_specs = [_mod_spec(sh, m, tm, d), _mod_spec(sc, m, tm, d), pl.BlockSpec((1, d), lambda i, j: (0, 0))]
        pre_args = [sh[0], sc[0], g.reshape(1, d)]
        scratch = [pltpu.VMEM((tm, d), BF16)]
    else:
        pre_specs, pre_args, scratch = [row], [h], []
    out_specs = [row]
    out_shape = [jax.ShapeDtypeStruct((m, d), F32)]
    if emit:
        out_specs += w_specs
        out_shape += [jax.ShapeDtypeStruct(w.shape, BF16) for w in (wg, wu, wd)]
    y, *wb = pl.pallas_call(
        functools.partial(_ffn_kernel, prenormed=h is not None, emit=emit),
        grid=(m // tm, f // tf),
        in_specs=[row, _mod_spec(gt, m, tm, d)] + pre_specs + w_specs,
        out_specs=out_specs,
        out_shape=out_shape,
        scratch_shapes=scratch + [pltpu.VMEM((tm, d), F32)],
        compiler_params=_params(("parallel", "arbitrary"), VMEM_LIMIT),
        name="ffn",
    )(x, gt[0], *pre_args, wg, wu, wd)
    return y, (tuple(wb) if emit else (wg, wu, wd))


def _conv_silu(xbuf_ref, x_new, cw_ref):
    rows = x_new.shape[0]
    xbuf_ref[SUBLANE:SUBLANE + rows, :] = x_new
    base = SUBLANE - (CONV_W - 1)
    y = xbuf_ref[base:base + rows, :] * cw_ref[0:1, :]
    for j in range(1, CONV_W):
        y = y + xbuf_ref[base + j:base + j + rows, :] * cw_ref[j:j + 1, :]
    tail_rows = xbuf_ref[rows:rows + SUBLANE, :]
    xbuf_ref[0:SUBLANE, :] = tail_rows
    return y * jax.nn.sigmoid(y)


def _proj_kernel(x_ref, sh_ref, sc_ref, g_ref, wm_ref, wt_ref, *rest, q_cols):
    h = _adaln(x_ref[...], g_ref[...], sc_ref[...], sh_ref[...])
    nm = wm_ref.shape[1]
    o_ref = rest[1] if q_cols else rest[0]
    o_ref[:, 0:nm] = _dot(h, wm_ref[...])
    p = _dot(h, wt_ref[...])
    o_ref[:, nm:] = p
    if not q_cols:
        return
    qg_ref, _, qn_ref, qib_ref = rest
    a0, aw, i0, iw, hd = q_cols
    for hh in range(aw // hd):
        qn = _rms(p[:, a0 - nm + hh * hd:a0 - nm + (hh + 1) * hd], qg_ref[...]) * (hd ** -0.5 * LOG2E)
        qn_ref[:, hh * hd:(hh + 1) * hd] = qn.astype(BF16)
    qib_ref[...] = p[:, i0 - nm:i0 - nm + iw].astype(BF16)


PROJ_ROW_TILE = 256


def _proj(x, sh, sc, g, w, prompt=None):
    m, d = x.shape
    wm, wt = w
    n = wm.shape[1] + wt.shape[1]
    tm = PROJ_ROW_TILE if m % PROJ_ROW_TILE == 0 else m
    row = lambda width: pl.BlockSpec((tm, width), lambda i, j: (i, 0))
    resident = lambda arr: pl.BlockSpec(arr.shape, lambda i, j: (0, 0), pipeline_mode=pl.Buffered(1))
    in_specs = [row(d), _mod_spec(sh, m, tm, d), _mod_spec(sc, m, tm, d), pl.BlockSpec((1, d), lambda i, j: (0, 0)),
                resident(wm), resident(wt)]
    args = [x, sh[0], sc[0], g.reshape(1, d), wm, wt]
    out_specs, out_shape = [row(n)], [jax.ShapeDtypeStruct((m, n), F32)]
    q_cols = None
    if prompt:
        qn_g, q_cols = prompt
        a0, aw, i0, iw, hd = q_cols
        assert a0 >= wm.shape[1]
        in_specs.append(pl.BlockSpec((1, hd), lambda i, j: (0, 0)))
        args.append(qn_g.reshape(1, hd))
        out_specs += [row(aw), row(iw)]
        out_shape += [jax.ShapeDtypeStruct((m, aw), BF16), jax.ShapeDtypeStruct((m, iw), BF16)]
    out = pl.pallas_call(
        functools.partial(_proj_kernel, q_cols=q_cols),
        grid=(m // tm, 1),
        in_specs=in_specs,
        out_specs=out_specs,
        out_shape=out_shape,
        compiler_params=_params(("parallel", "arbitrary"), VMEM_LIMIT),
        name="proj",
    )(*args)
    return out if prompt else (out[0], None, None)


def _pack_w_in(w_in, mw, aw, kvw):
    d = w_in.shape[0]
    o = np.cumsum([0, mw, mw, mw, mw, M_HEADS, M_HEADS, aw, kvw, kvw, IDX_HEADS * IDX_DIM, IDX_DIM, IDX_HEADS])
    seg = lambda k: w_in[:, o[k]:o[k + 1]]
    pad = (-o[-1]) % LANE
    tail = jnp.concatenate([seg(k) for k in (6, 7, 8, 9, 10, 11, 4, 5)] + [jnp.zeros((d, pad), w_in.dtype)], axis=1)
    return w_in[:, :o[4]].astype(BF16), tail.astype(BF16)


def _log_sigmoid(x):
    return jnp.minimum(x, 0.0) - jnp.log1p(jnp.exp(-jnp.abs(x)))


def _mlstm_kernel(qk_ref, v_ref, o_ref, tail_ref, prev_ref, cw_ref, gb_ref, og_ref, c0_ref, n0_ref, m0_ref,
                  mo_ref, cst_ref, nst_ref, mst_ref, xbuf_ref, *, L, valid, dk, dv, one_chunk):
    if one_chunk:
        xbuf_ref[:, 0:SUBLANE, :] = prev_ref[...]
        mst_ref[...] = m0_ref[...]
        state_in = (c0_ref, n0_ref, m0_ref)
    else:
        state_in = (cst_ref, nst_ref, mst_ref)

        @pl.when(pl.program_id(1) == 0)
        def _():
            xbuf_ref[:, 0:SUBLANE, :] = prev_ref[...]
            cst_ref[...] = c0_ref[...]
            nst_ref[...] = n0_ref[...]
            mst_ref[...] = m0_ref[...]

    for sb in range(qk_ref.shape[0]):
        _mlstm_one(qk_ref.at[sb], v_ref.at[sb], o_ref.at[sb], tail_ref.at[sb], cw_ref, gb_ref, og_ref, mo_ref.at[sb],
                   *(r.at[sb] for r in state_in), cst_ref.at[sb], nst_ref.at[sb], mst_ref.at[sb], xbuf_ref.at[sb],
                   L=L, valid=valid, dk=dk, dv=dv)


def _mlstm_one(qk_ref, v_ref, o_ref, tail_ref, cw_ref, gb_ref, og_ref, mo_ref, cin_ref, nin_ref, min_ref,
               cst_ref, nst_ref, mst_ref, xbuf_ref, *, L, valid, dk, dv):
    width = M_HEADS * dk
    qk = _conv_silu(xbuf_ref, qk_ref[...], cw_ref)

    gates = tail_ref[...] + gb_ref[...]
    row = lax.broadcasted_iota(jnp.int32, (L, LANE), 0)
    lf = _log_sigmoid(gates)
    ig = gates
    if valid < L:
        lf = jnp.where(row < valid, lf, 0.0)
        ig = jnp.where(row < valid, ig, -jnp.inf)
    bcum = lf
    s = 1
    while s < L:
        bcum = bcum + jnp.where(row >= s, pltpu.roll(bcum, s, axis=0), 0.0)
        s *= 2
    bmi_t = (pltpu.roll(ig, TAIL_FG - TAIL_IG, axis=1) - bcum).T

    ri = lax.broadcasted_iota(jnp.int32, (L, L), 0)
    ci = lax.broadcasted_iota(jnp.int32, (L, L), 1)
    causal = ri >= ci

    for h in range(M_HEADS):
        li, lfh = TAIL_IG + h, TAIL_FG + h
        a_col = bcum[:, lfh:lfh + 1]
        ig_col = ig[:, li:li + 1]
        b_row = bmi_t[lfh:lfh + 1, :]
        m0 = min_ref[:, h:h + 1]
        d_log = jnp.where(causal, a_col + b_row, -jnp.inf)
        s_log = a_col + m0
        m_col = jnp.maximum(s_log, jnp.max(d_log, axis=-1, keepdims=True))
        dw = jnp.exp(d_log - m_col)
        sw = jnp.exp(s_log - m_col)

        q = qk[:, h * dk:(h + 1) * dk]
        k = qk[:, width + h * dk:width + (h + 1) * dk] * (dk ** -0.5)
        v = v_ref[:, h * dv:(h + 1) * dv]
        qb, kb = q.astype(BF16), k.astype(BF16)
        c_prev = cin_ref[h]
        n_prev = nin_ref[h:h + 1, :]
        scores = _dot_nt(qb, kb) * dw
        num = _dot(scores.astype(BF16), v.astype(BF16)) + sw * _dot(qb, c_prev.astype(BF16))
        den = jnp.sum(scores, axis=-1, keepdims=True) + sw * jnp.sum(q * n_prev, axis=-1, keepdims=True)
        hh = num / jnp.maximum(jnp.abs(den), jnp.exp(-m_col))

        m_new = m_col[L - 1:L, :]
        a_last = a_col[L - 1:L, :]
        wl = jnp.exp(a_last - a_col + ig_col - m_new)
        decay = jnp.exp(a_last + m0 - m_new)
        cst_ref[h] = decay * c_prev + _dot_tn(kb, (wl * v).astype(BF16))
        nst_ref[h:h + 1, :] = decay * n_prev + jnp.sum(wl * k, axis=0, keepdims=True)
        mst_ref[:, h:h + 1] = m_new

        hn = hh * lax.rsqrt(jnp.mean(hh * hh, axis=-1, keepdims=True) + EPS) * og_ref[:, h * dv:(h + 1) * dv]
        gate = jax.nn.sigmoid(o_ref[:, h * dv:(h + 1) * dv])
        mo_ref[:, h * dv:(h + 1) * dv] = (hn * gate).astype(mo_ref.dtype)


def _mlstm(p3, prev8, conv_w, gate_row, out_g, c0, n0, m0, *, L, valid, seqs_per_step, out_dtype):
    n_seq, t_all, npk = p3.shape
    nc = t_all // L
    sb = seqs_per_step
    assert n_seq % sb == 0
    dk = c0.shape[2]
    dv = c0.shape[3]
    mw = M_HEADS * dv
    qkw = 2 * M_HEADS * dk
    assert qkw % mw == 0 and (2 * qkw) % mw == 0
    kern = functools.partial(_mlstm_kernel, L=L, valid=valid, dk=dk, dv=dv, one_chunk=nc == 1)
    seq = lambda shape: pl.BlockSpec((sb,) + shape, lambda b, c: (b,) + (0,) * len(shape))
    return pl.pallas_call(
        kern,
        grid=(n_seq // sb, nc),
        in_specs=[pl.BlockSpec((sb, L, qkw), lambda b, c: (b, c, 0)),
                  pl.BlockSpec((sb, L, mw), lambda b, c: (b, c, qkw // mw)),
                  pl.BlockSpec((sb, L, mw), lambda b, c: (b, c, qkw // mw + 1)),
                  pl.BlockSpec((sb, L, LANE), lambda b, c: (b, c, npk // LANE - 1)),
                  seq((SUBLANE, qkw)),
                  pl.BlockSpec((CONV_W, qkw), lambda b, c: (0, 0)),
                  pl.BlockSpec((1, LANE), lambda b, c: (0, 0)),
                  pl.BlockSpec((1, mw), lambda b, c: (0, 0)),
                  seq((M_HEADS, dk, dv)), seq((M_HEADS, dk)), seq((1, LANE))],
        out_specs=[pl.BlockSpec((sb, L, mw), lambda b, c: (b, c, 0)),
                   seq((M_HEADS, dk, dv)), seq((M_HEADS, dk)), seq((1, LANE))],
        out_shape=[jax.ShapeDtypeStruct((n_seq, t_all, mw), out_dtype),
                   jax.ShapeDtypeStruct(c0.shape, F32),
                   jax.ShapeDtypeStruct(n0.shape, F32),
                   jax.ShapeDtypeStruct(m0.shape, F32)],
        scratch_shapes=[pltpu.VMEM((sb, SUBLANE + L, qkw), F32)],
        compiler_params=_params(("parallel", "arbitrary"), VMEM_LIMIT),
        name="mlstm",
    )(p3, p3, p3, p3, prev8, conv_w, gate_row, out_g.reshape(1, mw), c0, n0, m0)


def _bucket_np(dist):
    me = N_BUCKETS // 2
    d = np.maximum(dist, 0)
    ratio = np.log(np.maximum(d, 1).astype(np.float64) / me) / math.log(MAX_DISTANCE / me)
    large = np.minimum(me + (ratio * (N_BUCKETS - me)).astype(np.int64), N_BUCKETS - 1)
    return np.where(d < me, d, large).astype(np.int32)


def _bias_kernel(t5_ref, bkp_ref, bks_ref, vs_ref, far_ref, op_ref, os_ref):
    rep = A_HEADS // A_KV_HEADS

    def lookup(bk, h):
        acc = jnp.zeros(bk.shape, F32)
        for b in range(N_BUCKETS):
            acc = jnp.where(bk == b, t5_ref[b, h], acc)
        return acc

    for h in range(A_HEADS):
        half = (h % 2) * Q_BLOCK
        op_ref[h // 2, :, half:half + Q_BLOCK] = (lookup(bkp_ref[...], h) - t5_ref[far_ref[0], h]) * LOG2E
        os_ref[h] = jnp.where(vs_ref[h // rep] > 0, lookup(bks_ref[...], h), NEG)


def _bias_tables(t5_bias, t_prompt, past, t_new):
    c = np.arange(2 * Q_BLOCK)[:, None]
    r = np.arange(Q_BLOCK)[None, :]
    bkp = _bucket_np(Q_BLOCK + r - c)
    far = _bucket_np(np.arange(Q_BLOCK + 1, max(t_prompt, past + t_new) + Q_BLOCK))
    assert (far == far[0]).all(), "bias must be constant beyond one query block"
    nk = past + LANE
    tq = np.arange(SUBLANE)[:, None]
    col = np.arange(2 * nk)[None, :]
    s_pos, s_grp = col // A_KV_HEADS, col % A_KV_HEADS
    bks = _bucket_np(past + tq - s_pos)
    vis = np.stack([(s_grp == g) & (s_pos - past <= tq) for g in range(A_KV_HEADS)]).astype(np.int32)
    return pl.pallas_call(
        _bias_kernel,
        in_specs=[pl.BlockSpec(memory_space=pltpu.SMEM),
                  pl.BlockSpec(memory_space=pltpu.VMEM),
                  pl.BlockSpec(memory_space=pltpu.VMEM),
                  pl.BlockSpec(memory_space=pltpu.VMEM),
                  pl.BlockSpec(memory_space=pltpu.SMEM)],
        out_specs=[pl.BlockSpec(memory_space=pltpu.VMEM), pl.BlockSpec(memory_space=pltpu.VMEM)],
        out_shape=[jax.ShapeDtypeStruct((A_HEADS // 2, 2 * Q_BLOCK, 2 * Q_BLOCK), F32),
                   jax.ShapeDtypeStruct((A_HEADS, SUBLANE, 2 * nk), F32)],
        name="t5_bias_tables",
    )(t5_bias, jnp.asarray(bkp), jnp.asarray(bks), jnp.asarray(vis), jnp.asarray(far[:1]))


def _sort_key(score):
    bits = lax.bitcast_convert_type(score, jnp.int32)
    return jnp.where(bits < 0, bits ^ jnp.int32(0x7FFFFFFF), bits)


def _kth_key(count_ge, shape, k_sel):
    def body(it, ans_u):
        cand_u = ans_u | lax.shift_left(jnp.int32(1), jnp.int32(31) - it)
        cnt = count_ge(cand_u ^ jnp.int32(INT_MIN))
        return jnp.where(cnt >= k_sel, cand_u, ans_u)

    ans_u = lax.fori_loop(0, 32, body, jnp.zeros(shape, jnp.int32))
    return ans_u ^ jnp.int32(INT_MIN)


def _triangle(below):
    r = lax.broadcasted_iota(jnp.int32, (LANE, LANE), 0)
    c = lax.broadcasted_iota(jnp.int32, (LANE, LANE), 1)
    return jnp.where((c < r) if below else (r < c), 1.0, 0.0).astype(BF16)


def _select_tiles(key_tiles, thr, k_sel, key_axis):
    count = lambda x: jnp.sum(x, axis=key_axis, keepdims=True)
    c_gt = jnp.zeros(thr.shape, F32)
    for kt in key_tiles:
        c_gt = c_gt + count(jnp.where(kt > thr, 1.0, 0.0))
    need = k_sel - c_gt
    tri = _triangle(below=(key_axis == 0))
    run = jnp.zeros(thr.shape, F32)
    out = []
    for kt in key_tiles:
        eq = jnp.where(kt == thr, 1.0, 0.0)
        eqb = eq.astype(BF16)
        before = (_dot(tri, eqb) if key_axis == 0 else _dot(eqb, tri)) + run
        out.append(jnp.where(kt > thr, 1.0, jnp.where(before < need, eq, 0.0)))
        run = run + count(eq)
    return out


FAR_STEP = 1


def _dsa_prompt_kernel(qn_ref, qi_ref, qtail_ref, k_ref, v_ref, ktail_ref, kg_ref, bias_ref,
                       ao_ref, kn_ref, vo_ref,
                       kb_ref, vt_ref, kib_ref, key_ref, negb_ref, *, T, k_sel, hd):
    i = pl.program_id(1)
    nb = T // Q_BLOCK
    rep = A_HEADS // A_KV_HEADS
    near_w = 2 * LANE

    @pl.when(i == 0)
    def _():
        kb_ref[0:LANE, :] = jnp.zeros((LANE, kb_ref.shape[1]), BF16)
        vt_ref[:, 0:LANE] = jnp.zeros((vt_ref.shape[0], LANE), BF16)
        kib_ref[0:LANE, :] = jnp.zeros((LANE, IDX_DIM), BF16)
        for g in range(A_KV_HEADS):
            kn = _rms(k_ref[:, g * hd:(g + 1) * hd], kg_ref[...])
            kn_ref[pl.ds(g, T, stride=A_KV_HEADS), :] = kn
            vo_ref[pl.ds(g, T, stride=A_KV_HEADS), :] = v_ref[:, g * hd:(g + 1) * hd]
            kb_ref[LANE:, g * hd:(g + 1) * hd] = kn.astype(BF16)
        vt_ref[:, LANE:] = v_ref[...].T.astype(BF16)
        kib_ref[LANE:, :] = ktail_ref[:, TAIL_KI:TAIL_KI + IDX_DIM].astype(BF16)

    wi_t = qtail_ref[...].T[TAIL_WI:TAIL_WI + IDX_HEADS, :] * (IDX_HEADS ** -0.5) * (IDX_DIM ** -0.5)
    qi_pairs = [jnp.concatenate([qi_ref[:, (2 * a + hh) * IDX_DIM:(2 * a + hh + 1) * IDX_DIM] for hh in range(2)],
                                axis=0) for a in range(IDX_HEADS // 2)]
    qn_pairs = [jnp.concatenate([qn_ref[:, (2 * a + hh) * hd:(2 * a + hh + 1) * hd] for hh in range(2)], axis=0)
                for a in range(A_HEADS // 2)]

    near0 = pl.multiple_of(i * LANE, LANE)
    ki_near = kib_ref[pl.ds(near0, near_w), :]
    k_near = kb_ref[pl.ds(near0, near_w), :]
    vt_near = vt_ref[:, pl.ds(near0, near_w)]
    kc = lax.broadcasted_iota(jnp.int32, (near_w, Q_BLOCK), 0)
    qr = lax.broadcasted_iota(jnp.int32, (near_w, Q_BLOCK), 1)
    near_ok = jnp.where(kc <= qr + LANE, 1.0, 0.0) * jnp.maximum(jnp.where(kc >= LANE, 1.0, 0.0),
                                                                  (i >= 1).astype(F32))

    def scores(ki):
        sc = jnp.zeros((ki.shape[0], Q_BLOCK), F32)
        for a in range(IDX_HEADS // 2):
            s2 = _dot_nt(ki, qi_pairs[a])
            for hh in range(2):
                j = 2 * a + hh
                sc = sc + wi_t[j:j + 1, :] * jnp.maximum(s2[:, hh * Q_BLOCK:(hh + 1) * Q_BLOCK], 0.0)
        return sc

    def body(wf):
        fw = wf * LANE
        w_all = fw + near_w
        if w_all <= k_sel:
            negb_ref[0:w_all, :] = jnp.where(near_ok > 0.0, 0.0, NEG)
            attend(wf)
            return
        key_ref[fw:w_all, :] = _sort_key(jnp.where(near_ok > 0.0, scores(ki_near) + 0.0, NEG))
        if wf:
            far_ok = lax.broadcasted_iota(jnp.int32, (fw, Q_BLOCK), 0) < (i - 1) * LANE
            key_ref[0:fw, :] = _sort_key(jnp.where(far_ok, scores(kib_ref[LANE:LANE + fw, :]) + 0.0, NEG))

        def count_ge(cand):
            return _reduce_rows(jnp.where(key_ref[0:w_all, :] >= cand, 1.0, 0.0), jnp.sum)

        thr = _kth_key(count_ge, (1, Q_BLOCK), float(k_sel))
        sel = _select_tiles([key_ref[t * LANE:(t + 1) * LANE, :] for t in range(wf + 2)], thr, float(k_sel), 0)
        for t in range(wf):
            ok = sel[t] * (t < i - 1).astype(F32)
            negb_ref[t * LANE:(t + 1) * LANE, :] = jnp.where(ok > 0.0, 0.0, NEG)
        for t in range(2):
            ok = sel[wf + t] * near_ok[t * LANE:(t + 1) * LANE, :]
            negb_ref[fw + t * LANE:fw + (t + 1) * LANE, :] = jnp.where(ok > 0.0, 0.0, NEG)
        attend(wf)

    def attend(wf):
        fw = wf * LANE
        w_all = fw + near_w
        for a in range(A_HEADS // 2):
            g = (2 * a) // rep
            cs = slice(g * hd, (g + 1) * hd)
            qn2 = qn_pairs[a]
            nb_near = negb_ref[fw:w_all, :]
            s_near = _dot_nt(k_near[:, cs], qn2) + bias_ref[a] + jnp.concatenate([nb_near, nb_near], axis=1)
            m = _reduce_rows(s_near, jnp.max)
            if wf:
                nb_far = negb_ref[0:fw, :]
                s_far = _dot_nt(kb_ref[LANE:LANE + fw, cs], qn2) + jnp.concatenate([nb_far, nb_far], axis=1)
                m = jnp.maximum(m, _reduce_rows(s_far, jnp.max))
            p_near = jnp.exp2(s_near - m)
            l = _reduce_rows(p_near, jnp.sum)
            acc = _dot(vt_near[cs, :], p_near.astype(BF16))
            if wf:
                p_far = jnp.exp2(s_far - m)
                l = l + _reduce_rows(p_far, jnp.sum)
                acc = acc + _dot(vt_ref[cs, LANE:LANE + fw], p_far.astype(BF16))
            out_t = acc / l
            for hh in range(2):
                h = 2 * a + hh
                ao_ref[:, h * hd:(h + 1) * hd] = out_t[:, hh * Q_BLOCK:(hh + 1) * Q_BLOCK].T.astype(ao_ref.dtype)

    variants = sorted({min(FAR_STEP * -(-x // FAR_STEP), nb) for x in range(max(nb - 1, 1))})
    wf_needed = jnp.minimum((jnp.maximum(i - 1, 0) + FAR_STEP - 1) // FAR_STEP * FAR_STEP, nb)
    for wf in variants:
        pl.when(wf_needed == wf)(functools.partial(body, wf))


def _dsa_prompt(p, qn, qib, kn_g, bias_p, *, n_seq, T, hd):
    rows = p.shape[0]
    nb = T // Q_BLOCK
    aw = A_HEADS * hd
    kvw = A_KV_HEADS * hd
    iw = IDX_HEADS * IDX_DIM
    k_sel = min(TOPK_MAX, T // 4)
    kcol = p.shape[1] - LANE - iw - 2 * kvw
    assert kcol % kvw == 0
    k_blk = kcol // kvw
    tail_blk = p.shape[1] // LANE - 1
    kern = functools.partial(_dsa_prompt_kernel, T=T, k_sel=k_sel, hd=hd)
    return pl.pallas_call(
        kern,
        grid=(n_seq, nb),
        in_specs=[pl.BlockSpec((Q_BLOCK, aw), lambda b, i: (b * nb + i, 0)),
                  pl.BlockSpec((Q_BLOCK, iw), lambda b, i: (b * nb + i, 0)),
                  pl.BlockSpec((Q_BLOCK, LANE), lambda b, i: (b * nb + i, tail_blk)),
                  pl.BlockSpec((T, kvw), lambda b, i: (b, k_blk)),
                  pl.BlockSpec((T, kvw), lambda b, i: (b, k_blk + 1)),
                  pl.BlockSpec((T, LANE), lambda b, i: (b, tail_blk)),
                  pl.BlockSpec((1, hd), lambda b, i: (0, 0)),
                  pl.BlockSpec((A_HEADS // 2, 2 * Q_BLOCK, 2 * Q_BLOCK), lambda b, i: (0, 0, 0))],
        out_specs=[pl.BlockSpec((Q_BLOCK, aw), lambda b, i: (b * nb + i, 0)),
                   pl.BlockSpec((A_KV_HEADS * T, hd), lambda b, i: (b, 0)),
                   pl.BlockSpec((A_KV_HEADS * T, hd), lambda b, i: (b, 0))],
        out_shape=[jax.ShapeDtypeStruct((rows, aw), BF16),
                   jax.ShapeDtypeStruct((A_KV_HEADS * rows, hd), F32),
                   jax.ShapeDtypeStruct((A_KV_HEADS * rows, hd), F32)],
        scratch_shapes=[pltpu.VMEM((T + LANE, kvw), BF16), pltpu.VMEM((kvw, T + LANE), BF16),
                        pltpu.VMEM((T + LANE, IDX_DIM), BF16),
                        pltpu.VMEM((T + 2 * LANE, Q_BLOCK), jnp.int32),
                        pltpu.VMEM((T + 2 * LANE, Q_BLOCK), F32)],
        compiler_params=_params(("parallel", "arbitrary"), VMEM_LIMIT),
        name="dsa_prompt",
    )(qn, qib, p, p, p, p, kn_g.reshape(1, hd), bias_p)


def _pad_rows(x, n):
    return jnp.concatenate([x, jnp.zeros((n - x.shape[0], x.shape[1]), x.dtype)], axis=0)


def _idx_sample_kernel(pt_ref, qi_ref, wi_ref, kinew_ref, *rest, n_pages, t_new):
    ip, sc_ref = rest[:n_pages], rest[n_pages]
    del pt_ref
    qi = qi_ref[...].astype(BF16)
    wi = wi_ref[...] * (IDX_HEADS ** -0.5) * (IDX_DIM ** -0.5)
    tq = lax.broadcasted_iota(jnp.int32, (SUBLANE, LANE), 0)
    lane = lax.broadcasted_iota(jnp.int32, (SUBLANE, LANE), 1)
    for t in range(n_pages + 1):
        r = _dot(qi, ip[t][...].astype(BF16)) if t < n_pages else _dot_nt(qi, _pad_rows(kinew_ref[...], LANE).astype(BF16))
        sc = jnp.zeros((SUBLANE, LANE), F32)
        for j in range(IDX_HEADS):
            sc = sc + wi[j * SUBLANE:(j + 1) * SUBLANE, :] * jnp.maximum(r[j * SUBLANE:(j + 1) * SUBLANE, :], 0.0)
        sc = sc + 0.0
        if t == n_pages:
            sc = jnp.where(lane <= tq, sc, jnp.where(lane < t_new, NEG, -jnp.inf))
        sc_ref[:, t * LANE:(t + 1) * LANE] = sc


def _select_kernel(sc_ref, sel_ref, key_ref, *, k_sel):
    rows, nk = sc_ref.shape
    key_ref[...] = _sort_key(sc_ref[...])

    def count_ge(cand):
        return jnp.sum(jnp.where(key_ref[...] >= cand, 1.0, 0.0), axis=1, keepdims=True)

    thr = _kth_key(count_ge, (rows, 1), float(k_sel))
    sel = _select_tiles([key_ref[:, t * LANE:(t + 1) * LANE] for t in range(nk // LANE)], thr, float(k_sel), 1)
    for t in range(nk // LANE):
        sel_ref[:, t * LANE:(t + 1) * LANE] = sel[t]


def _attn_sample_kernel(pt_ref, qa_ref, sel_ref, knew_ref, vnew_ref, qg_ref, kg_ref, bias_ref, *rest,
                        n_pages, hd):
    kp, vp = rest[:n_pages], rest[n_pages:2 * n_pages]
    ao_ref, kn_ref = rest[2 * n_pages:]
    del pt_ref
    nt = n_pages + 1
    pw = A_KV_HEADS * PAGE_SIZE
    scale = hd ** -0.5
    rows_q = qa_ref.shape[0]

    kn_new = _rms(knew_ref[...], kg_ref[...])
    kn_ref[...] = kn_new
    k_new = _pad_rows(kn_new, pw).astype(BF16)
    v_new = _pad_rows(vnew_ref[...], pw).astype(BF16)
    qn = _rms(qa_ref[...], qg_ref[...]).astype(BF16)

    tok = lax.broadcasted_iota(jnp.int32, (PAGE_SIZE, pw), 0)
    col = lax.broadcasted_iota(jnp.int32, (PAGE_SIZE, pw), 1)
    spread = jnp.where(col // A_KV_HEADS == tok, 1.0, 0.0).astype(BF16)
    sel_rows = jnp.concatenate([sel_ref[:, t * LANE:(t + 1) * LANE] for t in range(nt)], axis=0)
    negb = (_dot(sel_rows.astype(BF16), spread) - 1.0) * (-NEG)

    logits = []
    for t in range(nt):
        kt = kp[t][...].astype(BF16) if t < n_pages else k_new
        nb = jnp.concatenate([negb[t * SUBLANE:(t + 1) * SUBLANE, :]] * (rows_q // SUBLANE), axis=0)
        logits.append(_dot_nt(qn, kt) * scale + bias_ref[:, t * pw:(t + 1) * pw] + nb)
    mx = logits[0]
    for t in range(1, nt):
        mx = jnp.maximum(mx, logits[t])
    m = jnp.max(mx, axis=-1, keepdims=True)
    psum = jnp.zeros((rows_q, pw), F32)
    acc = jnp.zeros((rows_q, hd), F32)
    for t in range(nt):
        pr = jnp.exp(logits[t] - m)
        psum = psum + pr
        acc = acc + _dot(pr.astype(BF16), vp[t][...].astype(BF16) if t < n_pages else v_new)
    ao_ref[...] = acc / jnp.sum(psum, axis=-1, keepdims=True)


def _dsa_sample(qa, qi, wi, knew, vnew, kinew, qn_g, kn_g, bias_s, cache_k, cache_v, cache_i, page_table, *, t_new, hd):
    n_seq, n_pages = page_table.shape
    nk = n_pages * PAGE_SIZE + LANE
    k_sel = min(TOPK_MAX, (n_pages * PAGE_SIZE + t_new) // 4)
    pt = page_table.reshape(-1)

    def page_spec(arr, t):
        return pl.BlockSpec((None,) + arr.shape[1:], lambda b, pt: (pt[b * n_pages + t], 0, 0))

    seq_spec = lambda arr: pl.BlockSpec((None,) + arr.shape[1:], lambda b, pt: (b, 0, 0))
    const_spec = lambda arr: pl.BlockSpec(arr.shape, lambda b, pt: (0, 0))

    scores = pl.pallas_call(
        functools.partial(_idx_sample_kernel, n_pages=n_pages, t_new=t_new),
        grid_spec=pltpu.PrefetchScalarGridSpec(
            num_scalar_prefetch=1, grid=(n_seq,),
            in_specs=[seq_spec(qi), seq_spec(wi), seq_spec(kinew)] + [page_spec(cache_i, t) for t in range(n_pages)],
            out_specs=pl.BlockSpec((SUBLANE, nk), lambda b, pt: (b, 0))),
        out_shape=jax.ShapeDtypeStruct((n_seq * SUBLANE, nk), F32),
        compiler_params=_params(("arbitrary",), VMEM_LIMIT),
        name="idx_sample",
    )(pt, qi, wi, kinew, *([cache_i] * n_pages))

    rows = scores.shape[0]
    tr = LANE if rows % LANE == 0 else rows
    sel = pl.pallas_call(
        functools.partial(_select_kernel, k_sel=k_sel),
        grid=(rows // tr,),
        in_specs=[pl.BlockSpec((tr, nk), lambda r: (r, 0))],
        out_specs=pl.BlockSpec((tr, nk), lambda r: (r, 0)),
        out_shape=jax.ShapeDtypeStruct((rows, nk), F32),
        scratch_shapes=[pltpu.VMEM((tr, nk), jnp.int32)],
        compiler_params=_params(("parallel",), VMEM_LIMIT),
        name="select_sample",
    )(scores)

    qg, kg = qn_g.reshape(1, hd), kn_g.reshape(1, hd)
    return pl.pallas_call(
        functools.partial(_attn_sample_kernel, n_pages=n_pages, hd=hd),
        grid_spec=pltpu.PrefetchScalarGridSpec(
            num_scalar_prefetch=1, grid=(n_seq,),
            in_specs=[seq_spec(qa), pl.BlockSpec((SUBLANE, nk), lambda b, pt: (b, 0)), seq_spec(knew), seq_spec(vnew),
                      const_spec(qg), const_spec(kg), const_spec(bias_s)]
                     + [page_spec(cache_k, t) for t in range(n_pages)]
                     + [page_spec(cache_v, t) for t in range(n_pages)],
            out_specs=[seq_spec(qa), seq_spec(knew)]),
        out_shape=[jax.ShapeDtypeStruct(qa.shape, F32), jax.ShapeDtypeStruct(knew.shape, F32)],
        compiler_params=_params(("arbitrary",), VMEM_LIMIT),
        name="attn_sample",
    )(pt, qa, sel, knew, vnew, qg, kg, bias_s, *([cache_k] * n_pages), *([cache_v] * n_pages))


def _outproj_kernel(x_ref, mo_ref, ao_ref, gt_ref, sh_ref, sc_ref, g_ref, wm_ref, wa_ref, o_ref, h_ref):
    y = _dot(mo_ref[...].astype(BF16), wm_ref[...]) + _dot(ao_ref[...].astype(BF16), wa_ref[...])
    x = x_ref[...] + gt_ref[...] * y
    o_ref[...] = x
    h_ref[...] = _adaln(x, g_ref[...], sc_ref[...], sh_ref[...])


def _outproj(x, mo, ao, gt, w_out, sh, sc, g):
    m, d = x.shape
    mw, aw = mo.shape[1], ao.shape[1]
    tm = _row_tile(m)
    row = lambda w: pl.BlockSpec((tm, w), lambda i, j: (i, 0))
    return pl.pallas_call(
        _outproj_kernel,
        grid=(m // tm, 1),
        in_specs=[row(d), row(mw), row(aw), _mod_spec(gt, m, tm, d), _mod_spec(sh, m, tm, d), _mod_spec(sc, m, tm, d),
                  pl.BlockSpec((1, d), lambda i, j: (0, 0)),
                  pl.BlockSpec((mw, d), lambda i, j: (0, 0)),
                  pl.BlockSpec((aw, d), lambda i, j: (0, 0))],
        out_specs=[row(d), row(d)],
        out_shape=[jax.ShapeDtypeStruct((m, d), F32), jax.ShapeDtypeStruct((m, d), BF16)],
        compiler_params=_params(("parallel", "arbitrary"), VMEM_LIMIT),
        name="outproj",
    )(x, mo, ao, gt[0], sh[0], sc[0], g.reshape(1, d), w_out[:mw], w_out[mw:])


def _pad_tokens(a, t_pad):
    pad = [(0, 0)] * a.ndim
    pad[1] = (0, t_pad - a.shape[1])
    return jnp.pad(a, pad)


def _layer(x3, mods, lw, bias_p, bias_s, rec_state, past):
    n_seq, T, d = x3.shape
    x = x3.reshape(n_seq * T, d)
    dk, dv = rec_state[0].shape[2], rec_state[0].shape[3]
    mw = M_HEADS * dv
    hd = lw["q_norm_g"].shape[0]
    aw, kvw = A_HEADS * hd, A_KV_HEADS * hd
    iw = IDX_HEADS * IDX_DIM
    sh1, sc1, g1, sh2, sc2, g2, sh3, sc3, g3 = [(mods, k) for k in range(N_MOD)]

    x, ffn1_w = _ffn(x, g1, *lw["ffn1_w"], norm=(sh1, sc1, lw["ffn1_norm_g"]))
    qkw = 2 * M_HEADS * dk
    a_off = qkw + 2 * mw
    c0, n0, m0, conv_buf = rec_state
    prev8 = jnp.zeros((n_seq, SUBLANE, qkw), F32).at[:, SUBLANE - (CONV_W - 1):, :].set(conv_buf)
    prompt = None
    if past is None:
        prompt = (lw["q_norm_g"], (a_off, aw, a_off + aw + 2 * kvw, iw, hd))
    p, qn, qib = _proj(x, sh2, sc2, lw["mix_norm_g"], lw["w_in"], prompt)
    npk = p.shape[1]
    gate_row = jnp.zeros((1, LANE), F32).at[0, TAIL_IG:TAIL_IG + 2 * M_HEADS].set(lw["mlstm_gate_b"])
    m0_pad = jnp.zeros((n_seq, 1, LANE), F32).at[:, 0, :M_HEADS].set(m0)
    p3 = p.reshape(n_seq, T, npk)
    v_a = p3[:, :, a_off + aw + kvw:a_off + aw + 2 * kvw]
    ki = p3[:, :, npk - LANE + TAIL_KI:npk - LANE + TAIL_KI + IDX_DIM]
    conv_new = p3[:, T - (CONV_W - 1):, :qkw]

    if past is None:
        L = M_CHUNK if T % M_CHUNK == 0 else T
        mo, c_new, n_new, m_new = _mlstm(p3, prev8, lw["mlstm_conv_w"], gate_row, lw["mlstm_out_g"], c0, n0, m0_pad,
                                         L=L, valid=L, seqs_per_step=2 if n_seq % 2 == 0 else 1, out_dtype=BF16)
        mo = mo.reshape(n_seq * T, mw)
        ao, k_n, v_n = _dsa_prompt(p, qn, qib, lw["k_norm_g"], bias_p, n_seq=n_seq, T=T, hd=hd)
        k_n = k_n.reshape(n_seq, T, A_KV_HEADS, hd)
        v_n = v_n.reshape(n_seq, T, A_KV_HEADS, hd)
    else:
        tp = SUBLANE
        assert T <= tp
        mo, c_new, n_new, m_new = _mlstm(_pad_tokens(p3, tp), prev8, lw["mlstm_conv_w"], gate_row, lw["mlstm_out_g"],
                                         c0, n0, m0_pad, L=tp, valid=T,
                                         seqs_per_step=SUBLANE if n_seq % SUBLANE == 0 else 1, out_dtype=F32)
        mo = mo[:, :T].reshape(n_seq * T, mw)
        cache_k, cache_v, cache_i, page_table = past
        rep = A_HEADS // A_KV_HEADS
        qa = _pad_tokens(p3[:, :, a_off:a_off + aw], tp).reshape(n_seq, tp, A_KV_HEADS, rep, hd)
        qa = qa.transpose(0, 2, 3, 1, 4).reshape(n_seq, A_HEADS * tp, hd)
        qi = _pad_tokens(p3[:, :, a_off + aw + 2 * kvw:a_off + aw + 2 * kvw + iw], tp)
        qi = qi.reshape(n_seq, tp, IDX_HEADS, IDX_DIM).transpose(0, 2, 1, 3).reshape(n_seq, IDX_HEADS * tp, IDX_DIM)
        wi = _pad_tokens(p3[:, :, npk - LANE + TAIL_WI:npk - LANE + TAIL_WI + IDX_HEADS], tp)
        wi = wi.transpose(0, 2, 1).reshape(n_seq, IDX_HEADS * tp, 1)
        knew = _pad_tokens(p3[:, :, a_off + aw:a_off + aw + kvw], tp).reshape(n_seq, tp * A_KV_HEADS, hd)
        vnew = _pad_tokens(v_a, tp).reshape(n_seq, tp * A_KV_HEADS, hd)
        n_pool = cache_k.shape[0]
        ao4, k_n8 = _dsa_sample(qa, qi, wi, knew, vnew, _pad_tokens(ki, tp), lw["q_norm_g"], lw["k_norm_g"], bias_s,
                                cache_k.reshape(n_pool, PAGE_SIZE * A_KV_HEADS, hd),
                                cache_v.reshape(n_pool, PAGE_SIZE * A_KV_HEADS, hd),
                                jnp.swapaxes(cache_i, 1, 2), page_table, t_new=T, hd=hd)
        k_n8 = k_n8.reshape(n_seq, tp, kvw)
        ao = ao4.reshape(n_seq, A_KV_HEADS, rep, tp, hd)[:, :, :, :T].transpose(0, 3, 1, 2, 4).reshape(n_seq * T, aw)
        k_n = k_n8[:, :T].reshape(n_seq, T, A_KV_HEADS, hd)
        v_n = v_a.reshape(n_seq, T, A_KV_HEADS, hd)

    x, h3 = _outproj(x, mo, ao, g2, lw["w_out"], sh3, sc3, lw["ffn2_norm_g"])
    x, ffn2_w = _ffn(x, g3, *lw["ffn2_w"], h=h3)
    state = (k_n, v_n, ki, c_new, n_new, m_new[:, 0, :M_HEADS], conv_new)
    return x.reshape(n_seq, T, d), state, dict(lw, ffn1_w=ffn1_w, ffn2_w=ffn2_w)


def kernel(x_prompt, x_sample, c_prompt, c_sample, cache_k, cache_v, cache_idx_k, page_table, state_C, state_n,
           state_m, state_conv, ffn1_norm_g, ffn1_w_gate, ffn1_w_up, ffn1_w_down, mix_norm_g, w_in, mlstm_conv_w,
           mlstm_gate_b, mlstm_out_g, q_norm_g, k_norm_g, t5_bias, w_out, ffn2_norm_g, ffn2_w_gate, ffn2_w_up,
           ffn2_w_down, w_ada, b_ada):
    depth = w_in.shape[0]
    bp, tp_len, d = x_prompt.shape
    bs, ts_len, _ = x_sample.shape
    dk, dv = state_C.shape[3], state_C.shape[4]
    mw = M_HEADS * dv
    hd = q_norm_g.shape[1]
    past_len = page_table.shape[1] * PAGE_SIZE
    bias_p, bias_s = _bias_tables(t5_bias, tp_len, past_len, ts_len)
    rep = A_HEADS // A_KV_HEADS
    bias_s = bias_s.reshape(A_HEADS * SUBLANE, A_KV_HEADS * (past_len + LANE))

    xp, xs = x_prompt, x_sample
    st_p, st_s = [], []
    for l in range(depth):
        lw = dict(ffn1_norm_g=ffn1_norm_g[l], ffn1_w=(ffn1_w_gate[l], ffn1_w_up[l], ffn1_w_down[l]),
                  mix_norm_g=mix_norm_g[l], w_in=_pack_w_in(w_in[l], mw, A_HEADS * hd, A_KV_HEADS * hd),
                  mlstm_conv_w=mlstm_conv_w[l], mlstm_gate_b=mlstm_gate_b[l], mlstm_out_g=mlstm_out_g[l],
                  q_norm_g=q_norm_g[l], k_norm_g=k_norm_g[l], w_out=w_out[l].astype(BF16),
                  ffn2_norm_g=ffn2_norm_g[l], ffn2_w=(ffn2_w_gate[l], ffn2_w_up[l], ffn2_w_down[l]))
        n_s = bs * ts_len
        mods = _ada(jnp.concatenate([jnp.repeat(c_sample, ts_len, axis=0), c_prompt], axis=0), w_ada[l], b_ada[l])
        per_seq = tp_len % ROW_TILE == 0
        mods_p = mods[n_s:].reshape(bp, 1, N_MOD * d) if per_seq else jnp.repeat(mods[n_s:], tp_len, axis=0)
        init_s = (state_C[l], state_n[l], state_m[l], state_conv[l])
        xs, ss, lw = _layer(xs, mods, lw, bias_p, bias_s, init_s,
                            (cache_k[l], cache_v[l], cache_idx_k[l], page_table))
        init_p = (jnp.zeros((bp, M_HEADS, dk, dv), F32), jnp.zeros((bp, M_HEADS, dk), F32),
                  jnp.zeros((bp, M_HEADS), F32), jnp.zeros((bp, CONV_W - 1, 2 * M_HEADS * dk), F32))
        xp, sp, _ = _layer(xp, mods_p, lw, bias_p, bias_s, init_p, None)
        st_p.append(sp)
        st_s.append(ss)
    stack = lambda sts, k: sts[0][k][None] if depth == 1 else jnp.stack([s[k] for s in sts])
    return (xp, xs) + tuple(stack(st_p, k) for k in range(7)) + tuple(stack(st_s, k) for k in range(7))
```

```python
import functools
import math

import numpy as np
import jax
import jax.numpy as jnp
from jax import lax
from jax.experimental import pallas as pl
from jax.experimental.pallas import tpu as pltpu

F32 = jnp.float32
BF16 = jnp.bfloat16

M_HEADS = 4
CONV_W = 4
M_CHUNK = 64
A_HEADS = 8
A_KV_HEADS = 2
IDX_HEADS = 8
IDX_DIM = 64
TOPK_MAX = 256
Q_BLOCK = 128
PAGE_SIZE = 128
N_BUCKETS = 32
MAX_DISTANCE = 128
N_MOD = 9
EPS = 1e-6
NEG = -1e30

LANE = 128
SUBLANE = 8
ROW_TILE = 512
VMEM_LIMIT = 56 * 1024 * 1024

TAIL_KI = 0
TAIL_WI = IDX_DIM
TAIL_IG = IDX_DIM + IDX_HEADS
TAIL_FG = TAIL_IG + M_HEADS

INT_MIN = -2 ** 31
LOG2E = math.log2(math.e)


def _params(sem, vmem=None, flags=None):
    return pltpu.CompilerParams(dimension_semantics=sem, vmem_limit_bytes=vmem, flags=flags)


def _dot(a, b):
    return jnp.dot(a, b, preferred_element_type=F32)


def _dot_nt(a, b):
    return lax.dot_general(a, b, (((1,), (1,)), ((), ())), preferred_element_type=F32)


def _dot_tn(a, b):
    return lax.dot_general(a, b, (((0,), (0,)), ((), ())), preferred_element_type=F32)


def _rms(x, g):
    return x * lax.rsqrt(jnp.mean(x * x, axis=-1, keepdims=True) + EPS) * g


def _row_tile(n):
    return ROW_TILE if n % ROW_TILE == 0 else n


REDUCE_CHAINS = 4


def _reduce_rows(x, op):
    n, c = x.shape
    groups = n // SUBLANE
    chains = REDUCE_CHAINS if groups % REDUCE_CHAINS == 0 else 1
    x3 = x.reshape(groups, SUBLANE, c)
    per = groups // chains
    parts = [op(x3[k * per:(k + 1) * per], axis=0) for k in range(chains)]
    while len(parts) > 1:
        parts = [op(jnp.stack(parts[k:k + 2]), axis=0) for k in range(0, len(parts), 2)]
    return op(parts[0], axis=0, keepdims=True)


def _ada_kernel(c_ref, w_ref, b_ref, o_ref):
    o_ref[...] = _dot(c_ref[...].astype(BF16), w_ref[...].astype(BF16)) + b_ref[...]


def _ada(c_all, w_ada, b_ada):
    r, d = c_all.shape
    n = w_ada.shape[1]
    tn = 1024 if n % 1024 == 0 else n
    return pl.pallas_call(
        _ada_kernel,
        grid=(n // tn,),
        in_specs=[pl.BlockSpec((r, d), lambda j: (0, 0)),
                  pl.BlockSpec((d, tn), lambda j: (0, j)),
                  pl.BlockSpec((1, tn), lambda j: (0, j))],
        out_specs=pl.BlockSpec((r, tn), lambda j: (0, j)),
        out_shape=jax.ShapeDtypeStruct((r, n), F32),
        compiler_params=_params(("arbitrary",), VMEM_LIMIT),
        name="ada",
    )(c_all, w_ada, b_ada.reshape(1, n))


def _adaln(x, g, sc, sh):
    return (_rms(x, g) * (1.0 + sc) + sh).astype(BF16)


def _ffn_kernel(x_ref, gt_ref, *rest, prenormed, emit):
    if prenormed:
        hin_ref, wg_ref, wu_ref, wd_ref, o_ref, *rest = rest
    else:
        sh_ref, sc_ref, g_ref, wg_ref, wu_ref, wd_ref, o_ref, *rest = rest
    bf16_out, rest = (rest[:3], rest[3:]) if emit else ((), rest)
    acc_ref = rest[-1]
    j = pl.program_id(1)

    @pl.when(j == 0)
    def _():
        if not prenormed:
            rest[0][...] = _adaln(x_ref[...], g_ref[...], sc_ref[...], sh_ref[...])
        acc_ref[...] = jnp.zeros_like(acc_ref)

    wg, wu, wd = wg_ref[...], wu_ref[...], wd_ref[...]
    if emit:
        wg, wu, wd = wg.astype(BF16), wu.astype(BF16), wd.astype(BF16)
        for ref, w in zip(bf16_out, (wg, wu, wd)):
            ref[...] = w
    h = hin_ref[...] if prenormed else rest[0][...]
    a = _dot(h, wg)
    u = _dot(h, wu)
    act = (a * jax.nn.sigmoid(a) * u).astype(BF16)
    acc_ref[...] += _dot(act, wd)

    @pl.when(j == pl.num_programs(1) - 1)
    def _():
        o_ref[...] = x_ref[...] + 0.5 * gt_ref[...] * acc_ref[...]


def _mod_spec(mod, m, tm, d):
    arr, k = mod
    if arr.ndim == 2:
        return pl.BlockSpec((tm, d), lambda i, j: (i, k))
    tiles_per_seq = (m // arr.shape[0]) // tm
    return pl.BlockSpec((None, 1, d), lambda i, j: (i // tiles_per_seq, 0, k))


def _ffn(x, gt, wg, wu, wd, *, norm=None, h=None):
    m, d = x.shape
    f = wg.shape[1]
    tm = _row_tile(m)
    emit = wg.dtype != BF16
    assert not emit or m == tm
    tf = (256 if emit else 512) if f % 512 == 0 else f
    row = pl.BlockSpec((tm, d), lambda i, j: (i, 0))
    w_specs = [pl.BlockSpec((d, tf), lambda i, j: (0, j)),
               pl.BlockSpec((d, tf), lambda i, j: (0, j)),
               pl.BlockSpec((tf, d), lambda i, j: (j, 0))]
    if h is None:
        sh, sc, g = norm
        pre_specs = [_mod_spec(sh, m, tm, d), _mod_spec(sc, m, tm, d), pl.BlockSpec((1, d), lambda i, j: (0, 0))]
        pre_args = [sh[0], sc[0], g.reshape(1, d)]
        scratch = [pltpu.VMEM((tm, d), BF16)]
    else:
        pre_specs, pre_args, scratch = [row], [h], []
    out_specs = [row]
    out_shape = [jax.ShapeDtypeStruct((m, d), F32)]
    if emit:
        out_specs += w_specs
        out_shape += [jax.ShapeDtypeStruct(w.shape, BF16) for w in (wg, wu, wd)]
    y, *wb = pl.pallas_call(
        functools.partial(_ffn_kernel, prenormed=h is not None, emit=emit),
        grid=(m // tm, f // tf),
        in_specs=[row, _mod_spec(gt, m, tm, d)] + pre_specs + w_specs,
        out_specs=out_specs,
        out_shape=out_shape,
        scratch_shapes=scratch + [pltpu.VMEM((tm, d), F32)],
        compiler_params=_params(("parallel", "arbitrary"), VMEM_LIMIT),
        name="ffn",
    )(x, gt[0], *pre_args, wg, wu, wd)
    return y, (tuple(wb) if emit else (wg, wu, wd))


def _proj_kernel(x_ref, sh_ref, sc_ref, g_ref, wm_ref, wt_ref, *rest, q_cols):
    h = _adaln(x_ref[...], g_ref[...], sc_ref[...], sh_ref[...])
    nm = wm_ref.shape[1]
    o_ref = rest[1] if q_cols else rest[0]
    o_ref[:, 0:nm] = _dot(h, wm_ref[...])
    p = _dot(h, wt_ref[...])
    o_ref[:, nm:] = p
    if not q_cols:
        return
    qg_ref, _, qn_ref, qib_ref = rest
    a0, aw, i0, iw, hd = q_cols
    for hh in range(aw // hd):
        qn = _rms(p[:, a0 - nm + hh * hd:a0 - nm + (hh + 1) * hd], qg_ref[...]) * (hd ** -0.5 * LOG2E)
        qn_ref[:, hh * hd:(hh + 1) * hd] = qn.astype(BF16)
    qib_ref[...] = p[:, i0 - nm:i0 - nm + iw].astype(BF16)


PROJ_ROW_TILE = 256


def _proj(x, sh, sc, g, w, prompt=None):
    m, d = x.shape
    wm, wt = w
    n = wm.shape[1] + wt.shape[1]
    tm = PROJ_ROW_TILE if m % PROJ_ROW_TILE == 0 else m
    row = lambda width: pl.BlockSpec((tm, width), lambda i, j: (i, 0))
    resident = lambda arr: pl.BlockSpec(arr.shape, lambda i, j: (0, 0), pipeline_mode=pl.Buffered(1))
    in_specs = [row(d), _mod_spec(sh, m, tm, d), _mod_spec(sc, m, tm, d), pl.BlockSpec((1, d), lambda i, j: (0, 0)),
                resident(wm), resident(wt)]
    args = [x, sh[0], sc[0], g.reshape(1, d), wm, wt]
    out_specs, out_shape = [row(n)], [jax.ShapeDtypeStruct((m, n), F32)]
    q_cols = None
    if prompt:
        qn_g, q_cols = prompt
        a0, aw, i0, iw, hd = q_cols
        assert a0 >= wm.shape[1]
        in_specs.append(pl.BlockSpec((1, hd), lambda i, j: (0, 0)))
        args.append(qn_g.reshape(1, hd))
        out_specs += [row(aw), row(iw)]
        out_shape += [jax.ShapeDtypeStruct((m, aw), BF16), jax.ShapeDtypeStruct((m, iw), BF16)]
    out = pl.pallas_call(
        functools.partial(_proj_kernel, q_cols=q_cols),
        grid=(m // tm, 1),
        in_specs=in_specs,
        out_specs=out_specs,
        out_shape=out_shape,
        compiler_params=_params(("parallel", "arbitrary"), VMEM_LIMIT),
        name="proj",
    )(*args)
    return out if prompt else (out[0], None, None)


def _pack_w_in(w_in, mw, aw, kvw):
    d = w_in.shape[0]
    o = np.cumsum([0, mw, mw, mw, mw, M_HEADS, M_HEADS, aw, kvw, kvw, IDX_HEADS * IDX_DIM, IDX_DIM, IDX_HEADS])
    seg = lambda k: w_in[:, o[k]:o[k + 1]]
    pad = (-o[-1]) % LANE
    tail = jnp.concatenate([seg(k) for k in (6, 7, 8, 9, 10, 11, 4, 5)] + [jnp.zeros((d, pad), w_in.dtype)], axis=1)
    return w_in[:, :o[4]].astype(BF16), tail.astype(BF16)


def _log_sigmoid(x):
    return jnp.minimum(x, 0.0) - jnp.log1p(jnp.exp(-jnp.abs(x)))


def _conv_silu(xbuf_ref, x_new, cw_ref):
    rows = x_new.shape[0]
    xbuf_ref[SUBLANE:SUBLANE + rows, :] = x_new
    base = SUBLANE - (CONV_W - 1)
    y = xbuf_ref[base:base + rows, :] * cw_ref[0:1, :]
    for j in range(1, CONV_W):
        y = y + xbuf_ref[base + j:base + j + rows, :] * cw_ref[j:j + 1, :]
    tail_rows = xbuf_ref[rows:rows + SUBLANE, :]
    xbuf_ref[0:SUBLANE, :] = tail_rows
    return y * jax.nn.sigmoid(y)


def _mlstm_kernel(qk_ref, v_ref, o_ref, tail_ref, prev_ref, cw_ref, gb_ref, og_ref, c0_ref, n0_ref, m0_ref,
                  mo_ref, cst_ref, nst_ref, mst_ref, xbuf_ref, *, L, valid, dk, dv, one_chunk):
    if one_chunk:
        xbuf_ref[:, 0:SUBLANE, :] = prev_ref[...]
        mst_ref[...] = m0_ref[...]
        state_in = (c0_ref, n0_ref, m0_ref)
    else:
        state_in = (cst_ref, nst_ref, mst_ref)

        @pl.when(pl.program_id(1) == 0)
        def _():
            xbuf_ref[:, 0:SUBLANE, :] = prev_ref[...]
            cst_ref[...] = c0_ref[...]
            nst_ref[...] = n0_ref[...]
            mst_ref[...] = m0_ref[...]

    for sb in range(qk_ref.shape[0]):
        _mlstm_one(qk_ref.at[sb], v_ref.at[sb], o_ref.at[sb], tail_ref.at[sb], cw_ref, gb_ref, og_ref, mo_ref.at[sb],
                   *(r.at[sb] for r in state_in), cst_ref.at[sb], nst_ref.at[sb], mst_ref.at[sb], xbuf_ref.at[sb],
                   L=L, valid=valid, dk=dk, dv=dv)


def _mlstm_one(qk_ref, v_ref, o_ref, tail_ref, cw_ref, gb_ref, og_ref, mo_ref, cin_ref, nin_ref, min_ref,
               cst_ref, nst_ref, mst_ref, xbuf_ref, *, L, valid, dk, dv):
    width = M_HEADS * dk
    qk = _conv_silu(xbuf_ref, qk_ref[...], cw_ref)

    gates = tail_ref[...] + gb_ref[...]
    row = lax.broadcasted_iota(jnp.int32, (L, LANE), 0)
    lf = _log_sigmoid(gates)
    ig = gates
    if valid < L:
        lf = jnp.where(row < valid, lf, 0.0)
        ig = jnp.where(row < valid, ig, -jnp.inf)
    bcum = lf
    s = 1
    while s < L:
        bcum = bcum + jnp.where(row >= s, pltpu.roll(bcum, s, axis=0), 0.0)
        s *= 2
    bmi_t = (pltpu.roll(ig, TAIL_FG - TAIL_IG, axis=1) - bcum).T

    ri = lax.broadcasted_iota(jnp.int32, (L, L), 0)
    ci = lax.broadcasted_iota(jnp.int32, (L, L), 1)
    causal = ri >= ci

    for h in range(M_HEADS):
        li, lfh = TAIL_IG + h, TAIL_FG + h
        a_col = bcum[:, lfh:lfh + 1]
        ig_col = ig[:, li:li + 1]
        b_row = bmi_t[lfh:lfh + 1, :]
        m0 = min_ref[:, h:h + 1]
        d_log = jnp.where(causal, a_col + b_row, -jnp.inf)
        s_log = a_col + m0
        m_col = jnp.maximum(s_log, jnp.max(d_log, axis=-1, keepdims=True))
        dw = jnp.exp(d_log - m_col)
        sw = jnp.exp(s_log - m_col)

        q = qk[:, h * dk:(h + 1) * dk]
        k = qk[:, width + h * dk:width + (h + 1) * dk] * (dk ** -0.5)
        v = v_ref[:, h * dv:(h + 1) * dv]
        qb, kb = q.astype(BF16), k.astype(BF16)
        c_prev = cin_ref[h]
        n_prev = nin_ref[h:h + 1, :]
        scores = _dot_nt(qb, kb) * dw
        num = _dot(scores.astype(BF16), v.astype(BF16)) + sw * _dot(qb, c_prev.astype(BF16))
        den = jnp.sum(scores, axis=-1, keepdims=True) + sw * jnp.sum(q * n_prev, axis=-1, keepdims=True)
        hh = num / jnp.maximum(jnp.abs(den), jnp.exp(-m_col))

        m_new = m_col[L - 1:L, :]
        a_last = a_col[L - 1:L, :]
        wl = jnp.exp(a_last - a_col + ig_col - m_new)
        decay = jnp.exp(a_last + m0 - m_new)
        cst_ref[h] = decay * c_prev + _dot_tn(kb, (wl * v).astype(BF16))
        nst_ref[h:h + 1, :] = decay * n_prev + jnp.sum(wl * k, axis=0, keepdims=True)
        mst_ref[:, h:h + 1] = m_new

        hn = hh * lax.rsqrt(jnp.mean(hh * hh, axis=-1, keepdims=True) + EPS) * og_ref[:, h * dv:(h + 1) * dv]
        gate = jax.nn.sigmoid(o_ref[:, h * dv:(h + 1) * dv])
        mo_ref[:, h * dv:(h + 1) * dv] = (hn * gate).astype(mo_ref.dtype)


def _mlstm(p3, prev8, conv_w, gate_row, out_g, c0, n0, m0, *, L, valid, seqs_per_step, out_dtype):
    n_seq, t_all, npk = p3.shape
    nc = t_all // L
    sb = seqs_per_step
    assert n_seq % sb == 0
    dk = c0.shape[2]
    dv = c0.shape[3]
    mw = M_HEADS * dv
    qkw = 2 * M_HEADS * dk
    assert qkw % mw == 0 and (2 * qkw) % mw == 0
    kern = functools.partial(_mlstm_kernel, L=L, valid=valid, dk=dk, dv=dv, one_chunk=nc == 1)
    seq = lambda shape: pl.BlockSpec((sb,) + shape, lambda b, c: (b,) + (0,) * len(shape))
    return pl.pallas_call(
        kern,
        grid=(n_seq // sb, nc),
        in_specs=[pl.BlockSpec((sb, L, qkw), lambda b, c: (b, c, 0)),
                  pl.BlockSpec((sb, L, mw), lambda b, c: (b, c, qkw // mw)),
                  pl.BlockSpec((sb, L, mw), lambda b, c: (b, c, qkw // mw + 1)),
                  pl.BlockSpec((sb, L, LANE), lambda b, c: (b, c, npk // LANE - 1)),
                  seq((SUBLANE, qkw)),
                  pl.BlockSpec((CONV_W, qkw), lambda b, c: (0, 0)),
                  pl.BlockSpec((1, LANE), lambda b, c: (0, 0)),
                  pl.BlockSpec((1, mw), lambda b, c: (0, 0)),
                  seq((M_HEADS, dk, dv)), seq((M_HEADS, dk)), seq((1, LANE))],
        out_specs=[pl.BlockSpec((sb, L, mw), lambda b, c: (b, c, 0)),
                   seq((M_HEADS, dk, dv)), seq((M_HEADS, dk)), seq((1, LANE))],
        out_shape=[jax.ShapeDtypeStruct((n_seq, t_all, mw), out_dtype),
                   jax.ShapeDtypeStruct(c0.shape, F32),
                   jax.ShapeDtypeStruct(n0.shape, F32),
                   jax.ShapeDtypeStruct(m0.shape, F32)],
        scratch_shapes=[pltpu.VMEM((sb, SUBLANE + L, qkw), F32)],
        compiler_params=_params(("parallel", "arbitrary"), VMEM_LIMIT),
        name="mlstm",
    )(p3, p3, p3, p3, prev8, conv_w, gate_row, out_g.reshape(1, mw), c0, n0, m0)


def _bucket_np(dist):
    me = N_BUCKETS // 2
    d = np.maximum(dist, 0)
    ratio = np.log(np.maximum(d, 1).astype(np.float64) / me) / math.log(MAX_DISTANCE / me)
    large = np.minimum(me + (ratio * (N_BUCKETS - me)).astype(np.int64), N_BUCKETS - 1)
    return np.where(d < me, d, large).astype(np.int32)


def _bias_kernel(t5_ref, bkp_ref, bks_ref, vs_ref, far_ref, op_ref, os_ref):
    rep = A_HEADS // A_KV_HEADS

    def lookup(bk, h):
        acc = jnp.zeros(bk.shape, F32)
        for b in range(N_BUCKETS):
            acc = jnp.where(bk == b, t5_ref[b, h], acc)
        return acc

    for h in range(A_HEADS):
        half = (h % 2) * Q_BLOCK
        op_ref[h // 2, :, half:half + Q_BLOCK] = (lookup(bkp_ref[...], h) - t5_ref[far_ref[0], h]) * LOG2E
        os_ref[h] = jnp.where(vs_ref[h // rep] > 0, lookup(bks_ref[...], h), NEG)


def _bias_tables(t5_bias, t_prompt, past, t_new):
    c = np.arange(2 * Q_BLOCK)[:, None]
    r = np.arange(Q_BLOCK)[None, :]
    bkp = _bucket_np(Q_BLOCK + r - c)
    far = _bucket_np(np.arange(Q_BLOCK + 1, max(t_prompt, past + t_new) + Q_BLOCK))
    assert (far == far[0]).all(), "bias must be constant beyond one query block"
    nk = past + LANE
    tq = np.arange(SUBLANE)[:, None]
    col = np.arange(2 * nk)[None, :]
    s_pos, s_grp = col // A_KV_HEADS, col % A_KV_HEADS
    bks = _bucket_np(past + tq - s_pos)
    vis = np.stack([(s_grp == g) & (s_pos - past <= tq) for g in range(A_KV_HEADS)]).astype(np.int32)
    return pl.pallas_call(
        _bias_kernel,
        in_specs=[pl.BlockSpec(memory_space=pltpu.SMEM),
                  pl.BlockSpec(memory_space=pltpu.VMEM),
                  pl.BlockSpec(memory_space=pltpu.VMEM),
                  pl.BlockSpec(memory_space=pltpu.VMEM),
                  pl.BlockSpec(memory_space=pltpu.SMEM)],
        out_specs=[pl.BlockSpec(memory_space=pltpu.VMEM), pl.BlockSpec(memory_space=pltpu.VMEM)],
        out_shape=[jax.ShapeDtypeStruct((A_HEADS // 2, 2 * Q_BLOCK, 2 * Q_BLOCK), F32),
                   jax.ShapeDtypeStruct((A_HEADS, SUBLANE, 2 * nk), F32)],
        name="t5_bias_tables",
    )(t5_bias, jnp.asarray(bkp), jnp.asarray(bks), jnp.asarray(vis), jnp.asarray(far[:1]))


def _sort_key(score):
    bits = lax.bitcast_convert_type(score, jnp.int32)
    return jnp.where(bits < 0, bits ^ jnp.int32(0x7FFFFFFF), bits)


def _kth_key(count_ge, shape, k_sel):
    def body(it, ans_u):
        cand_u = ans_u | lax.shift_left(jnp.int32(1), jnp.int32(31) - it)
        cnt = count_ge(cand_u ^ jnp.int32(INT_MIN))
        return jnp.where(cnt >= k_sel, cand_u, ans_u)

    ans_u = lax.fori_loop(0, 32, body, jnp.zeros(shape, jnp.int32))
    return ans_u ^ jnp.int32(INT_MIN)


def _triangle(below):
    r = lax.broadcasted_iota(jnp.int32, (LANE, LANE), 0)
    c = lax.broadcasted_iota(jnp.int32, (LANE, LANE), 1)
    return jnp.where((c < r) if below else (r < c), 1.0, 0.0).astype(BF16)


def _select_tiles(key_tiles, thr, k_sel, key_axis):
    count = lambda x: jnp.sum(x, axis=key_axis, keepdims=True)
    c_gt = jnp.zeros(thr.shape, F32)
    for kt in key_tiles:
        c_gt = c_gt + count(jnp.where(kt > thr, 1.0, 0.0))
    need = k_sel - c_gt
    tri = _triangle(below=(key_axis == 0))
    run = jnp.zeros(thr.shape, F32)
    out = []
    for kt in key_tiles:
        eq = jnp.where(kt == thr, 1.0, 0.0)
        eqb = eq.astype(BF16)
        before = (_dot(tri, eqb) if key_axis == 0 else _dot(eqb, tri)) + run
        out.append(jnp.where(kt > thr, 1.0, jnp.where(before < need, eq, 0.0)))
        run = run + count(eq)
    return out


FAR_STEP = 2


def _dsa_prompt_kernel(qn_ref, qi_ref, qtail_ref, k_ref, v_ref, ktail_ref, kg_ref, bias_ref,
                       ao_ref, kn_ref, vo_ref,
                       kb_ref, vt_ref, kib_ref, key_ref, negb_ref, *, T, k_sel, hd):
    i = pl.program_id(1)
    nb = T // Q_BLOCK
    rep = A_HEADS // A_KV_HEADS
    near_w = 2 * LANE

    @pl.when(i == 0)
    def _():
        kb_ref[0:LANE, :] = jnp.zeros((LANE, kb_ref.shape[1]), BF16)
        vt_ref[:, 0:LANE] = jnp.zeros((vt_ref.shape[0], LANE), BF16)
        kib_ref[0:LANE, :] = jnp.zeros((LANE, IDX_DIM), BF16)
        for g in range(A_KV_HEADS):
            kn = _rms(k_ref[:, g * hd:(g + 1) * hd], kg_ref[...])
            kn_ref[pl.ds(g, T, stride=A_KV_HEADS), :] = kn
            vo_ref[pl.ds(g, T, stride=A_KV_HEADS), :] = v_ref[:, g * hd:(g + 1) * hd]
            kb_ref[LANE:, g * hd:(g + 1) * hd] = kn.astype(BF16)
        vt_ref[:, LANE:] = v_ref[...].T.astype(BF16)
        kib_ref[LANE:, :] = ktail_ref[:, TAIL_KI:TAIL_KI + IDX_DIM].astype(BF16)

    wi_t = qtail_ref[...].T[TAIL_WI:TAIL_WI + IDX_HEADS, :] * (IDX_HEADS ** -0.5) * (IDX_DIM ** -0.5)
    qi_pairs = [jnp.concatenate([qi_ref[:, (2 * a + hh) * IDX_DIM:(2 * a + hh + 1) * IDX_DIM] for hh in range(2)],
                                axis=0) for a in range(IDX_HEADS // 2)]
    qn_pairs = [jnp.concatenate([qn_ref[:, (2 * a + hh) * hd:(2 * a + hh + 1) * hd] for hh in range(2)], axis=0)
                for a in range(A_HEADS // 2)]

    near0 = pl.multiple_of(i * LANE, LANE)
    ki_near = kib_ref[pl.ds(near0, near_w), :]
    k_near = kb_ref[pl.ds(near0, near_w), :]
    vt_near = vt_ref[:, pl.ds(near0, near_w)]
    kc = lax.broadcasted_iota(jnp.int32, (near_w, Q_BLOCK), 0)
    qr = lax.broadcasted_iota(jnp.int32, (near_w, Q_BLOCK), 1)
    near_ok = jnp.where(kc <= qr + LANE, 1.0, 0.0) * jnp.maximum(jnp.where(kc >= LANE, 1.0, 0.0),
                                                                  (i >= 1).astype(F32))

    def scores(ki):
        sc = jnp.zeros((ki.shape[0], Q_BLOCK), F32)
        for a in range(IDX_HEADS // 2):
            s2 = _dot_nt(ki, qi_pairs[a])
            for hh in range(2):
                j = 2 * a + hh
                sc = sc + wi_t[j:j + 1, :] * jnp.maximum(s2[:, hh * Q_BLOCK:(hh + 1) * Q_BLOCK], 0.0)
        return sc

    def body(wf):
        fw = wf * LANE
        w_all = fw + near_w
        if w_all <= k_sel:
            negb_ref[0:w_all, :] = jnp.where(near_ok > 0.0, 0.0, NEG)
            attend(wf)
            return
        key_ref[fw:w_all, :] = _sort_key(jnp.where(near_ok > 0.0, scores(ki_near) + 0.0, NEG))
        if wf:
            far_ok = lax.broadcasted_iota(jnp.int32, (fw, Q_BLOCK), 0) < (i - 1) * LANE
            key_ref[0:fw, :] = _sort_key(jnp.where(far_ok, scores(kib_ref[LANE:LANE + fw, :]) + 0.0, NEG))

        def count_ge(cand):
            return _reduce_rows(jnp.where(key_ref[0:w_all, :] >= cand, 1.0, 0.0), jnp.sum)

        thr = _kth_key(count_ge, (1, Q_BLOCK), float(k_sel))
        sel = _select_tiles([key_ref[t * LANE:(t + 1) * LANE, :] for t in range(wf + 2)], thr, float(k_sel), 0)
        for t in range(wf):
            ok = sel[t] * (t < i - 1).astype(F32)
            negb_ref[t * LANE:(t + 1) * LANE, :] = jnp.where(ok > 0.0, 0.0, NEG)
        for t in range(2):
            ok = sel[wf + t] * near_ok[t * LANE:(t + 1) * LANE, :]
            negb_ref[fw + t * LANE:fw + (t + 1) * LANE, :] = jnp.where(ok > 0.0, 0.0, NEG)
        attend(wf)

    def attend(wf):
        fw = wf * LANE
        w_all = fw + near_w
        for a in range(A_HEADS // 2):
            g = (2 * a) // rep
            cs = slice(g * hd, (g + 1) * hd)
            qn2 = qn_pairs[a]
            nb_near = negb_ref[fw:w_all, :]
            s_near = _dot_nt(k_near[:, cs], qn2) + bias_ref[a] + jnp.concatenate([nb_near, nb_near], axis=1)
            m = _reduce_rows(s_near, jnp.max)
            if wf:
                nb_far = negb_ref[0:fw, :]
                s_far = _dot_nt(kb_ref[LANE:LANE + fw, cs], qn2) + jnp.concatenate([nb_far, nb_far], axis=1)
                m = jnp.maximum(m, _reduce_rows(s_far, jnp.max))
            p_near = jnp.exp2(s_near - m)
            l = _reduce_rows(p_near, jnp.sum)
            acc = _dot(vt_near[cs, :], p_near.astype(BF16))
            if wf:
                p_far = jnp.exp2(s_far - m)
                l = l + _reduce_rows(p_far, jnp.sum)
                acc = acc + _dot(vt_ref[cs, LANE:LANE + fw], p_far.astype(BF16))
            out_t = acc / l
            for hh in range(2):
                h = 2 * a + hh
                ao_ref[:, h * hd:(h + 1) * hd] = out_t[:, hh * Q_BLOCK:(hh + 1) * Q_BLOCK].T.astype(ao_ref.dtype)

    variants = sorted({min(FAR_STEP * -(-x // FAR_STEP), nb) for x in range(max(nb - 1, 1))})
    wf_needed = jnp.minimum((jnp.maximum(i - 1, 0) + FAR_STEP - 1) // FAR_STEP * FAR_STEP, nb)
    for wf in variants:
        pl.when(wf_needed == wf)(functools.partial(body, wf))


def _dsa_prompt(p, qn, qib, kn_g, bias_p, *, n_seq, T, hd):
    rows = p.shape[0]
    nb = T // Q_BLOCK
    aw = A_HEADS * hd
    kvw = A_KV_HEADS * hd
    iw = IDX_HEADS * IDX_DIM
    k_sel = min(TOPK_MAX, T // 4)
    kcol = p.shape[1] - LANE - iw - 2 * kvw
    assert kcol % kvw == 0
    k_blk = kcol // kvw
    tail_blk = p.shape[1] // LANE - 1
    kern = functools.partial(_dsa_prompt_kernel, T=T, k_sel=k_sel, hd=hd)
    return pl.pallas_call(
        kern,
        grid=(n_seq, nb),
        in_specs=[pl.BlockSpec((Q_BLOCK, aw), lambda b, i: (b * nb + i, 0)),
                  pl.BlockSpec((Q_BLOCK, iw), lambda b, i: (b * nb + i, 0)),
                  pl.BlockSpec((Q_BLOCK, LANE), lambda b, i: (b * nb + i, tail_blk)),
                  pl.BlockSpec((T, kvw), lambda b, i: (b, k_blk)),
                  pl.BlockSpec((T, kvw), lambda b, i: (b, k_blk + 1)),
                  pl.BlockSpec((T, LANE), lambda b, i: (b, tail_blk)),
                  pl.BlockSpec((1, hd), lambda b, i: (0, 0)),
                  pl.BlockSpec((A_HEADS // 2, 2 * Q_BLOCK, 2 * Q_BLOCK), lambda b, i: (0, 0, 0))],
        out_specs=[pl.BlockSpec((Q_BLOCK, aw), lambda b, i: (b * nb + i, 0)),
                   pl.BlockSpec((A_KV_HEADS * T, hd), lambda b, i: (b, 0)),
                   pl.BlockSpec((A_KV_HEADS * T, hd), lambda b, i: (b, 0))],
        out_shape=[jax.ShapeDtypeStruct((rows, aw), BF16),
                   jax.ShapeDtypeStruct((A_KV_HEADS * rows, hd), F32),
                   jax.ShapeDtypeStruct((A_KV_HEADS * rows, hd), F32)],
        scratch_shapes=[pltpu.VMEM((T + LANE, kvw), BF16), pltpu.VMEM((kvw, T + LANE), BF16),
                        pltpu.VMEM((T + LANE, IDX_DIM), BF16),
                        pltpu.VMEM((T + 2 * LANE, Q_BLOCK), jnp.int32),
                        pltpu.VMEM((T + 2 * LANE, Q_BLOCK), F32)],
        compiler_params=_params(("parallel", "arbitrary"), VMEM_LIMIT),
        name="dsa_prompt",
    )(qn, qib, p, p, p, p, kn_g.reshape(1, hd), bias_p)


def _pad_rows(x, n):
    return jnp.concatenate([x, jnp.zeros((n - x.shape[0], x.shape[1]), x.dtype)], axis=0)


def _idx_sample_kernel(pt_ref, qi_ref, wi_ref, kinew_ref, *rest, n_pages, t_new):
    ip, sc_ref = rest[:n_pages], rest[n_pages]
    del pt_ref
    qi = qi_ref[...].astype(BF16)
    wi = wi_ref[...] * (IDX_HEADS ** -0.5) * (IDX_DIM ** -0.5)
    tq = lax.broadcasted_iota(jnp.int32, (SUBLANE, LANE), 0)
    lane = lax.broadcasted_iota(jnp.int32, (SUBLANE, LANE), 1)
    for t in range(n_pages + 1):
        r = _dot(qi, ip[t][...].astype(BF16)) if t < n_pages else _dot_nt(qi, _pad_rows(kinew_ref[...], LANE).astype(BF16))
        sc = jnp.zeros((SUBLANE, LANE), F32)
        for j in range(IDX_HEADS):
            sc = sc + wi[j * SUBLANE:(j + 1) * SUBLANE, :] * jnp.maximum(r[j * SUBLANE:(j + 1) * SUBLANE, :], 0.0)
        sc = sc + 0.0
        if t == n_pages:
            sc = jnp.where(lane <= tq, sc, jnp.where(lane < t_new, NEG, -jnp.inf))
        sc_ref[:, t * LANE:(t + 1) * LANE] = sc


def _select_kernel(sc_ref, sel_ref, key_ref, *, k_sel):
    rows, nk = sc_ref.shape
    key_ref[...] = _sort_key(sc_ref[...])

    def count_ge(cand):
        return jnp.sum(jnp.where(key_ref[...] >= cand, 1.0, 0.0), axis=1, keepdims=True)

    thr = _kth_key(count_ge, (rows, 1), float(k_sel))
    sel = _select_tiles([key_ref[:, t * LANE:(t + 1) * LANE] for t in range(nk // LANE)], thr, float(k_sel), 1)
    for t in range(nk // LANE):
        sel_ref[:, t * LANE:(t + 1) * LANE] = sel[t]


def _attn_sample_kernel(pt_ref, qa_ref, sel_ref, knew_ref, vnew_ref, qg_ref, kg_ref, bias_ref, *rest,
                        n_pages, hd):
    kp, vp = rest[:n_pages], rest[n_pages:2 * n_pages]
    ao_ref, kn_ref = rest[2 * n_pages:]
    del pt_ref
    nt = n_pages + 1
    pw = A_KV_HEADS * PAGE_SIZE
    scale = hd ** -0.5
    rows_q = qa_ref.shape[0]

    kn_new = _rms(knew_ref[...], kg_ref[...])
    kn_ref[...] = kn_new
    k_new = _pad_rows(kn_new, pw).astype(BF16)
    v_new = _pad_rows(vnew_ref[...], pw).astype(BF16)
    qn = _rms(qa_ref[...], qg_ref[...]).astype(BF16)

    tok = lax.broadcasted_iota(jnp.int32, (PAGE_SIZE, pw), 0)
    col = lax.broadcasted_iota(jnp.int32, (PAGE_SIZE, pw), 1)
    spread = jnp.where(col // A_KV_HEADS == tok, 1.0, 0.0).astype(BF16)
    sel_rows = jnp.concatenate([sel_ref[:, t * LANE:(t + 1) * LANE] for t in range(nt)], axis=0)
    negb = (_dot(sel_rows.astype(BF16), spread) - 1.0) * (-NEG)

    logits = []
    for t in range(nt):
        kt = kp[t][...].astype(BF16) if t < n_pages else k_new
        nb = jnp.concatenate([negb[t * SUBLANE:(t + 1) * SUBLANE, :]] * (rows_q // SUBLANE), axis=0)
        logits.append(_dot_nt(qn, kt) * scale + bias_ref[:, t * pw:(t + 1) * pw] + nb)
    mx = logits[0]
    for t in range(1, nt):
        mx = jnp.maximum(mx, logits[t])
    m = jnp.max(mx, axis=-1, keepdims=True)
    psum = jnp.zeros((rows_q, pw), F32)
    acc = jnp.zeros((rows_q, hd), F32)
    for t in range(nt):
        pr = jnp.exp(logits[t] - m)
        psum = psum + pr
        acc = acc + _dot(pr.astype(BF16), vp[t][...].astype(BF16) if t < n_pages else v_new)
    ao_ref[...] = acc / jnp.sum(psum, axis=-1, keepdims=True)


def _dsa_sample(qa, qi, wi, knew, vnew, kinew, qn_g, kn_g, bias_s, cache_k, cache_v, cache_i, page_table, *, t_new, hd):
    n_seq, n_pages = page_table.shape
    nk = n_pages * PAGE_SIZE + LANE
    k_sel = min(TOPK_MAX, (n_pages * PAGE_SIZE + t_new) // 4)
    pt = page_table.reshape(-1)

    def page_spec(arr, t):
        return pl.BlockSpec((None,) + arr.shape[1:], lambda b, pt: (pt[b * n_pages + t], 0, 0))

    seq_spec = lambda arr: pl.BlockSpec((None,) + arr.shape[1:], lambda b, pt: (b, 0, 0))
    const_spec = lambda arr: pl.BlockSpec(arr.shape, lambda b, pt: (0, 0))

    scores = pl.pallas_call(
        functools.partial(_idx_sample_kernel, n_pages=n_pages, t_new=t_new),
        grid_spec=pltpu.PrefetchScalarGridSpec(
            num_scalar_prefetch=1, grid=(n_seq,),
            in_specs=[seq_spec(qi), seq_spec(wi), seq_spec(kinew)] + [page_spec(cache_i, t) for t in range(n_pages)],
            out_specs=pl.BlockSpec((SUBLANE, nk), lambda b, pt: (b, 0))),
        out_shape=jax.ShapeDtypeStruct((n_seq * SUBLANE, nk), F32),
        compiler_params=_params(("arbitrary",), VMEM_LIMIT),
        name="idx_sample",
    )(pt, qi, wi, kinew, *([cache_i] * n_pages))

    rows = scores.shape[0]
    tr = LANE if rows % LANE == 0 else rows
    sel = pl.pallas_call(
        functools.partial(_select_kernel, k_sel=k_sel),
        grid=(rows // tr,),
        in_specs=[pl.BlockSpec((tr, nk), lambda r: (r, 0))],
        out_specs=pl.BlockSpec((tr, nk), lambda r: (r, 0)),
        out_shape=jax.ShapeDtypeStruct((rows, nk), F32),
        scratch_shapes=[pltpu.VMEM((tr, nk), jnp.int32)],
        compiler_params=_params(("parallel",), VMEM_LIMIT),
        name="select_sample",
    )(scores)

    qg, kg = qn_g.reshape(1, hd), kn_g.reshape(1, hd)
    return pl.pallas_call(
        functools.partial(_attn_sample_kernel, n_pages=n_pages, hd=hd),
        grid_spec=pltpu.PrefetchScalarGridSpec(
            num_scalar_prefetch=1, grid=(n_seq,),
            in_specs=[seq_spec(qa), pl.BlockSpec((SUBLANE, nk), lambda b, pt: (b, 0)), seq_spec(knew), seq_spec(vnew),
                      const_spec(qg), const_spec(kg), const_spec(bias_s)]
                     + [page_spec(cache_k, t) for t in range(n_pages)]
                     + [page_spec(cache_v, t) for t in range(n_pages)],
            out_specs=[seq_spec(qa), seq_spec(knew)]),
        out_shape=[jax.ShapeDtypeStruct(qa.shape, F32), jax.ShapeDtypeStruct(knew.shape, F32)],
        compiler_params=_params(("arbitrary",), VMEM_LIMIT),
        name="attn_sample",
    )(pt, qa, sel, knew, vnew, qg, kg, bias_s, *([cache_k] * n_pages), *([cache_v] * n_pages))


def _outproj_kernel(x_ref, mo_ref, ao_ref, gt_ref, sh_ref, sc_ref, g_ref, wm_ref, wa_ref, o_ref, h_ref):
    y = _dot(mo_ref[...].astype(BF16), wm_ref[...]) + _dot(ao_ref[...].astype(BF16), wa_ref[...])
    x = x_ref[...] + gt_ref[...] * y
    o_ref[...] = x
    h_ref[...] = _adaln(x, g_ref[...], sc_ref[...], sh_ref[...])


def _outproj(x, mo, ao, gt, w_out, sh, sc, g):
    m, d = x.shape
    mw, aw = mo.shape[1], ao.shape[1]
    tm = _row_tile(m)
    row = lambda w: pl.BlockSpec((tm, w), lambda i, j: (i, 0))
    return pl.pallas_call(
        _outproj_kernel,
        grid=(m // tm, 1),
        in_specs=[row(d), row(mw), row(aw), _mod_spec(gt, m, tm, d), _mod_spec(sh, m, tm, d), _mod_spec(sc, m, tm, d),
                  pl.BlockSpec((1, d), lambda i, j: (0, 0)),
                  pl.BlockSpec((mw, d), lambda i, j: (0, 0)),
                  pl.BlockSpec((aw, d), lambda i, j: (0, 0))],
        out_specs=[row(d), row(d)],
        out_shape=[jax.ShapeDtypeStruct((m, d), F32), jax.ShapeDtypeStruct((m, d), BF16)],
        compiler_params=_params(("parallel", "arbitrary"), VMEM_LIMIT),
        name="outproj",
    )(x, mo, ao, gt[0], sh[0], sc[0], g.reshape(1, d), w_out[:mw], w_out[mw:])


def _pad_tokens(a, t_pad):
    pad = [(0, 0)] * a.ndim
    pad[1] = (0, t_pad - a.shape[1])
    return jnp.pad(a, pad)


def _layer(x3, mods, lw, bias_p, bias_s, rec_state, past):
    n_seq, T, d = x3.shape
    x = x3.reshape(n_seq * T, d)
    dk, dv = rec_state[0].shape[2], rec_state[0].shape[3]
    mw = M_HEADS * dv
    hd = lw["q_norm_g"].shape[0]
    aw, kvw = A_HEADS * hd, A_KV_HEADS * hd
    iw = IDX_HEADS * IDX_DIM
    sh1, sc1, g1, sh2, sc2, g2, sh3, sc3, g3 = [(mods, k) for k in range(N_MOD)]

    x, ffn1_w = _ffn(x, g1, *lw["ffn1_w"], norm=(sh1, sc1, lw["ffn1_norm_g"]))
    qkw = 2 * M_HEADS * dk
    a_off = qkw + 2 * mw
    c0, n0, m0, conv_buf = rec_state
    prev8 = jnp.zeros((n_seq, SUBLANE, qkw), F32).at[:, SUBLANE - (CONV_W - 1):, :].set(conv_buf)
    prompt = None
    if past is None:
        prompt = (lw["q_norm_g"], (a_off, aw, a_off + aw + 2 * kvw, iw, hd))
    p, qn, qib = _proj(x, sh2, sc2, lw["mix_norm_g"], lw["w_in"], prompt)
    npk = p.shape[1]
    gate_row = jnp.zeros((1, LANE), F32).at[0, TAIL_IG:TAIL_IG + 2 * M_HEADS].set(lw["mlstm_gate_b"])
    m0_pad = jnp.zeros((n_seq, 1, LANE), F32).at[:, 0, :M_HEADS].set(m0)
    p3 = p.reshape(n_seq, T, npk)
    v_a = p3[:, :, a_off + aw + kvw:a_off + aw + 2 * kvw]
    ki = p3[:, :, npk - LANE + TAIL_KI:npk - LANE + TAIL_KI + IDX_DIM]
    conv_new = p3[:, T - (CONV_W - 1):, :qkw]

    if past is None:
        L = M_CHUNK if T % M_CHUNK == 0 else T
        mo, c_new, n_new, m_new = _mlstm(p3, prev8, lw["mlstm_conv_w"], gate_row, lw["mlstm_out_g"], c0, n0, m0_pad,
                                         L=L, valid=L, seqs_per_step=2 if n_seq % 2 == 0 else 1, out_dtype=BF16)
        mo = mo.reshape(n_seq * T, mw)
        ao, k_n, v_n = _dsa_prompt(p, qn, qib, lw["k_norm_g"], bias_p, n_seq=n_seq, T=T, hd=hd)
        k_n = k_n.reshape(n_seq, T, A_KV_HEADS, hd)
        v_n = v_n.reshape(n_seq, T, A_KV_HEADS, hd)
    else:
        tp = SUBLANE
        assert T <= tp
        mo, c_new, n_new, m_new = _mlstm(_pad_tokens(p3, tp), prev8, lw["mlstm_conv_w"], gate_row, lw["mlstm_out_g"],
                                         c0, n0, m0_pad, L=tp, valid=T,
                                         seqs_per_step=SUBLANE if n_seq % SUBLANE == 0 else 1, out_dtype=F32)
        mo = mo[:, :T].reshape(n_seq * T, mw)
        cache_k, cache_v, cache_i, page_table = past
        rep = A_HEADS // A_KV_HEADS
        qa = _pad_tokens(p3[:, :, a_off:a_off + aw], tp).reshape(n_seq, tp, A_KV_HEADS, rep, hd)
        qa = qa.transpose(0, 2, 3, 1, 4).reshape(n_seq, A_HEADS * tp, hd)
        qi = _pad_tokens(p3[:, :, a_off + aw + 2 * kvw:a_off + aw + 2 * kvw + iw], tp)
        qi = qi.reshape(n_seq, tp, IDX_HEADS, IDX_DIM).transpose(0, 2, 1, 3).reshape(n_seq, IDX_HEADS * tp, IDX_DIM)
        wi = _pad_tokens(p3[:, :, npk - LANE + TAIL_WI:npk - LANE + TAIL_WI + IDX_HEADS], tp)
        wi = wi.transpose(0, 2, 1).reshape(n_seq, IDX_HEADS * tp, 1)
        knew = _pad_tokens(p3[:, :, a_off + aw:a_off + aw + kvw], tp).reshape(n_seq, tp * A_KV_HEADS, hd)
        vnew = _pad_tokens(v_a, tp).reshape(n_seq, tp * A_KV_HEADS, hd)
        n_pool = cache_k.shape[0]
        ao4, k_n8 = _dsa_sample(qa, qi, wi, knew, vnew, _pad_tokens(ki, tp), lw["q_norm_g"], lw["k_norm_g"], bias_s,
                                cache_k.reshape(n_pool, PAGE_SIZE * A_KV_HEADS, hd),
                                cache_v.reshape(n_pool, PAGE_SIZE * A_KV_HEADS, hd),
                                jnp.swapaxes(cache_i, 1, 2), page_table, t_new=T, hd=hd)
        k_n8 = k_n8.reshape(n_seq, tp, kvw)
        ao = ao4.reshape(n_seq, A_KV_HEADS, rep, tp, hd)[:, :, :, :T].transpose(0, 3, 1, 2, 4).reshape(n_seq * T, aw)
        k_n = k_n8[:, :T].reshape(n_seq, T, A_KV_HEADS, hd)
        v_n = v_a.reshape(n_seq, T, A_KV_HEADS, hd)

    x, h3 = _outproj(x, mo, ao, g2, lw["w_out"], sh3, sc3, lw["ffn2_norm_g"])
    x, ffn2_w = _ffn(x, g3, *lw["ffn2_w"], h=h3)
    state = (k_n, v_n, ki, c_new, n_new, m_new[:, 0, :M_HEADS], conv_new)
    return x.reshape(n_seq, T, d), state, dict(lw, ffn1_w=ffn1_w, ffn2_w=ffn2_w)


def kernel(x_prompt, x_sample, c_prompt, c_sample, cache_k, cache_v, cache_idx_k, page_table, state_C, state_n,
           state_m, state_conv, ffn1_norm_g, ffn1_w_gate, ffn1_w_up, ffn1_w_down, mix_norm_g, w_in, mlstm_conv_w,
           mlstm_gate_b, mlstm_out_g, q_norm_g, k_norm_g, t5_bias, w_out, ffn2_norm_g, ffn2_w_gate, ffn2_w_up,
           ffn2_w_down, w_ada, b_ada):
    depth = w_in.shape[0]
    bp, tp_len, d = x_prompt.shape
    bs, ts_len, _ = x_sample.shape
    dk, dv = state_C.shape[3], state_C.shape[4]
    mw = M_HEADS * dv
    hd = q_norm_g.shape[1]
    past_len = page_table.shape[1] * PAGE_SIZE
    bias_p, bias_s = _bias_tables(t5_bias, tp_len, past_len, ts_len)
    bias_s = bias_s.reshape(A_HEADS * SUBLANE, A_KV_HEADS * (past_len + LANE))

    xp, xs = x_prompt, x_sample
    st_p, st_s = [], []
    for l in range(depth):
        lw = dict(ffn1_norm_g=ffn1_norm_g[l], ffn1_w=(ffn1_w_gate[l], ffn1_w_up[l], ffn1_w_down[l]),
                  mix_norm_g=mix_norm_g[l], w_in=_pack_w_in(w_in[l], mw, A_HEADS * hd, A_KV_HEADS * hd),
                  mlstm_conv_w=mlstm_conv_w[l], mlstm_gate_b=mlstm_gate_b[l], mlstm_out_g=mlstm_out_g[l],
                  q_norm_g=q_norm_g[l], k_norm_g=k_norm_g[l], w_out=w_out[l].astype(BF16),
                  ffn2_norm_g=ffn2_norm_g[l], ffn2_w=(ffn2_w_gate[l], ffn2_w_up[l], ffn2_w_down[l]))
        n_s = bs * ts_len
        mods = _ada(jnp.concatenate([jnp.repeat(c_sample, ts_len, axis=0), c_prompt], axis=0), w_ada[l], b_ada[l])
        per_seq = tp_len % ROW_TILE == 0
        mods_p = mods[n_s:].reshape(bp, 1, N_MOD * d) if per_seq else jnp.repeat(mods[n_s:], tp_len, axis=0)
        init_s = (state_C[l], state_n[l], state_m[l], state_conv[l])
        xs, ss, lw = _layer(xs, mods, lw, bias_p, bias_s, init_s,
                            (cache_k[l], cache_v[l], cache_idx_k[l], page_table))
        init_p = (jnp.zeros((bp, M_HEADS, dk, dv), F32), jnp.zeros((bp, M_HEADS, dk), F32),
                  jnp.zeros((bp, M_HEADS), F32), jnp.zeros((bp, CONV_W - 1, 2 * M_HEADS * dk), F32))
        xp, sp, _ = _layer(xp, mods_p, lw, bias_p, bias_s, init_p, None)
        st_p.append(sp)
        st_s.append(ss)
    stack = lambda sts, k: sts[0][k][None] if depth == 1 else jnp.stack([s[k] for s in sts])
    return (xp, xs) + tuple(stack(st_p, k) for k in range(7)) + tuple(stack(st_s, k) for k in range(7))
```

```python
import functools
import math

import numpy as np
import jax
import jax.numpy as jnp
from jax import lax
from jax.experimental import pallas as pl
from jax.experimental.pallas import tpu as pltpu

F32 = jnp.float32
BF16 = jnp.bfloat16

M_HEADS = 4
CONV_W = 4
M_CHUNK = 64
MLSTM_CHUNK = 4 * M_CHUNK
A_HEADS = 8
A_KV_HEADS = 2
IDX_HEADS = 8
IDX_DIM = 64
TOPK_MAX = 256
Q_BLOCK = 128
PAGE_SIZE = 128
N_BUCKETS = 32
MAX_DISTANCE = 128
N_MOD = 9
EPS = 1e-6
NEG = -1e30

LANE = 128
SUBLANE = 8
ROW_TILE = 512
VMEM_LIMIT = 56 * 1024 * 1024

TAIL_KI = 0
TAIL_WI = IDX_DIM
TAIL_IG = IDX_DIM + IDX_HEADS
TAIL_FG = TAIL_IG + M_HEADS

INT_MIN = -2 ** 31
LOG2E = math.log2(math.e)


def _params(sem, vmem=None, flags=None):
    return pltpu.CompilerParams(dimension_semantics=sem, vmem_limit_bytes=vmem, flags=flags)


def _dot(a, b):
    return jnp.dot(a, b, preferred_element_type=F32)


def _dot_nt(a, b):
    return lax.dot_general(a, b, (((1,), (1,)), ((), ())), preferred_element_type=F32)


def _dot_tn(a, b):
    return lax.dot_general(a, b, (((0,), (0,)), ((), ())), preferred_element_type=F32)


def _rms(x, g):
    return x * lax.rsqrt(jnp.mean(x * x, axis=-1, keepdims=True) + EPS) * g


def _row_tile(n):
    return ROW_TILE if n % ROW_TILE == 0 else n


REDUCE_CHAINS = 4


def _reduce_rows(x, op):
    n, c = x.shape
    groups = n // SUBLANE
    chains = REDUCE_CHAINS if groups % REDUCE_CHAINS == 0 else 1
    x3 = x.reshape(groups, SUBLANE, c)
    per = groups // chains
    parts = [op(x3[k * per:(k + 1) * per], axis=0) for k in range(chains)]
    while len(parts) > 1:
        parts = [op(jnp.stack(parts[k:k + 2]), axis=0) for k in range(0, len(parts), 2)]
    return op(parts[0], axis=0, keepdims=True)


def _ada_kernel(c_ref, w_ref, b_ref, o_ref):
    o_ref[...] = _dot(c_ref[...].astype(BF16), w_ref[...].astype(BF16)) + b_ref[...]


def _ada(c_all, w_ada, b_ada):
    r, d = c_all.shape
    n = w_ada.shape[1]
    tn = 1024 if n % 1024 == 0 else n
    return pl.pallas_call(
        _ada_kernel,
        grid=(n // tn,),
        in_specs=[pl.BlockSpec((r, d), lambda j: (0, 0)),
                  pl.BlockSpec((d, tn), lambda j: (0, j)),
                  pl.BlockSpec((1, tn), lambda j: (0, j))],
        out_specs=pl.BlockSpec((r, tn), lambda j: (0, j)),
        out_shape=jax.ShapeDtypeStruct((r, n), F32),
        compiler_params=_params(("arbitrary",), VMEM_LIMIT),
        name="ada",
    )(c_all, w_ada, b_ada.reshape(1, n))


def _adaln(x, g, sc, sh):
    return (_rms(x, g) * (1.0 + sc) + sh).astype(BF16)


def _ffn_kernel(x_ref, gt_ref, *rest, prenormed, emit):
    if prenormed:
        hin_ref, wg_ref, wu_ref, wd_ref, o_ref, *rest = rest
    else:
        sh_ref, sc_ref, g_ref, wg_ref, wu_ref, wd_ref, o_ref, *rest = rest
    bf16_out, rest = (rest[:3], rest[3:]) if emit else ((), rest)
    acc_ref = rest[-1]
    j = pl.program_id(1)

    @pl.when(j == 0)
    def _():
        if not prenormed:
            rest[0][...] = _adaln(x_ref[...], g_ref[...], sc_ref[...], sh_ref[...])
        acc_ref[...] = jnp.zeros_like(acc_ref)

    wg, wu, wd = wg_ref[...], wu_ref[...], wd_ref[...]
    if emit:
        wg, wu, wd = wg.astype(BF16), wu.astype(BF16), wd.astype(BF16)
        for ref, w in zip(bf16_out, (wg, wu, wd)):
            ref[...] = w
    h = hin_ref[...] if prenormed else rest[0][...]
    a = _dot(h, wg)
    u = _dot(h, wu)
    act = (a * jax.nn.sigmoid(a) * u).astype(BF16)
    acc_ref[...] += _dot(act, wd)

    @pl.when(j == pl.num_programs(1) - 1)
    def _():
        o_ref[...] = x_ref[...] + 0.5 * gt_ref[...] * acc_ref[...]


def _mod_spec(mod, m, tm, d):
    arr, k = mod
    if arr.ndim == 2:
        return pl.BlockSpec((tm, d), lambda i, j: (i, k))
    tiles_per_seq = (m // arr.shape[0]) // tm
    return pl.BlockSpec((None, 1, d), lambda i, j: (i // tiles_per_seq, 0, k))


def _ffn(x, gt, wg, wu, wd, *, norm=None, h=None):
    m, d = x.shape
    f = wg.shape[1]
    tm = _row_tile(m)
    emit = wg.dtype != BF16
    assert not emit or m == tm
    tf = (256 if emit else 512) if f % 512 == 0 else f
    row = pl.BlockSpec((tm, d), lambda i, j: (i, 0))
    w_specs = [pl.BlockSpec((d, tf), lambda i, j: (0, j)),
               pl.BlockSpec((d, tf), lambda i, j: (0, j)),
               pl.BlockSpec((tf, d), lambda i, j: (j, 0))]
    if h is None:
        sh, sc, g = norm
        pre_specs = [_mod_spec(sh, m, tm, d), _mod_spec(sc, m, tm, d), pl.BlockSpec((1, d), lambda i, j: (0, 0))]
        pre_args = [sh[0], sc[0], g.reshape(1, d)]
        scratch = [pltpu.VMEM((tm, d), BF16)]
    else:
        pre_specs, pre_args, scratch = [row], [h], []
    out_specs = [row]
    out_shape = [jax.ShapeDtypeStruct((m, d), F32)]
    if emit:
        out_specs += w_specs
        out_shape += [jax.ShapeDtypeStruct(w.shape, BF16) for w in (wg, wu, wd)]
    y, *wb = pl.pallas_call(
        functools.partial(_ffn_kernel, prenormed=h is not None, emit=emit),
        grid=(m // tm, f // tf),
        in_specs=[row, _mod_spec(gt, m, tm, d)] + pre_specs + w_specs,
        out_specs=out_specs,
        out_shape=out_shape,
        scratch_shapes=scratch + [pltpu.VMEM((tm, d), F32)],
        compiler_params=_params(("parallel", "arbitrary"), VMEM_LIMIT),
        name="ffn",
    )(x, gt[0], *pre_args, wg, wu, wd)
    return y, (tuple(wb) if emit else (wg, wu, wd))


def _proj_kernel(x_ref, sh_ref, sc_ref, g_ref, wm_ref, wt_ref, *rest, q_cols):
    h = _adaln(x_ref[...], g_ref[...], sc_ref[...], sh_ref[...])
    nm = wm_ref.shape[1]
    o_ref = rest[1] if q_cols else rest[0]
    o_ref[:, 0:nm] = _dot(h, wm_ref[...])
    p = _dot(h, wt_ref[...])
    o_ref[:, nm:] = p
    if not q_cols:
        return
    qg_ref, _, qn_ref, qib_ref = rest
    a0, aw, i0, iw, hd = q_cols
    for hh in range(aw // hd):
        qn = _rms(p[:, a0 - nm + hh * hd:a0 - nm + (hh + 1) * hd], qg_ref[...]) * (hd ** -0.5 * LOG2E)
        qn_ref[:, hh * hd:(hh + 1) * hd] = qn.astype(BF16)
    qib_ref[...] = p[:, i0 - nm:i0 - nm + iw].astype(BF16)


PROJ_ROW_TILE = 256


def _proj(x, sh, sc, g, w, prompt=None):
    m, d = x.shape
    wm, wt = w
    n = wm.shape[1] + wt.shape[1]
    tm = PROJ_ROW_TILE if m % PROJ_ROW_TILE == 0 else m
    row = lambda width: pl.BlockSpec((tm, width), lambda i, j: (i, 0))
    resident = lambda arr: pl.BlockSpec(arr.shape, lambda i, j: (0, 0), pipeline_mode=pl.Buffered(1))
    in_specs = [row(d), _mod_spec(sh, m, tm, d), _mod_spec(sc, m, tm, d), pl.BlockSpec((1, d), lambda i, j: (0, 0)),
                resident(wm), resident(wt)]
    args = [x, sh[0], sc[0], g.reshape(1, d), wm, wt]
    out_specs, out_shape = [row(n)], [jax.ShapeDtypeStruct((m, n), F32)]
    q_cols = None
    if prompt:
        qn_g, q_cols = prompt
        a0, aw, i0, iw, hd = q_cols
        assert a0 >= wm.shape[1]
        in_specs.append(pl.BlockSpec((1, hd), lambda i, j: (0, 0)))
        args.append(qn_g.reshape(1, hd))
        out_specs += [row(aw), row(iw)]
        out_shape += [jax.ShapeDtypeStruct((m, aw), BF16), jax.ShapeDtypeStruct((m, iw), BF16)]
    out = pl.pallas_call(
        functools.partial(_proj_kernel, q_cols=q_cols),
        grid=(m // tm, 1),
        in_specs=in_specs,
        out_specs=out_specs,
        out_shape=out_shape,
        compiler_params=_params(("parallel", "arbitrary"), VMEM_LIMIT),
        name="proj",
    )(*args)
    return out if prompt else (out[0], None, None)


def _pack_w_in(w_in, mw, aw, kvw):
    d = w_in.shape[0]
    o = np.cumsum([0, mw, mw, mw, mw, M_HEADS, M_HEADS, aw, kvw, kvw, IDX_HEADS * IDX_DIM, IDX_DIM, IDX_HEADS])
    seg = lambda k: w_in[:, o[k]:o[k + 1]]
    pad = (-o[-1]) % LANE
    tail = jnp.concatenate([seg(k) for k in (6, 7, 8, 9, 10, 11, 4, 5)] + [jnp.zeros((d, pad), w_in.dtype)], axis=1)
    return w_in[:, :o[4]].astype(BF16), tail.astype(BF16)


def _log_sigmoid(x):
    return jnp.minimum(x, 0.0) - jnp.log1p(jnp.exp(-jnp.abs(x)))


def _conv_silu(xbuf_ref, x_new, cw_ref):
    rows = x_new.shape[0]
    xbuf_ref[SUBLANE:SUBLANE + rows, :] = x_new
    base = SUBLANE - (CONV_W - 1)
    y = xbuf_ref[base:base + rows, :] * cw_ref[0:1, :]
    for j in range(1, CONV_W):
        y = y + xbuf_ref[base + j:base + j + rows, :] * cw_ref[j:j + 1, :]
    tail_rows = xbuf_ref[rows:rows + SUBLANE, :]
    xbuf_ref[0:SUBLANE, :] = tail_rows
    return y * jax.nn.sigmoid(y)


def _mlstm_kernel(qk_ref, v_ref, o_ref, tail_ref, prev_ref, cw_ref, gb_ref, og_ref, c0_ref, n0_ref, m0_ref,
                  mo_ref, cst_ref, nst_ref, mst_ref, xbuf_ref, *, L, valid, dk, dv, one_chunk):
    if one_chunk:
        xbuf_ref[:, 0:SUBLANE, :] = prev_ref[...]
        mst_ref[...] = m0_ref[...]
        state_in = (c0_ref, n0_ref, m0_ref)
    else:
        state_in = (cst_ref, nst_ref, mst_ref)

        @pl.when(pl.program_id(1) == 0)
        def _():
            xbuf_ref[:, 0:SUBLANE, :] = prev_ref[...]
            cst_ref[...] = c0_ref[...]
            nst_ref[...] = n0_ref[...]
            mst_ref[...] = m0_ref[...]

    for sb in range(qk_ref.shape[0]):
        _mlstm_one(qk_ref.at[sb], v_ref.at[sb], o_ref.at[sb], tail_ref.at[sb], cw_ref, gb_ref, og_ref, mo_ref.at[sb],
                   *(r.at[sb] for r in state_in), cst_ref.at[sb], nst_ref.at[sb], mst_ref.at[sb], xbuf_ref.at[sb],
                   L=L, valid=valid, dk=dk, dv=dv)


def _mlstm_one(qk_ref, v_ref, o_ref, tail_ref, cw_ref, gb_ref, og_ref, mo_ref, cin_ref, nin_ref, min_ref,
               cst_ref, nst_ref, mst_ref, xbuf_ref, *, L, valid, dk, dv):
    width = M_HEADS * dk
    qk = _conv_silu(xbuf_ref, qk_ref[...], cw_ref)

    gates = tail_ref[...] + gb_ref[...]
    row = lax.broadcasted_iota(jnp.int32, (L, LANE), 0)
    lf = _log_sigmoid(gates)
    ig = gates
    if valid < L:
        lf = jnp.where(row < valid, lf, 0.0)
        ig = jnp.where(row < valid, ig, -jnp.inf)
    bcum = lf
    s = 1
    while s < L:
        bcum = bcum + jnp.where(row >= s, pltpu.roll(bcum, s, axis=0), 0.0)
        s *= 2
    bmi_t = (pltpu.roll(ig, TAIL_FG - TAIL_IG, axis=1) - bcum).T

    ri = lax.broadcasted_iota(jnp.int32, (L, L), 0)
    ci = lax.broadcasted_iota(jnp.int32, (L, L), 1)
    causal = ri >= ci

    for h in range(M_HEADS):
        li, lfh = TAIL_IG + h, TAIL_FG + h
        a_col = bcum[:, lfh:lfh + 1]
        ig_col = ig[:, li:li + 1]
        b_row = bmi_t[lfh:lfh + 1, :]
        m0 = min_ref[:, h:h + 1]
        d_log = jnp.where(causal, a_col + b_row, -jnp.inf)
        s_log = a_col + m0
        m_col = jnp.maximum(s_log, jnp.max(d_log, axis=-1, keepdims=True))
        dw = jnp.exp(d_log - m_col)
        sw = jnp.exp(s_log - m_col)

        q = qk[:, h * dk:(h + 1) * dk]
        k = qk[:, width + h * dk:width + (h + 1) * dk] * (dk ** -0.5)
        v = v_ref[:, h * dv:(h + 1) * dv]
        qb, kb = q.astype(BF16), k.astype(BF16)
        c_prev = cin_ref[h]
        n_prev = nin_ref[h:h + 1, :]
        scores = _dot_nt(qb, kb) * dw
        num = _dot(scores.astype(BF16), v.astype(BF16)) + sw * _dot(qb, c_prev.astype(BF16))
        den = jnp.sum(scores, axis=-1, keepdims=True) + sw * jnp.sum(q * n_prev, axis=-1, keepdims=True)
        hh = num / jnp.maximum(jnp.abs(den), jnp.exp(-m_col))

        m_new = m_col[L - 1:L, :]
        a_last = a_col[L - 1:L, :]
        wl = jnp.exp(a_last - a_col + ig_col - m_new)
        decay = jnp.exp(a_last + m0 - m_new)
        cst_ref[h] = decay * c_prev + _dot_tn(kb, (wl * v).astype(BF16))
        nst_ref[h:h + 1, :] = decay * n_prev + jnp.sum(wl * k, axis=0, keepdims=True)
        mst_ref[:, h:h + 1] = m_new

        hn = hh * lax.rsqrt(jnp.mean(hh * hh, axis=-1, keepdims=True) + EPS) * og_ref[:, h * dv:(h + 1) * dv]
        gate = jax.nn.sigmoid(o_ref[:, h * dv:(h + 1) * dv])
        mo_ref[:, h * dv:(h + 1) * dv] = (hn * gate).astype(mo_ref.dtype)


def _mlstm(p3, prev8, conv_w, gate_row, out_g, c0, n0, m0, *, L, valid, seqs_per_step, out_dtype):
    n_seq, t_all, npk = p3.shape
    nc = t_all // L
    sb = seqs_per_step
    assert n_seq % sb == 0
    dk = c0.shape[2]
    dv = c0.shape[3]
    mw = M_HEADS * dv
    qkw = 2 * M_HEADS * dk
    assert qkw % mw == 0 and (2 * qkw) % mw == 0
    kern = functools.partial(_mlstm_kernel, L=L, valid=valid, dk=dk, dv=dv, one_chunk=nc == 1)
    seq = lambda shape: pl.BlockSpec((sb,) + shape, lambda b, c: (b,) + (0,) * len(shape))
    return pl.pallas_call(
        kern,
        grid=(n_seq // sb, nc),
        in_specs=[pl.BlockSpec((sb, L, qkw), lambda b, c: (b, c, 0)),
                  pl.BlockSpec((sb, L, mw), lambda b, c: (b, c, qkw // mw)),
                  pl.BlockSpec((sb, L, mw), lambda b, c: (b, c, qkw // mw + 1)),
                  pl.BlockSpec((sb, L, LANE), lambda b, c: (b, c, npk // LANE - 1)),
                  seq((SUBLANE, qkw)),
                  pl.BlockSpec((CONV_W, qkw), lambda b, c: (0, 0)),
                  pl.BlockSpec((1, LANE), lambda b, c: (0, 0)),
                  pl.BlockSpec((1, mw), lambda b, c: (0, 0)),
                  seq((M_HEADS, dk, dv)), seq((M_HEADS, dk)), seq((1, LANE))],
        out_specs=[pl.BlockSpec((sb, L, mw), lambda b, c: (b, c, 0)),
                   seq((M_HEADS, dk, dv)), seq((M_HEADS, dk)), seq((1, LANE))],
        out_shape=[jax.ShapeDtypeStruct((n_seq, t_all, mw), out_dtype),
                   jax.ShapeDtypeStruct(c0.shape, F32),
                   jax.ShapeDtypeStruct(n0.shape, F32),
                   jax.ShapeDtypeStruct(m0.shape, F32)],
        scratch_shapes=[pltpu.VMEM((sb, SUBLANE + L, qkw), F32)],
        compiler_params=_params(("parallel", "arbitrary"), VMEM_LIMIT),
        name="mlstm",
    )(p3, p3, p3, p3, prev8, conv_w, gate_row, out_g.reshape(1, mw), c0, n0, m0)


def _bucket_np(dist):
    me = N_BUCKETS // 2
    d = np.maximum(dist, 0)
    ratio = np.log(np.maximum(d, 1).astype(np.float64) / me) / math.log(MAX_DISTANCE / me)
    large = np.minimum(me + (ratio * (N_BUCKETS - me)).astype(np.int64), N_BUCKETS - 1)
    return np.where(d < me, d, large).astype(np.int32)


def _bias_kernel(t5_ref, bkp_ref, bks_ref, vs_ref, far_ref, op_ref, os_ref):
    rep = A_HEADS // A_KV_HEADS

    def lookup(bk, h):
        acc = jnp.zeros(bk.shape, F32)
        for b in range(N_BUCKETS):
            acc = jnp.where(bk == b, t5_ref[b, h], acc)
        return acc

    for h in range(A_HEADS):
        half = (h % 2) * Q_BLOCK
        op_ref[h // 2, :, half:half + Q_BLOCK] = (lookup(bkp_ref[...], h) - t5_ref[far_ref[0], h]) * LOG2E
        os_ref[h] = jnp.where(vs_ref[h // rep] > 0, lookup(bks_ref[...], h), NEG)


def _bias_tables(t5_bias, t_prompt, past, t_new):
    c = np.arange(2 * Q_BLOCK)[:, None]
    r = np.arange(Q_BLOCK)[None, :]
    bkp = _bucket_np(Q_BLOCK + r - c)
    far = _bucket_np(np.arange(Q_BLOCK + 1, max(t_prompt, past + t_new) + Q_BLOCK))
    assert (far == far[0]).all(), "bias must be constant beyond one query block"
    nk = past + LANE
    tq = np.arange(SUBLANE)[:, None]
    col = np.arange(2 * nk)[None, :]
    s_pos, s_grp = col // A_KV_HEADS, col % A_KV_HEADS
    bks = _bucket_np(past + tq - s_pos)
    vis = np.stack([(s_grp == g) & (s_pos - past <= tq) for g in range(A_KV_HEADS)]).astype(np.int32)
    return pl.pallas_call(
        _bias_kernel,
        in_specs=[pl.BlockSpec(memory_space=pltpu.SMEM),
                  pl.BlockSpec(memory_space=pltpu.VMEM),
                  pl.BlockSpec(memory_space=pltpu.VMEM),
                  pl.BlockSpec(memory_space=pltpu.VMEM),
                  pl.BlockSpec(memory_space=pltpu.SMEM)],
        out_specs=[pl.BlockSpec(memory_space=pltpu.VMEM), pl.BlockSpec(memory_space=pltpu.VMEM)],
        out_shape=[jax.ShapeDtypeStruct((A_HEADS // 2, 2 * Q_BLOCK, 2 * Q_BLOCK), F32),
                   jax.ShapeDtypeStruct((A_HEADS, SUBLANE, 2 * nk), F32)],
        name="t5_bias_tables",
    )(t5_bias, jnp.asarray(bkp), jnp.asarray(bks), jnp.asarray(vis), jnp.asarray(far[:1]))


def _sort_key(score):
    bits = lax.bitcast_convert_type(score, jnp.int32)
    return jnp.where(bits < 0, bits ^ jnp.int32(0x7FFFFFFF), bits)


def _kth_key(count_ge, shape, k_sel):
    def body(it, ans_u):
        cand_u = ans_u | lax.shift_left(jnp.int32(1), jnp.int32(31) - it)
        cnt = count_ge(cand_u ^ jnp.int32(INT_MIN))
        return jnp.where(cnt >= k_sel, cand_u, ans_u)

    ans_u = lax.fori_loop(0, 32, body, jnp.zeros(shape, jnp.int32))
    return ans_u ^ jnp.int32(INT_MIN)


def _triangle(below):
    r = lax.broadcasted_iota(jnp.int32, (LANE, LANE), 0)
    c = lax.broadcasted_iota(jnp.int32, (LANE, LANE), 1)
    return jnp.where((c < r) if below else (r < c), 1.0, 0.0).astype(BF16)


def _select_tiles(key_tiles, thr, k_sel, key_axis):
    count = lambda x: jnp.sum(x, axis=key_axis, keepdims=True)
    c_gt = jnp.zeros(thr.shape, F32)
    for kt in key_tiles:
        c_gt = c_gt + count(jnp.where(kt > thr, 1.0, 0.0))
    need = k_sel - c_gt
    tri = _triangle(below=(key_axis == 0))
    run = jnp.zeros(thr.shape, F32)
    out = []
    for kt in key_tiles:
        eq = jnp.where(kt == thr, 1.0, 0.0)
        eqb = eq.astype(BF16)
        before = (_dot(tri, eqb) if key_axis == 0 else _dot(eqb, tri)) + run
        out.append(jnp.where(kt > thr, 1.0, jnp.where(before < need, eq, 0.0)))
        run = run + count(eq)
    return out


FAR_STEP = 2


def _dsa_prompt_kernel(qn_ref, qi_ref, qtail_ref, k_ref, v_ref, ktail_ref, kg_ref, bias_ref,
                       ao_ref, kn_ref, vo_ref,
                       kb_ref, vt_ref, kib_ref, key_ref, negb_ref, *, T, k_sel, hd):
    i = pl.program_id(1)
    nb = T // Q_BLOCK
    rep = A_HEADS // A_KV_HEADS
    near_w = 2 * LANE

    @pl.when(i == 0)
    def _():
        kb_ref[0:LANE, :] = jnp.zeros((LANE, kb_ref.shape[1]), BF16)
        vt_ref[:, 0:LANE] = jnp.zeros((vt_ref.shape[0], LANE), BF16)
        kib_ref[0:LANE, :] = jnp.zeros((LANE, IDX_DIM), BF16)
        for g in range(A_KV_HEADS):
            kn = _rms(k_ref[:, g * hd:(g + 1) * hd], kg_ref[...])
            kn_ref[pl.ds(g, T, stride=A_KV_HEADS), :] = kn
            vo_ref[pl.ds(g, T, stride=A_KV_HEADS), :] = v_ref[:, g * hd:(g + 1) * hd]
            kb_ref[LANE:, g * hd:(g + 1) * hd] = kn.astype(BF16)
        vt_ref[:, LANE:] = v_ref[...].T.astype(BF16)
        kib_ref[LANE:, :] = ktail_ref[:, TAIL_KI:TAIL_KI + IDX_DIM].astype(BF16)

    wi_t = qtail_ref[...].T[TAIL_WI:TAIL_WI + IDX_HEADS, :] * (IDX_HEADS ** -0.5) * (IDX_DIM ** -0.5)
    qi_pairs = [jnp.concatenate([qi_ref[:, (2 * a + hh) * IDX_DIM:(2 * a + hh + 1) * IDX_DIM] for hh in range(2)],
                                axis=0) for a in range(IDX_HEADS // 2)]
    qn_pairs = [jnp.concatenate([qn_ref[:, (2 * a + hh) * hd:(2 * a + hh + 1) * hd] for hh in range(2)], axis=0)
                for a in range(A_HEADS // 2)]

    near0 = pl.multiple_of(i * LANE, LANE)
    ki_near = kib_ref[pl.ds(near0, near_w), :]
    k_near = kb_ref[pl.ds(near0, near_w), :]
    vt_near = vt_ref[:, pl.ds(near0, near_w)]
    kc = lax.broadcasted_iota(jnp.int32, (near_w, Q_BLOCK), 0)
    qr = lax.broadcasted_iota(jnp.int32, (near_w, Q_BLOCK), 1)
    near_ok = jnp.where(kc <= qr + LANE, 1.0, 0.0) * jnp.maximum(jnp.where(kc >= LANE, 1.0, 0.0),
                                                                  (i >= 1).astype(F32))

    def scores(ki):
        sc = jnp.zeros((ki.shape[0], Q_BLOCK), F32)
        for a in range(IDX_HEADS // 2):
            s2 = _dot_nt(ki, qi_pairs[a])
            for hh in range(2):
                j = 2 * a + hh
                sc = sc + wi_t[j:j + 1, :] * jnp.maximum(s2[:, hh * Q_BLOCK:(hh + 1) * Q_BLOCK], 0.0)
        return sc

    def body(wf):
        fw = wf * LANE
        w_all = fw + near_w
        if w_all <= k_sel:
            negb_ref[0:w_all, :] = jnp.where(near_ok > 0.0, 0.0, NEG)
            attend(wf)
            return
        key_ref[fw:w_all, :] = _sort_key(jnp.where(near_ok > 0.0, scores(ki_near) + 0.0, NEG))
        if wf:
            far_ok = lax.broadcasted_iota(jnp.int32, (fw, Q_BLOCK), 0) < (i - 1) * LANE
            key_ref[0:fw, :] = _sort_key(jnp.where(far_ok, scores(kib_ref[LANE:LANE + fw, :]) + 0.0, NEG))

        def count_ge(cand):
            return _reduce_rows(jnp.where(key_ref[0:w_all, :] >= cand, 1.0, 0.0), jnp.sum)

        thr = _kth_key(count_ge, (1, Q_BLOCK), float(k_sel))
        sel = _select_tiles([key_ref[t * LANE:(t + 1) * LANE, :] for t in range(wf + 2)], thr, float(k_sel), 0)
        for t in range(wf):
            ok = sel[t] * (t < i - 1).astype(F32)
            negb_ref[t * LANE:(t + 1) * LANE, :] = jnp.where(ok > 0.0, 0.0, NEG)
        for t in range(2):
            ok = sel[wf + t] * near_ok[t * LANE:(t + 1) * LANE, :]
            negb_ref[fw + t * LANE:fw + (t + 1) * LANE, :] = jnp.where(ok > 0.0, 0.0, NEG)
        attend(wf)

    def attend(wf):
        fw = wf * LANE
        w_all = fw + near_w
        for a in range(A_HEADS // 2):
            g = (2 * a) // rep
            cs = slice(g * hd, (g + 1) * hd)
            qn2 = qn_pairs[a]
            nb_near = negb_ref[fw:w_all, :]
            s_near = _dot_nt(k_near[:, cs], qn2) + bias_ref[a] + jnp.concatenate([nb_near, nb_near], axis=1)
            m = _reduce_rows(s_near, jnp.max)
            if wf:
                nb_far = negb_ref[0:fw, :]
                s_far = _dot_nt(kb_ref[LANE:LANE + fw, cs], qn2) + jnp.concatenate([nb_far, nb_far], axis=1)
                m = jnp.maximum(m, _reduce_rows(s_far, jnp.max))
            p_near = jnp.exp2(s_near - m)
            l = _reduce_rows(p_near, jnp.sum)
            acc = _dot(vt_near[cs, :], p_near.astype(BF16))
            if wf:
                p_far = jnp.exp2(s_far - m)
                l = l + _reduce_rows(p_far, jnp.sum)
                acc = acc + _dot(vt_ref[cs, LANE:LANE + fw], p_far.astype(BF16))
            out_t = acc / l
            for hh in range(2):
                h = 2 * a + hh
                ao_ref[:, h * hd:(h + 1) * hd] = out_t[:, hh * Q_BLOCK:(hh + 1) * Q_BLOCK].T.astype(ao_ref.dtype)

    variants = sorted({min(FAR_STEP * -(-x // FAR_STEP), nb) for x in range(max(nb - 1, 1))})
    wf_needed = jnp.minimum((jnp.maximum(i - 1, 0) + FAR_STEP - 1) // FAR_STEP * FAR_STEP, nb)
    for wf in variants:
        pl.when(wf_needed == wf)(functools.partial(body, wf))


def _dsa_prompt(p, qn, qib, kn_g, bias_p, *, n_seq, T, hd):
    rows = p.shape[0]
    nb = T // Q_BLOCK
    aw = A_HEADS * hd
    kvw = A_KV_HEADS * hd
    iw = IDX_HEADS * IDX_DIM
    k_sel = min(TOPK_MAX, T // 4)
    kcol = p.shape[1] - LANE - iw - 2 * kvw
    assert kcol % kvw == 0
    k_blk = kcol // kvw
    tail_blk = p.shape[1] // LANE - 1
    kern = functools.partial(_dsa_prompt_kernel, T=T, k_sel=k_sel, hd=hd)
    return pl.pallas_call(
        kern,
        grid=(n_seq, nb),
        in_specs=[pl.BlockSpec((Q_BLOCK, aw), lambda b, i: (b * nb + i, 0)),
                  pl.BlockSpec((Q_BLOCK, iw), lambda b, i: (b * nb + i, 0)),
                  pl.BlockSpec((Q_BLOCK, LANE), lambda b, i: (b * nb + i, tail_blk)),
                  pl.BlockSpec((T, kvw), lambda b, i: (b, k_blk)),
                  pl.BlockSpec((T, kvw), lambda b, i: (b, k_blk + 1)),
                  pl.BlockSpec((T, LANE), lambda b, i: (b, tail_blk)),
                  pl.BlockSpec((1, hd), lambda b, i: (0, 0)),
                  pl.BlockSpec((A_HEADS // 2, 2 * Q_BLOCK, 2 * Q_BLOCK), lambda b, i: (0, 0, 0))],
        out_specs=[pl.BlockSpec((Q_BLOCK, aw), lambda b, i: (b * nb + i, 0)),
                   pl.BlockSpec((A_KV_HEADS * T, hd), lambda b, i: (b, 0)),
                   pl.BlockSpec((A_KV_HEADS * T, hd), lambda b, i: (b, 0))],
        out_shape=[jax.ShapeDtypeStruct((rows, aw), BF16),
                   jax.ShapeDtypeStruct((A_KV_HEADS * rows, hd), F32),
                   jax.ShapeDtypeStruct((A_KV_HEADS * rows, hd), F32)],
        scratch_shapes=[pltpu.VMEM((T + LANE, kvw), BF16), pltpu.VMEM((kvw, T + LANE), BF16),
                        pltpu.VMEM((T + LANE, IDX_DIM), BF16),
                        pltpu.VMEM((T + 2 * LANE, Q_BLOCK), jnp.int32),
                        pltpu.VMEM((T + 2 * LANE, Q_BLOCK), F32)],
        compiler_params=_params(("parallel", "arbitrary"), VMEM_LIMIT),
        name="dsa_prompt",
    )(qn, qib, p, p, p, p, kn_g.reshape(1, hd), bias_p)


def _pad_rows(x, n):
    return jnp.concatenate([x, jnp.zeros((n - x.shape[0], x.shape[1]), x.dtype)], axis=0)


def _idx_sample_kernel(pt_ref, qi_ref, wi_ref, kinew_ref, *rest, n_pages, t_new):
    ip, sc_ref = rest[:n_pages], rest[n_pages]
    del pt_ref
    qi = qi_ref[...].astype(BF16)
    wi = wi_ref[...] * (IDX_HEADS ** -0.5) * (IDX_DIM ** -0.5)
    tq = lax.broadcasted_iota(jnp.int32, (SUBLANE, LANE), 0)
    lane = lax.broadcasted_iota(jnp.int32, (SUBLANE, LANE), 1)
    for t in range(n_pages + 1):
        r = _dot(qi, ip[t][...].astype(BF16)) if t < n_pages else _dot_nt(qi, _pad_rows(kinew_ref[...], LANE).astype(BF16))
        sc = jnp.zeros((SUBLANE, LANE), F32)
        for j in range(IDX_HEADS):
            sc = sc + wi[j * SUBLANE:(j + 1) * SUBLANE, :] * jnp.maximum(r[j * SUBLANE:(j + 1) * SUBLANE, :], 0.0)
        sc = sc + 0.0
        if t == n_pages:
            sc = jnp.where(lane <= tq, sc, jnp.where(lane < t_new, NEG, -jnp.inf))
        sc_ref[:, t * LANE:(t + 1) * LANE] = sc


def _select_kernel(sc_ref, sel_ref, key_ref, *, k_sel):
    rows, nk = sc_ref.shape
    key_ref[...] = _sort_key(sc_ref[...])

    def count_ge(cand):
        return jnp.sum(jnp.where(key_ref[...] >= cand, 1.0, 0.0), axis=1, keepdims=True)

    thr = _kth_key(count_ge, (rows, 1), float(k_sel))
    sel = _select_tiles([key_ref[:, t * LANE:(t + 1) * LANE] for t in range(nk // LANE)], thr, float(k_sel), 1)
    for t in range(nk // LANE):
        sel_ref[:, t * LANE:(t + 1) * LANE] = sel[t]


def _attn_sample_kernel(pt_ref, qa_ref, sel_ref, knew_ref, vnew_ref, qg_ref, kg_ref, bias_ref, *rest,
                        n_pages, hd):
    kp, vp = rest[:n_pages], rest[n_pages:2 * n_pages]
    ao_ref, kn_ref = rest[2 * n_pages:]
    del pt_ref
    nt = n_pages + 1
    pw = A_KV_HEADS * PAGE_SIZE
    scale = hd ** -0.5
    rows_q = qa_ref.shape[0]

    kn_new = _rms(knew_ref[...], kg_ref[...])
    kn_ref[...] = kn_new
    k_new = _pad_rows(kn_new, pw).astype(BF16)
    v_new = _pad_rows(vnew_ref[...], pw).astype(BF16)
    qn = _rms(qa_ref[...], qg_ref[...]).astype(BF16)

    tok = lax.broadcasted_iota(jnp.int32, (PAGE_SIZE, pw), 0)
    col = lax.broadcasted_iota(jnp.int32, (PAGE_SIZE, pw), 1)
    spread = jnp.where(col // A_KV_HEADS == tok, 1.0, 0.0).astype(BF16)
    sel_rows = jnp.concatenate([sel_ref[:, t * LANE:(t + 1) * LANE] for t in range(nt)], axis=0)
    negb = (_dot(sel_rows.astype(BF16), spread) - 1.0) * (-NEG)

    logits = []
    for t in range(nt):
        kt = kp[t][...].astype(BF16) if t < n_pages else k_new
        nb = jnp.concatenate([negb[t * SUBLANE:(t + 1) * SUBLANE, :]] * (rows_q // SUBLANE), axis=0)
        logits.append(_dot_nt(qn, kt) * scale + bias_ref[:, t * pw:(t + 1) * pw] + nb)
    mx = logits[0]
    for t in range(1, nt):
        mx = jnp.maximum(mx, logits[t])
    m = jnp.max(mx, axis=-1, keepdims=True)
    psum = jnp.zeros((rows_q, pw), F32)
    acc = jnp.zeros((rows_q, hd), F32)
    for t in range(nt):
        pr = jnp.exp(logits[t] - m)
        psum = psum + pr
        acc = acc + _dot(pr.astype(BF16), vp[t][...].astype(BF16) if t < n_pages else v_new)
    ao_ref[...] = acc / jnp.sum(psum, axis=-1, keepdims=True)


def _dsa_sample(qa, qi, wi, knew, vnew, kinew, qn_g, kn_g, bias_s, cache_k, cache_v, cache_i, page_table, *, t_new, hd):
    n_seq, n_pages = page_table.shape
    nk = n_pages * PAGE_SIZE + LANE
    k_sel = min(TOPK_MAX, (n_pages * PAGE_SIZE + t_new) // 4)
    pt = page_table.reshape(-1)

    def page_spec(arr, t):
        return pl.BlockSpec((None,) + arr.shape[1:], lambda b, pt: (pt[b * n_pages + t], 0, 0))

    seq_spec = lambda arr: pl.BlockSpec((None,) + arr.shape[1:], lambda b, pt: (b, 0, 0))
    const_spec = lambda arr: pl.BlockSpec(arr.shape, lambda b, pt: (0, 0))

    scores = pl.pallas_call(
        functools.partial(_idx_sample_kernel, n_pages=n_pages, t_new=t_new),
        grid_spec=pltpu.PrefetchScalarGridSpec(
            num_scalar_prefetch=1, grid=(n_seq,),
            in_specs=[seq_spec(qi), seq_spec(wi), seq_spec(kinew)] + [page_spec(cache_i, t) for t in range(n_pages)],
            out_specs=pl.BlockSpec((SUBLANE, nk), lambda b, pt: (b, 0))),
        out_shape=jax.ShapeDtypeStruct((n_seq * SUBLANE, nk), F32),
        compiler_params=_params(("arbitrary",), VMEM_LIMIT),
        name="idx_sample",
    )(pt, qi, wi, kinew, *([cache_i] * n_pages))

    rows = scores.shape[0]
    tr = LANE if rows % LANE == 0 else rows
    sel = pl.pallas_call(
        functools.partial(_select_kernel, k_sel=k_sel),
        grid=(rows // tr,),
        in_specs=[pl.BlockSpec((tr, nk), lambda r: (r, 0))],
        out_specs=pl.BlockSpec((tr, nk), lambda r: (r, 0)),
        out_shape=jax.ShapeDtypeStruct((rows, nk), F32),
        scratch_shapes=[pltpu.VMEM((tr, nk), jnp.int32)],
        compiler_params=_params(("parallel",), VMEM_LIMIT),
        name="select_sample",
    )(scores)

    qg, kg = qn_g.reshape(1, hd), kn_g.reshape(1, hd)
    return pl.pallas_call(
        functools.partial(_attn_sample_kernel, n_pages=n_pages, hd=hd),
        grid_spec=pltpu.PrefetchScalarGridSpec(
            num_scalar_prefetch=1, grid=(n_seq,),
            in_specs=[seq_spec(qa), pl.BlockSpec((SUBLANE, nk), lambda b, pt: (b, 0)), seq_spec(knew), seq_spec(vnew),
                      const_spec(qg), const_spec(kg), const_spec(bias_s)]
                     + [page_spec(cache_k, t) for t in range(n_pages)]
                     + [page_spec(cache_v, t) for t in range(n_pages)],
            out_specs=[seq_spec(qa), seq_spec(knew)]),
        out_shape=[jax.ShapeDtypeStruct(qa.shape, F32), jax.ShapeDtypeStruct(knew.shape, F32)],
        compiler_params=_params(("arbitrary",), VMEM_LIMIT),
        name="attn_sample",
    )(pt, qa, sel, knew, vnew, qg, kg, bias_s, *([cache_k] * n_pages), *([cache_v] * n_pages))


def _outproj_kernel(x_ref, mo_ref, ao_ref, gt_ref, sh_ref, sc_ref, g_ref, wm_ref, wa_ref, o_ref, h_ref):
    y = _dot(mo_ref[...].astype(BF16), wm_ref[...]) + _dot(ao_ref[...].astype(BF16), wa_ref[...])
    x = x_ref[...] + gt_ref[...] * y
    o_ref[...] = x
    h_ref[...] = _adaln(x, g_ref[...], sc_ref[...], sh_ref[...])


def _outproj(x, mo, ao, gt, w_out, sh, sc, g):
    m, d = x.shape
    mw, aw = mo.shape[1], ao.shape[1]
    tm = _row_tile(m)
    row = lambda w: pl.BlockSpec((tm, w), lambda i, j: (i, 0))
    return pl.pallas_call(
        _outproj_kernel,
        grid=(m // tm, 1),
        in_specs=[row(d), row(mw), row(aw), _mod_spec(gt, m, tm, d), _mod_spec(sh, m, tm, d), _mod_spec(sc, m, tm, d),
                  pl.BlockSpec((1, d), lambda i, j: (0, 0)),
                  pl.BlockSpec((mw, d), lambda i, j: (0, 0)),
                  pl.BlockSpec((aw, d), lambda i, j: (0, 0))],
        out_specs=[row(d), row(d)],
        out_shape=[jax.ShapeDtypeStruct((m, d), F32), jax.ShapeDtypeStruct((m, d), BF16)],
        compiler_params=_params(("parallel", "arbitrary"), VMEM_LIMIT),
        name="outproj",
    )(x, mo, ao, gt[0], sh[0], sc[0], g.reshape(1, d), w_out[:mw], w_out[mw:])


def _pad_tokens(a, t_pad):
    pad = [(0, 0)] * a.ndim
    pad[1] = (0, t_pad - a.shape[1])
    return jnp.pad(a, pad)


def _layer(x3, mods, lw, bias_p, bias_s, rec_state, past):
    n_seq, T, d = x3.shape
    x = x3.reshape(n_seq * T, d)
    dk, dv = rec_state[0].shape[2], rec_state[0].shape[3]
    mw = M_HEADS * dv
    hd = lw["q_norm_g"].shape[0]
    aw, kvw = A_HEADS * hd, A_KV_HEADS * hd
    iw = IDX_HEADS * IDX_DIM
    sh1, sc1, g1, sh2, sc2, g2, sh3, sc3, g3 = [(mods, k) for k in range(N_MOD)]

    x, ffn1_w = _ffn(x, g1, *lw["ffn1_w"], norm=(sh1, sc1, lw["ffn1_norm_g"]))
    qkw = 2 * M_HEADS * dk
    a_off = qkw + 2 * mw
    c0, n0, m0, conv_buf = rec_state
    prev8 = jnp.zeros((n_seq, SUBLANE, qkw), F32).at[:, SUBLANE - (CONV_W - 1):, :].set(conv_buf)
    prompt = None
    if past is None:
        prompt = (lw["q_norm_g"], (a_off, aw, a_off + aw + 2 * kvw, iw, hd))
    p, qn, qib = _proj(x, sh2, sc2, lw["mix_norm_g"], lw["w_in"], prompt)
    npk = p.shape[1]
    gate_row = jnp.zeros((1, LANE), F32).at[0, TAIL_IG:TAIL_IG + 2 * M_HEADS].set(lw["mlstm_gate_b"])
    m0_pad = jnp.zeros((n_seq, 1, LANE), F32).at[:, 0, :M_HEADS].set(m0)
    p3 = p.reshape(n_seq, T, npk)
    v_a = p3[:, :, a_off + aw + kvw:a_off + aw + 2 * kvw]
    ki = p3[:, :, npk - LANE + TAIL_KI:npk - LANE + TAIL_KI + IDX_DIM]
    conv_new = p3[:, T - (CONV_W - 1):, :qkw]

    if past is None:
        L = MLSTM_CHUNK if T % MLSTM_CHUNK == 0 else (M_CHUNK if T % M_CHUNK == 0 else T)
        mo, c_new, n_new, m_new = _mlstm(p3, prev8, lw["mlstm_conv_w"], gate_row, lw["mlstm_out_g"], c0, n0, m0_pad,
                                         L=L, valid=L, seqs_per_step=2 if n_seq % 2 == 0 else 1, out_dtype=BF16)
        mo = mo.reshape(n_seq * T, mw)
        ao, k_n, v_n = _dsa_prompt(p, qn, qib, lw["k_norm_g"], bias_p, n_seq=n_seq, T=T, hd=hd)
        k_n = k_n.reshape(n_seq, T, A_KV_HEADS, hd)
        v_n = v_n.reshape(n_seq, T, A_KV_HEADS, hd)
    else:
        tp = SUBLANE
        assert T <= tp
        mo, c_new, n_new, m_new = _mlstm(_pad_tokens(p3, tp), prev8, lw["mlstm_conv_w"], gate_row, lw["mlstm_out_g"],
                                         c0, n0, m0_pad, L=tp, valid=T,
                                         seqs_per_step=SUBLANE if n_seq % SUBLANE == 0 else 1, out_dtype=F32)
        mo = mo[:, :T].reshape(n_seq * T, mw)
        cache_k, cache_v, cache_i, page_table = past
        rep = A_HEADS // A_KV_HEADS
        qa = _pad_tokens(p3[:, :, a_off:a_off + aw], tp).reshape(n_seq, tp, A_KV_HEADS, rep, hd)
        qa = qa.transpose(0, 2, 3, 1, 4).reshape(n_seq, A_HEADS * tp, hd)
        qi = _pad_tokens(p3[:, :, a_off + aw + 2 * kvw:a_off + aw + 2 * kvw + iw], tp)
        qi = qi.reshape(n_seq, tp, IDX_HEADS, IDX_DIM).transpose(0, 2, 1, 3).reshape(n_seq, IDX_HEADS * tp, IDX_DIM)
        wi = _pad_tokens(p3[:, :, npk - LANE + TAIL_WI:npk - LANE + TAIL_WI + IDX_HEADS], tp)
        wi = wi.transpose(0, 2, 1).reshape(n_seq, IDX_HEADS * tp, 1)
        knew = _pad_tokens(p3[:, :, a_off + aw:a_off + aw + kvw], tp).reshape(n_seq, tp * A_KV_HEADS, hd)
        vnew = _pad_tokens(v_a, tp).reshape(n_seq, tp * A_KV_HEADS, hd)
        n_pool = cache_k.shape[0]
        ao4, k_n8 = _dsa_sample(qa, qi, wi, knew, vnew, _pad_tokens(ki, tp), lw["q_norm_g"], lw["k_norm_g"], bias_s,
                                cache_k.reshape(n_pool, PAGE_SIZE * A_KV_HEADS, hd),
                                cache_v.reshape(n_pool, PAGE_SIZE * A_KV_HEADS, hd),
                                jnp.swapaxes(cache_i, 1, 2), page_table, t_new=T, hd=hd)
        k_n8 = k_n8.reshape(n_seq, tp, kvw)
        ao = ao4.reshape(n_seq, A_KV_HEADS, rep, tp, hd)[:, :, :, :T].transpose(0, 3, 1, 2, 4).reshape(n_seq * T, aw)
        k_n = k_n8[:, :T].reshape(n_seq, T, A_KV_HEADS, hd)
        v_n = v_a.reshape(n_seq, T, A_KV_HEADS, hd)

    x, h3 = _outproj(x, mo, ao, g2, lw["w_out"], sh3, sc3, lw["ffn2_norm_g"])
    x, ffn2_w = _ffn(x, g3, *lw["ffn2_w"], h=h3)
    state = (k_n, v_n, ki, c_new, n_new, m_new[:, 0, :M_HEADS], conv_new)
    return x.reshape(n_seq, T, d), state, dict(lw, ffn1_w=ffn1_w, ffn2_w=ffn2_w)


def kernel(x_prompt, x_sample, c_prompt, c_sample, cache_k, cache_v, cache_idx_k, page_table, state_C, state_n,
           state_m, state_conv, ffn1_norm_g, ffn1_w_gate, ffn1_w_up, ffn1_w_down, mix_norm_g, w_in, mlstm_conv_w,
           mlstm_gate_b, mlstm_out_g, q_norm_g, k_norm_g, t5_bias, w_out, ffn2_norm_g, ffn2_w_gate, ffn2_w_up,
           ffn2_w_down, w_ada, b_ada):
    depth = w_in.shape[0]
    bp, tp_len, d = x_prompt.shape
    bs, ts_len, _ = x_sample.shape
    dk, dv = state_C.shape[3], state_C.shape[4]
    mw = M_HEADS * dv
    hd = q_norm_g.shape[1]
    past_len = page_table.shape[1] * PAGE_SIZE
    bias_p, bias_s = _bias_tables(t5_bias, tp_len, past_len, ts_len)
    bias_s = bias_s.reshape(A_HEADS * SUBLANE, A_KV_HEADS * (past_len + LANE))

    xp, xs = x_prompt, x_sample
    st_p, st_s = [], []
    for l in range(depth):
        lw = dict(ffn1_norm_g=ffn1_norm_g[l], ffn1_w=(ffn1_w_gate[l], ffn1_w_up[l], ffn1_w_down[l]),
                  mix_norm_g=mix_norm_g[l], w_in=_pack_w_in(w_in[l], mw, A_HEADS * hd, A_KV_HEADS * hd),
                  mlstm_conv_w=mlstm_conv_w[l], mlstm_gate_b=mlstm_gate_b[l], mlstm_out_g=mlstm_out_g[l],
                  q_norm_g=q_norm_g[l], k_norm_g=k_norm_g[l], w_out=w_out[l].astype(BF16),
                  ffn2_norm_g=ffn2_norm_g[l], ffn2_w=(ffn2_w_gate[l], ffn2_w_up[l], ffn2_w_down[l]))
        n_s = bs * ts_len
        mods = _ada(jnp.concatenate([jnp.repeat(c_sample, ts_len, axis=0), c_prompt], axis=0), w_ada[l], b_ada[l])
        per_seq = tp_len % ROW_TILE == 0
        mods_p = mods[n_s:].reshape(bp, 1, N_MOD * d) if per_seq else jnp.repeat(mods[n_s:], tp_len, axis=0)
        init_s = (state_C[l], state_n[l], state_m[l], state_conv[l])
        xs, ss, lw = _layer(xs, mods, lw, bias_p, bias_s, init_s,
                            (cache_k[l], cache_v[l], cache_idx_k[l], page_table))
        init_p = (jnp.zeros((bp, M_HEADS, dk, dv), F32), jnp.zeros((bp, M_HEADS, dk), F32),
                  jnp.zeros((bp, M_HEADS), F32), jnp.zeros((bp, CONV_W - 1, 2 * M_HEADS * dk), F32))
        xp, sp, _ = _layer(xp, mods_p, lw, bias_p, bias_s, init_p, None)
        st_p.append(sp)
        st_s.append(ss)
    stack = lambda sts, k: sts[0][k][None] if depth == 1 else jnp.stack([s[k] for s in sts])
    return (xp, xs) + tuple(stack(st_p, k) for k in range(7)) + tuple(stack(st_s, k) for k in range(7))
```

```python
import functools
import math

import numpy as np
import jax
import jax.numpy as jnp
from jax import lax
from jax.experimental import pallas as pl
from jax.experimental.pallas import tpu as pltpu

F32 = jnp.float32
BF16 = jnp.bfloat16

M_HEADS = 4
CONV_W = 4
M_CHUNK = 64
MLSTM_CHUNK = 8 * M_CHUNK
A_HEADS = 8
A_KV_HEADS = 2
IDX_HEADS = 8
IDX_DIM = 64
TOPK_MAX = 256
Q_BLOCK = 128
PAGE_SIZE = 128
N_BUCKETS = 32
MAX_DISTANCE = 128
N_MOD = 9
EPS = 1e-6
NEG = -1e30

LANE = 128
SUBLANE = 8
ROW_TILE = 512
VMEM_LIMIT = 56 * 1024 * 1024

TAIL_KI = 0
TAIL_WI = IDX_DIM
TAIL_IG = IDX_DIM + IDX_HEADS
TAIL_FG = TAIL_IG + M_HEADS

INT_MIN = -2 ** 31
LOG2E = math.log2(math.e)


def _params(sem, vmem=None, flags=None):
    return pltpu.CompilerParams(dimension_semantics=sem, vmem_limit_bytes=vmem, flags=flags)


def _dot(a, b):
    return jnp.dot(a, b, preferred_element_type=F32)


def _dot_nt(a, b):
    return lax.dot_general(a, b, (((1,), (1,)), ((), ())), preferred_element_type=F32)


def _dot_tn(a, b):
    return lax.dot_general(a, b, (((0,), (0,)), ((), ())), preferred_element_type=F32)


def _rms(x, g):
    return x * lax.rsqrt(jnp.mean(x * x, axis=-1, keepdims=True) + EPS) * g


def _row_tile(n):
    return ROW_TILE if n % ROW_TILE == 0 else n


REDUCE_CHAINS = 4


def _reduce_rows(x, op):
    n, c = x.shape
    groups = n // SUBLANE
    chains = REDUCE_CHAINS if groups % REDUCE_CHAINS == 0 else 1
    x3 = x.reshape(groups, SUBLANE, c)
    per = groups // chains
    parts = [op(x3[k * per:(k + 1) * per], axis=0) for k in range(chains)]
    while len(parts) > 1:
        parts = [op(jnp.stack(parts[k:k + 2]), axis=0) for k in range(0, len(parts), 2)]
    return op(parts[0], axis=0, keepdims=True)


def _ada_kernel(c_ref, w_ref, b_ref, o_ref):
    o_ref[...] = _dot(c_ref[...].astype(BF16), w_ref[...].astype(BF16)) + b_ref[...]


def _ada(c_all, w_ada, b_ada):
    r, d = c_all.shape
    n = w_ada.shape[1]
    tn = 2048 if n % 2048 == 0 else n
    return pl.pallas_call(
        _ada_kernel,
        grid=(n // tn,),
        in_specs=[pl.BlockSpec((r, d), lambda j: (0, 0)),
                  pl.BlockSpec((d, tn), lambda j: (0, j)),
                  pl.BlockSpec((1, tn), lambda j: (0, j))],
        out_specs=pl.BlockSpec((r, tn), lambda j: (0, j)),
        out_shape=jax.ShapeDtypeStruct((r, n), F32),
        compiler_params=_params(("arbitrary",), VMEM_LIMIT),
        name="ada",
    )(c_all, w_ada, b_ada.reshape(1, n))


def _adaln(x, g, sc, sh):
    return (_rms(x, g) * (1.0 + sc) + sh).astype(BF16)


def _ffn_kernel(x_ref, gt_ref, *rest, prenormed, emit):
    if prenormed:
        hin_ref, wg_ref, wu_ref, wd_ref, o_ref, *rest = rest
    else:
        sh_ref, sc_ref, g_ref, wg_ref, wu_ref, wd_ref, o_ref, *rest = rest
    bf16_out, rest = (rest[:3], rest[3:]) if emit else ((), rest)
    acc_ref = rest[-1]
    j = pl.program_id(1)

    @pl.when(j == 0)
    def _():
        if not prenormed:
            rest[0][...] = _adaln(x_ref[...], g_ref[...], sc_ref[...], sh_ref[...])
        acc_ref[...] = jnp.zeros_like(acc_ref)

    wg, wu, wd = wg_ref[...], wu_ref[...], wd_ref[...]
    if emit:
        wg, wu, wd = wg.astype(BF16), wu.astype(BF16), wd.astype(BF16)
        for ref, w in zip(bf16_out, (wg, wu, wd)):
            ref[...] = w
    h = hin_ref[...] if prenormed else rest[0][...]
    a = _dot(h, wg)
    u = _dot(h, wu)
    act = (a * jax.nn.sigmoid(a) * u).astype(BF16)
    acc_ref[...] += _dot(act, wd)

    @pl.when(j == pl.num_programs(1) - 1)
    def _():
        o_ref[...] = x_ref[...] + 0.5 * gt_ref[...] * acc_ref[...]


def _mod_spec(mod, m, tm, d):
    arr, k = mod
    if arr.ndim == 2:
        return pl.BlockSpec((tm, d), lambda i, j: (i, k))
    tiles_per_seq = (m // arr.shape[0]) // tm
    return pl.BlockSpec((None, 1, d), lambda i, j: (i // tiles_per_seq, 0, k))


def _ffn(x, gt, wg, wu, wd, *, norm=None, h=None):
    m, d = x.shape
    f = wg.shape[1]
    tm = _row_tile(m)
    emit = wg.dtype != BF16
    assert not emit or m == tm
    tf = (256 if emit else 512) if f % 512 == 0 else f
    row = pl.BlockSpec((tm, d), lambda i, j: (i, 0))
    w_specs = [pl.BlockSpec((d, tf), lambda i, j: (0, j)),
               pl.BlockSpec((d, tf), lambda i, j: (0, j)),
               pl.BlockSpec((tf, d), lambda i, j: (j, 0))]
    if h is None:
        sh, sc, g = norm
        pre_specs = [_mod_spec(sh, m, tm, d), _mod_spec(sc, m, tm, d), pl.BlockSpec((1, d), lambda i, j: (0, 0))]
        pre_args = [sh[0], sc[0], g.reshape(1, d)]
        scratch = [pltpu.VMEM((tm, d), BF16)]
    else:
        pre_specs, pre_args, scratch = [row], [h], []
    out_specs = [row]
    out_shape = [jax.ShapeDtypeStruct((m, d), F32)]
    if emit:
        out_specs += w_specs
        out_shape += [jax.ShapeDtypeStruct(w.shape, BF16) for w in (wg, wu, wd)]
    y, *wb = pl.pallas_call(
        functools.partial(_ffn_kernel, prenormed=h is not None, emit=emit),
        grid=(m // tm, f // tf),
        in_specs=[row, _mod_spec(gt, m, tm, d)] + pre_specs + w_specs,
        out_specs=out_specs,
        out_shape=out_shape,
        scratch_shapes=scratch + [pltpu.VMEM((tm, d), F32)],
        compiler_params=_params(("parallel", "arbitrary"), VMEM_LIMIT),
        name="ffn",
    )(x, gt[0], *pre_args, wg, wu, wd)
    return y, (tuple(wb) if emit else (wg, wu, wd))


def _proj_kernel(x_ref, sh_ref, sc_ref, g_ref, wm_ref, wt_ref, *rest, q_cols):
    h = _adaln(x_ref[...], g_ref[...], sc_ref[...], sh_ref[...])
    nm = wm_ref.shape[1]
    o_ref = rest[1] if q_cols else rest[0]
    o_ref[:, 0:nm] = _dot(h, wm_ref[...])
    p = _dot(h, wt_ref[...])
    o_ref[:, nm:] = p
    if not q_cols:
        return
    qg_ref, _, qn_ref, qib_ref = rest
    a0, aw, i0, iw, hd = q_cols
    for hh in range(aw // hd):
        qn = _rms(p[:, a0 - nm + hh * hd:a0 - nm + (hh + 1) * hd], qg_ref[...]) * (hd ** -0.5 * LOG2E)
        qn_ref[:, hh * hd:(hh + 1) * hd] = qn.astype(BF16)
    qib_ref[...] = p[:, i0 - nm:i0 - nm + iw].astype(BF16)


PROJ_ROW_TILE = 256


def _proj(x, sh, sc, g, w, prompt=None):
    m, d = x.shape
    wm, wt = w
    n = wm.shape[1] + wt.shape[1]
    tm = PROJ_ROW_TILE if m % PROJ_ROW_TILE == 0 else m
    row = lambda width: pl.BlockSpec((tm, width), lambda i, j: (i, 0))
    resident = lambda arr: pl.BlockSpec(arr.shape, lambda i, j: (0, 0), pipeline_mode=pl.Buffered(1))
    in_specs = [row(d), _mod_spec(sh, m, tm, d), _mod_spec(sc, m, tm, d), pl.BlockSpec((1, d), lambda i, j: (0, 0)),
                resident(wm), resident(wt)]
    args = [x, sh[0], sc[0], g.reshape(1, d), wm, wt]
    out_specs, out_shape = [row(n)], [jax.ShapeDtypeStruct((m, n), F32)]
    q_cols = None
    if prompt:
        qn_g, q_cols = prompt
        a0, aw, i0, iw, hd = q_cols
        assert a0 >= wm.shape[1]
        in_specs.append(pl.BlockSpec((1, hd), lambda i, j: (0, 0)))
        args.append(qn_g.reshape(1, hd))
        out_specs += [row(aw), row(iw)]
        out_shape += [jax.ShapeDtypeStruct((m, aw), BF16), jax.ShapeDtypeStruct((m, iw), BF16)]
    out = pl.pallas_call(
        functools.partial(_proj_kernel, q_cols=q_cols),
        grid=(m // tm, 1),
        in_specs=in_specs,
        out_specs=out_specs,
        out_shape=out_shape,
        compiler_params=_params(("parallel", "arbitrary"), VMEM_LIMIT),
        name="proj",
    )(*args)
    return out if prompt else (out[0], None, None)


def _pack_w_in(w_in, mw, aw, kvw):
    d = w_in.shape[0]
    o = np.cumsum([0, mw, mw, mw, mw, M_HEADS, M_HEADS, aw, kvw, kvw, IDX_HEADS * IDX_DIM, IDX_DIM, IDX_HEADS])
    seg = lambda k: w_in[:, o[k]:o[k + 1]]
    pad = (-o[-1]) % LANE
    tail = jnp.concatenate([seg(k) for k in (6, 7, 8, 9, 10, 11, 4, 5)] + [jnp.zeros((d, pad), w_in.dtype)], axis=1)
    return w_in[:, :o[4]].astype(BF16), tail.astype(BF16)


def _log_sigmoid(x):
    return jnp.minimum(x, 0.0) - jnp.log1p(jnp.exp(-jnp.abs(x)))


def _conv_silu(xbuf_ref, x_new, cw_ref):
    rows = x_new.shape[0]
    xbuf_ref[SUBLANE:SUBLANE + rows, :] = x_new
    base = SUBLANE - (CONV_W - 1)
    y = xbuf_ref[base:base + rows, :] * cw_ref[0:1, :]
    for j in range(1, CONV_W):
        y = y + xbuf_ref[base + j:base + j + rows, :] * cw_ref[j:j + 1, :]
    tail_rows = xbuf_ref[rows:rows + SUBLANE, :]
    xbuf_ref[0:SUBLANE, :] = tail_rows
    return y * jax.nn.sigmoid(y)


def _mlstm_kernel(qk_ref, v_ref, o_ref, tail_ref, prev_ref, cw_ref, gb_ref, og_ref, c0_ref, n0_ref, m0_ref,
                  mo_ref, cst_ref, nst_ref, mst_ref, xbuf_ref, *, L, valid, dk, dv, one_chunk):
    if one_chunk:
        xbuf_ref[:, 0:SUBLANE, :] = prev_ref[...]
        mst_ref[...] = m0_ref[...]
        state_in = (c0_ref, n0_ref, m0_ref)
    else:
        state_in = (cst_ref, nst_ref, mst_ref)

        @pl.when(pl.program_id(1) == 0)
        def _():
            xbuf_ref[:, 0:SUBLANE, :] = prev_ref[...]
            cst_ref[...] = c0_ref[...]
            nst_ref[...] = n0_ref[...]
            mst_ref[...] = m0_ref[...]

    for sb in range(qk_ref.shape[0]):
        _mlstm_one(qk_ref.at[sb], v_ref.at[sb], o_ref.at[sb], tail_ref.at[sb], cw_ref, gb_ref, og_ref, mo_ref.at[sb],
                   *(r.at[sb] for r in state_in), cst_ref.at[sb], nst_ref.at[sb], mst_ref.at[sb], xbuf_ref.at[sb],
                   L=L, valid=valid, dk=dk, dv=dv)


def _mlstm_one(qk_ref, v_ref, o_ref, tail_ref, cw_ref, gb_ref, og_ref, mo_ref, cin_ref, nin_ref, min_ref,
               cst_ref, nst_ref, mst_ref, xbuf_ref, *, L, valid, dk, dv):
    width = M_HEADS * dk
    qk = _conv_silu(xbuf_ref, qk_ref[...], cw_ref)

    gates = tail_ref[...] + gb_ref[...]
    row = lax.broadcasted_iota(jnp.int32, (L, LANE), 0)
    lf = _log_sigmoid(gates)
    ig = gates
    if valid < L:
        lf = jnp.where(row < valid, lf, 0.0)
        ig = jnp.where(row < valid, ig, -jnp.inf)
    bcum = lf
    s = 1
    while s < L:
        bcum = bcum + jnp.where(row >= s, pltpu.roll(bcum, s, axis=0), 0.0)
        s *= 2
    bmi_t = (pltpu.roll(ig, TAIL_FG - TAIL_IG, axis=1) - bcum).T

    ri = lax.broadcasted_iota(jnp.int32, (L, L), 0)
    ci = lax.broadcasted_iota(jnp.int32, (L, L), 1)
    causal = ri >= ci

    for h in range(M_HEADS):
        li, lfh = TAIL_IG + h, TAIL_FG + h
        a_col = bcum[:, lfh:lfh + 1]
        ig_col = ig[:, li:li + 1]
        b_row = bmi_t[lfh:lfh + 1, :]
        m0 = min_ref[:, h:h + 1]
        d_log = jnp.where(causal, a_col + b_row, -jnp.inf)
        s_log = a_col + m0
        m_col = jnp.maximum(s_log, jnp.max(d_log, axis=-1, keepdims=True))
        dw = jnp.exp(d_log - m_col)
        sw = jnp.exp(s_log - m_col)

        q = qk[:, h * dk:(h + 1) * dk]
        k = qk[:, width + h * dk:width + (h + 1) * dk] * (dk ** -0.5)
        v = v_ref[:, h * dv:(h + 1) * dv]
        qb, kb = q.astype(BF16), k.astype(BF16)
        c_prev = cin_ref[h]
        n_prev = nin_ref[h:h + 1, :]
        scores = _dot_nt(qb, kb) * dw
        num = _dot(scores.astype(BF16), v.astype(BF16)) + sw * _dot(qb, c_prev.astype(BF16))
        den = jnp.sum(scores, axis=-1, keepdims=True) + sw * jnp.sum(q * n_prev, axis=-1, keepdims=True)
        hh = num / jnp.maximum(jnp.abs(den), jnp.exp(-m_col))

        m_new = m_col[L - 1:L, :]
        a_last = a_col[L - 1:L, :]
        wl = jnp.exp(a_last - a_col + ig_col - m_new)
        decay = jnp.exp(a_last + m0 - m_new)
        cst_ref[h] = decay * c_prev + _dot_tn(kb, (wl * v).astype(BF16))
        nst_ref[h:h + 1, :] = decay * n_prev + jnp.sum(wl * k, axis=0, keepdims=True)
        mst_ref[:, h:h + 1] = m_new

        hn = hh * lax.rsqrt(jnp.mean(hh * hh, axis=-1, keepdims=True) + EPS) * og_ref[:, h * dv:(h + 1) * dv]
        gate = jax.nn.sigmoid(o_ref[:, h * dv:(h + 1) * dv])
        mo_ref[:, h * dv:(h + 1) * dv] = (hn * gate).astype(mo_ref.dtype)


def _mlstm(p3, prev8, conv_w, gate_row, out_g, c0, n0, m0, *, L, valid, seqs_per_step, out_dtype):
    n_seq, t_all, npk = p3.shape
    nc = t_all // L
    sb = seqs_per_step
    assert n_seq % sb == 0
    dk = c0.shape[2]
    dv = c0.shape[3]
    mw = M_HEADS * dv
    qkw = 2 * M_HEADS * dk
    assert qkw % mw == 0 and (2 * qkw) % mw == 0
    kern = functools.partial(_mlstm_kernel, L=L, valid=valid, dk=dk, dv=dv, one_chunk=nc == 1)
    seq = lambda shape: pl.BlockSpec((sb,) + shape, lambda b, c: (b,) + (0,) * len(shape))
    return pl.pallas_call(
        kern,
        grid=(n_seq // sb, nc),
        in_specs=[pl.BlockSpec((sb, L, qkw), lambda b, c: (b, c, 0)),
                  pl.BlockSpec((sb, L, mw), lambda b, c: (b, c, qkw // mw)),
                  pl.BlockSpec((sb, L, mw), lambda b, c: (b, c, qkw // mw + 1)),
                  pl.BlockSpec((sb, L, LANE), lambda b, c: (b, c, npk // LANE - 1)),
                  seq((SUBLANE, qkw)),
                  pl.BlockSpec((CONV_W, qkw), lambda b, c: (0, 0)),
                  pl.BlockSpec((1, LANE), lambda b, c: (0, 0)),
                  pl.BlockSpec((1, mw), lambda b, c: (0, 0)),
                  seq((M_HEADS, dk, dv)), seq((M_HEADS, dk)), seq((1, LANE))],
        out_specs=[pl.BlockSpec((sb, L, mw), lambda b, c: (b, c, 0)),
                   seq((M_HEADS, dk, dv)), seq((M_HEADS, dk)), seq((1, LANE))],
        out_shape=[jax.ShapeDtypeStruct((n_seq, t_all, mw), out_dtype),
                   jax.ShapeDtypeStruct(c0.shape, F32),
                   jax.ShapeDtypeStruct(n0.shape, F32),
                   jax.ShapeDtypeStruct(m0.shape, F32)],
        scratch_shapes=[pltpu.VMEM((sb, SUBLANE + L, qkw), F32)],
        compiler_params=_params(("parallel", "arbitrary"), VMEM_LIMIT),
        name="mlstm",
    )(p3, p3, p3, p3, prev8, conv_w, gate_row, out_g.reshape(1, mw), c0, n0, m0)


def _bucket_np(dist):
    me = N_BUCKETS // 2
    d = np.maximum(dist, 0)
    ratio = np.log(np.maximum(d, 1).astype(np.float64) / me) / math.log(MAX_DISTANCE / me)
    large = np.minimum(me + (ratio * (N_BUCKETS - me)).astype(np.int64), N_BUCKETS - 1)
    return np.where(d < me, d, large).astype(np.int32)


def _bias_kernel(t5_ref, bkp_ref, bks_ref, vs_ref, far_ref, op_ref, os_ref):
    rep = A_HEADS // A_KV_HEADS

    def lookup(bk, h):
        acc = jnp.zeros(bk.shape, F32)
        for b in range(N_BUCKETS):
            acc = jnp.where(bk == b, t5_ref[b, h], acc)
        return acc

    for h in range(A_HEADS):
        half = (h % 2) * Q_BLOCK
        op_ref[h // 2, :, half:half + Q_BLOCK] = (lookup(bkp_ref[...], h) - t5_ref[far_ref[0], h]) * LOG2E
        os_ref[h] = jnp.where(vs_ref[h // rep] > 0, lookup(bks_ref[...], h), NEG)


def _bias_tables(t5_bias, t_prompt, past, t_new):
    c = np.arange(2 * Q_BLOCK)[:, None]
    r = np.arange(Q_BLOCK)[None, :]
    bkp = _bucket_np(Q_BLOCK + r - c)
    far = _bucket_np(np.arange(Q_BLOCK + 1, max(t_prompt, past + t_new) + Q_BLOCK))
    assert (far == far[0]).all(), "bias must be constant beyond one query block"
    nk = past + LANE
    tq = np.arange(SUBLANE)[:, None]
    col = np.arange(2 * nk)[None, :]
    s_pos, s_grp = col // A_KV_HEADS, col % A_KV_HEADS
    bks = _bucket_np(past + tq - s_pos)
    vis = np.stack([(s_grp == g) & (s_pos - past <= tq) for g in range(A_KV_HEADS)]).astype(np.int32)
    return pl.pallas_call(
        _bias_kernel,
        in_specs=[pl.BlockSpec(memory_space=pltpu.SMEM),
                  pl.BlockSpec(memory_space=pltpu.VMEM),
                  pl.BlockSpec(memory_space=pltpu.VMEM),
                  pl.BlockSpec(memory_space=pltpu.VMEM),
                  pl.BlockSpec(memory_space=pltpu.SMEM)],
        out_specs=[pl.BlockSpec(memory_space=pltpu.VMEM), pl.BlockSpec(memory_space=pltpu.VMEM)],
        out_shape=[jax.ShapeDtypeStruct((A_HEADS // 2, 2 * Q_BLOCK, 2 * Q_BLOCK), F32),
                   jax.ShapeDtypeStruct((A_HEADS, SUBLANE, 2 * nk), F32)],
        name="t5_bias_tables",
    )(t5_bias, jnp.asarray(bkp), jnp.asarray(bks), jnp.asarray(vis), jnp.asarray(far[:1]))


def _sort_key(score):
    bits = lax.bitcast_convert_type(score, jnp.int32)
    return jnp.where(bits < 0, bits ^ jnp.int32(0x7FFFFFFF), bits)


def _kth_key(count_ge, shape, k_sel):
    def body(it, ans_u):
        cand_u = ans_u | lax.shift_left(jnp.int32(1), jnp.int32(31) - it)
        cnt = count_ge(cand_u ^ jnp.int32(INT_MIN))
        return jnp.where(cnt >= k_sel, cand_u, ans_u)

    ans_u = lax.fori_loop(0, 32, body, jnp.zeros(shape, jnp.int32))
    return ans_u ^ jnp.int32(INT_MIN)


def _triangle(below):
    r = lax.broadcasted_iota(jnp.int32, (LANE, LANE), 0)
    c = lax.broadcasted_iota(jnp.int32, (LANE, LANE), 1)
    return jnp.where((c < r) if below else (r < c), 1.0, 0.0).astype(BF16)


def _select_tiles(key_tiles, thr, k_sel, key_axis):
    count = lambda x: jnp.sum(x, axis=key_axis, keepdims=True)
    c_gt = jnp.zeros(thr.shape, F32)
    for kt in key_tiles:
        c_gt = c_gt + count(jnp.where(kt > thr, 1.0, 0.0))
    need = k_sel - c_gt
    tri = _triangle(below=(key_axis == 0))
    run = jnp.zeros(thr.shape, F32)
    out = []
    for kt in key_tiles:
        eq = jnp.where(kt == thr, 1.0, 0.0)
        eqb = eq.astype(BF16)
        before = (_dot(tri, eqb) if key_axis == 0 else _dot(eqb, tri)) + run
        out.append(jnp.where(kt > thr, 1.0, jnp.where(before < need, eq, 0.0)))
        run = run + count(eq)
    return out


FAR_STEP = 2


def _dsa_prompt_kernel(qn_ref, qi_ref, qtail_ref, k_ref, v_ref, ktail_ref, kg_ref, bias_ref,
                       ao_ref, kn_ref, vo_ref,
                       kb_ref, vt_ref, kib_ref, key_ref, negb_ref, *, T, k_sel, hd):
    i = pl.program_id(1)
    nb = T // Q_BLOCK
    rep = A_HEADS // A_KV_HEADS
    near_w = 2 * LANE

    @pl.when(i == 0)
    def _():
        kb_ref[0:LANE, :] = jnp.zeros((LANE, kb_ref.shape[1]), BF16)
        vt_ref[:, 0:LANE] = jnp.zeros((vt_ref.shape[0], LANE), BF16)
        kib_ref[0:LANE, :] = jnp.zeros((LANE, IDX_DIM), BF16)
        for g in range(A_KV_HEADS):
            kn = _rms(k_ref[:, g * hd:(g + 1) * hd], kg_ref[...])
            kn_ref[pl.ds(g, T, stride=A_KV_HEADS), :] = kn
            vo_ref[pl.ds(g, T, stride=A_KV_HEADS), :] = v_ref[:, g * hd:(g + 1) * hd]
            kb_ref[LANE:, g * hd:(g + 1) * hd] = kn.astype(BF16)
        vt_ref[:, LANE:] = v_ref[...].T.astype(BF16)
        kib_ref[LANE:, :] = ktail_ref[:, TAIL_KI:TAIL_KI + IDX_DIM].astype(BF16)

    wi_t = qtail_ref[...].T[TAIL_WI:TAIL_WI + IDX_HEADS, :] * (IDX_HEADS ** -0.5) * (IDX_DIM ** -0.5)
    qi_pairs = [jnp.concatenate([qi_ref[:, (2 * a + hh) * IDX_DIM:(2 * a + hh + 1) * IDX_DIM] for hh in range(2)],
                                axis=0) for a in range(IDX_HEADS // 2)]
    qn_pairs = [jnp.concatenate([qn_ref[:, (2 * a + hh) * hd:(2 * a + hh + 1) * hd] for hh in range(2)], axis=0)
                for a in range(A_HEADS // 2)]

    near0 = pl.multiple_of(i * LANE, LANE)
    ki_near = kib_ref[pl.ds(near0, near_w), :]
    k_near = kb_ref[pl.ds(near0, near_w), :]
    vt_near = vt_ref[:, pl.ds(near0, near_w)]
    kc = lax.broadcasted_iota(jnp.int32, (near_w, Q_BLOCK), 0)
    qr = lax.broadcasted_iota(jnp.int32, (near_w, Q_BLOCK), 1)
    near_ok = jnp.where(kc <= qr + LANE, 1.0, 0.0) * jnp.maximum(jnp.where(kc >= LANE, 1.0, 0.0),
                                                                  (i >= 1).astype(F32))

    def scores(ki):
        sc = jnp.zeros((ki.shape[0], Q_BLOCK), F32)
        for a in range(IDX_HEADS // 2):
            s2 = _dot_nt(ki, qi_pairs[a])
            for hh in range(2):
                j = 2 * a + hh
                sc = sc + wi_t[j:j + 1, :] * jnp.maximum(s2[:, hh * Q_BLOCK:(hh + 1) * Q_BLOCK], 0.0)
        return sc

    def body(wf):
        fw = wf * LANE
        w_all = fw + near_w
        if w_all <= k_sel:
            negb_ref[0:w_all, :] = jnp.where(near_ok > 0.0, 0.0, NEG)
            attend(wf)
            return
        key_ref[fw:w_all, :] = _sort_key(jnp.where(near_ok > 0.0, scores(ki_near) + 0.0, NEG))
        if wf:
            far_ok = lax.broadcasted_iota(jnp.int32, (fw, Q_BLOCK), 0) < (i - 1) * LANE
            key_ref[0:fw, :] = _sort_key(jnp.where(far_ok, scores(kib_ref[LANE:LANE + fw, :]) + 0.0, NEG))

        def count_ge(cand):
            return _reduce_rows(jnp.where(key_ref[0:w_all, :] >= cand, 1.0, 0.0), jnp.sum)

        thr = _kth_key(count_ge, (1, Q_BLOCK), float(k_sel))
        sel = _select_tiles([key_ref[t * LANE:(t + 1) * LANE, :] for t in range(wf + 2)], thr, float(k_sel), 0)
        for t in range(wf):
            ok = sel[t] * (t < i - 1).astype(F32)
            negb_ref[t * LANE:(t + 1) * LANE, :] = jnp.where(ok > 0.0, 0.0, NEG)
        for t in range(2):
            ok = sel[wf + t] * near_ok[t * LANE:(t + 1) * LANE, :]
            negb_ref[fw + t * LANE:fw + (t + 1) * LANE, :] = jnp.where(ok > 0.0, 0.0, NEG)
        attend(wf)

    def attend(wf):
        fw = wf * LANE
        w_all = fw + near_w
        for a in range(A_HEADS // 2):
            g = (2 * a) // rep
            cs = slice(g * hd, (g + 1) * hd)
            qn2 = qn_pairs[a]
            nb_near = negb_ref[fw:w_all, :]
            s_near = _dot_nt(k_near[:, cs], qn2) + bias_ref[a] + jnp.concatenate([nb_near, nb_near], axis=1)
            m = _reduce_rows(s_near, jnp.max)
            if wf:
                nb_far = negb_ref[0:fw, :]
                s_far = _dot_nt(kb_ref[LANE:LANE + fw, cs], qn2) + jnp.concatenate([nb_far, nb_far], axis=1)
                m = jnp.maximum(m, _reduce_rows(s_far, jnp.max))
            p_near = jnp.exp2(s_near - m)
            l = _reduce_rows(p_near, jnp.sum)
            acc = _dot(vt_near[cs, :], p_near.astype(BF16))
            if wf:
                p_far = jnp.exp2(s_far - m)
                l = l + _reduce_rows(p_far, jnp.sum)
                acc = acc + _dot(vt_ref[cs, LANE:LANE + fw], p_far.astype(BF16))
            out_t = acc / l
            for hh in range(2):
                h = 2 * a + hh
                ao_ref[:, h * hd:(h + 1) * hd] = out_t[:, hh * Q_BLOCK:(hh + 1) * Q_BLOCK].T.astype(ao_ref.dtype)

    variants = sorted({min(FAR_STEP * -(-x // FAR_STEP), nb) for x in range(max(nb - 1, 1))})
    wf_needed = jnp.minimum((jnp.maximum(i - 1, 0) + FAR_STEP - 1) // FAR_STEP * FAR_STEP, nb)
    for wf in variants:
        pl.when(wf_needed == wf)(functools.partial(body, wf))


def _dsa_prompt(p, qn, qib, kn_g, bias_p, *, n_seq, T, hd):
    rows = p.shape[0]
    nb = T // Q_BLOCK
    aw = A_HEADS * hd
    kvw = A_KV_HEADS * hd
    iw = IDX_HEADS * IDX_DIM
    k_sel = min(TOPK_MAX, T // 4)
    kcol = p.shape[1] - LANE - iw - 2 * kvw
    assert kcol % kvw == 0
    k_blk = kcol // kvw
    tail_blk = p.shape[1] // LANE - 1
    kern = functools.partial(_dsa_prompt_kernel, T=T, k_sel=k_sel, hd=hd)
    return pl.pallas_call(
        kern,
        grid=(n_seq, nb),
        in_specs=[pl.BlockSpec((Q_BLOCK, aw), lambda b, i: (b * nb + i, 0)),
                  pl.BlockSpec((Q_BLOCK, iw), lambda b, i: (b * nb + i, 0)),
                  pl.BlockSpec((Q_BLOCK, LANE), lambda b, i: (b * nb + i, tail_blk)),
                  pl.BlockSpec((T, kvw), lambda b, i: (b, k_blk)),
                  pl.BlockSpec((T, kvw), lambda b, i: (b, k_blk + 1)),
                  pl.BlockSpec((T, LANE), lambda b, i: (b, tail_blk)),
                  pl.BlockSpec((1, hd), lambda b, i: (0, 0)),
                  pl.BlockSpec((A_HEADS // 2, 2 * Q_BLOCK, 2 * Q_BLOCK), lambda b, i: (0, 0, 0))],
        out_specs=[pl.BlockSpec((Q_BLOCK, aw), lambda b, i: (b * nb + i, 0)),
                   pl.BlockSpec((A_KV_HEADS * T, hd), lambda b, i: (b, 0)),
                   pl.BlockSpec((A_KV_HEADS * T, hd), lambda b, i: (b, 0))],
        out_shape=[jax.ShapeDtypeStruct((rows, aw), BF16),
                   jax.ShapeDtypeStruct((A_KV_HEADS * rows, hd), F32),
                   jax.ShapeDtypeStruct((A_KV_HEADS * rows, hd), F32)],
        scratch_shapes=[pltpu.VMEM((T + LANE, kvw), BF16), pltpu.VMEM((kvw, T + LANE), BF16),
                        pltpu.VMEM((T + LANE, IDX_DIM), BF16),
                        pltpu.VMEM((T + 2 * LANE, Q_BLOCK), jnp.int32),
                        pltpu.VMEM((T + 2 * LANE, Q_BLOCK), F32)],
        compiler_params=_params(("parallel", "arbitrary"), VMEM_LIMIT),
        name="dsa_prompt",
    )(qn, qib, p, p, p, p, kn_g.reshape(1, hd), bias_p)


def _pad_rows(x, n):
    return jnp.concatenate([x, jnp.zeros((n - x.shape[0], x.shape[1]), x.dtype)], axis=0)


def _idx_sample_kernel(pt_ref, qi_ref, wi_ref, kinew_ref, *rest, n_pages, t_new):
    ip, sc_ref = rest[:n_pages], rest[n_pages]
    del pt_ref
    qi = qi_ref[...].astype(BF16)
    wi = wi_ref[...] * (IDX_HEADS ** -0.5) * (IDX_DIM ** -0.5)
    tq = lax.broadcasted_iota(jnp.int32, (SUBLANE, LANE), 0)
    lane = lax.broadcasted_iota(jnp.int32, (SUBLANE, LANE), 1)
    for t in range(n_pages + 1):
        r = _dot(qi, ip[t][...].astype(BF16)) if t < n_pages else _dot_nt(qi, _pad_rows(kinew_ref[...], LANE).astype(BF16))
        sc = jnp.zeros((SUBLANE, LANE), F32)
        for j in range(IDX_HEADS):
            sc = sc + wi[j * SUBLANE:(j + 1) * SUBLANE, :] * jnp.maximum(r[j * SUBLANE:(j + 1) * SUBLANE, :], 0.0)
        sc = sc + 0.0
        if t == n_pages:
            sc = jnp.where(lane <= tq, sc, jnp.where(lane < t_new, NEG, -jnp.inf))
        sc_ref[:, t * LANE:(t + 1) * LANE] = sc


SELECT_ROWS = 256


def _select_kernel(sc_ref, sel_ref, key_ref, *, k_sel):
    rows, nk = sc_ref.shape
    key_ref[...] = _sort_key(sc_ref[...])

    def count_ge(cand):
        return jnp.sum(jnp.where(key_ref[...] >= cand, 1.0, 0.0), axis=1, keepdims=True)

    thr = _kth_key(count_ge, (rows, 1), float(k_sel))
    sel = _select_tiles([key_ref[:, t * LANE:(t + 1) * LANE] for t in range(nk // LANE)], thr, float(k_sel), 1)
    for t in range(nk // LANE):
        sel_ref[:, t * LANE:(t + 1) * LANE] = sel[t]


def _attn_sample_kernel(pt_ref, qa_ref, sel_ref, knew_ref, vnew_ref, qg_ref, kg_ref, bias_ref, *rest,
                        n_pages, hd):
    kp, vp = rest[:n_pages], rest[n_pages:2 * n_pages]
    ao_ref, kn_ref = rest[2 * n_pages:]
    del pt_ref
    nt = n_pages + 1
    pw = A_KV_HEADS * PAGE_SIZE
    scale = hd ** -0.5
    rows_q = qa_ref.shape[0]

    kn_new = _rms(knew_ref[...], kg_ref[...])
    kn_ref[...] = kn_new
    k_new = _pad_rows(kn_new, pw).astype(BF16)
    v_new = _pad_rows(vnew_ref[...], pw).astype(BF16)
    qn = _rms(qa_ref[...], qg_ref[...]).astype(BF16)

    tok = lax.broadcasted_iota(jnp.int32, (PAGE_SIZE, pw), 0)
    col = lax.broadcasted_iota(jnp.int32, (PAGE_SIZE, pw), 1)
    spread = jnp.where(col // A_KV_HEADS == tok, 1.0, 0.0).astype(BF16)
    sel_rows = jnp.concatenate([sel_ref[:, t * LANE:(t + 1) * LANE] for t in range(nt)], axis=0)
    negb = (_dot(sel_rows.astype(BF16), spread) - 1.0) * (-NEG)

    logits = []
    for t in range(nt):
        kt = kp[t][...].astype(BF16) if t < n_pages else k_new
        nb = jnp.concatenate([negb[t * SUBLANE:(t + 1) * SUBLANE, :]] * (rows_q // SUBLANE), axis=0)
        logits.append(_dot_nt(qn, kt) * scale + bias_ref[:, t * pw:(t + 1) * pw] + nb)
    mx = logits[0]
    for t in range(1, nt):
        mx = jnp.maximum(mx, logits[t])
    m = jnp.max(mx, axis=-1, keepdims=True)
    psum = jnp.zeros((rows_q, pw), F32)
    acc = jnp.zeros((rows_q, hd), F32)
    for t in range(nt):
        pr = jnp.exp(logits[t] - m)
        psum = psum + pr
        acc = acc + _dot(pr.astype(BF16), vp[t][...].astype(BF16) if t < n_pages else v_new)
    ao_ref[...] = acc / jnp.sum(psum, axis=-1, keepdims=True)


def _dsa_sample(qa, qi, wi, knew, vnew, kinew, qn_g, kn_g, bias_s, cache_k, cache_v, cache_i, page_table, *, t_new, hd):
    n_seq, n_pages = page_table.shape
    nk = n_pages * PAGE_SIZE + LANE
    k_sel = min(TOPK_MAX, (n_pages * PAGE_SIZE + t_new) // 4)
    pt = page_table.reshape(-1)

    def page_spec(arr, t):
        return pl.BlockSpec((None,) + arr.shape[1:], lambda b, pt: (pt[b * n_pages + t], 0, 0))

    seq_spec = lambda arr: pl.BlockSpec((None,) + arr.shape[1:], lambda b, pt: (b, 0, 0))
    const_spec = lambda arr: pl.BlockSpec(arr.shape, lambda b, pt: (0, 0))

    scores = pl.pallas_call(
        functools.partial(_idx_sample_kernel, n_pages=n_pages, t_new=t_new),
        grid_spec=pltpu.PrefetchScalarGridSpec(
            num_scalar_prefetch=1, grid=(n_seq,),
            in_specs=[seq_spec(qi), seq_spec(wi), seq_spec(kinew)] + [page_spec(cache_i, t) for t in range(n_pages)],
            out_specs=pl.BlockSpec((SUBLANE, nk), lambda b, pt: (b, 0))),
        out_shape=jax.ShapeDtypeStruct((n_seq * SUBLANE, nk), F32),
        compiler_params=_params(("arbitrary",), VMEM_LIMIT),
        name="idx_sample",
    )(pt, qi, wi, kinew, *([cache_i] * n_pages))

    rows = scores.shape[0]
    tr = SELECT_ROWS if rows % SELECT_ROWS == 0 else rows
    sel = pl.pallas_call(
        functools.partial(_select_kernel, k_sel=k_sel),
        grid=(rows // tr,),
        in_specs=[pl.BlockSpec((tr, nk), lambda r: (r, 0))],
        out_specs=pl.BlockSpec((tr, nk), lambda r: (r, 0)),
        out_shape=jax.ShapeDtypeStruct((rows, nk), F32),
        scratch_shapes=[pltpu.VMEM((tr, nk), jnp.int32)],
        compiler_params=_params(("parallel",), VMEM_LIMIT),
        name="select_sample",
    )(scores)

    qg, kg = qn_g.reshape(1, hd), kn_g.reshape(1, hd)
    return pl.pallas_call(
        functools.partial(_attn_sample_kernel, n_pages=n_pages, hd=hd),
        grid_spec=pltpu.PrefetchScalarGridSpec(
            num_scalar_prefetch=1, grid=(n_seq,),
            in_specs=[seq_spec(qa), pl.BlockSpec((SUBLANE, nk), lambda b, pt: (b, 0)), seq_spec(knew), seq_spec(vnew),
                      const_spec(qg), const_spec(kg), const_spec(bias_s)]
                     + [page_spec(cache_k, t) for t in range(n_pages)]
                     + [page_spec(cache_v, t) for t in range(n_pages)],
            out_specs=[seq_spec(qa), seq_spec(knew)]),
        out_shape=[jax.ShapeDtypeStruct(qa.shape, F32), jax.ShapeDtypeStruct(knew.shape, F32)],
        compiler_params=_params(("arbitrary",), VMEM_LIMIT),
        name="attn_sample",
    )(pt, qa, sel, knew, vnew, qg, kg, bias_s, *([cache_k] * n_pages), *([cache_v] * n_pages))


def _outproj_kernel(x_ref, mo_ref, ao_ref, gt_ref, sh_ref, sc_ref, g_ref, wm_ref, wa_ref, o_ref, h_ref):
    y = _dot(mo_ref[...].astype(BF16), wm_ref[...]) + _dot(ao_ref[...].astype(BF16), wa_ref[...])
    x = x_ref[...] + gt_ref[...] * y
    o_ref[...] = x
    h_ref[...] = _adaln(x, g_ref[...], sc_ref[...], sh_ref[...])


def _outproj(x, mo, ao, gt, w_out, sh, sc, g):
    m, d = x.shape
    mw, aw = mo.shape[1], ao.shape[1]
    tm = _row_tile(m)
    row = lambda w: pl.BlockSpec((tm, w), lambda i, j: (i, 0))
    return pl.pallas_call(
        _outproj_kernel,
        grid=(m // tm, 1),
        in_specs=[row(d), row(mw), row(aw), _mod_spec(gt, m, tm, d), _mod_spec(sh, m, tm, d), _mod_spec(sc, m, tm, d),
                  pl.BlockSpec((1, d), lambda i, j: (0, 0)),
                  pl.BlockSpec((mw, d), lambda i, j: (0, 0)),
                  pl.BlockSpec((aw, d), lambda i, j: (0, 0))],
        out_specs=[row(d), row(d)],
        out_shape=[jax.ShapeDtypeStruct((m, d), F32), jax.ShapeDtypeStruct((m, d), BF16)],
        compiler_params=_params(("parallel", "arbitrary"), VMEM_LIMIT),
        name="outproj",
    )(x, mo, ao, gt[0], sh[0], sc[0], g.reshape(1, d), w_out[:mw], w_out[mw:])


def _pad_tokens(a, t_pad):
    pad = [(0, 0)] * a.ndim
    pad[1] = (0, t_pad - a.shape[1])
    return jnp.pad(a, pad)


def _layer(x3, mods, lw, bias_p, bias_s, rec_state, past):
    n_seq, T, d = x3.shape
    x = x3.reshape(n_seq * T, d)
    dk, dv = rec_state[0].shape[2], rec_state[0].shape[3]
    mw = M_HEADS * dv
    hd = lw["q_norm_g"].shape[0]
    aw, kvw = A_HEADS * hd, A_KV_HEADS * hd
    iw = IDX_HEADS * IDX_DIM
    sh1, sc1, g1, sh2, sc2, g2, sh3, sc3, g3 = [(mods, k) for k in range(N_MOD)]

    x, ffn1_w = _ffn(x, g1, *lw["ffn1_w"], norm=(sh1, sc1, lw["ffn1_norm_g"]))
    qkw = 2 * M_HEADS * dk
    a_off = qkw + 2 * mw
    c0, n0, m0, conv_buf = rec_state
    prev8 = jnp.zeros((n_seq, SUBLANE, qkw), F32).at[:, SUBLANE - (CONV_W - 1):, :].set(conv_buf)
    prompt = None
    if past is None:
        prompt = (lw["q_norm_g"], (a_off, aw, a_off + aw + 2 * kvw, iw, hd))
    p, qn, qib = _proj(x, sh2, sc2, lw["mix_norm_g"], lw["w_in"], prompt)
    npk = p.shape[1]
    gate_row = jnp.zeros((1, LANE), F32).at[0, TAIL_IG:TAIL_IG + 2 * M_HEADS].set(lw["mlstm_gate_b"])
    m0_pad = jnp.zeros((n_seq, 1, LANE), F32).at[:, 0, :M_HEADS].set(m0)
    p3 = p.reshape(n_seq, T, npk)
    v_a = p3[:, :, a_off + aw + kvw:a_off + aw + 2 * kvw]
    ki = p3[:, :, npk - LANE + TAIL_KI:npk - LANE + TAIL_KI + IDX_DIM]
    conv_new = p3[:, T - (CONV_W - 1):, :qkw]

    if past is None:
        L = MLSTM_CHUNK if T % MLSTM_CHUNK == 0 else (M_CHUNK if T % M_CHUNK == 0 else T)
        mo, c_new, n_new, m_new = _mlstm(p3, prev8, lw["mlstm_conv_w"], gate_row, lw["mlstm_out_g"], c0, n0, m0_pad,
                                         L=L, valid=L, seqs_per_step=1, out_dtype=BF16)
        mo = mo.reshape(n_seq * T, mw)
        ao, k_n, v_n = _dsa_prompt(p, qn, qib, lw["k_norm_g"], bias_p, n_seq=n_seq, T=T, hd=hd)
        k_n = k_n.reshape(n_seq, T, A_KV_HEADS, hd)
        v_n = v_n.reshape(n_seq, T, A_KV_HEADS, hd)
    else:
        tp = SUBLANE
        assert T <= tp
        mo, c_new, n_new, m_new = _mlstm(_pad_tokens(p3, tp), prev8, lw["mlstm_conv_w"], gate_row, lw["mlstm_out_g"],
                                         c0, n0, m0_pad, L=tp, valid=T,
                                         seqs_per_step=SUBLANE if n_seq % SUBLANE == 0 else 1, out_dtype=F32)
        mo = mo[:, :T].reshape(n_seq * T, mw)
        cache_k, cache_v, cache_i, page_table = past
        rep = A_HEADS // A_KV_HEADS
        qa = _pad_tokens(p3[:, :, a_off:a_off + aw], tp).reshape(n_seq, tp, A_KV_HEADS, rep, hd)
        qa = qa.transpose(0, 2, 3, 1, 4).reshape(n_seq, A_HEADS * tp, hd)
        qi = _pad_tokens(p3[:, :, a_off + aw + 2 * kvw:a_off + aw + 2 * kvw + iw], tp)
        qi = qi.reshape(n_seq, tp, IDX_HEADS, IDX_DIM).transpose(0, 2, 1, 3).reshape(n_seq, IDX_HEADS * tp, IDX_DIM)
        wi = _pad_tokens(p3[:, :, npk - LANE + TAIL_WI:npk - LANE + TAIL_WI + IDX_HEADS], tp)
        wi = wi.transpose(0, 2, 1).reshape(n_seq, IDX_HEADS * tp, 1)
        knew = _pad_tokens(p3[:, :, a_off + aw:a_off + aw + kvw], tp).reshape(n_seq, tp * A_KV_HEADS, hd)
        vnew = _pad_tokens(v_a, tp).reshape(n_seq, tp * A_KV_HEADS, hd)
        n_pool = cache_k.shape[0]
        ao4, k_n8 = _dsa_sample(qa, qi, wi, knew, vnew, _pad_tokens(ki, tp), lw["q_norm_g"], lw["k_norm_g"], bias_s,
                                cache_k.reshape(n_pool, PAGE_SIZE * A_KV_HEADS, hd),
                                cache_v.reshape(n_pool, PAGE_SIZE * A_KV_HEADS, hd),
                                jnp.swapaxes(cache_i, 1, 2), page_table, t_new=T, hd=hd)
        k_n8 = k_n8.reshape(n_seq, tp, kvw)
        ao = ao4.reshape(n_seq, A_KV_HEADS, rep, tp, hd)[:, :, :, :T].transpose(0, 3, 1, 2, 4).reshape(n_seq * T, aw)
        k_n = k_n8[:, :T].reshape(n_seq, T, A_KV_HEADS, hd)
        v_n = v_a.reshape(n_seq, T, A_KV_HEADS, hd)

    x, h3 = _outproj(x, mo, ao, g2, lw["w_out"], sh3, sc3, lw["ffn2_norm_g"])
    x, ffn2_w = _ffn(x, g3, *lw["ffn2_w"], h=h3)
    state = (k_n, v_n, ki, c_new, n_new, m_new[:, 0, :M_HEADS], conv_new)
    return x.reshape(n_seq, T, d), state, dict(lw, ffn1_w=ffn1_w, ffn2_w=ffn2_w)


def kernel(x_prompt, x_sample, c_prompt, c_sample, cache_k, cache_v, cache_idx_k, page_table, state_C, state_n,
           state_m, state_conv, ffn1_norm_g, ffn1_w_gate, ffn1_w_up, ffn1_w_down, mix_norm_g, w_in, mlstm_conv_w,
           mlstm_gate_b, mlstm_out_g, q_norm_g, k_norm_g, t5_bias, w_out, ffn2_norm_g, ffn2_w_gate, ffn2_w_up,
           ffn2_w_down, w_ada, b_ada):
    depth = w_in.shape[0]
    bp, tp_len, d = x_prompt.shape
    bs, ts_len, _ = x_sample.shape
    dk, dv = state_C.shape[3], state_C.shape[4]
    mw = M_HEADS * dv
    hd = q_norm_g.shape[1]
    past_len = page_table.shape[1] * PAGE_SIZE
    bias_p, bias_s = _bias_tables(t5_bias, tp_len, past_len, ts_len)
    bias_s = bias_s.reshape(A_HEADS * SUBLANE, A_KV_HEADS * (past_len + LANE))

    xp, xs = x_prompt, x_sample
    st_p, st_s = [], []
    for l in range(depth):
        lw = dict(ffn1_norm_g=ffn1_norm_g[l], ffn1_w=(ffn1_w_gate[l], ffn1_w_up[l], ffn1_w_down[l]),
                  mix_norm_g=mix_norm_g[l], w_in=_pack_w_in(w_in[l], mw, A_HEADS * hd, A_KV_HEADS * hd),
                  mlstm_conv_w=mlstm_conv_w[l], mlstm_gate_b=mlstm_gate_b[l], mlstm_out_g=mlstm_out_g[l],
                  q_norm_g=q_norm_g[l], k_norm_g=k_norm_g[l], w_out=w_out[l].astype(BF16),
                  ffn2_norm_g=ffn2_norm_g[l], ffn2_w=(ffn2_w_gate[l], ffn2_w_up[l], ffn2_w_down[l]))
        n_s = bs * ts_len
        mods = _ada(jnp.concatenate([jnp.repeat(c_sample, ts_len, axis=0), c_prompt], axis=0), w_ada[l], b_ada[l])
        per_seq = tp_len % ROW_TILE == 0
        mods_p = mods[n_s:].reshape(bp, 1, N_MOD * d) if per_seq else jnp.repeat(mods[n_s:], tp_len, axis=0)
        init_s = (state_C[l], state_n[l], state_m[l], state_conv[l])
        xs, ss, lw = _layer(xs, mods, lw, bias_p, bias_s, init_s,
                            (cache_k[l], cache_v[l], cache_idx_k[l], page_table))
        init_p = (jnp.zeros((bp, M_HEADS, dk, dv), F32), jnp.zeros((bp, M_HEADS, dk), F32),
                  jnp.zeros((bp, M_HEADS), F32), jnp.zeros((bp, CONV_W - 1, 2 * M_HEADS * dk), F32))
        xp, sp, _ = _layer(xp, mods_p, lw, bias_p, bias_s, init_p, None)
        st_p.append(sp)
        st_s.append(ss)
    stack = lambda sts, k: sts[0][k][None] if depth == 1 else jnp.stack([s[k] for s in sts])
    return (xp, xs) + tuple(stack(st_p, k) for k in range(7)) + tuple(stack(st_s, k) for k in range(7))
```

```python
import functools
import math

import numpy as np
import jax
import jax.numpy as jnp
from jax import lax
from jax.experimental import pallas as pl
from jax.experimental.pallas import tpu as pltpu

F32 = jnp.float32
BF16 = jnp.bfloat16

M_HEADS = 4
CONV_W = 4
M_CHUNK = 64
MLSTM_CHUNK = 8 * M_CHUNK
A_HEADS = 8
A_KV_HEADS = 2
IDX_HEADS = 8
IDX_DIM = 64
TOPK_MAX = 256
Q_BLOCK = 128
PAGE_SIZE = 128
N_BUCKETS = 32
MAX_DISTANCE = 128
N_MOD = 9
EPS = 1e-6
NEG = -1e30

LANE = 128
SUBLANE = 8
ROW_TILE = 512
VMEM_LIMIT = 56 * 1024 * 1024

TAIL_KI = 0
TAIL_WI = IDX_DIM
TAIL_IG = IDX_DIM + IDX_HEADS
TAIL_FG = TAIL_IG + M_HEADS

INT_MIN = -2 ** 31
LOG2E = math.log2(math.e)


def _params(sem, vmem=None, flags=None):
    return pltpu.CompilerParams(dimension_semantics=sem, vmem_limit_bytes=vmem, flags=flags)


def _dot(a, b):
    return jnp.dot(a, b, preferred_element_type=F32)


def _dot_nt(a, b):
    return lax.dot_general(a, b, (((1,), (1,)), ((), ())), preferred_element_type=F32)


def _dot_tn(a, b):
    return lax.dot_general(a, b, (((0,), (0,)), ((), ())), preferred_element_type=F32)


def _rms(x, g):
    return x * lax.rsqrt(jnp.mean(x * x, axis=-1, keepdims=True) + EPS) * g


def _row_tile(n):
    return ROW_TILE if n % ROW_TILE == 0 else n


REDUCE_CHAINS = 4


def _reduce_rows(x, op):
    n, c = x.shape
    groups = n // SUBLANE
    chains = REDUCE_CHAINS if groups % REDUCE_CHAINS == 0 else 1
    x3 = x.reshape(groups, SUBLANE, c)
    per = groups // chains
    parts = [op(x3[k * per:(k + 1) * per], axis=0) for k in range(chains)]
    while len(parts) > 1:
        parts = [op(jnp.stack(parts[k:k + 2]), axis=0) for k in range(0, len(parts), 2)]
    return op(parts[0], axis=0, keepdims=True)


def _ada_kernel(c_ref, w_ref, b_ref, o_ref):
    o_ref[...] = _dot(c_ref[...].astype(BF16), w_ref[...].astype(BF16)) + b_ref[...]


def _ada(c_all, w_ada, b_ada):
    r, d = c_all.shape
    n = w_ada.shape[1]
    tn = 2048 if n % 2048 == 0 else n
    return pl.pallas_call(
        _ada_kernel,
        grid=(n // tn,),
        in_specs=[pl.BlockSpec((r, d), lambda j: (0, 0)),
                  pl.BlockSpec((d, tn), lambda j: (0, j)),
                  pl.BlockSpec((1, tn), lambda j: (0, j))],
        out_specs=pl.BlockSpec((r, tn), lambda j: (0, j)),
        out_shape=jax.ShapeDtypeStruct((r, n), F32),
        compiler_params=_params(("arbitrary",), VMEM_LIMIT),
        name="ada",
    )(c_all, w_ada, b_ada.reshape(1, n))


def _adaln(x, g, sc, sh):
    return (_rms(x, g) * (1.0 + sc) + sh).astype(BF16)


def _ffn_kernel(x_ref, gt_ref, *rest, prenormed, emit):
    if prenormed:
        hin_ref, wg_ref, wu_ref, wd_ref, o_ref, *rest = rest
    else:
        sh_ref, sc_ref, g_ref, wg_ref, wu_ref, wd_ref, o_ref, *rest = rest
    bf16_out, rest = (rest[:3], rest[3:]) if emit else ((), rest)
    acc_ref = rest[-1]
    j = pl.program_id(1)

    @pl.when(j == 0)
    def _():
        if not prenormed:
            rest[0][...] = _adaln(x_ref[...], g_ref[...], sc_ref[...], sh_ref[...])
        acc_ref[...] = jnp.zeros_like(acc_ref)

    wg, wu, wd = wg_ref[...], wu_ref[...], wd_ref[...]
    if emit:
        wg, wu, wd = wg.astype(BF16), wu.astype(BF16), wd.astype(BF16)
        for ref, w in zip(bf16_out, (wg, wu, wd)):
            ref[...] = w
    h = hin_ref[...] if prenormed else rest[0][...]
    a = _dot(h, wg)
    u = _dot(h, wu)
    act = (a * jax.nn.sigmoid(a) * u).astype(BF16)
    acc_ref[...] += _dot(act, wd)

    @pl.when(j == pl.num_programs(1) - 1)
    def _():
        o_ref[...] = x_ref[...] + 0.5 * gt_ref[...] * acc_ref[...]


def _mod_spec(mod, m, tm, d):
    arr, k = mod
    if arr.ndim == 2:
        return pl.BlockSpec((tm, d), lambda i, j: (i, k))
    tiles_per_seq = (m // arr.shape[0]) // tm
    return pl.BlockSpec((None, 1, d), lambda i, j: (i // tiles_per_seq, 0, k))


def _ffn(x, gt, wg, wu, wd, *, norm=None, h=None):
    m, d = x.shape
    f = wg.shape[1]
    tm = _row_tile(m)
    emit = wg.dtype != BF16
    assert not emit or m == tm
    tf = (256 if emit else 512) if f % 512 == 0 else f
    row = pl.BlockSpec((tm, d), lambda i, j: (i, 0))
    w_specs = [pl.BlockSpec((d, tf), lambda i, j: (0, j)),
               pl.BlockSpec((d, tf), lambda i, j: (0, j)),
               pl.BlockSpec((tf, d), lambda i, j: (j, 0))]
    if h is None:
        sh, sc, g = norm
        pre_specs = [_mod_spec(sh, m, tm, d), _mod_spec(sc, m, tm, d), pl.BlockSpec((1, d), lambda i, j: (0, 0))]
        pre_args = [sh[0], sc[0], g.reshape(1, d)]
        scratch = [pltpu.VMEM((tm, d), BF16)]
    else:
        pre_specs, pre_args, scratch = [row], [h], []
    out_specs = [row]
    out_shape = [jax.ShapeDtypeStruct((m, d), F32)]
    if emit:
        out_specs += w_specs
        out_shape += [jax.ShapeDtypeStruct(w.shape, BF16) for w in (wg, wu, wd)]
    y, *wb = pl.pallas_call(
        functools.partial(_ffn_kernel, prenormed=h is not None, emit=emit),
        grid=(m // tm, f // tf),
        in_specs=[row, _mod_spec(gt, m, tm, d)] + pre_specs + w_specs,
        out_specs=out_specs,
        out_shape=out_shape,
        scratch_shapes=scratch + [pltpu.VMEM((tm, d), F32)],
        compiler_params=_params(("parallel", "arbitrary"), VMEM_LIMIT),
        name="ffn",
    )(x, gt[0], *pre_args, wg, wu, wd)
    return y, (tuple(wb) if emit else (wg, wu, wd))


def _proj_kernel(x_ref, sh_ref, sc_ref, g_ref, wm_ref, wt_ref, *rest, q_cols):
    h = _adaln(x_ref[...], g_ref[...], sc_ref[...], sh_ref[...])
    nm = wm_ref.shape[1]
    o_ref = rest[1] if q_cols else rest[0]
    o_ref[:, 0:nm] = _dot(h, wm_ref[...])
    p = _dot(h, wt_ref[...])
    o_ref[:, nm:] = p
    if not q_cols:
        return
    qg_ref, _, qn_ref, qib_ref = rest
    a0, aw, i0, iw, hd = q_cols
    for hh in range(aw // hd):
        qn = _rms(p[:, a0 - nm + hh * hd:a0 - nm + (hh + 1) * hd], qg_ref[...]) * (hd ** -0.5 * LOG2E)
        qn_ref[:, hh * hd:(hh + 1) * hd] = qn.astype(BF16)
    qib_ref[...] = p[:, i0 - nm:i0 - nm + iw].astype(BF16)


PROJ_ROW_TILE = 256


def _proj(x, sh, sc, g, w, prompt=None):
    m, d = x.shape
    wm, wt = w
    n = wm.shape[1] + wt.shape[1]
    tm = PROJ_ROW_TILE if m % PROJ_ROW_TILE == 0 else m
    row = lambda width: pl.BlockSpec((tm, width), lambda i, j: (i, 0))
    resident = lambda arr: pl.BlockSpec(arr.shape, lambda i, j: (0, 0), pipeline_mode=pl.Buffered(1))
    in_specs = [row(d), _mod_spec(sh, m, tm, d), _mod_spec(sc, m, tm, d), pl.BlockSpec((1, d), lambda i, j: (0, 0)),
                resident(wm), resident(wt)]
    args = [x, sh[0], sc[0], g.reshape(1, d), wm, wt]
    out_specs, out_shape = [row(n)], [jax.ShapeDtypeStruct((m, n), F32)]
    q_cols = None
    if prompt:
        qn_g, q_cols = prompt
        a0, aw, i0, iw, hd = q_cols
        assert a0 >= wm.shape[1]
        in_specs.append(pl.BlockSpec((1, hd), lambda i, j: (0, 0)))
        args.append(qn_g.reshape(1, hd))
        out_specs += [row(aw), row(iw)]
        out_shape += [jax.ShapeDtypeStruct((m, aw), BF16), jax.ShapeDtypeStruct((m, iw), BF16)]
    out = pl.pallas_call(
        functools.partial(_proj_kernel, q_cols=q_cols),
        grid=(m // tm, 1),
        in_specs=in_specs,
        out_specs=out_specs,
        out_shape=out_shape,
        compiler_params=_params(("parallel", "arbitrary"), VMEM_LIMIT),
        name="proj",
    )(*args)
    return out if prompt else (out[0], None, None)


def _pack_w_in(w_in, mw, aw, kvw):
    d = w_in.shape[0]
    o = np.cumsum([0, mw, mw, mw, mw, M_HEADS, M_HEADS, aw, kvw, kvw, IDX_HEADS * IDX_DIM, IDX_DIM, IDX_HEADS])
    seg = lambda k: w_in[:, o[k]:o[k + 1]]
    pad = (-o[-1]) % LANE
    tail = jnp.concatenate([seg(k) for k in (6, 7, 8, 9, 10, 11, 4, 5)] + [jnp.zeros((d, pad), w_in.dtype)], axis=1)
    return w_in[:, :o[4]].astype(BF16), tail.astype(BF16)


def _log_sigmoid(x):
    return jnp.minimum(x, 0.0) - jnp.log1p(jnp.exp(-jnp.abs(x)))


def _conv_silu(xbuf_ref, x_new, cw_ref):
    rows = x_new.shape[0]
    xbuf_ref[SUBLANE:SUBLANE + rows, :] = x_new
    base = SUBLANE - (CONV_W - 1)
    y = xbuf_ref[base:base + rows, :] * cw_ref[0:1, :]
    for j in range(1, CONV_W):
        y = y + xbuf_ref[base + j:base + j + rows, :] * cw_ref[j:j + 1, :]
    tail_rows = xbuf_ref[rows:rows + SUBLANE, :]
    xbuf_ref[0:SUBLANE, :] = tail_rows
    return y * jax.nn.sigmoid(y)


def _mlstm_kernel(qk_ref, v_ref, o_ref, tail_ref, prev_ref, cw_ref, gb_ref, og_ref, c0_ref, n0_ref, m0_ref,
                  mo_ref, cst_ref, nst_ref, mst_ref, xbuf_ref, *, L, valid, dk, dv, one_chunk):
    if one_chunk:
        xbuf_ref[:, 0:SUBLANE, :] = prev_ref[...]
        mst_ref[...] = m0_ref[...]
        state_in = (c0_ref, n0_ref, m0_ref)
    else:
        state_in = (cst_ref, nst_ref, mst_ref)

        @pl.when(pl.program_id(1) == 0)
        def _():
            xbuf_ref[:, 0:SUBLANE, :] = prev_ref[...]
            cst_ref[...] = c0_ref[...]
            nst_ref[...] = n0_ref[...]
            mst_ref[...] = m0_ref[...]

    for sb in range(qk_ref.shape[0]):
        _mlstm_one(qk_ref.at[sb], v_ref.at[sb], o_ref.at[sb], tail_ref.at[sb], cw_ref, gb_ref, og_ref, mo_ref.at[sb],
                   *(r.at[sb] for r in state_in), cst_ref.at[sb], nst_ref.at[sb], mst_ref.at[sb], xbuf_ref.at[sb],
                   L=L, valid=valid, dk=dk, dv=dv)


def _mlstm_one(qk_ref, v_ref, o_ref, tail_ref, cw_ref, gb_ref, og_ref, mo_ref, cin_ref, nin_ref, min_ref,
               cst_ref, nst_ref, mst_ref, xbuf_ref, *, L, valid, dk, dv):
    width = M_HEADS * dk
    qk = _conv_silu(xbuf_ref, qk_ref[...], cw_ref)

    gates = tail_ref[...] + gb_ref[...]
    row = lax.broadcasted_iota(jnp.int32, (L, LANE), 0)
    lf = _log_sigmoid(gates)
    ig = gates
    if valid < L:
        lf = jnp.where(row < valid, lf, 0.0)
        ig = jnp.where(row < valid, ig, -jnp.inf)
    bcum = lf
    s = 1
    while s < L:
        bcum = bcum + jnp.where(row >= s, pltpu.roll(bcum, s, axis=0), 0.0)
        s *= 2
    bmi_t = (pltpu.roll(ig, TAIL_FG - TAIL_IG, axis=1) - bcum).T

    ri = lax.broadcasted_iota(jnp.int32, (L, L), 0)
    ci = lax.broadcasted_iota(jnp.int32, (L, L), 1)
    causal = ri >= ci

    for h in range(M_HEADS):
        li, lfh = TAIL_IG + h, TAIL_FG + h
        a_col = bcum[:, lfh:lfh + 1]
        ig_col = ig[:, li:li + 1]
        b_row = bmi_t[lfh:lfh + 1, :]
        m0 = min_ref[:, h:h + 1]
        d_log = jnp.where(causal, a_col + b_row, -jnp.inf)
        s_log = a_col + m0
        m_col = jnp.maximum(s_log, jnp.max(d_log, axis=-1, keepdims=True))
        dw = jnp.exp(d_log - m_col)
        sw = jnp.exp(s_log - m_col)

        q = qk[:, h * dk:(h + 1) * dk]
        k = qk[:, width + h * dk:width + (h + 1) * dk] * (dk ** -0.5)
        v = v_ref[:, h * dv:(h + 1) * dv]
        qb, kb = q.astype(BF16), k.astype(BF16)
        c_prev = cin_ref[h]
        n_prev = nin_ref[h:h + 1, :]
        scores = _dot_nt(qb, kb) * dw
        num = _dot(scores.astype(BF16), v.astype(BF16)) + sw * _dot(qb, c_prev.astype(BF16))
        den = jnp.sum(scores, axis=-1, keepdims=True) + sw * jnp.sum(q * n_prev, axis=-1, keepdims=True)
        hh = num / jnp.maximum(jnp.abs(den), jnp.exp(-m_col))

        m_new = m_col[L - 1:L, :]
        a_last = a_col[L - 1:L, :]
        wl = jnp.exp(a_last - a_col + ig_col - m_new)
        decay = jnp.exp(a_last + m0 - m_new)
        cst_ref[h] = decay * c_prev + _dot_tn(kb, (wl * v).astype(BF16))
        nst_ref[h:h + 1, :] = decay * n_prev + jnp.sum(wl * k, axis=0, keepdims=True)
        mst_ref[:, h:h + 1] = m_new

        hn = hh * lax.rsqrt(jnp.mean(hh * hh, axis=-1, keepdims=True) + EPS) * og_ref[:, h * dv:(h + 1) * dv]
        gate = jax.nn.sigmoid(o_ref[:, h * dv:(h + 1) * dv])
        mo_ref[:, h * dv:(h + 1) * dv] = (hn * gate).astype(mo_ref.dtype)


def _mlstm(p3, prev8, conv_w, gate_row, out_g, c0, n0, m0, *, L, valid, seqs_per_step, out_dtype):
    n_seq, t_all, npk = p3.shape
    nc = t_all // L
    sb = seqs_per_step
    assert n_seq % sb == 0
    dk = c0.shape[2]
    dv = c0.shape[3]
    mw = M_HEADS * dv
    qkw = 2 * M_HEADS * dk
    assert qkw % mw == 0 and (2 * qkw) % mw == 0
    kern = functools.partial(_mlstm_kernel, L=L, valid=valid, dk=dk, dv=dv, one_chunk=nc == 1)
    seq = lambda shape: pl.BlockSpec((sb,) + shape, lambda b, c: (b,) + (0,) * len(shape))
    return pl.pallas_call(
        kern,
        grid=(n_seq // sb, nc),
        in_specs=[pl.BlockSpec((sb, L, qkw), lambda b, c: (b, c, 0)),
                  pl.BlockSpec((sb, L, mw), lambda b, c: (b, c, qkw // mw)),
                  pl.BlockSpec((sb, L, mw), lambda b, c: (b, c, qkw // mw + 1)),
                  pl.BlockSpec((sb, L, LANE), lambda b, c: (b, c, npk // LANE - 1)),
                  seq((SUBLANE, qkw)),
                  pl.BlockSpec((CONV_W, qkw), lambda b, c: (0, 0)),
                  pl.BlockSpec((1, LANE), lambda b, c: (0, 0)),
                  pl.BlockSpec((1, mw), lambda b, c: (0, 0)),
                  seq((M_HEADS, dk, dv)), seq((M_HEADS, dk)), seq((1, LANE))],
        out_specs=[pl.BlockSpec((sb, L, mw), lambda b, c: (b, c, 0)),
                   seq((M_HEADS, dk, dv)), seq((M_HEADS, dk)), seq((1, LANE))],
        out_shape=[jax.ShapeDtypeStruct((n_seq, t_all, mw), out_dtype),
                   jax.ShapeDtypeStruct(c0.shape, F32),
                   jax.ShapeDtypeStruct(n0.shape, F32),
                   jax.ShapeDtypeStruct(m0.shape, F32)],
        scratch_shapes=[pltpu.VMEM((sb, SUBLANE + L, qkw), F32)],
        compiler_params=_params(("parallel", "arbitrary"), VMEM_LIMIT),
        name="mlstm",
    )(p3, p3, p3, p3, prev8, conv_w, gate_row, out_g.reshape(1, mw), c0, n0, m0)


def _bucket_np(dist):
    me = N_BUCKETS // 2
    d = np.maximum(dist, 0)
    ratio = np.log(np.maximum(d, 1).astype(np.float64) / me) / math.log(MAX_DISTANCE / me)
    large = np.minimum(me + (ratio * (N_BUCKETS - me)).astype(np.int64), N_BUCKETS - 1)
    return np.where(d < me, d, large).astype(np.int32)


def _bias_kernel(t5_ref, bkp_ref, bks_ref, vs_ref, far_ref, op_ref, os_ref):
    rep = A_HEADS // A_KV_HEADS

    def lookup(bk, h):
        acc = jnp.zeros(bk.shape, F32)
        for b in range(N_BUCKETS):
            acc = jnp.where(bk == b, t5_ref[b, h], acc)
        return acc

    for h in range(A_HEADS):
        half = (h % 2) * Q_BLOCK
        op_ref[h // 2, :, half:half + Q_BLOCK] = (lookup(bkp_ref[...], h) - t5_ref[far_ref[0], h]) * LOG2E
        os_ref[h] = jnp.where(vs_ref[h // rep] > 0, lookup(bks_ref[...], h), NEG)


def _bias_tables(t5_bias, t_prompt, past, t_new):
    c = np.arange(2 * Q_BLOCK)[:, None]
    r = np.arange(Q_BLOCK)[None, :]
    bkp = _bucket_np(Q_BLOCK + r - c)
    far = _bucket_np(np.arange(Q_BLOCK + 1, max(t_prompt, past + t_new) + Q_BLOCK))
    assert (far == far[0]).all(), "bias must be constant beyond one query block"
    nk = past + LANE
    tq = np.arange(SUBLANE)[:, None]
    col = np.arange(2 * nk)[None, :]
    s_pos, s_grp = col // A_KV_HEADS, col % A_KV_HEADS
    bks = _bucket_np(past + tq - s_pos)
    vis = np.stack([(s_grp == g) & (s_pos - past <= tq) for g in range(A_KV_HEADS)]).astype(np.int32)
    return pl.pallas_call(
        _bias_kernel,
        in_specs=[pl.BlockSpec(memory_space=pltpu.SMEM),
                  pl.BlockSpec(memory_space=pltpu.VMEM),
                  pl.BlockSpec(memory_space=pltpu.VMEM),
                  pl.BlockSpec(memory_space=pltpu.VMEM),
                  pl.BlockSpec(memory_space=pltpu.SMEM)],
        out_specs=[pl.BlockSpec(memory_space=pltpu.VMEM), pl.BlockSpec(memory_space=pltpu.VMEM)],
        out_shape=[jax.ShapeDtypeStruct((A_HEADS // 2, 2 * Q_BLOCK, 2 * Q_BLOCK), F32),
                   jax.ShapeDtypeStruct((A_HEADS, SUBLANE, 2 * nk), F32)],
        name="t5_bias_tables",
    )(t5_bias, jnp.asarray(bkp), jnp.asarray(bks), jnp.asarray(vis), jnp.asarray(far[:1]))


def _sort_key(score):
    bits = lax.bitcast_convert_type(score, jnp.int32)
    return jnp.where(bits < 0, bits ^ jnp.int32(0x7FFFFFFF), bits)


def _kth_key(count_ge, shape, k_sel):
    def body(it, ans_u):
        cand_u = ans_u | lax.shift_left(jnp.int32(1), jnp.int32(31) - it)
        cnt = count_ge(cand_u ^ jnp.int32(INT_MIN))
        return jnp.where(cnt >= k_sel, cand_u, ans_u)

    ans_u = lax.fori_loop(0, 32, body, jnp.zeros(shape, jnp.int32))
    return ans_u ^ jnp.int32(INT_MIN)


def _triangle(below):
    r = lax.broadcasted_iota(jnp.int32, (LANE, LANE), 0)
    c = lax.broadcasted_iota(jnp.int32, (LANE, LANE), 1)
    return jnp.where((c < r) if below else (r < c), 1.0, 0.0).astype(BF16)


def _select_tiles(key_tiles, thr, k_sel, key_axis):
    count = lambda x: jnp.sum(x, axis=key_axis, keepdims=True)
    c_gt = jnp.zeros(thr.shape, F32)
    for kt in key_tiles:
        c_gt = c_gt + count(jnp.where(kt > thr, 1.0, 0.0))
    need = k_sel - c_gt
    tri = _triangle(below=(key_axis == 0))
    run = jnp.zeros(thr.shape, F32)
    out = []
    for kt in key_tiles:
        eq = jnp.where(kt == thr, 1.0, 0.0)
        eqb = eq.astype(BF16)
        before = (_dot(tri, eqb) if key_axis == 0 else _dot(eqb, tri)) + run
        out.append(jnp.where(kt > thr, 1.0, jnp.where(before < need, eq, 0.0)))
        run = run + count(eq)
    return out


FAR_STEP = 2
FAR_CHUNK = 4 * LANE


def _dsa_prompt_kernel(qn_ref, qi_ref, qtail_ref, k_ref, v_ref, ktail_ref, kg_ref, bias_ref,
                       ao_ref, kn_ref, vo_ref,
                       kb_ref, vt_ref, kib_ref, key_ref, negb_ref, s_ref, *, T, k_sel, hd):
    i = pl.program_id(1)
    nb = T // Q_BLOCK
    rep = A_HEADS // A_KV_HEADS
    near_w = 2 * LANE

    @pl.when(i == 0)
    def _():
        kb_ref[0:LANE, :] = jnp.zeros((LANE, kb_ref.shape[1]), BF16)
        vt_ref[:, 0:LANE] = jnp.zeros((vt_ref.shape[0], LANE), BF16)
        kib_ref[0:LANE, :] = jnp.zeros((LANE, IDX_DIM), BF16)
        for g in range(A_KV_HEADS):
            kn = _rms(k_ref[:, g * hd:(g + 1) * hd], kg_ref[...])
            kn_ref[pl.ds(g, T, stride=A_KV_HEADS), :] = kn
            vo_ref[pl.ds(g, T, stride=A_KV_HEADS), :] = v_ref[:, g * hd:(g + 1) * hd]
            kb_ref[LANE:, g * hd:(g + 1) * hd] = kn.astype(BF16)
        vt_ref[:, LANE:] = v_ref[...].T.astype(BF16)
        kib_ref[LANE:, :] = ktail_ref[:, TAIL_KI:TAIL_KI + IDX_DIM].astype(BF16)

    wi_t = qtail_ref[...].T[TAIL_WI:TAIL_WI + IDX_HEADS, :] * (IDX_HEADS ** -0.5) * (IDX_DIM ** -0.5)
    qi_pairs = [jnp.concatenate([qi_ref[:, (2 * a + hh) * IDX_DIM:(2 * a + hh + 1) * IDX_DIM] for hh in range(2)],
                                axis=0) for a in range(IDX_HEADS // 2)]
    qn_pairs = [jnp.concatenate([qn_ref[:, (2 * a + hh) * hd:(2 * a + hh + 1) * hd] for hh in range(2)], axis=0)
                for a in range(A_HEADS // 2)]

    near0 = pl.multiple_of(i * LANE, LANE)
    ki_near = kib_ref[pl.ds(near0, near_w), :]
    k_near = kb_ref[pl.ds(near0, near_w), :]
    vt_near = vt_ref[:, pl.ds(near0, near_w)]
    kc = lax.broadcasted_iota(jnp.int32, (near_w, Q_BLOCK), 0)
    qr = lax.broadcasted_iota(jnp.int32, (near_w, Q_BLOCK), 1)
    near_ok = jnp.where(kc <= qr + LANE, 1.0, 0.0) * jnp.maximum(jnp.where(kc >= LANE, 1.0, 0.0),
                                                                  (i >= 1).astype(F32))

    def scores(ki):
        sc = jnp.zeros((ki.shape[0], Q_BLOCK), F32)
        for a in range(IDX_HEADS // 2):
            s2 = _dot_nt(ki, qi_pairs[a])
            for hh in range(2):
                j = 2 * a + hh
                sc = sc + wi_t[j:j + 1, :] * jnp.maximum(s2[:, hh * Q_BLOCK:(hh + 1) * Q_BLOCK], 0.0)
        return sc

    def body(wf):
        fw = wf * LANE
        w_all = fw + near_w
        if w_all <= k_sel:
            negb_ref[0:w_all, :] = jnp.where(near_ok > 0.0, 0.0, NEG)
            attend(wf)
            return
        key_ref[fw:w_all, :] = _sort_key(jnp.where(near_ok > 0.0, scores(ki_near) + 0.0, NEG))
        if wf:
            far_ok = lax.broadcasted_iota(jnp.int32, (fw, Q_BLOCK), 0) < (i - 1) * LANE
            key_ref[0:fw, :] = _sort_key(jnp.where(far_ok, scores(kib_ref[LANE:LANE + fw, :]) + 0.0, NEG))

        def count_ge(cand):
            return _reduce_rows(jnp.where(key_ref[0:w_all, :] >= cand, 1.0, 0.0), jnp.sum)

        thr = _kth_key(count_ge, (1, Q_BLOCK), float(k_sel))
        sel = _select_tiles([key_ref[t * LANE:(t + 1) * LANE, :] for t in range(wf + 2)], thr, float(k_sel), 0)
        for t in range(wf):
            ok = sel[t] * (t < i - 1).astype(F32)
            negb_ref[t * LANE:(t + 1) * LANE, :] = jnp.where(ok > 0.0, 0.0, NEG)
        for t in range(2):
            ok = sel[wf + t] * near_ok[t * LANE:(t + 1) * LANE, :]
            negb_ref[fw + t * LANE:fw + (t + 1) * LANE, :] = jnp.where(ok > 0.0, 0.0, NEG)
        attend(wf)

    def attend(wf):
        fw = wf * LANE
        w_all = fw + near_w
        for a in range(A_HEADS // 2):
            g = (2 * a) // rep
            cs = slice(g * hd, (g + 1) * hd)
            qn2 = qn_pairs[a]
            nb_near = negb_ref[fw:w_all, :]
            s_near = _dot_nt(k_near[:, cs], qn2) + bias_ref[a] + jnp.concatenate([nb_near, nb_near], axis=1)
            m = _reduce_rows(s_near, jnp.max)
            chunks = [(c0, min(FAR_CHUNK, fw - c0)) for c0 in range(0, fw, FAR_CHUNK)]
            for c0, cw in chunks:
                nb_far = negb_ref[c0:c0 + cw, :]
                s_far = _dot_nt(kb_ref[LANE + c0:LANE + c0 + cw, cs], qn2) + jnp.concatenate([nb_far, nb_far], axis=1)
                s_ref[a, c0:c0 + cw, :] = s_far
                m = jnp.maximum(m, _reduce_rows(s_far, jnp.max))
            p_near = jnp.exp2(s_near - m)
            l = _reduce_rows(p_near, jnp.sum)
            acc = _dot(vt_near[cs, :], p_near.astype(BF16))
            for c0, cw in chunks:
                p_far = jnp.exp2(s_ref[a, c0:c0 + cw, :] - m)
                l = l + _reduce_rows(p_far, jnp.sum)
                acc = acc + _dot(vt_ref[cs, LANE + c0:LANE + c0 + cw], p_far.astype(BF16))
            out_t = acc / l
            for hh in range(2):
                h = 2 * a + hh
                ao_ref[:, h * hd:(h + 1) * hd] = out_t[:, hh * Q_BLOCK:(hh + 1) * Q_BLOCK].T.astype(ao_ref.dtype)

    variants = sorted({min(FAR_STEP * -(-x // FAR_STEP), nb) for x in range(max(nb - 1, 1))})
    wf_needed = jnp.minimum((jnp.maximum(i - 1, 0) + FAR_STEP - 1) // FAR_STEP * FAR_STEP, nb)
    for wf in variants:
        pl.when(wf_needed == wf)(functools.partial(body, wf))


def _dsa_prompt(p, qn, qib, kn_g, bias_p, *, n_seq, T, hd):
    rows = p.shape[0]
    nb = T // Q_BLOCK
    aw = A_HEADS * hd
    kvw = A_KV_HEADS * hd
    iw = IDX_HEADS * IDX_DIM
    k_sel = min(TOPK_MAX, T // 4)
    kcol = p.shape[1] - LANE - iw - 2 * kvw
    assert kcol % kvw == 0
    k_blk = kcol // kvw
    tail_blk = p.shape[1] // LANE - 1
    kern = functools.partial(_dsa_prompt_kernel, T=T, k_sel=k_sel, hd=hd)
    return pl.pallas_call(
        kern,
        grid=(n_seq, nb),
        in_specs=[pl.BlockSpec((Q_BLOCK, aw), lambda b, i: (b * nb + i, 0)),
                  pl.BlockSpec((Q_BLOCK, iw), lambda b, i: (b * nb + i, 0)),
                  pl.BlockSpec((Q_BLOCK, LANE), lambda b, i: (b * nb + i, tail_blk)),
                  pl.BlockSpec((T, kvw), lambda b, i: (b, k_blk)),
                  pl.BlockSpec((T, kvw), lambda b, i: (b, k_blk + 1)),
                  pl.BlockSpec((T, LANE), lambda b, i: (b, tail_blk)),
                  pl.BlockSpec((1, hd), lambda b, i: (0, 0)),
                  pl.BlockSpec((A_HEADS // 2, 2 * Q_BLOCK, 2 * Q_BLOCK), lambda b, i: (0, 0, 0))],
        out_specs=[pl.BlockSpec((Q_BLOCK, aw), lambda b, i: (b * nb + i, 0)),
                   pl.BlockSpec((A_KV_HEADS * T, hd), lambda b, i: (b, 0)),
                   pl.BlockSpec((A_KV_HEADS * T, hd), lambda b, i: (b, 0))],
        out_shape=[jax.ShapeDtypeStruct((rows, aw), BF16),
                   jax.ShapeDtypeStruct((A_KV_HEADS * rows, hd), F32),
                   jax.ShapeDtypeStruct((A_KV_HEADS * rows, hd), F32)],
        scratch_shapes=[pltpu.VMEM((T + LANE, kvw), BF16), pltpu.VMEM((kvw, T + LANE), BF16),
                        pltpu.VMEM((T + LANE, IDX_DIM), BF16),
                        pltpu.VMEM((T + 2 * LANE, Q_BLOCK), jnp.int32),
                        pltpu.VMEM((T + 2 * LANE, Q_BLOCK), F32),
                        pltpu.VMEM((A_HEADS // 2, T, 2 * Q_BLOCK), F32)],
        compiler_params=_params(("parallel", "arbitrary"), VMEM_LIMIT),
        name="dsa_prompt",
    )(qn, qib, p, p, p, p, kn_g.reshape(1, hd), bias_p)


def _pad_rows(x, n):
    return jnp.concatenate([x, jnp.zeros((n - x.shape[0], x.shape[1]), x.dtype)], axis=0)


def _idx_sample_kernel(pt_ref, qi_ref, wi_ref, kinew_ref, *rest, n_pages, t_new):
    ip, sc_ref = rest[:n_pages], rest[n_pages]
    del pt_ref
    qi = qi_ref[...].astype(BF16)
    wi = wi_ref[...] * (IDX_HEADS ** -0.5) * (IDX_DIM ** -0.5)
    tq = lax.broadcasted_iota(jnp.int32, (SUBLANE, LANE), 0)
    lane = lax.broadcasted_iota(jnp.int32, (SUBLANE, LANE), 1)
    for t in range(n_pages + 1):
        r = _dot(qi, ip[t][...].astype(BF16)) if t < n_pages else _dot_nt(qi, _pad_rows(kinew_ref[...], LANE).astype(BF16))
        sc = jnp.zeros((SUBLANE, LANE), F32)
        for j in range(IDX_HEADS):
            sc = sc + wi[j * SUBLANE:(j + 1) * SUBLANE, :] * jnp.maximum(r[j * SUBLANE:(j + 1) * SUBLANE, :], 0.0)
        sc = sc + 0.0
        if t == n_pages:
            sc = jnp.where(lane <= tq, sc, jnp.where(lane < t_new, NEG, -jnp.inf))
        sc_ref[:, t * LANE:(t + 1) * LANE] = sc


SELECT_ROWS = 256


def _select_kernel(sc_ref, sel_ref, key_ref, *, k_sel):
    rows, nk = sc_ref.shape
    key_ref[...] = _sort_key(sc_ref[...])

    def count_ge(cand):
        return jnp.sum(jnp.where(key_ref[...] >= cand, 1.0, 0.0), axis=1, keepdims=True)

    thr = _kth_key(count_ge, (rows, 1), float(k_sel))
    sel = _select_tiles([key_ref[:, t * LANE:(t + 1) * LANE] for t in range(nk // LANE)], thr, float(k_sel), 1)
    for t in range(nk // LANE):
        sel_ref[:, t * LANE:(t + 1) * LANE] = sel[t]


def _attn_sample_kernel(pt_ref, qa_ref, sel_ref, knew_ref, vnew_ref, qg_ref, kg_ref, bias_ref, *rest,
                        n_pages, hd):
    kp, vp = rest[:n_pages], rest[n_pages:2 * n_pages]
    ao_ref, kn_ref = rest[2 * n_pages:]
    del pt_ref
    nt = n_pages + 1
    pw = A_KV_HEADS * PAGE_SIZE
    scale = hd ** -0.5
    rows_q = qa_ref.shape[0]

    kn_new = _rms(knew_ref[...], kg_ref[...])
    kn_ref[...] = kn_new
    k_new = _pad_rows(kn_new, pw).astype(BF16)
    v_new = _pad_rows(vnew_ref[...], pw).astype(BF16)
    qn = _rms(qa_ref[...], qg_ref[...]).astype(BF16)

    tok = lax.broadcasted_iota(jnp.int32, (PAGE_SIZE, pw), 0)
    col = lax.broadcasted_iota(jnp.int32, (PAGE_SIZE, pw), 1)
    spread = jnp.where(col // A_KV_HEADS == tok, 1.0, 0.0).astype(BF16)
    sel_rows = jnp.concatenate([sel_ref[:, t * LANE:(t + 1) * LANE] for t in range(nt)], axis=0)
    negb = (_dot(sel_rows.astype(BF16), spread) - 1.0) * (-NEG)

    logits = []
    for t in range(nt):
        kt = kp[t][...].astype(BF16) if t < n_pages else k_new
        nb = jnp.concatenate([negb[t * SUBLANE:(t + 1) * SUBLANE, :]] * (rows_q // SUBLANE), axis=0)
        logits.append(_dot_nt(qn, kt) * scale + bias_ref[:, t * pw:(t + 1) * pw] + nb)
    mx = logits[0]
    for t in range(1, nt):
        mx = jnp.maximum(mx, logits[t])
    m = jnp.max(mx, axis=-1, keepdims=True)
    psum = jnp.zeros((rows_q, pw), F32)
    acc = jnp.zeros((rows_q, hd), F32)
    for t in range(nt):
        pr = jnp.exp(logits[t] - m)
        psum = psum + pr
        acc = acc + _dot(pr.astype(BF16), vp[t][...].astype(BF16) if t < n_pages else v_new)
    ao_ref[...] = acc / jnp.sum(psum, axis=-1, keepdims=True)


def _dsa_sample(qa, qi, wi, knew, vnew, kinew, qn_g, kn_g, bias_s, cache_k, cache_v, cache_i, page_table, *, t_new, hd):
    n_seq, n_pages = page_table.shape
    nk = n_pages * PAGE_SIZE + LANE
    k_sel = min(TOPK_MAX, (n_pages * PAGE_SIZE + t_new) // 4)
    pt = page_table.reshape(-1)

    def page_spec(arr, t):
        return pl.BlockSpec((None,) + arr.shape[1:], lambda b, pt: (pt[b * n_pages + t], 0, 0))

    seq_spec = lambda arr: pl.BlockSpec((None,) + arr.shape[1:], lambda b, pt: (b, 0, 0))
    const_spec = lambda arr: pl.BlockSpec(arr.shape, lambda b, pt: (0, 0))

    scores = pl.pallas_call(
        functools.partial(_idx_sample_kernel, n_pages=n_pages, t_new=t_new),
        grid_spec=pltpu.PrefetchScalarGridSpec(
            num_scalar_prefetch=1, grid=(n_seq,),
            in_specs=[seq_spec(qi), seq_spec(wi), seq_spec(kinew)] + [page_spec(cache_i, t) for t in range(n_pages)],
            out_specs=pl.BlockSpec((SUBLANE, nk), lambda b, pt: (b, 0))),
        out_shape=jax.ShapeDtypeStruct((n_seq * SUBLANE, nk), F32),
        compiler_params=_params(("arbitrary",), VMEM_LIMIT),
        name="idx_sample",
    )(pt, qi, wi, kinew, *([cache_i] * n_pages))

    rows = scores.shape[0]
    tr = SELECT_ROWS if rows % SELECT_ROWS == 0 else rows
    sel = pl.pallas_call(
        functools.partial(_select_kernel, k_sel=k_sel),
        grid=(rows // tr,),
        in_specs=[pl.BlockSpec((tr, nk), lambda r: (r, 0))],
        out_specs=pl.BlockSpec((tr, nk), lambda r: (r, 0)),
        out_shape=jax.ShapeDtypeStruct((rows, nk), F32),
        scratch_shapes=[pltpu.VMEM((tr, nk), jnp.int32)],
        compiler_params=_params(("parallel",), VMEM_LIMIT),
        name="select_sample",
    )(scores)

    qg, kg = qn_g.reshape(1, hd), kn_g.reshape(1, hd)
    return pl.pallas_call(
        functools.partial(_attn_sample_kernel, n_pages=n_pages, hd=hd),
        grid_spec=pltpu.PrefetchScalarGridSpec(
            num_scalar_prefetch=1, grid=(n_seq,),
            in_specs=[seq_spec(qa), pl.BlockSpec((SUBLANE, nk), lambda b, pt: (b, 0)), seq_spec(knew), seq_spec(vnew),
                      const_spec(qg), const_spec(kg), const_spec(bias_s)]
                     + [page_spec(cache_k, t) for t in range(n_pages)]
                     + [page_spec(cache_v, t) for t in range(n_pages)],
            out_specs=[seq_spec(qa), seq_spec(knew)]),
        out_shape=[jax.ShapeDtypeStruct(qa.shape, F32), jax.ShapeDtypeStruct(knew.shape, F32)],
        compiler_params=_params(("arbitrary",), VMEM_LIMIT),
        name="attn_sample",
    )(pt, qa, sel, knew, vnew, qg, kg, bias_s, *([cache_k] * n_pages), *([cache_v] * n_pages))


def _outproj_kernel(x_ref, mo_ref, ao_ref, gt_ref, sh_ref, sc_ref, g_ref, wm_ref, wa_ref, o_ref, h_ref):
    y = _dot(mo_ref[...].astype(BF16), wm_ref[...]) + _dot(ao_ref[...].astype(BF16), wa_ref[...])
    x = x_ref[...] + gt_ref[...] * y
    o_ref[...] = x
    h_ref[...] = _adaln(x, g_ref[...], sc_ref[...], sh_ref[...])


def _outproj(x, mo, ao, gt, w_out, sh, sc, g):
    m, d = x.shape
    mw, aw = mo.shape[1], ao.shape[1]
    tm = _row_tile(m)
    row = lambda w: pl.BlockSpec((tm, w), lambda i, j: (i, 0))
    return pl.pallas_call(
        _outproj_kernel,
        grid=(m // tm, 1),
        in_specs=[row(d), row(mw), row(aw), _mod_spec(gt, m, tm, d), _mod_spec(sh, m, tm, d), _mod_spec(sc, m, tm, d),
                  pl.BlockSpec((1, d), lambda i, j: (0, 0)),
                  pl.BlockSpec((mw, d), lambda i, j: (0, 0)),
                  pl.BlockSpec((aw, d), lambda i, j: (0, 0))],
        out_specs=[row(d), row(d)],
        out_shape=[jax.ShapeDtypeStruct((m, d), F32), jax.ShapeDtypeStruct((m, d), BF16)],
        compiler_params=_params(("parallel", "arbitrary"), VMEM_LIMIT),
        name="outproj",
    )(x, mo, ao, gt[0], sh[0], sc[0], g.reshape(1, d), w_out[:mw], w_out[mw:])


def _pad_tokens(a, t_pad):
    pad = [(0, 0)] * a.ndim
    pad[1] = (0, t_pad - a.shape[1])
    return jnp.pad(a, pad)


def _layer(x3, mods, lw, bias_p, bias_s, rec_state, past):
    n_seq, T, d = x3.shape
    x = x3.reshape(n_seq * T, d)
    dk, dv = rec_state[0].shape[2], rec_state[0].shape[3]
    mw = M_HEADS * dv
    hd = lw["q_norm_g"].shape[0]
    aw, kvw = A_HEADS * hd, A_KV_HEADS * hd
    iw = IDX_HEADS * IDX_DIM
    sh1, sc1, g1, sh2, sc2, g2, sh3, sc3, g3 = [(mods, k) for k in range(N_MOD)]

    x, ffn1_w = _ffn(x, g1, *lw["ffn1_w"], norm=(sh1, sc1, lw["ffn1_norm_g"]))
    qkw = 2 * M_HEADS * dk
    a_off = qkw + 2 * mw
    c0, n0, m0, conv_buf = rec_state
    prev8 = jnp.zeros((n_seq, SUBLANE, qkw), F32).at[:, SUBLANE - (CONV_W - 1):, :].set(conv_buf)
    prompt = None
    if past is None:
        prompt = (lw["q_norm_g"], (a_off, aw, a_off + aw + 2 * kvw, iw, hd))
    p, qn, qib = _proj(x, sh2, sc2, lw["mix_norm_g"], lw["w_in"], prompt)
    npk = p.shape[1]
    gate_row = jnp.zeros((1, LANE), F32).at[0, TAIL_IG:TAIL_IG + 2 * M_HEADS].set(lw["mlstm_gate_b"])
    m0_pad = jnp.zeros((n_seq, 1, LANE), F32).at[:, 0, :M_HEADS].set(m0)
    p3 = p.reshape(n_seq, T, npk)
    v_a = p3[:, :, a_off + aw + kvw:a_off + aw + 2 * kvw]
    ki = p3[:, :, npk - LANE + TAIL_KI:npk - LANE + TAIL_KI + IDX_DIM]
    conv_new = p3[:, T - (CONV_W - 1):, :qkw]

    if past is None:
        L = MLSTM_CHUNK if T % MLSTM_CHUNK == 0 else (M_CHUNK if T % M_CHUNK == 0 else T)
        mo, c_new, n_new, m_new = _mlstm(p3, prev8, lw["mlstm_conv_w"], gate_row, lw["mlstm_out_g"], c0, n0, m0_pad,
                                         L=L, valid=L, seqs_per_step=1, out_dtype=BF16)
        mo = mo.reshape(n_seq * T, mw)
        ao, k_n, v_n = _dsa_prompt(p, qn, qib, lw["k_norm_g"], bias_p, n_seq=n_seq, T=T, hd=hd)
        k_n = k_n.reshape(n_seq, T, A_KV_HEADS, hd)
        v_n = v_n.reshape(n_seq, T, A_KV_HEADS, hd)
    else:
        tp = SUBLANE
        assert T <= tp
        mo, c_new, n_new, m_new = _mlstm(_pad_tokens(p3, tp), prev8, lw["mlstm_conv_w"], gate_row, lw["mlstm_out_g"],
                                         c0, n0, m0_pad, L=tp, valid=T,
                                         seqs_per_step=SUBLANE if n_seq % SUBLANE == 0 else 1, out_dtype=F32)
        mo = mo[:, :T].reshape(n_seq * T, mw)
        cache_k, cache_v, cache_i, page_table = past
        rep = A_HEADS // A_KV_HEADS
        qa = _pad_tokens(p3[:, :, a_off:a_off + aw], tp).reshape(n_seq, tp, A_KV_HEADS, rep, hd)
        qa = qa.transpose(0, 2, 3, 1, 4).reshape(n_seq, A_HEADS * tp, hd)
        qi = _pad_tokens(p3[:, :, a_off + aw + 2 * kvw:a_off + aw + 2 * kvw + iw], tp)
        qi = qi.reshape(n_seq, tp, IDX_HEADS, IDX_DIM).transpose(0, 2, 1, 3).reshape(n_seq, IDX_HEADS * tp, IDX_DIM)
        wi = _pad_tokens(p3[:, :, npk - LANE + TAIL_WI:npk - LANE + TAIL_WI + IDX_HEADS], tp)
        wi = wi.transpose(0, 2, 1).reshape(n_seq, IDX_HEADS * tp, 1)
        knew = _pad_tokens(p3[:, :, a_off + aw:a_off + aw + kvw], tp).reshape(n_seq, tp * A_KV_HEADS, hd)
        vnew = _pad_tokens(v_a, tp).reshape(n_seq, tp * A_KV_HEADS, hd)
        n_pool = cache_k.shape[0]
        ao4, k_n8 = _dsa_sample(qa, qi, wi, knew, vnew, _pad_tokens(ki, tp), lw["q_norm_g"], lw["k_norm_g"], bias_s,
                                cache_k.reshape(n_pool, PAGE_SIZE * A_KV_HEADS, hd),
                                cache_v.reshape(n_pool, PAGE_SIZE * A_KV_HEADS, hd),
                                jnp.swapaxes(cache_i, 1, 2), page_table, t_new=T, hd=hd)
        k_n8 = k_n8.reshape(n_seq, tp, kvw)
        ao = ao4.reshape(n_seq, A_KV_HEADS, rep, tp, hd)[:, :, :, :T].transpose(0, 3, 1, 2, 4).reshape(n_seq * T, aw)
        k_n = k_n8[:, :T].reshape(n_seq, T, A_KV_HEADS, hd)
        v_n = v_a.reshape(n_seq, T, A_KV_HEADS, hd)

    x, h3 = _outproj(x, mo, ao, g2, lw["w_out"], sh3, sc3, lw["ffn2_norm_g"])
    x, ffn2_w = _ffn(x, g3, *lw["ffn2_w"], h=h3)
    state = (k_n, v_n, ki, c_new, n_new, m_new[:, 0, :M_HEADS], conv_new)
    return x.reshape(n_seq, T, d), state, dict(lw, ffn1_w=ffn1_w, ffn2_w=ffn2_w)


def kernel(x_prompt, x_sample, c_prompt, c_sample, cache_k, cache_v, cache_idx_k, page_table, state_C, state_n,
           state_m, state_conv, ffn1_norm_g, ffn1_w_gate, ffn1_w_up, ffn1_w_down, mix_norm_g, w_in, mlstm_conv_w,
           mlstm_gate_b, mlstm_out_g, q_norm_g, k_norm_g, t5_bias, w_out, ffn2_norm_g, ffn2_w_gate, ffn2_w_up,
           ffn2_w_down, w_ada, b_ada):
    depth = w_in.shape[0]
    bp, tp_len, d = x_prompt.shape
    bs, ts_len, _ = x_sample.shape
    dk, dv = state_C.shape[3], state_C.shape[4]
    mw = M_HEADS * dv
    hd = q_norm_g.shape[1]
    past_len = page_table.shape[1] * PAGE_SIZE
    bias_p, bias_s = _bias_tables(t5_bias, tp_len, past_len, ts_len)
    bias_s = bias_s.reshape(A_HEADS * SUBLANE, A_KV_HEADS * (past_len + LANE))

    xp, xs = x_prompt, x_sample
    st_p, st_s = [], []
    for l in range(depth):
        lw = dict(ffn1_norm_g=ffn1_norm_g[l], ffn1_w=(ffn1_w_gate[l], ffn1_w_up[l], ffn1_w_down[l]),
                  mix_norm_g=mix_norm_g[l], w_in=_pack_w_in(w_in[l], mw, A_HEADS * hd, A_KV_HEADS * hd),
                  mlstm_conv_w=mlstm_conv_w[l], mlstm_gate_b=mlstm_gate_b[l], mlstm_out_g=mlstm_out_g[l],
                  q_norm_g=q_norm_g[l], k_norm_g=k_norm_g[l], w_out=w_out[l].astype(BF16),
                  ffn2_norm_g=ffn2_norm_g[l], ffn2_w=(ffn2_w_gate[l], ffn2_w_up[l], ffn2_w_down[l]))
        n_s = bs * ts_len
        mods = _ada(jnp.concatenate([jnp.repeat(c_sample, ts_len, axis=0), c_prompt], axis=0), w_ada[l], b_ada[l])
        per_seq = tp_len % ROW_TILE == 0
        mods_p = mods[n_s:].reshape(bp, 1, N_MOD * d) if per_seq else jnp.repeat(mods[n_s:], tp_len, axis=0)
        init_s = (state_C[l], state_n[l], state_m[l], state_conv[l])
        xs, ss, lw = _layer(xs, mods, lw, bias_p, bias_s, init_s,
                            (cache_k[l], cache_v[l], cache_idx_k[l], page_table))
        init_p = (jnp.zeros((bp, M_HEADS, dk, dv), F32), jnp.zeros((bp, M_HEADS, dk), F32),
                  jnp.zeros((bp, M_HEADS), F32), jnp.zeros((bp, CONV_W - 1, 2 * M_HEADS * dk), F32))
        xp, sp, _ = _layer(xp, mods_p, lw, bias_p, bias_s, init_p, None)
        st_p.append(sp)
        st_s.append(ss)
    stack = lambda sts, k: sts[0][k][None] if depth == 1 else jnp.stack([s[k] for s in sts])
    return (xp, xs) + tuple(stack(st_p, k) for k in range(7)) + tuple(stack(st_s, k) for k in range(7))
```
